```python
import math
import jax
import jax.numpy as jnp
from jax import lax
import numpy as np

D_MODEL = 1024
BATCH = 8
SEQ = 4096
DEPTH = 2

CTX_LEN = 256
GRID_W = 64

LRU_WIDTH = 512
LRU_HEADS = 8
LRU_BLOCK = LRU_WIDTH // LRU_HEADS
CONV_W = 4
LRU_C = 8.0
DA_HEADS = 4
DA_HEAD_DIM = 64
DA_V_DIM = 2 * DA_HEAD_DIM
DA_WIDTH = DA_HEADS * DA_V_DIM
Q_BLOCK = 128
ROPE_BASE = 10000.0
IN_EVEN = 2 * LRU_WIDTH + 3 * DA_WIDTH
IN_SPLITS = (LRU_WIDTH, 2 * LRU_WIDTH, 2 * LRU_WIDTH + DA_WIDTH, 2 * LRU_WIDTH + 2 * DA_WIDTH)
FNET_GROUPS = 4
FNET_GROUP_DIM = D_MODEL // FNET_GROUPS
N_GROUPS = 4
EXPERTS_PER_GROUP = 8
N_EXPERTS = N_GROUPS * EXPERTS_PER_GROUP
TOP_K = 2
EXPERT_FF = 512
MOE_BLOCK = 128
LN_EPS = 1e-6
ALPHA = (2.0 * DEPTH) ** 0.25
BETA = (8.0 * DEPTH) ** -0.25

kernel_name = "hybrid_rglru_diffattn_fnet_hmoe_dit"


def _layer_norm(x, gain=None, bias=None):
    xf = x.astype(jnp.float32)
    mu = jnp.mean(xf, axis=-1, keepdims=True)
    var = jnp.mean(jnp.square(xf - mu), axis=-1, keepdims=True)
    y = (xf - mu) * lax.rsqrt(var + LN_EPS)
    if gain is not None:
        y = y * gain.astype(jnp.float32) + bias.astype(jnp.float32)
    return y.astype(x.dtype)


def _rms_norm(x, gain):
    xf = x.astype(jnp.float32)
    y = xf * lax.rsqrt(jnp.mean(jnp.square(xf), axis=-1, keepdims=True) + LN_EPS)
    return (y * gain.astype(jnp.float32)).astype(x.dtype)


def _modulate(x, shift, scale):
    return _layer_norm(x) * (1.0 + scale) + shift


def _ada_terms(cond, w, b):
    m = jax.nn.silu(cond) @ w + b
    m = m.reshape(m.shape[:-1] + (1, 6, D_MODEL))
    return [m[..., k, :] for k in range(6)]


def _axial_rope_tables(n_tok, dtype):
    t = jnp.arange(n_tok)
    row = (t // GRID_W).astype(jnp.float32)
    col = (t % GRID_W).astype(jnp.float32)
    nf = DA_HEAD_DIM // 4
    freqs = ROPE_BASE ** (-jnp.arange(nf, dtype=jnp.float32) / nf)
    ang = jnp.stack([row[:, None] * freqs, col[:, None] * freqs], axis=1)
    return jnp.cos(ang).astype(dtype), jnp.sin(ang).astype(dtype)


def _apply_axial_rope(x, cos, sin):
    xs = x.reshape(x.shape[:-1] + (2, 2, DA_HEAD_DIM // 4))
    x1 = xs[..., 0, :]
    x2 = xs[..., 1, :]
    out = jnp.stack([x1 * cos - x2 * sin, x2 * cos + x1 * sin], axis=-2)
    return out.reshape(x.shape)


def _diff_attend(q1, q2, k1, k2, v, lam):
    scale = DA_HEAD_DIM ** -0.5
    s1 = jnp.einsum('bhqd,bhkd->bhqk', q1, k1).astype(jnp.float32) * scale
    s2 = jnp.einsum('bhqd,bhkd->bhqk', q2, k2).astype(jnp.float32) * scale
    a = jax.nn.softmax(s1, axis=-1) - lam * jax.nn.softmax(s2, axis=-1)
    return jnp.einsum('bhqk,bhkv->bhqv', a.astype(v.dtype), v)


def _diff_attn_latent(q1, q2, k1, k2, v, lam):
    b, h, n, dh = q1.shape
    nb = n // Q_BLOCK

    def to_blocks(q):
        return jnp.moveaxis(q.reshape(b, h, nb, Q_BLOCK, dh), 2, 0)

    def one_block(qs):
        return _diff_attend(qs[0], qs[1], k1, k2, v, lam)

    out = lax.map(one_block, (to_blocks(q1), to_blocks(q2)))
    return jnp.moveaxis(out, 0, 2).reshape(b, h, n, DA_V_DIM)


def _merge_heads(o, gain, lam_init):
    b, h, n, dv = o.shape
    o = _rms_norm(o, gain) * (1.0 - lam_init)
    return o.transpose(0, 2, 1, 3).reshape(b, n, h * dv)


def _short_conv(x, w, b):
    n = x.shape[1]
    left = CONV_W // 2
    xp = jnp.pad(x, ((0, 0), (left, CONV_W - 1 - left), (0, 0)))
    y = b
    for k in range(CONV_W):
        y = y + xp[:, k:k + n] * w[k]
    return y


def _lru_coeffs(xc, wa, ba, wx, bx, lam):
    b, n, w = xc.shape
    xb = xc.reshape(b, n, LRU_HEADS, LRU_BLOCK)
    r = jax.nn.sigmoid((jnp.einsum('bnhi,hij->bnhj', xb, wa).reshape(b, n, w) + ba).astype(jnp.float32))
    i = jax.nn.sigmoid((jnp.einsum('bnhi,hij->bnhj', xb, wx).reshape(b, n, w) + bx).astype(jnp.float32))
    log_a = -LRU_C * jax.nn.softplus(-lam.astype(jnp.float32)) * r
    a = jnp.exp(log_a)
    mult = jnp.sqrt(-jnp.expm1(2.0 * log_a))
    return a, mult * i * xc.astype(jnp.float32)


def _combine(left, right):
    return left[0] * right[0], right[0] * left[1] + right[1]


def _linear_scan(a, b, h0, reverse):
    if reverse:
        a = jnp.flip(a, axis=1)
        b = jnp.flip(b, axis=1)
    a_cum, b_cum = lax.associative_scan(_combine, (a, b), axis=1)
    h = a_cum * h0[:, None, :] + b_cum
    return jnp.flip(h, axis=1) if reverse else h


def _rglru_bidir(xc_ctx, xc_lat, wa, ba, wx, bx, lam):
    y_ctx = jnp.zeros(xc_ctx.shape, jnp.float32)
    y_lat = jnp.zeros(xc_lat.shape, jnp.float32)
    h_zero = jnp.zeros((xc_ctx.shape[0], xc_ctx.shape[2]), jnp.float32)
    for d, rev in ((0, False), (1, True)):
        a_c, b_c = _lru_coeffs(xc_ctx, wa[d], ba[d], wx[d], bx[d], lam[d])
        h_c = _linear_scan(a_c, b_c, h_zero, rev)
        h_final = h_c[:, 0] if rev else h_c[:, -1]
        a_l, b_l = _lru_coeffs(xc_lat, wa[d], ba[d], wx[d], bx[d], lam[d])
        h_l = _linear_scan(a_l, b_l, h_final, rev)
        y_ctx = y_ctx + h_c
        y_lat = y_lat + h_l
    return y_ctx, y_lat


def _project_even(h, w_in):
    b, n, _ = h.shape
    gate, xr, q, k, v = jnp.split(h @ w_in, IN_SPLITS, axis=-1)
    q = q.reshape(b, n, DA_HEADS, 2, DA_HEAD_DIM).transpose(3, 0, 2, 1, 4)
    k = k.reshape(b, n, DA_HEADS, 2, DA_HEAD_DIM).transpose(3, 0, 2, 1, 4)
    v = v.reshape(b, n, DA_HEADS, DA_V_DIM).transpose(0, 2, 1, 3)
    return gate, xr, q, k, v


def _even_mixer(h_lat, h_ctx, w_in, conv_w, conv_b, ga_w, ga_b, gx_w, gx_b, lru_lambda,
                da_lambda, da_subln, w_out, lam_init, ctx_out):
    g_l, xr_l, q_l, k_l, v_l = _project_even(h_lat, w_in)
    g_c, xr_c, q_c, k_c, v_c = _project_even(h_ctx, w_in)
    cos, sin = _axial_rope_tables(h_lat.shape[1], q_l.dtype)
    q_l = _apply_axial_rope(q_l, cos, sin)
    k_l = _apply_axial_rope(k_l, cos, sin)
    lf = da_lambda.astype(jnp.float32)
    lam = jnp.exp(jnp.sum(lf[0] * lf[1])) - jnp.exp(jnp.sum(lf[2] * lf[3])) + lam_init
    k_all = jnp.concatenate([k_c, k_l], axis=3)
    v_all = jnp.concatenate([v_c, v_l], axis=2)
    o_l = _diff_attn_latent(q_l[0], q_l[1], k_all[0], k_all[1], v_all, lam)
    xc_l = _short_conv(xr_l, conv_w, conv_b)
    xc_c = _short_conv(xr_c, conv_w, conv_b)
    r_c, r_l = _rglru_bidir(xc_c, xc_l, ga_w, ga_b, gx_w, gx_b, lru_lambda)
    y_lat = jnp.concatenate([r_l.astype(h_lat.dtype) * jax.nn.gelu(g_l),
                             _merge_heads(o_l, da_subln, lam_init)], axis=-1) @ w_out
    if not ctx_out:
        return y_lat, None
    o_c = _diff_attend(q_c[0], q_c[1], k_c[0], k_c[1], v_c, lam)
    y_ctx = jnp.concatenate([r_c.astype(h_ctx.dtype) * jax.nn.gelu(g_c),
                             _merge_heads(o_c, da_subln, lam_init)], axis=-1) @ w_out
    return y_lat, y_ctx


def _fourier_mix(h, w, b):
    bsz, n, d = h.shape
    hg = h.astype(jnp.float32).reshape(bsz, n, FNET_GROUPS, FNET_GROUP_DIM).transpose(0, 2, 1, 3)
    y = jnp.fft.fft2(hg, axes=(-2, -1), norm='ortho').real
    y = y.transpose(0, 2, 1, 3).reshape(bsz, n, d).astype(h.dtype)
    return y @ w + b


def _hier_moe(t, wg, bg, wf, bf, w1, w3, w2):
    n_tok, d = t.shape
    g_logits = (t @ wg + bg).astype(jnp.float32)
    g_prob = jax.nn.softmax(g_logits, axis=-1)
    g_idx = jnp.argmax(g_logits, axis=-1)
    p_g = jnp.take_along_axis(g_prob, g_idx[:, None], axis=-1)
    f_logits = (t @ wf + bf).astype(jnp.float32).reshape(n_tok, N_GROUPS, EXPERTS_PER_GROUP)
    f_sel = jnp.take_along_axis(f_logits, g_idx[:, None, None], axis=1)[:, 0]
    top_v, top_j = lax.top_k(f_sel, TOP_K)
    gate_w = (jax.nn.softmax(top_v, axis=-1) * p_g).astype(t.dtype)
    e_idx = g_idx[:, None].astype(jnp.int32) * EXPERTS_PER_GROUP + top_j.astype(jnp.int32)
    m = n_tok * TOP_K
    flat_e = e_idx.reshape(m)
    flat_w = gate_w.reshape(m)
    flat_tok = jnp.repeat(jnp.arange(n_tok, dtype=jnp.int32), TOP_K)
    order = jnp.argsort(flat_e)
    s_e = flat_e[order]
    counts = jnp.bincount(flat_e, length=N_EXPERTS).astype(jnp.int32)
    padded = ((counts + MOE_BLOCK - 1) // MOE_BLOCK) * MOE_BLOCK
    start = jnp.cumsum(counts) - counts
    pend = jnp.cumsum(padded)
    pstart = pend - padded
    dest = pstart[s_e] + jnp.arange(m, dtype=jnp.int32) - start[s_e]
    n_blocks = -(-m // MOE_BLOCK) + N_EXPERTS
    n_rows = n_blocks * MOE_BLOCK
    row_tok = jnp.full((n_rows,), n_tok, dtype=jnp.int32).at[dest].set(flat_tok[order])
    row_w = jnp.zeros((n_rows,), t.dtype).at[dest].set(flat_w[order])
    block_start = jnp.arange(n_blocks, dtype=jnp.int32) * MOE_BLOCK
    block_e = jnp.minimum(jnp.searchsorted(pend, block_start, side='right'), N_EXPERTS - 1)
    t_pad = jnp.concatenate([t, jnp.zeros((1, d), t.dtype)], axis=0)
    xb = t_pad[row_tok].reshape(n_blocks, MOE_BLOCK, d)

    def expert_block(args):
        xe, e = args
        hid = jax.nn.silu(xe @ w1[e]) * (xe @ w3[e])
        return hid @ w2[e]

    yb = lax.map(expert_block, (xb, block_e))
    y = yb.reshape(n_rows, d) * row_w[:, None]
    return jnp.zeros((n_tok + 1, d), t.dtype).at[row_tok].add(y)[:n_tok]


def setup_inputs(seed: int = 0) -> dict:
    key = jax.random.key(seed)
    ks = iter(jax.random.split(key, 40))
    n_even = (DEPTH + 1) // 2
    n_odd = DEPTH // 2
    d = D_MODEL

    def nrm(shape, scale):
        return jax.random.normal(next(ks), shape, jnp.float32) * scale

    x = nrm((BATCH, SEQ, d), 1.0)
    c = nrm((BATCH, d), 1.0)
    ctx = nrm((BATCH, CTX_LEN, d), 1.0)
    c_ctx = nrm((d,), 1.0)
    ada_w = nrm((DEPTH, d, 6 * d), 0.5 * d ** -0.5)
    ada_b = nrm((DEPTH, 6 * d), 0.02)
    ln_g = 1.0 + nrm((DEPTH, 2, d), 0.02)
    ln_b = nrm((DEPTH, 2, d), 0.02)
    ev_w_in = nrm((n_even, d, IN_EVEN), d ** -0.5)
    ev_conv_w = nrm((n_even, CONV_W, LRU_WIDTH), CONV_W ** -0.5)
    ev_conv_b = nrm((n_even, LRU_WIDTH), 0.02)
    ev_gate_a_w = nrm((n_even, 2, LRU_HEADS, LRU_BLOCK, LRU_BLOCK), LRU_BLOCK ** -0.5)
    ev_gate_a_b = nrm((n_even, 2, LRU_WIDTH), 0.02)
    ev_gate_x_w = nrm((n_even, 2, LRU_HEADS, LRU_BLOCK, LRU_BLOCK), LRU_BLOCK ** -0.5)
    ev_gate_x_b = nrm((n_even, 2, LRU_WIDTH), 0.02)
    u = jax.random.uniform(next(ks), (n_even, 2, LRU_WIDTH), jnp.float32, minval=0.9, maxval=0.999)
    a0 = u ** (1.0 / LRU_C)
    ev_lru_lambda = jnp.log(a0) - jnp.log1p(-a0)
    ev_da_lambda = nrm((n_even, 4, DA_HEAD_DIM), 0.1)
    ev_da_subln = 1.0 + nrm((n_even, DA_V_DIM), 0.02)
    ev_w_out = nrm((n_even, LRU_WIDTH + DA_WIDTH, d), BETA * (LRU_WIDTH + DA_WIDTH) ** -0.5)
    od_w_out = nrm((n_odd, d, d), BETA * d ** -0.5)
    od_b_out = nrm((n_odd, d), 0.02)
    moe_wg = nrm((DEPTH, d, N_GROUPS), d ** -0.5)
    moe_bg = nrm((DEPTH, N_GROUPS), 0.01)
    moe_wf = nrm((DEPTH, d, N_EXPERTS), d ** -0.5)
    moe_bf = nrm((DEPTH, N_EXPERTS), 0.01)
    moe_w1 = nrm((DEPTH, N_EXPERTS, d, EXPERT_FF), d ** -0.5)
    moe_w3 = nrm((DEPTH, N_EXPERTS, d, EXPERT_FF), d ** -0.5)
    moe_w2 = nrm((DEPTH, N_EXPERTS, EXPERT_FF, d), BETA * EXPERT_FF ** -0.5)
    return {"x": x, "c": c, "ctx": ctx, "c_ctx": c_ctx, "ada_w": ada_w, "ada_b": ada_b,
            "ln_g": ln_g, "ln_b": ln_b, "ev_w_in": ev_w_in, "ev_conv_w": ev_conv_w,
            "ev_conv_b": ev_conv_b, "ev_gate_a_w": ev_gate_a_w, "ev_gate_a_b": ev_gate_a_b,
            "ev_gate_x_w": ev_gate_x_w, "ev_gate_x_b": ev_gate_x_b, "ev_lru_lambda": ev_lru_lambda,
            "ev_da_lambda": ev_da_lambda, "ev_da_subln": ev_da_subln, "ev_w_out": ev_w_out,
            "od_w_out": od_w_out, "od_b_out": od_b_out, "moe_wg": moe_wg, "moe_bg": moe_bg,
            "moe_wf": moe_wf, "moe_bf": moe_bf, "moe_w1": moe_w1, "moe_w3": moe_w3, "moe_w2": moe_w2}


def reference(x, c, ctx, c_ctx, ada_w, ada_b, ln_g, ln_b, ev_w_in, ev_conv_w, ev_conv_b,
              ev_gate_a_w, ev_gate_a_b, ev_gate_x_w, ev_gate_x_b, ev_lru_lambda, ev_da_lambda,
              ev_da_subln, ev_w_out, od_w_out, od_b_out, moe_wg, moe_bg, moe_wf, moe_bf,
              moe_w1, moe_w3, moe_w2):
    bsz, n_lat, d = x.shape
    n_ctx_tok = ctx.shape[0] * ctx.shape[1]
    for l in range(DEPTH):
        ctx_live = any(m % 2 == 0 for m in range(l + 1, DEPTH))
        sh1, sc1, g1, sh2, sc2, g2 = _ada_terms(c, ada_w[l], ada_b[l])
        csh1, csc1, cg1, csh2, csc2, cg2 = _ada_terms(c_ctx, ada_w[l], ada_b[l])
        if l % 2 == 0:
            e = l // 2
            lam_init = 0.8 - 0.6 * math.exp(-0.3 * l)
            y_lat, y_ctx = _even_mixer(_modulate(x, sh1, sc1), _modulate(ctx, csh1, csc1),
                                       ev_w_in[e], ev_conv_w[e], ev_conv_b[e], ev_gate_a_w[e],
                                       ev_gate_a_b[e], ev_gate_x_w[e], ev_gate_x_b[e],
                                       ev_lru_lambda[e], ev_da_lambda[e], ev_da_subln[e],
                                       ev_w_out[e], lam_init, ctx_live)
        else:
            o = l // 2
            y_lat = _fourier_mix(_modulate(x, sh1, sc1), od_w_out[o], od_b_out[o])
            y_ctx = _fourier_mix(_modulate(ctx, csh1, csc1), od_w_out[o], od_b_out[o]) if ctx_live else None
        x = _layer_norm(ALPHA * x + g1 * y_lat, ln_g[l, 0], ln_b[l, 0])
        if ctx_live:
            ctx = _layer_norm(ALPHA * ctx + cg1 * y_ctx, ln_g[l, 0], ln_b[l, 0])
            tok = jnp.concatenate([_modulate(ctx, csh2, csc2).reshape(-1, d),
                                   _modulate(x, sh2, sc2).reshape(-1, d)], axis=0)
            mo = _hier_moe(tok, moe_wg[l], moe_bg[l], moe_wf[l], moe_bf[l], moe_w1[l], moe_w3[l], moe_w2[l])
            m_ctx = mo[:n_ctx_tok].reshape(ctx.shape)
            m_lat = mo[n_ctx_tok:].reshape(x.shape)
            ctx = _layer_norm(ALPHA * ctx + cg2 * m_ctx, ln_g[l, 1], ln_b[l, 1])
        else:
            m_lat = _hier_moe(_modulate(x, sh2, sc2).reshape(-1, d), moe_wg[l], moe_bg[l], moe_wf[l],
                              moe_bf[l], moe_w1[l], moe_w3[l], moe_w2[l]).reshape(x.shape)
        x = _layer_norm(ALPHA * x + g2 * m_lat, ln_g[l, 1], ln_b[l, 1])
    return x
```

```python
import functools
import math

import numpy as np
import jax
import jax.numpy as jnp
from jax import lax
from jax.experimental import pallas as pl
from jax.experimental.pallas import tpu as pltpu

F32 = jnp.float32
BF16 = jnp.bfloat16
I32 = jnp.int32

LN_EPS = 1e-6
LRU_C = 8.0
ROPE_BASE = 10000.0
GRID_W = 64
CONV_W = 4
LANES = 128
SUBLANES = 8
MOE_ROWS = 256
VMEM_LIMIT = 56 * 1024 * 1024


def _cp(sem, vmem=None):
    return pltpu.CompilerParams(dimension_semantics=sem, vmem_limit_bytes=vmem)


def _ln(x):
    mu = jnp.mean(x, axis=-1, keepdims=True)
    xc = x - mu
    var = jnp.mean(xc * xc, axis=-1, keepdims=True)
    return xc * lax.rsqrt(var + LN_EPS)


def _silu(x):
    return x * jax.nn.sigmoid(x)


def _ada_kernel(c_ref, w_ref, b_ref, o_ref):
    s = _silu(c_ref[...]).astype(BF16)
    o_ref[...] = jnp.dot(s, w_ref[...].astype(BF16), preferred_element_type=F32) + b_ref[...]


def _ada_terms(cond, ada_w, ada_b):
    nl, d, d6 = ada_w.shape
    r = cond.shape[0]
    tn = 1024
    return pl.pallas_call(
        _ada_kernel,
        out_shape=jax.ShapeDtypeStruct((nl, r, d6), F32),
        grid=(nl, d6 // tn),
        in_specs=[pl.BlockSpec((r, d), lambda l, j: (0, 0)),
                  pl.BlockSpec((None, d, tn), lambda l, j: (l, 0, j)),
                  pl.BlockSpec((None, 1, tn), lambda l, j: (l, 0, j))],
        out_specs=pl.BlockSpec((None, r, tn), lambda l, j: (l, 0, j)),
        compiler_params=_cp(("parallel", "parallel")),
        name="ada_terms",
    )(cond, ada_w, ada_b.reshape(nl, 1, d6))


def _rope_apply(x, cos, sin_signed):
    tm = x.shape[0]
    lane = lax.broadcasted_iota(I32, (tm, LANES), 1)
    first_half = (lane % 32) < 16
    outs = []
    for j in range(x.shape[1] // LANES):
        xh = x[:, j * LANES:(j + 1) * LANES]
        partner = jnp.where(first_half, pltpu.roll(xh, LANES - 16, 1), pltpu.roll(xh, 16, 1))
        outs.append(xh * cos + partner * sin_signed)
    return jnp.concatenate(outs, axis=1)


def _proj_kernel(x_ref, sh_ref, sc_ref, w_ref, cos_ref, sin_ref,
                 g_ref, xr_ref, q_ref, k_ref, v_ref, *, rope, lw, aw, qscale):
    h = _ln(x_ref[...]) * (1.0 + sc_ref[...]) + sh_ref[...]
    hb = h.astype(BF16)

    def mm(c0, c1):
        return jnp.dot(hb, w_ref[:, c0:c1], preferred_element_type=F32)

    g_ref[...] = mm(0, lw).astype(BF16)
    xr_ref[...] = mm(lw, 2 * lw)
    q = mm(2 * lw, 2 * lw + aw)
    k = mm(2 * lw + aw, 2 * lw + 2 * aw)
    if rope:
        q = _rope_apply(q, cos_ref[...], sin_ref[...])
        k = _rope_apply(k, cos_ref[...], sin_ref[...])
    q_ref[...] = (q * qscale).astype(BF16)
    k_ref[...] = k.astype(BF16)
    v_ref[...] = mm(2 * lw + 2 * aw, 2 * lw + 3 * aw).astype(BF16)


def _project_even(x, shift, scale, w_in_b, cos_t, sin_t, *, rope, lw, aw, qscale, tm):
    b, n, d = x.shape
    tm = min(tm, n)
    nin = w_in_b.shape[1]
    tok = lambda bi, i: (bi, i, 0)
    per_b = lambda bi, i: (bi, 0, 0)
    outs = (jax.ShapeDtypeStruct((b, n, lw), BF16), jax.ShapeDtypeStruct((b, n, lw), F32),
            jax.ShapeDtypeStruct((b, n, aw), BF16), jax.ShapeDtypeStruct((b, n, aw), BF16),
            jax.ShapeDtypeStruct((b, n, aw), BF16))
    return pl.pallas_call(
        functools.partial(_proj_kernel, rope=rope, lw=lw, aw=aw, qscale=qscale),
        out_shape=outs,
        grid=(b, n // tm),
        in_specs=[pl.BlockSpec((None, tm, d), tok),
                  pl.BlockSpec((None, 1, d), per_b),
                  pl.BlockSpec((None, 1, d), per_b),
                  pl.BlockSpec((d, nin), lambda bi, i: (0, 0)),
                  pl.BlockSpec((tm, LANES), lambda bi, i: (i, 0)),
                  pl.BlockSpec((tm, LANES), lambda bi, i: (i, 0))],
        out_specs=(pl.BlockSpec((None, tm, lw), tok), pl.BlockSpec((None, tm, lw), tok),
                   pl.BlockSpec((None, tm, aw), tok), pl.BlockSpec((None, tm, aw), tok),
                   pl.BlockSpec((None, tm, aw), tok)),
        compiler_params=_cp(("parallel", "parallel"), VMEM_LIMIT),
        name="proj_even_rope" if rope else "proj_even_ctx",
    )(x, shift, scale, w_in_b, cos_t, sin_t)


def _rope_tables(n_tok, head_dim):
    t = jnp.arange(n_tok)
    row = (t // GRID_W).astype(F32)
    col = (t % GRID_W).astype(F32)
    nf = head_dim // 4
    freqs = ROPE_BASE ** (-jnp.arange(nf, dtype=F32) / nf)
    lane = np.arange(LANES)
    within = lane % head_dim
    axis = within // (2 * nf)
    half = (within % (2 * nf)) // nf
    f = within % nf
    pos = jnp.where(jnp.asarray(axis)[None, :] == 0, row[:, None], col[:, None])
    ang = pos * freqs[jnp.asarray(f)][None, :]
    sign = jnp.asarray(np.where(half == 0, -1.0, 1.0), F32)[None, :]
    return jnp.cos(ang).astype(F32), (jnp.sin(ang) * sign).astype(F32)


def _attn_kernel(q_ref, kc_ref, kl_ref, vc_ref, vl_ref, dl_ref, gain_ref, o_ref, kbuf, vbuf,
                 *, nc, nl, hd, lam_init, rows):
    @pl.when(pl.program_id(2) == 0)
    def _():
        kbuf[0:nc, :] = kc_ref[...]
        kbuf[nc:nc + nl, :] = kl_ref[...]
        vbuf[0:nc, :] = vc_ref[...]
        vbuf[nc:nc + nl, :] = vl_ref[...]

    lf = dl_ref[...]
    lam = (jnp.exp(jnp.sum(lf[0:1] * lf[1:2], axis=1, keepdims=True))
           - jnp.exp(jnp.sum(lf[2:3] * lf[3:4], axis=1, keepdims=True)) + lam_init)
    gain = gain_ref[...] * (1.0 - lam_init)
    tq = q_ref.shape[0]
    lane = lax.broadcasted_iota(I32, (rows, 2 * hd), 1)
    nt = (((1,), (1,)), ((), ()))

    def body(j, carry):
        r0 = pl.multiple_of(j * rows, rows)
        q = q_ref[pl.ds(r0, rows), :]
        zero = jnp.zeros_like(q)
        q1 = jnp.where(lane < hd, q, zero)
        q2 = jnp.where(lane >= hd, q, zero)
        kk = kbuf[...]
        s1 = lax.dot_general(q1, kk, nt, preferred_element_type=F32)
        s2 = lax.dot_general(q2, kk, nt, preferred_element_type=F32)
        e1 = jnp.exp(s1 - jnp.max(s1, axis=1, keepdims=True))
        e2 = jnp.exp(s2 - jnp.max(s2, axis=1, keepdims=True))
        c1 = 1.0 / jnp.sum(e1, axis=1, keepdims=True)
        c2 = lam / jnp.sum(e2, axis=1, keepdims=True)
        a = (e1 * c1 - e2 * c2).astype(BF16)
        o = jnp.dot(a, vbuf[...], preferred_element_type=F32)
        o = o * lax.rsqrt(jnp.mean(o * o, axis=1, keepdims=True) + LN_EPS) * gain
        o_ref[pl.ds(r0, rows), :] = o.astype(BF16)
        return carry

    lax.fori_loop(0, tq // rows, body, 0)


def _diff_attention(q, k_ctx, k_lat, v_ctx, v_lat, da_lambda, subln, *, heads, hd, lam_init, tq):
    b, n, aw = q.shape
    nc = k_ctx.shape[1]
    vd = aw // heads
    tq = min(tq, n)
    rows = min(128, tq)
    blk_q = pl.BlockSpec((None, tq, vd), lambda bi, h, i: (bi, i, h))
    blk_c = pl.BlockSpec((None, nc, vd), lambda bi, h, i: (bi, 0, h))
    blk_l = pl.BlockSpec((None, n, vd), lambda bi, h, i: (bi, 0, h))
    return pl.pallas_call(
        functools.partial(_attn_kernel, nc=nc, nl=n, hd=hd, lam_init=lam_init, rows=rows),
        out_shape=jax.ShapeDtypeStruct((b, n, aw), BF16),
        grid=(b, heads, n // tq),
        in_specs=[blk_q, blk_c, blk_l, blk_c, blk_l,
                  pl.BlockSpec(da_lambda.shape, lambda bi, h, i: (0, 0)),
                  pl.BlockSpec((1, vd), lambda bi, h, i: (0, 0))],
        out_specs=blk_q,
        scratch_shapes=[pltpu.VMEM((nc + n, vd), BF16), pltpu.VMEM((nc + n, vd), BF16)],
        compiler_params=_cp(("parallel", "parallel", "arbitrary"), VMEM_LIMIT),
        name="diff_attention",
    )(q, k_ctx, k_lat, v_ctx, v_lat, da_lambda, subln.reshape(1, vd))


def _lru_kernel(*refs, reverse, add_prev, nt, groups):
    if add_prev:
        (xp_ref, xc_ref, xn_ref, yprev_ref, cw_ref, cb_ref, wa_ref, ba_ref, wx_ref, bx_ref,
         lam_ref, h0_ref, y_ref, hf_ref, carry) = refs
    else:
        (xp_ref, xc_ref, xn_ref, cw_ref, cb_ref, wa_ref, ba_ref, wx_ref, bx_ref,
         lam_ref, h0_ref, y_ref, hf_ref, carry) = refs
        yprev_ref = None
    i = pl.program_id(1)
    ti = (nt - 1 - i) if reverse else i
    tn, w = xc_ref.shape

    @pl.when(i == 0)
    def _():
        carry[...] = h0_ref[...]

    prev = jnp.where(ti > 0, xp_ref[...], 0.0)
    nxt = jnp.where(ti < nt - 1, xn_ref[...], 0.0)
    ext = jnp.concatenate([prev, xc_ref[...], nxt], axis=0)
    left = CONV_W // 2
    xc = cb_ref[...]
    for k in range(CONV_W):
        off = SUBLANES - left + k
        xc = xc + ext[off:off + tn, :] * cw_ref[k:k + 1, :]

    xb = xc.astype(BF16)
    gw = w // groups

    def gate(w_ref, b_ref):
        parts = [jnp.dot(xb[:, g * gw:(g + 1) * gw], w_ref[g], preferred_element_type=F32)
                 for g in range(groups)]
        return jax.nn.sigmoid(jnp.concatenate(parts, axis=1) + b_ref[...])

    r = gate(wa_ref, ba_ref)
    ig = gate(wx_ref, bx_ref)
    log_a = (-LRU_C * jax.nn.softplus(-lam_ref[...])) * r
    a = jnp.exp(log_a)
    bcoef = jnp.sqrt(-jnp.tanh(log_a) * (a * a + 1.0)) * ig * xc

    row = lax.broadcasted_iota(I32, (tn, w), 0)
    d = 1
    while d < tn:
        if reverse:
            a_sh = pltpu.roll(a, tn - d, 0)
            b_sh = pltpu.roll(bcoef, tn - d, 0)
            live = row < tn - d
        else:
            a_sh = pltpu.roll(a, d, 0)
            b_sh = pltpu.roll(bcoef, d, 0)
            live = row >= d
        bcoef = jnp.where(live, a * b_sh + bcoef, bcoef)
        a = jnp.where(live, a * a_sh, a)
        d *= 2
    h = a * carry[...] + bcoef
    carry[...] = h[0:1, :] if reverse else h[tn - 1:tn, :]
    hf_ref[...] = carry[...]
    if add_prev:
        y_ref[...] = yprev_ref[...] + h
    else:
        y_ref[...] = h


def _rglru_dir(xr, y_prev, conv_w, conv_b, wa_bd, ba, wx_bd, bx, lam, h0, *, reverse, tn):
    b, n, w = xr.shape
    tn = min(tn, n)
    nt = n // tn
    groups = wa_bd.shape[0]
    nb8 = n // SUBLANES
    per8 = tn // SUBLANES

    def tmap(i):
        return (nt - 1 - i) if reverse else i

    cur = pl.BlockSpec((None, tn, w), lambda bi, i: (bi, tmap(i), 0))
    halo_p = pl.BlockSpec((None, SUBLANES, w), lambda bi, i: (bi, jnp.maximum(tmap(i) * per8 - 1, 0), 0))
    halo_n = pl.BlockSpec((None, SUBLANES, w), lambda bi, i: (bi, jnp.minimum((tmap(i) + 1) * per8, nb8 - 1), 0))
    row_w = pl.BlockSpec((1, w), lambda bi, i: (0, 0))
    per_b = pl.BlockSpec((None, 1, w), lambda bi, i: (bi, 0, 0))
    gate_w = pl.BlockSpec(wa_bd.shape, lambda bi, i: (0, 0, 0))
    add_prev = y_prev is not None
    in_specs = [halo_p, cur, halo_n] + ([cur] if add_prev else []) + [
        pl.BlockSpec((CONV_W, w), lambda bi, i: (0, 0)), row_w, gate_w, row_w, gate_w, row_w, row_w, per_b]
    args = [xr, xr, xr] + ([y_prev] if add_prev else []) + [
        conv_w, conv_b.reshape(1, w), wa_bd, ba.reshape(1, w), wx_bd, bx.reshape(1, w),
        lam.reshape(1, w), h0]
    return pl.pallas_call(
        functools.partial(_lru_kernel, reverse=reverse, add_prev=add_prev, nt=nt, groups=groups),
        out_shape=(jax.ShapeDtypeStruct((b, n, w), F32), jax.ShapeDtypeStruct((b, 1, w), F32)),
        grid=(b, nt),
        in_specs=in_specs,
        out_specs=(cur, per_b),
        scratch_shapes=[pltpu.VMEM((1, w), F32)],
        compiler_params=_cp(("parallel", "arbitrary"), VMEM_LIMIT),
        name="rglru_rev" if reverse else "rglru_fwd",
    )(*args)


def _block_diag_groups(wh, group_width):
    heads, blk, _ = wh.shape
    per = group_width // blk
    groups = heads // per
    whg = wh.reshape(groups, per, blk, blk)
    eye = jnp.eye(per, dtype=wh.dtype)
    bd = jnp.einsum('gpij,pq->gpiqj', whg, eye).reshape(groups, group_width, group_width)
    return bd.astype(BF16)


def _residual_ln_mod(x, y, g1, lng, lnb, sh2, sc2, alpha):
    x1 = _ln(alpha * x + g1 * y) * lng + lnb
    t = _ln(x1) * (1.0 + sc2) + sh2
    return x1, t


def _outproj_even_kernel(r_ref, g_ref, o_ref, x_ref, w_ref, g1_ref, lng_ref, lnb_ref, sh2_ref, sc2_ref,
                         x1_ref, t_ref, *, lw, alpha):
    z = (r_ref[...] * jax.nn.gelu(g_ref[...].astype(F32))).astype(BF16)
    y = (jnp.dot(z, w_ref[0:lw, :], preferred_element_type=F32)
         + jnp.dot(o_ref[...], w_ref[lw:, :], preferred_element_type=F32))
    x1, t = _residual_ln_mod(x_ref[...], y, g1_ref[...], lng_ref[...], lnb_ref[...],
                             sh2_ref[...], sc2_ref[...], alpha)
    x1_ref[...] = x1
    t_ref[...] = t


def _outproj_even(r, g, o, x, w_out_b, g1, lng, lnb, sh2, sc2, *, alpha, tm):
    b, n, d = x.shape
    lw = r.shape[2]
    aw = o.shape[2]
    tm = min(tm, n)
    tok = lambda bi, i: (bi, i, 0)
    per_b = pl.BlockSpec((None, 1, d), lambda bi, i: (bi, 0, 0))
    row = pl.BlockSpec((1, d), lambda bi, i: (0, 0))
    return pl.pallas_call(
        functools.partial(_outproj_even_kernel, lw=lw, alpha=alpha),
        out_shape=(jax.ShapeDtypeStruct((b, n, d), F32), jax.ShapeDtypeStruct((b, n, d), F32)),
        grid=(b, n // tm),
        in_specs=[pl.BlockSpec((None, tm, lw), tok), pl.BlockSpec((None, tm, lw), tok),
                  pl.BlockSpec((None, tm, aw), tok), pl.BlockSpec((None, tm, d), tok),
                  pl.BlockSpec(w_out_b.shape, lambda bi, i: (0, 0)),
                  per_b, row, row, per_b, per_b],
        out_specs=(pl.BlockSpec((None, tm, d), tok), pl.BlockSpec((None, tm, d), tok)),
        compiler_params=_cp(("parallel", "parallel"), VMEM_LIMIT),
        name="outproj_even",
    )(r, g, o, x, w_out_b, g1, lng.reshape(1, d), lnb.reshape(1, d), sh2, sc2)


def _chan_dft_kernel(x_ref, sh_ref, sc_ref, tab_ref, zr_ref, zi_ref, *, groups, gd):
    h = (_ln(x_ref[...]) * (1.0 + sc_ref[...]) + sh_ref[...]).astype(BF16)
    for g in range(groups):
        z = jnp.dot(h[:, g * gd:(g + 1) * gd], tab_ref[...], preferred_element_type=F32)
        zr_ref[:, g * gd:(g + 1) * gd] = z[:, 0:gd].astype(BF16)
        zi_ref[:, g * gd:(g + 1) * gd] = z[:, gd:2 * gd].astype(BF16)


def _chan_dft(x, shift, scale, tab, *, groups, tm):
    b, n, d = x.shape
    gd = d // groups
    tm = min(tm, n)
    tok = lambda bi, i: (bi, i, 0)
    per_b = pl.BlockSpec((None, 1, d), lambda bi, i: (bi, 0, 0))
    return pl.pallas_call(
        functools.partial(_chan_dft_kernel, groups=groups, gd=gd),
        out_shape=(jax.ShapeDtypeStruct((b, n, d), BF16), jax.ShapeDtypeStruct((b, n, d), BF16)),
        grid=(b, n // tm),
        in_specs=[pl.BlockSpec((None, tm, d), tok), per_b, per_b,
                  pl.BlockSpec(tab.shape, lambda bi, i: (0, 0))],
        out_specs=(pl.BlockSpec((None, tm, d), tok), pl.BlockSpec((None, tm, d), tok)),
        compiler_params=_cp(("parallel", "parallel"), VMEM_LIMIT),
        name="chan_dft",
    )(x, shift, scale, tab)


def _tok_dft_kernel(tc_ref, ts_ref, zr_ref, zi_ref, o_ref):
    acc = jnp.dot(tc_ref[...], zr_ref[...], preferred_element_type=F32)
    acc = acc - jnp.dot(ts_ref[...], zi_ref[...], preferred_element_type=F32)
    o_ref[...] = acc.astype(BF16)


def _tok_dft(tc, ts, zr, zi, *, tk):
    b, n, d = zr.shape
    tk = min(tk, n)
    lhs = pl.BlockSpec((tk, n), lambda bi, j: (j, 0))
    rhs = pl.BlockSpec((None, n, d), lambda bi, j: (bi, 0, 0))
    return pl.pallas_call(
        _tok_dft_kernel,
        out_shape=jax.ShapeDtypeStruct((b, n, d), BF16),
        grid=(b, n // tk),
        in_specs=[lhs, lhs, rhs, rhs],
        out_specs=pl.BlockSpec((None, tk, d), lambda bi, j: (bi, j, 0)),
        compiler_params=_cp(("parallel", "parallel"), VMEM_LIMIT),
        name="tok_dft",
    )(tc, ts, zr, zi)


def _dft_tables(n, gd):
    sub = 64 if n % 64 == 0 else 1
    t = np.arange(n, dtype=np.float64)
    a = np.arange(n // sub, dtype=np.float64)
    bb = np.arange(sub, dtype=np.float64)
    ang_a = 2.0 * np.pi * ((np.outer(a, t) * sub) % n) / n
    ang_b = 2.0 * np.pi * (np.outer(bb, t) % n) / n
    pc, ps = jnp.asarray(np.cos(ang_a), F32)[:, None, :], jnp.asarray(np.sin(ang_a), F32)[:, None, :]
    qc, qs = jnp.asarray(np.cos(ang_b), F32)[None, :, :], jnp.asarray(np.sin(ang_b), F32)[None, :, :]
    tc = (pc * qc - ps * qs).reshape(n, n).astype(BF16)
    ts = (ps * qc + pc * qs).reshape(n, n).astype(BF16)
    c = np.arange(gd, dtype=np.float64)
    ang_c = 2.0 * np.pi * (np.outer(c, c) % gd) / gd
    tab = jnp.asarray(np.concatenate([np.cos(ang_c), np.sin(ang_c)], axis=1), F32).astype(BF16)
    return tc, ts, tab


def _outproj_odd_kernel(wv_ref, x_ref, w_ref, b_ref, g1_ref, lng_ref, lnb_ref, sh2_ref, sc2_ref,
                        x1_ref, t_ref, *, alpha, norm):
    y = jnp.dot(wv_ref[...], w_ref[...], preferred_element_type=F32) * norm + b_ref[...]
    x1, t = _residual_ln_mod(x_ref[...], y, g1_ref[...], lng_ref[...], lnb_ref[...],
                             sh2_ref[...], sc2_ref[...], alpha)
    x1_ref[...] = x1
    t_ref[...] = t


def _outproj_odd(wv, x, w_b, bias, g1, lng, lnb, sh2, sc2, *, alpha, norm, tm):
    b, n, d = x.shape
    tm = min(tm, n)
    tok = lambda bi, i: (bi, i, 0)
    per_b = pl.BlockSpec((None, 1, d), lambda bi, i: (bi, 0, 0))
    row = pl.BlockSpec((1, d), lambda bi, i: (0, 0))
    return pl.pallas_call(
        functools.partial(_outproj_odd_kernel, alpha=alpha, norm=norm),
        out_shape=(jax.ShapeDtypeStruct((b, n, d), F32), jax.ShapeDtypeStruct((b, n, d), F32)),
        grid=(b, n // tm),
        in_specs=[pl.BlockSpec((None, tm, d), tok), pl.BlockSpec((None, tm, d), tok),
                  pl.BlockSpec(w_b.shape, lambda bi, i: (0, 0)), row,
                  per_b, row, row, per_b, per_b],
        out_specs=(pl.BlockSpec((None, tm, d), tok), pl.BlockSpec((None, tm, d), tok)),
        compiler_params=_cp(("parallel", "parallel"), VMEM_LIMIT),
        name="outproj_odd",
    )(wv, x, w_b, bias.reshape(1, d), g1, lng.reshape(1, d), lnb.reshape(1, d), sh2, sc2)


def _router_kernel(t_ref, w_ref, b_ref, idx_ref, wt_ref, *, ng, epg):
    tm = t_ref.shape[0]
    logits = jnp.dot(t_ref[...].astype(BF16), w_ref[...], preferred_element_type=F32) + b_ref[...]
    lt = logits.T
    best = lt[0:1, :]
    bi = jnp.zeros((1, tm), I32)
    for k in range(1, ng):
        gk = lt[k:k + 1, :]
        upd = gk > best
        bi = jnp.where(upd, k, bi)
        best = jnp.where(upd, gk, best)
    den = jnp.zeros((1, tm), F32)
    for k in range(ng):
        den = den + jnp.exp(lt[k:k + 1, :] - best)
    p_g = 1.0 / den
    fsel = lt[SUBLANES:SUBLANES + epg, :]
    for k in range(1, ng):
        fsel = jnp.where(bi == k, lt[SUBLANES + k * epg:SUBLANES + (k + 1) * epg, :], fsel)
    neg = jnp.full((1, tm), -jnp.inf, F32)
    m1, m2 = neg, neg
    i1 = jnp.zeros((1, tm), I32)
    i2 = jnp.zeros((1, tm), I32)
    for j in range(epg):
        v = fsel[j:j + 1, :]
        gt1 = v > m1
        gt2 = v > m2
        m2 = jnp.where(gt1, m1, jnp.where(gt2, v, m2))
        i2 = jnp.where(gt1, i1, jnp.where(gt2, j, i2))
        m1 = jnp.where(gt1, v, m1)
        i1 = jnp.where(gt1, j, i1)
    e21 = jnp.exp(m2 - m1)
    w1 = p_g / (1.0 + e21)
    w2 = p_g * e21 / (1.0 + e21)
    e1 = bi * epg + i1
    e2 = bi * epg + i2
    row8 = lax.broadcasted_iota(I32, (SUBLANES, tm), 0)
    idx_ref[...] = jnp.where(row8 == 0, e1, jnp.where(row8 == 1, e2, 0))
    rowl = lax.broadcasted_iota(I32, (LANES, tm), 0)
    wfull = jnp.where(rowl == 0, w1, jnp.where(rowl == 1, w2, 0.0))
    wt_ref[...] = wfull.T


def _router(t, wcat_b, bcat, *, ng, epg, tm):
    tt, d = t.shape
    tm = min(tm, tt)
    return pl.pallas_call(
        functools.partial(_router_kernel, ng=ng, epg=epg),
        out_shape=(jax.ShapeDtypeStruct((SUBLANES, tt), I32), jax.ShapeDtypeStruct((tt, LANES), F32)),
        grid=(tt // tm,),
        in_specs=[pl.BlockSpec((tm, d), lambda i: (i, 0)),
                  pl.BlockSpec((d, LANES), lambda i: (0, 0)),
                  pl.BlockSpec((1, LANES), lambda i: (0, 0))],
        out_specs=(pl.BlockSpec((SUBLANES, tm), lambda i: (0, i)),
                   pl.BlockSpec((tm, LANES), lambda i: (i, 0))),
        compiler_params=_cp(("parallel",), VMEM_LIMIT),
        name="router",
    )(t, wcat_b, bcat)


def _rank_kernel(e_ref, rank_ref, cnt_ref, carry, *, ne):
    i = pl.program_id(0)
    tt = e_ref.shape[1]

    @pl.when(i == 0)
    def _():
        carry[...] = jnp.zeros_like(carry)

    e = jnp.concatenate([e_ref[0:1, :], e_ref[1:2, :]], axis=1)
    rows = lax.broadcasted_iota(I32, (ne, 2 * tt), 0)
    onehot = jnp.where(rows == e, 1.0, 0.0)
    rr = lax.broadcasted_iota(I32, (2 * tt, 2 * tt), 0)
    cc = lax.broadcasted_iota(I32, (2 * tt, 2 * tt), 1)
    upper = jnp.where(rr < cc, 1.0, 0.0).astype(BF16)
    before = jnp.dot(onehot.astype(BF16), upper, preferred_element_type=F32)
    base = jnp.tile(carry[...], (1, 2 * tt // LANES))
    rank = jnp.sum(onehot * (before + base), axis=0, keepdims=True).astype(I32)
    row8 = lax.broadcasted_iota(I32, (SUBLANES, tt), 0)
    rank_ref[...] = jnp.where(row8 == 0, rank[:, 0:tt], jnp.where(row8 == 1, rank[:, tt:], 0))
    tot = jnp.sum(onehot, axis=1, keepdims=True)
    carry[...] = carry[...] + jnp.broadcast_to(tot, carry.shape)
    cnt_ref[...] = carry[...]


def _ranks(eidx, *, ne, tt):
    _, ttot = eidx.shape
    tt = min(tt, ttot)
    return pl.pallas_call(
        functools.partial(_rank_kernel, ne=ne),
        out_shape=(jax.ShapeDtypeStruct((SUBLANES, ttot), I32), jax.ShapeDtypeStruct((ne, LANES), F32)),
        grid=(ttot // tt,),
        in_specs=[pl.BlockSpec((SUBLANES, tt), lambda i: (0, i))],
        out_specs=(pl.BlockSpec((SUBLANES, tt), lambda i: (0, i)),
                   pl.BlockSpec((ne, LANES), lambda i: (0, 0))),
        scratch_shapes=[pltpu.VMEM((ne, LANES), F32)],
        compiler_params=_cp(("arbitrary",)),
        name="expert_ranks",
    )(eidx)


def _dest_kernel(ps_ref, e_ref, rank_ref, d_ref, *, ne):
    e = e_ref[...]
    acc = jnp.zeros(e.shape, I32)
    for k in range(ne):
        acc = jnp.where(e == k, ps_ref[k], acc)
    d_ref[...] = acc + rank_ref[...]


def _dest_rows(pstart, eidx, rank, *, ne, tt):
    _, ttot = eidx.shape
    tt = min(tt, ttot)
    blk = pl.BlockSpec((SUBLANES, tt), lambda i: (0, i))
    return pl.pallas_call(
        functools.partial(_dest_kernel, ne=ne),
        out_shape=jax.ShapeDtypeStruct((SUBLANES, ttot), I32),
        grid=(ttot // tt,),
        in_specs=[pl.BlockSpec(memory_space=pltpu.SMEM), blk, blk],
        out_specs=blk,
        compiler_params=_cp(("parallel",)),
        name="dest_rows",
    )(pstart, eidx, rank)


def _dispatch_kernel(dest_ref, t_ref, xs_in_ref, xs_ref, sbuf, sems, *, tm, ttot, nsteps):
    del xs_in_ref
    i = pl.program_id(0)
    slot = i % 2

    def row_copy(sl, r, dst):
        return pltpu.make_async_copy(sbuf.at[sl, pl.ds(r, 1)], xs_ref.at[pl.ds(dst, 1)], sems.at[sl])

    def drain(sl):
        def w(r, c):
            row_copy(sl, 0, 0).wait()
            return c
        lax.fori_loop(0, 2 * tm, w, 0)

    @pl.when(i >= 2)
    def _():
        drain(slot)

    sbuf[slot] = t_ref[...]
    base = i * tm

    def issue(r, c):
        row_copy(slot, r, dest_ref[base + r]).start()
        row_copy(slot, r, dest_ref[ttot + base + r]).start()
        return c
    lax.fori_loop(0, tm, issue, 0)

    @pl.when(i == nsteps - 1)
    def _():
        if nsteps >= 2:
            drain(1 - slot)
        drain(slot)


def _dispatch(dest_flat, t, xs_init, *, tm):
    ttot, d = t.shape
    tm = min(tm, ttot)
    nsteps = ttot // tm
    grid_spec = pltpu.PrefetchScalarGridSpec(
        num_scalar_prefetch=1,
        grid=(nsteps,),
        in_specs=[pl.BlockSpec((tm, d), lambda i, dest: (i, 0)),
                  pl.BlockSpec(memory_space=pl.ANY)],
        out_specs=pl.BlockSpec(memory_space=pl.ANY),
        scratch_shapes=[pltpu.VMEM((2, tm, d), t.dtype), pltpu.SemaphoreType.DMA((2,))],
    )
    return pl.pallas_call(
        functools.partial(_dispatch_kernel, tm=tm, ttot=ttot, nsteps=nsteps),
        out_shape=jax.ShapeDtypeStruct(xs_init.shape, xs_init.dtype),
        grid_spec=grid_spec,
        input_output_aliases={2: 0},
        compiler_params=_cp(("arbitrary",), VMEM_LIMIT),
        name="moe_dispatch",
    )(dest_flat, t, xs_init)


def _moe_kernel(be_ref, nu_ref, xs_ref, w1_ref, w3_ref, w2_ref, y_ref, w13b, w2b, *, ff):
    i = pl.program_id(0)

    @pl.when(i < nu_ref[0])
    def _():
        prev = be_ref[jnp.maximum(i - 1, 0)]

        @pl.when((i == 0) | (be_ref[i] != prev))
        def _():
            w13b[:, 0:ff] = w1_ref[...].astype(BF16)
            w13b[:, ff:2 * ff] = w3_ref[...].astype(BF16)
            w2b[...] = w2_ref[...].astype(BF16)

        x = xs_ref[...].astype(BF16)
        h = jnp.dot(x, w13b[...], preferred_element_type=F32)
        hid = (_silu(h[:, 0:ff]) * h[:, ff:2 * ff]).astype(BF16)
        y_ref[...] = jnp.dot(hid, w2b[...], preferred_element_type=F32)

    @pl.when(i >= nu_ref[0])
    def _():
        y_ref[...] = jnp.zeros_like(y_ref)


def _moe_experts(block_e, n_used, xs, w1, w3, w2):
    nrows, d = xs.shape
    ff = w1.shape[2]
    nb = nrows // MOE_ROWS
    grid_spec = pltpu.PrefetchScalarGridSpec(
        num_scalar_prefetch=2,
        grid=(nb,),
        in_specs=[pl.BlockSpec((MOE_ROWS, d), lambda i, be, nu: (i, 0)),
                  pl.BlockSpec((None, d, ff), lambda i, be, nu: (be[i], 0, 0)),
                  pl.BlockSpec((None, d, ff), lambda i, be, nu: (be[i], 0, 0)),
                  pl.BlockSpec((None, ff, d), lambda i, be, nu: (be[i], 0, 0))],
        out_specs=pl.BlockSpec((MOE_ROWS, d), lambda i, be, nu: (i, 0)),
        scratch_shapes=[pltpu.VMEM((d, 2 * ff), BF16), pltpu.VMEM((ff, d), BF16)],
    )
    return pl.pallas_call(
        functools.partial(_moe_kernel, ff=ff),
        out_shape=jax.ShapeDtypeStruct((nrows, d), F32),
        grid_spec=grid_spec,
        compiler_params=_cp(("arbitrary",), VMEM_LIMIT),
        name="moe_experts",
    )(block_e, n_used, xs, w1, w3, w2)


def _combine_kernel(dest_ref, yb_ref, x_ref, wt_ref, g2_ref, lng_ref, lnb_ref, o_ref, gbuf, sems,
                    *, tm, ttot, nsteps, alpha):
    i = pl.program_id(0)
    slot = i % 2

    def row_copy(sl, k, r, src):
        return pltpu.make_async_copy(yb_ref.at[pl.ds(src, 1)], gbuf.at[sl, k, pl.ds(r, 1)], sems.at[sl])

    def issue(step, sl):
        base = step * tm

        def body(r, c):
            row_copy(sl, 0, r, dest_ref[base + r]).start()
            row_copy(sl, 1, r, dest_ref[ttot + base + r]).start()
            return c
        lax.fori_loop(0, tm, body, 0)

    @pl.when(i == 0)
    def _():
        issue(0, 0)

    @pl.when(i + 1 < nsteps)
    def _():
        issue(i + 1, 1 - slot)

    def w(r, c):
        row_copy(slot, 0, 0, 0).wait()
        return c
    lax.fori_loop(0, 2 * tm, w, 0)

    wt = wt_ref[...]
    m = wt[:, 0:1] * gbuf[slot, 0] + wt[:, 1:2] * gbuf[slot, 1]
    o_ref[...] = _ln(alpha * x_ref[...] + g2_ref[...] * m) * lng_ref[...] + lnb_ref[...]


def _combine(dest_flat, yb, x, wt, g2, lng, lnb, *, n_per_batch, alpha, tm):
    ttot, d = x.shape
    tm = min(tm, n_per_batch)
    nsteps = ttot // tm
    per = n_per_batch // tm
    grid_spec = pltpu.PrefetchScalarGridSpec(
        num_scalar_prefetch=1,
        grid=(nsteps,),
        in_specs=[pl.BlockSpec(memory_space=pl.ANY),
                  pl.BlockSpec((tm, d), lambda i, dest: (i, 0)),
                  pl.BlockSpec((tm, LANES), lambda i, dest: (i, 0)),
                  pl.BlockSpec((None, 1, d), lambda i, dest: (i // per, 0, 0)),
                  pl.BlockSpec((1, d), lambda i, dest: (0, 0)),
                  pl.BlockSpec((1, d), lambda i, dest: (0, 0))],
        out_specs=pl.BlockSpec((tm, d), lambda i, dest: (i, 0)),
        scratch_shapes=[pltpu.VMEM((2, 2, tm, d), yb.dtype), pltpu.SemaphoreType.DMA((2,))],
    )
    return pl.pallas_call(
        functools.partial(_combine_kernel, tm=tm, ttot=ttot, nsteps=nsteps, alpha=alpha),
        out_shape=jax.ShapeDtypeStruct((ttot, d), F32),
        grid_spec=grid_spec,
        compiler_params=_cp(("arbitrary",), VMEM_LIMIT),
        name="moe_combine",
    )(dest_flat, yb, x, wt, g2, lng.reshape(1, d), lnb.reshape(1, d))


def _hier_moe_layer(x1, t, g2, lng, lnb, wg, bg, wf, bf, w1, w3, w2, *, alpha):
    b, n, d = x1.shape
    ttot = b * n
    ng = wg.shape[1]
    ne = wf.shape[1]
    epg = ne // ng
    tflat = t.reshape(ttot, d)
    wcat = jnp.zeros((d, LANES), F32).at[:, 0:ng].set(wg).at[:, SUBLANES:SUBLANES + ne].set(wf).astype(BF16)
    bcat = jnp.zeros((1, LANES), F32).at[0, 0:ng].set(bg).at[0, SUBLANES:SUBLANES + ne].set(bf)
    eidx, wt = _router(tflat, wcat, bcat, ng=ng, epg=epg, tm=512)
    rank, cnt = _ranks(eidx, ne=ne, tt=256)
    counts = cnt[:, 0].astype(I32)
    padded = ((counts + MOE_ROWS - 1) // MOE_ROWS) * MOE_ROWS
    pend = jnp.cumsum(padded)
    pstart = (pend - padded).astype(I32)
    nb = (2 * ttot) // MOE_ROWS + ne
    block_start = jnp.arange(nb, dtype=I32) * MOE_ROWS
    block_e = jnp.minimum(jnp.searchsorted(pend, block_start, side='right'), ne - 1).astype(I32)
    n_used = (pend[-1] // MOE_ROWS).astype(I32).reshape(1)
    dest = _dest_rows(pstart, eidx, rank, ne=ne, tt=2048)
    dest_flat = dest[0:2].reshape(2 * ttot)
    xs = _dispatch(dest_flat, tflat, jnp.zeros((nb * MOE_ROWS, d), F32), tm=256)
    yb = _moe_experts(block_e, n_used, xs, w1, w3, w2)
    out = _combine(dest_flat, yb, x1.reshape(ttot, d), wt, g2, lng, lnb, n_per_batch=n, alpha=alpha, tm=256)
    return out.reshape(b, n, d)


def kernel(x, c, ctx, c_ctx, ada_w, ada_b, ln_g, ln_b, ev_w_in, ev_conv_w, ev_conv_b, ev_gate_a_w,
           ev_gate_a_b, ev_gate_x_w, ev_gate_x_b, ev_lru_lambda, ev_da_lambda, ev_da_subln, ev_w_out,
           od_w_out, od_b_out, moe_wg, moe_bg, moe_wf, moe_bf, moe_w1, moe_w3, moe_w2):
    bsz, n_lat, d = x.shape
    depth = ada_w.shape[0]
    alpha = (2.0 * depth) ** 0.25
    lw = ev_conv_w.shape[-1]
    hd = ev_da_lambda.shape[-1]
    vd = ev_da_subln.shape[-1]
    aw = (ev_w_in.shape[-1] - 2 * lw) // 3
    heads = aw // vd
    fnet_groups = 4

    rows = ((bsz + 1 + SUBLANES - 1) // SUBLANES) * SUBLANES
    cond = jnp.zeros((rows, d), F32).at[0:bsz].set(c).at[bsz].set(c_ctx)
    ada = _ada_terms(cond, ada_w, ada_b).reshape(depth, rows, 6, d)

    def lat_term(l, k):
        return ada[l, 0:bsz, k, :].reshape(bsz, 1, d)

    def ctx_term(l, k):
        return jnp.broadcast_to(ada[l, bsz, k, :].reshape(1, 1, d), (bsz, 1, d))

    for l in range(depth):
        ctx_live = any(m % 2 == 0 for m in range(l + 1, depth))
        assert not ctx_live, "context stream update is only needed for depth > 2"
        sh1, sc1, g1, sh2, sc2, g2 = [lat_term(l, k) for k in range(6)]
        if l % 2 == 0:
            e = l // 2
            lam_init = 0.8 - 0.6 * math.exp(-0.3 * l)
            w_in_b = ev_w_in[e].astype(BF16)
            cos_t, sin_t = _rope_tables(n_lat, hd)
            qscale = hd ** -0.5
            g_l, xr_l, q_l, k_l, v_l = _project_even(x, sh1, sc1, w_in_b, cos_t, sin_t, rope=True,
                                                     lw=lw, aw=aw, qscale=qscale, tm=512)
            n_ctx = ctx.shape[1]
            _, xr_c, _, k_c, v_c = _project_even(ctx, ctx_term(l, 0), ctx_term(l, 1), w_in_b,
                                                 cos_t[0:n_ctx], sin_t[0:n_ctx], rope=False,
                                                 lw=lw, aw=aw, qscale=qscale, tm=256)
            o_l = _diff_attention(q_l, k_c, k_l, v_c, v_l, ev_da_lambda[e], ev_da_subln[e],
                                  heads=heads, hd=hd, lam_init=lam_init, tq=512)
            gwid = 256
            y = None
            for dirn, rev in ((0, False), (1, True)):
                wa_bd = _block_diag_groups(ev_gate_a_w[e, dirn], gwid)
                wx_bd = _block_diag_groups(ev_gate_x_w[e, dirn], gwid)
                common = (ev_conv_w[e], ev_conv_b[e], wa_bd, ev_gate_a_b[e, dirn], wx_bd,
                          ev_gate_x_b[e, dirn], ev_lru_lambda[e, dirn])
                h_zero = jnp.zeros((bsz, 1, lw), F32)
                _, h_fin = _rglru_dir(xr_c, None, *common, h_zero, reverse=rev, tn=256)
                y, _ = _rglru_dir(xr_l, y, *common, h_fin, reverse=rev, tn=256)
            x1, t = _outproj_even(y, g_l, o_l, x, ev_w_out[e].astype(BF16), g1, ln_g[l, 0], ln_b[l, 0],
                                  sh2, sc2, alpha=alpha, tm=512)
        else:
            o = l // 2
            gd = d // fnet_groups
            tc, ts, tab = _dft_tables(n_lat, gd)
            zr, zi = _chan_dft(x, sh1, sc1, tab, groups=fnet_groups, tm=512)
            wv = _tok_dft(tc, ts, zr, zi, tk=256)
            norm = 1.0 / math.sqrt(float(n_lat * gd))
            x1, t = _outproj_odd(wv, x, od_w_out[o].astype(BF16), od_b_out[o], g1, ln_g[l, 0], ln_b[l, 0],
                                 sh2, sc2, alpha=alpha, norm=norm, tm=512)
        x = _hier_moe_layer(x1, t, g2, ln_g[l, 1], ln_b[l, 1], moe_wg[l], moe_bg[l], moe_wf[l], moe_bf[l],
                            moe_w1[l], moe_w3[l], moe_w2[l], alpha=alpha)
    return x
```

```python
import functools
import math

import numpy as np
import jax
import jax.numpy as jnp
from jax import lax
from jax.experimental import pallas as pl
from jax.experimental.pallas import tpu as pltpu

F32 = jnp.float32
BF16 = jnp.bfloat16
I32 = jnp.int32

LN_EPS = 1e-6
LRU_C = 8.0
ROPE_BASE = 10000.0
GRID_W = 64
CONV_W = 4
LANES = 128
SUBLANES = 8
MOE_ROWS = 256
MOE_TILE = 256
MOE_CHUNK = 16
VMEM_LIMIT = 56 * 1024 * 1024


def _cp(sem, vmem=None):
    return pltpu.CompilerParams(dimension_semantics=sem, vmem_limit_bytes=vmem)


def _ln(x):
    mu = jnp.mean(x, axis=-1, keepdims=True)
    xc = x - mu
    var = jnp.mean(xc * xc, axis=-1, keepdims=True)
    return xc * lax.rsqrt(var + LN_EPS)


def _silu(x):
    return x * jax.nn.sigmoid(x)


def _tile_row(r, mult):
    if isinstance(r, int):
        return r * SUBLANES
    return pl.multiple_of(r * SUBLANES, mult)


def _rows_from_tiles(ref, nrows):
    return jnp.concatenate([ref[pl.ds(k, nrows, stride=SUBLANES), :] for k in range(SUBLANES)], axis=1)


def _rows_to_tiles(ref, val):
    nrows = val.shape[0]
    for k in range(SUBLANES):
        ref[pl.ds(k, nrows, stride=SUBLANES), :] = val[:, k * LANES:(k + 1) * LANES]


def _ada_kernel(c_ref, w_ref, b_ref, o_ref):
    s = _silu(c_ref[...]).astype(BF16)
    o_ref[...] = jnp.dot(s, w_ref[...].astype(BF16), preferred_element_type=F32) + b_ref[...]


def _ada_terms(cond, ada_w, ada_b):
    nl, d, d6 = ada_w.shape
    r = cond.shape[0]
    tn = 1024
    return pl.pallas_call(
        _ada_kernel,
        out_shape=jax.ShapeDtypeStruct((nl, r, d6), F32),
        grid=(nl, d6 // tn),
        in_specs=[pl.BlockSpec((r, d), lambda l, j: (0, 0)),
                  pl.BlockSpec((None, d, tn), lambda l, j: (l, 0, j)),
                  pl.BlockSpec((None, 1, tn), lambda l, j: (l, 0, j))],
        out_specs=pl.BlockSpec((None, r, tn), lambda l, j: (l, 0, j)),
        compiler_params=_cp(("parallel", "parallel")),
        name="ada_terms",
    )(cond, ada_w, ada_b.reshape(nl, 1, d6))


def _rope_apply(x, cos, sin_signed):
    tm = x.shape[0]
    lane = lax.broadcasted_iota(I32, (tm, LANES), 1)
    first_half = (lane % 32) < 16
    outs = []
    for j in range(x.shape[1] // LANES):
        xh = x[:, j * LANES:(j + 1) * LANES]
        partner = jnp.where(first_half, pltpu.roll(xh, LANES - 16, 1), pltpu.roll(xh, 16, 1))
        outs.append(xh * cos + partner * sin_signed)
    return jnp.concatenate(outs, axis=1)


def _proj_kernel(x_ref, sh_ref, sc_ref, w_ref, cos_ref, sin_ref,
                 g_ref, xr_ref, q_ref, k_ref, v_ref, *, rope, lw, aw, qscale):
    h = _ln(x_ref[...]) * (1.0 + sc_ref[...]) + sh_ref[...]
    hb = h.astype(BF16)

    def mm(c0, c1):
        return jnp.dot(hb, w_ref[:, c0:c1], preferred_element_type=F32)

    g_ref[...] = mm(0, lw).astype(BF16)
    xr_ref[...] = mm(lw, 2 * lw)
    q = mm(2 * lw, 2 * lw + aw)
    k = mm(2 * lw + aw, 2 * lw + 2 * aw)
    if rope:
        q = _rope_apply(q, cos_ref[...], sin_ref[...])
        k = _rope_apply(k, cos_ref[...], sin_ref[...])
    q_ref[...] = (q * qscale).astype(BF16)
    k_ref[...] = k.astype(BF16)
    v_ref[...] = mm(2 * lw + 2 * aw, 2 * lw + 3 * aw).astype(BF16)


def _project_even(x, shift, scale, w_in_b, cos_t, sin_t, *, rope, lw, aw, qscale, tm):
    b, n, d = x.shape
    tm = min(tm, n)
    nin = w_in_b.shape[1]
    tok = lambda bi, i: (bi, i, 0)
    per_b = lambda bi, i: (bi, 0, 0)
    outs = (jax.ShapeDtypeStruct((b, n, lw), BF16), jax.ShapeDtypeStruct((b, n, lw), F32),
            jax.ShapeDtypeStruct((b, n, aw), BF16), jax.ShapeDtypeStruct((b, n, aw), BF16),
            jax.ShapeDtypeStruct((b, n, aw), BF16))
    return pl.pallas_call(
        functools.partial(_proj_kernel, rope=rope, lw=lw, aw=aw, qscale=qscale),
        out_shape=outs,
        grid=(b, n // tm),
        in_specs=[pl.BlockSpec((None, tm, d), tok),
                  pl.BlockSpec((None, 1, d), per_b),
                  pl.BlockSpec((None, 1, d), per_b),
                  pl.BlockSpec((d, nin), lambda bi, i: (0, 0)),
                  pl.BlockSpec((tm, LANES), lambda bi, i: (i, 0)),
                  pl.BlockSpec((tm, LANES), lambda bi, i: (i, 0))],
        out_specs=(pl.BlockSpec((None, tm, lw), tok), pl.BlockSpec((None, tm, lw), tok),
                   pl.BlockSpec((None, tm, aw), tok), pl.BlockSpec((None, tm, aw), tok),
                   pl.BlockSpec((None, tm, aw), tok)),
        compiler_params=_cp(("parallel", "parallel"), VMEM_LIMIT),
        name="proj_even_rope" if rope else "proj_even_ctx",
    )(x, shift, scale, w_in_b, cos_t, sin_t)


def _rope_tables(n_tok, head_dim):
    t = jnp.arange(n_tok)
    row = (t // GRID_W).astype(F32)
    col = (t % GRID_W).astype(F32)
    nf = head_dim // 4
    freqs = ROPE_BASE ** (-jnp.arange(nf, dtype=F32) / nf)
    lane = np.arange(LANES)
    within = lane % head_dim
    axis = within // (2 * nf)
    half = (within % (2 * nf)) // nf
    f = within % nf
    pos = jnp.where(jnp.asarray(axis)[None, :] == 0, row[:, None], col[:, None])
    ang = pos * freqs[jnp.asarray(f)][None, :]
    sign = jnp.asarray(np.where(half == 0, -1.0, 1.0), F32)[None, :]
    return jnp.cos(ang).astype(F32), (jnp.sin(ang) * sign).astype(F32)


def _attn_kernel(q_ref, kc_ref, kl_ref, vc_ref, vl_ref, dl_ref, gain_ref, o_ref, kbuf, vbuf,
                 *, nc, nl, hd, lam_init, rows):
    @pl.when(pl.program_id(2) == 0)
    def _():
        kbuf[0:nc, :] = kc_ref[...]
        kbuf[nc:nc + nl, :] = kl_ref[...]
        vbuf[0:nc, :] = vc_ref[...]
        vbuf[nc:nc + nl, :] = vl_ref[...]

    lf = dl_ref[...]
    lam = (jnp.exp(jnp.sum(lf[0:1] * lf[1:2], axis=1, keepdims=True))
           - jnp.exp(jnp.sum(lf[2:3] * lf[3:4], axis=1, keepdims=True)) + lam_init)
    gain = gain_ref[...] * (1.0 - lam_init)
    tq = q_ref.shape[0]
    lane = lax.broadcasted_iota(I32, (rows, 2 * hd), 1)
    nt = (((1,), (1,)), ((), ()))

    def body(j, carry):
        r0 = pl.multiple_of(j * rows, rows)
        q = q_ref[pl.ds(r0, rows), :]
        zero = jnp.zeros_like(q)
        q1 = jnp.where(lane < hd, q, zero)
        q2 = jnp.where(lane >= hd, q, zero)
        kk = kbuf[...]
        s1 = lax.dot_general(q1, kk, nt, preferred_element_type=F32)
        s2 = lax.dot_general(q2, kk, nt, preferred_element_type=F32)
        e1 = jnp.exp(s1 - jnp.max(s1, axis=1, keepdims=True))
        e2 = jnp.exp(s2 - jnp.max(s2, axis=1, keepdims=True))
        c1 = 1.0 / jnp.sum(e1, axis=1, keepdims=True)
        c2 = lam / jnp.sum(e2, axis=1, keepdims=True)
        a = (e1 * c1 - e2 * c2).astype(BF16)
        o = jnp.dot(a, vbuf[...], preferred_element_type=F32)
        o = o * lax.rsqrt(jnp.mean(o * o, axis=1, keepdims=True) + LN_EPS) * gain
        o_ref[pl.ds(r0, rows), :] = o.astype(BF16)
        return carry

    lax.fori_loop(0, tq // rows, body, 0)


def _diff_attention(q, k_ctx, k_lat, v_ctx, v_lat, da_lambda, subln, *, heads, hd, lam_init, tq):
    b, n, aw = q.shape
    nc = k_ctx.shape[1]
    vd = aw // heads
    tq = min(tq, n)
    rows = min(128, tq)
    blk_q = pl.BlockSpec((None, tq, vd), lambda bi, h, i: (bi, i, h))
    blk_c = pl.BlockSpec((None, nc, vd), lambda bi, h, i: (bi, 0, h))
    blk_l = pl.BlockSpec((None, n, vd), lambda bi, h, i: (bi, 0, h))
    return pl.pallas_call(
        functools.partial(_attn_kernel, nc=nc, nl=n, hd=hd, lam_init=lam_init, rows=rows),
        out_shape=jax.ShapeDtypeStruct((b, n, aw), BF16),
        grid=(b, heads, n // tq),
        in_specs=[blk_q, blk_c, blk_l, blk_c, blk_l,
                  pl.BlockSpec(da_lambda.shape, lambda bi, h, i: (0, 0)),
                  pl.BlockSpec((1, vd), lambda bi, h, i: (0, 0))],
        out_specs=blk_q,
        scratch_shapes=[pltpu.VMEM((nc + n, vd), BF16), pltpu.VMEM((nc + n, vd), BF16)],
        compiler_params=_cp(("parallel", "parallel", "arbitrary"), VMEM_LIMIT),
        name="diff_attention",
    )(q, k_ctx, k_lat, v_ctx, v_lat, da_lambda, subln.reshape(1, vd))


def _lru_kernel(*refs, reverse, add_prev, nt, groups):
    if add_prev:
        (xp_ref, xc_ref, xn_ref, yprev_ref, cw_ref, cb_ref, wa_ref, ba_ref, wx_ref, bx_ref,
         lam_ref, h0_ref, y_ref, hf_ref, carry) = refs
    else:
        (xp_ref, xc_ref, xn_ref, cw_ref, cb_ref, wa_ref, ba_ref, wx_ref, bx_ref,
         lam_ref, h0_ref, y_ref, hf_ref, carry) = refs
        yprev_ref = None
    i = pl.program_id(1)
    ti = (nt - 1 - i) if reverse else i
    tn, w = xc_ref.shape

    @pl.when(i == 0)
    def _():
        carry[...] = h0_ref[...]

    prev = jnp.where(ti > 0, xp_ref[...], 0.0)
    nxt = jnp.where(ti < nt - 1, xn_ref[...], 0.0)
    ext = jnp.concatenate([prev, xc_ref[...], nxt], axis=0)
    left = CONV_W // 2
    xc = cb_ref[...]
    for k in range(CONV_W):
        off = SUBLANES - left + k
        xc = xc + ext[off:off + tn, :] * cw_ref[k:k + 1, :]

    xb = xc.astype(BF16)
    gw = w // groups

    def gate(w_ref, b_ref):
        parts = [jnp.dot(xb[:, g * gw:(g + 1) * gw], w_ref[g], preferred_element_type=F32)
                 for g in range(groups)]
        return jax.nn.sigmoid(jnp.concatenate(parts, axis=1) + b_ref[...])

    r = gate(wa_ref, ba_ref)
    ig = gate(wx_ref, bx_ref)
    log_a = (-LRU_C * jax.nn.softplus(-lam_ref[...])) * r
    a = jnp.exp(log_a)
    bcoef = jnp.sqrt(-jnp.tanh(log_a) * (a * a + 1.0)) * ig * xc

    row = lax.broadcasted_iota(I32, (tn, w), 0)
    d = 1
    while d < tn:
        if reverse:
            a_sh = pltpu.roll(a, tn - d, 0)
            b_sh = pltpu.roll(bcoef, tn - d, 0)
            live = row < tn - d
        else:
            a_sh = pltpu.roll(a, d, 0)
            b_sh = pltpu.roll(bcoef, d, 0)
            live = row >= d
        bcoef = jnp.where(live, a * b_sh + bcoef, bcoef)
        a = jnp.where(live, a * a_sh, a)
        d *= 2
    h = a * carry[...] + bcoef
    carry[...] = h[0:1, :] if reverse else h[tn - 1:tn, :]
    hf_ref[...] = carry[...]
    if add_prev:
        y_ref[...] = yprev_ref[...] + h
    else:
        y_ref[...] = h


def _rglru_dir(xr, y_prev, conv_w, conv_b, wa_bd, ba, wx_bd, bx, lam, h0, *, reverse, tn):
    b, n, w = xr.shape
    tn = min(tn, n)
    nt = n // tn
    groups = wa_bd.shape[0]
    nb8 = n // SUBLANES
    per8 = tn // SUBLANES

    def tmap(i):
        return (nt - 1 - i) if reverse else i

    cur = pl.BlockSpec((None, tn, w), lambda bi, i: (bi, tmap(i), 0))
    halo_p = pl.BlockSpec((None, SUBLANES, w), lambda bi, i: (bi, jnp.maximum(tmap(i) * per8 - 1, 0), 0))
    halo_n = pl.BlockSpec((None, SUBLANES, w), lambda bi, i: (bi, jnp.minimum((tmap(i) + 1) * per8, nb8 - 1), 0))
    row_w = pl.BlockSpec((1, w), lambda bi, i: (0, 0))
    per_b = pl.BlockSpec((None, 1, w), lambda bi, i: (bi, 0, 0))
    gate_w = pl.BlockSpec(wa_bd.shape, lambda bi, i: (0, 0, 0))
    add_prev = y_prev is not None
    in_specs = [halo_p, cur, halo_n] + ([cur] if add_prev else []) + [
        pl.BlockSpec((CONV_W, w), lambda bi, i: (0, 0)), row_w, gate_w, row_w, gate_w, row_w, row_w, per_b]
    args = [xr, xr, xr] + ([y_prev] if add_prev else []) + [
        conv_w, conv_b.reshape(1, w), wa_bd, ba.reshape(1, w), wx_bd, bx.reshape(1, w),
        lam.reshape(1, w), h0]
    return pl.pallas_call(
        functools.partial(_lru_kernel, reverse=reverse, add_prev=add_prev, nt=nt, groups=groups),
        out_shape=(jax.ShapeDtypeStruct((b, n, w), F32), jax.ShapeDtypeStruct((b, 1, w), F32)),
        grid=(b, nt),
        in_specs=in_specs,
        out_specs=(cur, per_b),
        scratch_shapes=[pltpu.VMEM((1, w), F32)],
        compiler_params=_cp(("parallel", "arbitrary"), VMEM_LIMIT),
        name="rglru_rev" if reverse else "rglru_fwd",
    )(*args)


def _block_diag_groups(wh, group_width):
    heads, blk, _ = wh.shape
    per = group_width // blk
    groups = heads // per
    whg = wh.reshape(groups, per, blk, blk)
    eye = jnp.eye(per, dtype=wh.dtype)
    bd = jnp.einsum('gpij,pq->gpiqj', whg, eye).reshape(groups, group_width, group_width)
    return bd.astype(BF16)


def _residual_ln_mod(x, y, g1, lng, lnb, sh2, sc2, alpha):
    x1 = _ln(alpha * x + g1 * y) * lng + lnb
    t = _ln(x1) * (1.0 + sc2) + sh2
    return x1, t


def _outproj_even_kernel(r_ref, g_ref, o_ref, x_ref, w_ref, g1_ref, lng_ref, lnb_ref, sh2_ref, sc2_ref,
                         x1_ref, t_ref, *, lw, alpha):
    z = (r_ref[...] * jax.nn.gelu(g_ref[...].astype(F32))).astype(BF16)
    y = (jnp.dot(z, w_ref[0:lw, :], preferred_element_type=F32)
         + jnp.dot(o_ref[...], w_ref[lw:, :], preferred_element_type=F32))
    x1, t = _residual_ln_mod(x_ref[...], y, g1_ref[...], lng_ref[...], lnb_ref[...],
                             sh2_ref[...], sc2_ref[...], alpha)
    x1_ref[...] = x1
    t_ref[...] = t


def _outproj_even(r, g, o, x, w_out_b, g1, lng, lnb, sh2, sc2, *, alpha, tm):
    b, n, d = x.shape
    lw = r.shape[2]
    aw = o.shape[2]
    tm = min(tm, n)
    tok = lambda bi, i: (bi, i, 0)
    per_b = pl.BlockSpec((None, 1, d), lambda bi, i: (bi, 0, 0))
    row = pl.BlockSpec((1, d), lambda bi, i: (0, 0))
    return pl.pallas_call(
        functools.partial(_outproj_even_kernel, lw=lw, alpha=alpha),
        out_shape=(jax.ShapeDtypeStruct((b, n, d), F32), jax.ShapeDtypeStruct((b, n, d), F32)),
        grid=(b, n // tm),
        in_specs=[pl.BlockSpec((None, tm, lw), tok), pl.BlockSpec((None, tm, lw), tok),
                  pl.BlockSpec((None, tm, aw), tok), pl.BlockSpec((None, tm, d), tok),
                  pl.BlockSpec(w_out_b.shape, lambda bi, i: (0, 0)),
                  per_b, row, row, per_b, per_b],
        out_specs=(pl.BlockSpec((None, tm, d), tok), pl.BlockSpec((None, tm, d), tok)),
        compiler_params=_cp(("parallel", "parallel"), VMEM_LIMIT),
        name="outproj_even",
    )(r, g, o, x, w_out_b, g1, lng.reshape(1, d), lnb.reshape(1, d), sh2, sc2)


def _chan_dft_kernel(x_ref, sh_ref, sc_ref, tab_ref, zr_ref, zi_ref, *, groups, gd):
    h = (_ln(x_ref[...]) * (1.0 + sc_ref[...]) + sh_ref[...]).astype(BF16)
    for g in range(groups):
        z = jnp.dot(h[:, g * gd:(g + 1) * gd], tab_ref[...], preferred_element_type=F32)
        zr_ref[:, g * gd:(g + 1) * gd] = z[:, 0:gd].astype(BF16)
        zi_ref[:, g * gd:(g + 1) * gd] = z[:, gd:2 * gd].astype(BF16)


def _chan_dft(x, shift, scale, tab, *, groups, tm):
    b, n, d = x.shape
    gd = d // groups
    tm = min(tm, n)
    tok = lambda bi, i: (bi, i, 0)
    per_b = pl.BlockSpec((None, 1, d), lambda bi, i: (bi, 0, 0))
    return pl.pallas_call(
        functools.partial(_chan_dft_kernel, groups=groups, gd=gd),
        out_shape=(jax.ShapeDtypeStruct((b, n, d), BF16), jax.ShapeDtypeStruct((b, n, d), BF16)),
        grid=(b, n // tm),
        in_specs=[pl.BlockSpec((None, tm, d), tok), per_b, per_b,
                  pl.BlockSpec(tab.shape, lambda bi, i: (0, 0))],
        out_specs=(pl.BlockSpec((None, tm, d), tok), pl.BlockSpec((None, tm, d), tok)),
        compiler_params=_cp(("parallel", "parallel"), VMEM_LIMIT),
        name="chan_dft",
    )(x, shift, scale, tab)


def _tok_dft_kernel(tc_ref, ts_ref, zr_ref, zi_ref, o_ref):
    acc = jnp.dot(tc_ref[...], zr_ref[...], preferred_element_type=F32)
    acc = acc - jnp.dot(ts_ref[...], zi_ref[...], preferred_element_type=F32)
    o_ref[...] = acc.astype(BF16)


def _tok_dft(tc, ts, zr, zi, *, tk):
    b, n, d = zr.shape
    tk = min(tk, n)
    lhs = pl.BlockSpec((tk, n), lambda bi, j: (j, 0))
    rhs = pl.BlockSpec((None, n, d), lambda bi, j: (bi, 0, 0))
    return pl.pallas_call(
        _tok_dft_kernel,
        out_shape=jax.ShapeDtypeStruct((b, n, d), BF16),
        grid=(b, n // tk),
        in_specs=[lhs, lhs, rhs, rhs],
        out_specs=pl.BlockSpec((None, tk, d), lambda bi, j: (bi, j, 0)),
        compiler_params=_cp(("parallel", "parallel"), VMEM_LIMIT),
        name="tok_dft",
    )(tc, ts, zr, zi)


def _dft_tables(n, gd):
    sub = 64 if n % 64 == 0 else 1
    t = np.arange(n, dtype=np.float64)
    a = np.arange(n // sub, dtype=np.float64)
    bb = np.arange(sub, dtype=np.float64)
    ang_a = 2.0 * np.pi * ((np.outer(a, t) * sub) % n) / n
    ang_b = 2.0 * np.pi * (np.outer(bb, t) % n) / n
    pc, ps = jnp.asarray(np.cos(ang_a), F32)[:, None, :], jnp.asarray(np.sin(ang_a), F32)[:, None, :]
    qc, qs = jnp.asarray(np.cos(ang_b), F32)[None, :, :], jnp.asarray(np.sin(ang_b), F32)[None, :, :]
    tc = (pc * qc - ps * qs).reshape(n, n).astype(BF16)
    ts = (ps * qc + pc * qs).reshape(n, n).astype(BF16)
    c = np.arange(gd, dtype=np.float64)
    ang_c = 2.0 * np.pi * (np.outer(c, c) % gd) / gd
    tab = jnp.asarray(np.concatenate([np.cos(ang_c), np.sin(ang_c)], axis=1), F32).astype(BF16)
    return tc, ts, tab


def _outproj_odd_kernel(wv_ref, x_ref, w_ref, b_ref, g1_ref, lng_ref, lnb_ref, sh2_ref, sc2_ref,
                        x1_ref, t_ref, *, alpha, norm):
    y = jnp.dot(wv_ref[...], w_ref[...], preferred_element_type=F32) * norm + b_ref[...]
    x1, t = _residual_ln_mod(x_ref[...], y, g1_ref[...], lng_ref[...], lnb_ref[...],
                             sh2_ref[...], sc2_ref[...], alpha)
    x1_ref[...] = x1
    t_ref[...] = t


def _outproj_odd(wv, x, w_b, bias, g1, lng, lnb, sh2, sc2, *, alpha, norm, tm):
    b, n, d = x.shape
    tm = min(tm, n)
    tok = lambda bi, i: (bi, i, 0)
    per_b = pl.BlockSpec((None, 1, d), lambda bi, i: (bi, 0, 0))
    row = pl.BlockSpec((1, d), lambda bi, i: (0, 0))
    return pl.pallas_call(
        functools.partial(_outproj_odd_kernel, alpha=alpha, norm=norm),
        out_shape=(jax.ShapeDtypeStruct((b, n, d), F32), jax.ShapeDtypeStruct((b, n, d), F32)),
        grid=(b, n // tm),
        in_specs=[pl.BlockSpec((None, tm, d), tok), pl.BlockSpec((None, tm, d), tok),
                  pl.BlockSpec(w_b.shape, lambda bi, i: (0, 0)), row,
                  per_b, row, row, per_b, per_b],
        out_specs=(pl.BlockSpec((None, tm, d), tok), pl.BlockSpec((None, tm, d), tok)),
        compiler_params=_cp(("parallel", "parallel"), VMEM_LIMIT),
        name="outproj_odd",
    )(wv, x, w_b, bias.reshape(1, d), g1, lng.reshape(1, d), lnb.reshape(1, d), sh2, sc2)


def _route_kernel(t_ref, w_ref, b_ref, up_ref, lp_ref, tokm_ref, cnt_ref, *, ng, epg, chunk):
    tm = t_ref.shape[0]
    ne = ng * epg
    logits = jnp.dot(t_ref[...].astype(BF16), w_ref[...], preferred_element_type=F32) + b_ref[...]
    lt = logits.T
    best = lt[0:1, :]
    bi = jnp.zeros((1, tm), I32)
    for k in range(1, ng):
        gk = lt[k:k + 1, :]
        upd = gk > best
        bi = jnp.where(upd, k, bi)
        best = jnp.where(upd, gk, best)
    den = jnp.zeros((1, tm), F32)
    for k in range(ng):
        den = den + jnp.exp(lt[k:k + 1, :] - best)
    p_g = 1.0 / den
    fsel = lt[SUBLANES:SUBLANES + epg, :]
    for k in range(1, ng):
        fsel = jnp.where(bi == k, lt[SUBLANES + k * epg:SUBLANES + (k + 1) * epg, :], fsel)
    neg = jnp.full((1, tm), -jnp.inf, F32)
    m1, m2 = neg, neg
    i1 = jnp.zeros((1, tm), I32)
    i2 = jnp.zeros((1, tm), I32)
    for j in range(epg):
        v = fsel[j:j + 1, :]
        gt1 = v > m1
        gt2 = v > m2
        m2 = jnp.where(gt1, m1, jnp.where(gt2, v, m2))
        i2 = jnp.where(gt1, i1, jnp.where(gt2, j, i2))
        m1 = jnp.where(gt1, v, m1)
        i1 = jnp.where(gt1, j, i1)
    e21 = jnp.exp(m2 - m1)
    w1 = p_g / (1.0 + e21)
    w2 = p_g * e21 / (1.0 + e21)
    e1 = bi * epg + i1
    e2 = bi * epg + i2

    e = jnp.concatenate([e1, e2], axis=1)
    rows = lax.broadcasted_iota(I32, (ne, 2 * tm), 0)
    onehot = jnp.where(rows == e, 1.0, 0.0)
    before = jnp.dot(onehot.astype(BF16), up_ref[...], preferred_element_type=F32)
    tot = jnp.sum(onehot, axis=1, keepdims=True)
    slots = jnp.floor((tot + (chunk - 1.0)) * (1.0 / chunk)) * chunk
    slots_b = jnp.broadcast_to(slots, (ne, LANES))
    rowe = lax.broadcasted_iota(I32, (ne, LANES), 0)
    incl = slots_b
    d = 1
    while d < ne:
        incl = incl + jnp.where(rowe >= d, pltpu.roll(incl, d, 0), 0.0)
        d *= 2
    seg_off = jnp.tile(incl - slots_b, (1, 2 * tm // LANES))
    lpos = jnp.sum(onehot * (before + seg_off), axis=0, keepdims=True)
    lp0 = lpos[:, 0:tm]
    lp1 = lpos[:, tm:2 * tm]
    row8 = lax.broadcasted_iota(I32, (SUBLANES, tm), 0)
    lp_ref[...] = jnp.where(row8 == 0, lp0, jnp.where(row8 == 1, lp1, 0.0)).astype(I32)
    rowl = lax.broadcasted_iota(I32, (LANES, tm), 0)
    tokm = jnp.where(rowl == 0, w1, jnp.where(rowl == 1, w2, jnp.where(rowl == 2, lp0, jnp.where(rowl == 3, lp1, 0.0))))
    tokm_ref[...] = tokm.T
    cnt_ref[...] = jnp.broadcast_to(tot, (ne, LANES))


def _route(t, wcat_b, bcat, upper, *, ng, epg, tm, chunk):
    tt, d = t.shape
    ne = ng * epg
    nt = tt // tm
    return pl.pallas_call(
        functools.partial(_route_kernel, ng=ng, epg=epg, chunk=chunk),
        out_shape=(jax.ShapeDtypeStruct((SUBLANES, tt), I32), jax.ShapeDtypeStruct((tt, LANES), F32),
                   jax.ShapeDtypeStruct((ne, nt * LANES), F32)),
        grid=(nt,),
        in_specs=[pl.BlockSpec((tm, d), lambda i: (i, 0)),
                  pl.BlockSpec((d, LANES), lambda i: (0, 0)),
                  pl.BlockSpec((1, LANES), lambda i: (0, 0)),
                  pl.BlockSpec(upper.shape, lambda i: (0, 0))],
        out_specs=(pl.BlockSpec((SUBLANES, tm), lambda i: (0, i)),
                   pl.BlockSpec((tm, LANES), lambda i: (i, 0)),
                   pl.BlockSpec((ne, LANES), lambda i: (0, i))),
        compiler_params=_cp(("parallel",), VMEM_LIMIT),
        name="route_sort",
    )(t, wcat_b, bcat, upper)


def _dispatch_kernel(nch_ref, off_ref, dst_ref, tot_ref, t_ref, lp_ref, xs_in_ref, xs_ref, stage, sems,
                     *, ne, chunk, nsteps):
    del xs_in_ref
    i = pl.program_id(0)
    slot = i % 2
    sp = stage.shape[1] // SUBLANES
    tm = t_ref.shape[0]
    crow = chunk * SUBLANES

    lp = lp_ref[...]
    prow = lax.broadcasted_iota(I32, (sp, tm), 0)
    perm = jnp.where(prow == lp[0:1, :], 1.0, jnp.where(prow == lp[1:2, :], 1.0, 0.0)).astype(BF16)
    _rows_to_tiles(stage.at[slot], jnp.dot(perm, t_ref[...].astype(BF16), preferred_element_type=F32))

    def chunk_copy(sl, src, dst):
        return pltpu.make_async_copy(stage.at[sl, pl.ds(_tile_row(src, crow), crow)],
                                     xs_ref.at[pl.ds(_tile_row(dst, SUBLANES), crow)], sems.at[sl])

    def drain(sl, n):
        def w(c, carry):
            chunk_copy(sl, 0, 0).wait()
            return carry
        lax.fori_loop(0, n, w, 0)

    @pl.when(i >= 1)
    def _():
        drain(1 - slot, tot_ref[jnp.maximum(i - 1, 0)])

    for e in range(ne):
        src0 = pl.multiple_of(off_ref[i * ne + e], chunk)
        dst0 = dst_ref[i * ne + e]

        def issue(c, carry, src0=src0, dst0=dst0):
            chunk_copy(slot, src0 + c * chunk, dst0 + c * chunk).start()
            return carry
        lax.fori_loop(0, nch_ref[i * ne + e], issue, 0)

    @pl.when(i == nsteps - 1)
    def _():
        drain(slot, tot_ref[i])


def _dispatch(seg_nch, seg_off, seg_dst, tile_nch, t, lp, xs_init, *, ne, tm, chunk):
    ttot, d = t.shape
    nsteps = ttot // tm
    sp = 2 * tm + ne * chunk
    grid_spec = pltpu.PrefetchScalarGridSpec(
        num_scalar_prefetch=4,
        grid=(nsteps,),
        in_specs=[pl.BlockSpec((tm, d), lambda i, *_: (i, 0)),
                  pl.BlockSpec((SUBLANES, tm), lambda i, *_: (0, i)),
                  pl.BlockSpec(memory_space=pl.ANY)],
        out_specs=pl.BlockSpec(memory_space=pl.ANY),
        scratch_shapes=[pltpu.VMEM((2, sp * SUBLANES, LANES), F32), pltpu.SemaphoreType.DMA((2,))],
    )
    assert d == SUBLANES * LANES
    return pl.pallas_call(
        functools.partial(_dispatch_kernel, ne=ne, chunk=chunk, nsteps=nsteps),
        out_shape=jax.ShapeDtypeStruct(xs_init.shape, xs_init.dtype),
        grid_spec=grid_spec,
        input_output_aliases={6: 0},
        compiler_params=_cp(("arbitrary",), VMEM_LIMIT),
        name="moe_dispatch",
    )(seg_nch, seg_off, seg_dst, tile_nch, t, lp, xs_init)


def _moe_kernel(be_ref, bv_ref, nu_ref, xs_ref, w1_ref, w3_ref, w2_ref, y_ref, w13b, w2b, *, ff):
    i = pl.program_id(0)

    @pl.when(i < nu_ref[0])
    def _():
        prev = be_ref[jnp.maximum(i - 1, 0)]

        @pl.when((i == 0) | (be_ref[i] != prev))
        def _():
            w13b[:, 0:ff] = w1_ref[...].astype(BF16)
            w13b[:, ff:2 * ff] = w3_ref[...].astype(BF16)
            w2b[...] = w2_ref[...].astype(BF16)

        xs = _rows_from_tiles(xs_ref, MOE_ROWS)
        row = lax.broadcasted_iota(I32, xs.shape, 0)
        x = jnp.where(row < bv_ref[i], xs, 0.0).astype(BF16)
        h = jnp.dot(x, w13b[...], preferred_element_type=F32)
        hid = (_silu(h[:, 0:ff]) * h[:, ff:2 * ff]).astype(BF16)
        _rows_to_tiles(y_ref, jnp.dot(hid, w2b[...], preferred_element_type=F32))

    @pl.when(i >= nu_ref[0])
    def _():
        y_ref[...] = jnp.zeros_like(y_ref)


def _moe_experts(block_e, block_valid, n_used, xs, w1, w3, w2, *, layer):
    d = w1.shape[-2]
    ff = w1.shape[-1]
    blk = MOE_ROWS * SUBLANES
    nb = xs.shape[0] // blk
    wmap = lambda i, be, bv, nu: (layer, be[i], 0, 0)
    grid_spec = pltpu.PrefetchScalarGridSpec(
        num_scalar_prefetch=3,
        grid=(nb,),
        in_specs=[pl.BlockSpec((blk, LANES), lambda i, be, bv, nu: (i, 0)),
                  pl.BlockSpec((None, None, d, ff), wmap),
                  pl.BlockSpec((None, None, d, ff), wmap),
                  pl.BlockSpec((None, None, ff, d), wmap)],
        out_specs=pl.BlockSpec((blk, LANES), lambda i, be, bv, nu: (i, 0)),
        scratch_shapes=[pltpu.VMEM((d, 2 * ff), BF16), pltpu.VMEM((ff, d), BF16)],
    )
    return pl.pallas_call(
        functools.partial(_moe_kernel, ff=ff),
        out_shape=jax.ShapeDtypeStruct(xs.shape, F32),
        grid_spec=grid_spec,
        compiler_params=_cp(("arbitrary",), VMEM_LIMIT),
        name="moe_experts",
    )(block_e, block_valid, n_used, xs, w1, w3, w2)


def _combine_kernel(nch_ref, off_ref, src_ref, tot_ref, yb_ref, x_ref, tokm_ref, g2_ref, lng_ref, lnb_ref,
                    o_ref, stage, sems, *, ne, chunk, nsteps, alpha):
    i = pl.program_id(0)
    slot = i % 2
    sp = stage.shape[1] // SUBLANES
    tm = x_ref.shape[0]
    crow = chunk * SUBLANES

    def chunk_copy(sl, src, dst):
        return pltpu.make_async_copy(yb_ref.at[pl.ds(_tile_row(src, SUBLANES), crow)],
                                     stage.at[sl, pl.ds(_tile_row(dst, crow), crow)], sems.at[sl])

    def issue_tile(step, sl):
        for e in range(ne):
            src0 = src_ref[step * ne + e]
            dst0 = pl.multiple_of(off_ref[step * ne + e], chunk)

            def issue(c, carry, src0=src0, dst0=dst0):
                chunk_copy(sl, src0 + c * chunk, dst0 + c * chunk).start()
                return carry
            lax.fori_loop(0, nch_ref[step * ne + e], issue, 0)

    @pl.when(i == 0)
    def _():
        stage[...] = jnp.zeros_like(stage)
        issue_tile(0, 0)

    @pl.when(i + 1 < nsteps)
    def _():
        issue_tile(jnp.minimum(i + 1, nsteps - 1), 1 - slot)

    def w(c, carry):
        chunk_copy(slot, 0, 0).wait()
        return carry
    lax.fori_loop(0, tot_ref[i], w, 0)

    tk = tokm_ref[...]
    pos = lax.broadcasted_iota(I32, (tm, sp), 1).astype(F32)
    st = _rows_from_tiles(stage.at[slot], sp).astype(BF16)
    g0 = jnp.dot(jnp.where(pos == tk[:, 2:3], 1.0, 0.0).astype(BF16), st, preferred_element_type=F32)
    g1 = jnp.dot(jnp.where(pos == tk[:, 3:4], 1.0, 0.0).astype(BF16), st, preferred_element_type=F32)
    m = tk[:, 0:1] * g0 + tk[:, 1:2] * g1
    o_ref[...] = _ln(alpha * x_ref[...] + g2_ref[...] * m) * lng_ref[...] + lnb_ref[...]


def _combine(seg_nch, seg_off, seg_src, tile_nch, yb, x, tokm, g2, lng, lnb, *, ne, tm, chunk, n_per_batch, alpha):
    ttot, d = x.shape
    nsteps = ttot // tm
    per = n_per_batch // tm
    sp = 2 * tm + ne * chunk
    grid_spec = pltpu.PrefetchScalarGridSpec(
        num_scalar_prefetch=4,
        grid=(nsteps,),
        in_specs=[pl.BlockSpec(memory_space=pl.ANY),
                  pl.BlockSpec((tm, d), lambda i, *_: (i, 0)),
                  pl.BlockSpec((tm, LANES), lambda i, *_: (i, 0)),
                  pl.BlockSpec((None, 1, d), lambda i, *_: (i // per, 0, 0)),
                  pl.BlockSpec((1, d), lambda i, *_: (0, 0)),
                  pl.BlockSpec((1, d), lambda i, *_: (0, 0))],
        out_specs=pl.BlockSpec((tm, d), lambda i, *_: (i, 0)),
        scratch_shapes=[pltpu.VMEM((2, sp * SUBLANES, LANES), yb.dtype), pltpu.SemaphoreType.DMA((2,))],
    )
    return pl.pallas_call(
        functools.partial(_combine_kernel, ne=ne, chunk=chunk, nsteps=nsteps, alpha=alpha),
        out_shape=jax.ShapeDtypeStruct((ttot, d), F32),
        grid_spec=grid_spec,
        compiler_params=_cp(("arbitrary",), VMEM_LIMIT),
        name="moe_combine",
    )(seg_nch, seg_off, seg_src, tile_nch, yb, x, tokm, g2, lng.reshape(1, d), lnb.reshape(1, d))


def _hier_moe_layer(x1, t, g2, lng, lnb, wg, bg, wf, bf, w1, w3, w2, *, layer, alpha):
    b, n, d = x1.shape
    ttot = b * n
    ng = wg.shape[1]
    ne = wf.shape[1]
    epg = ne // ng
    tm = min(MOE_TILE, n)
    chunk = MOE_CHUNK
    nt = ttot // tm
    tflat = t.reshape(ttot, d)
    wcat = jnp.zeros((d, LANES), F32).at[:, 0:ng].set(wg).at[:, SUBLANES:SUBLANES + ne].set(wf).astype(BF16)
    bcat = jnp.zeros((1, LANES), F32).at[0, 0:ng].set(bg).at[0, SUBLANES:SUBLANES + ne].set(bf)
    ar = jnp.arange(2 * tm, dtype=I32)
    upper = (ar[:, None] < ar[None, :]).astype(BF16)
    lp, tokm, cnt = _route(tflat, wcat, bcat, upper, ng=ng, epg=epg, tm=tm, chunk=chunk)

    tile_cnt = cnt.reshape(ne, nt, LANES)[:, :, 0].T.astype(I32)
    counts = jnp.sum(tile_cnt, axis=0)
    padded = ((counts + chunk + MOE_ROWS - 1) // MOE_ROWS) * MOE_ROWS
    pend = jnp.cumsum(padded)
    pstart = pend - padded
    base = jnp.cumsum(tile_cnt, axis=0) - tile_cnt
    seg_row = (pstart[None, :] + base).astype(I32).reshape(nt * ne)
    nch = (tile_cnt + chunk - 1) // chunk
    seg_nch = nch.astype(I32).reshape(nt * ne)
    seg_off = ((jnp.cumsum(nch, axis=1) - nch) * chunk).astype(I32).reshape(nt * ne)
    tile_nch = jnp.sum(nch, axis=1).astype(I32)
    nb = -(-(2 * ttot + ne * (chunk + MOE_ROWS)) // MOE_ROWS)
    bstart = jnp.arange(nb, dtype=I32) * MOE_ROWS
    block_e = jnp.minimum(jnp.sum((pend[None, :] <= bstart[:, None]).astype(I32), axis=1), ne - 1)
    block_valid = jnp.clip(counts[block_e] - (bstart - pstart[block_e]), 0, MOE_ROWS).astype(I32)
    n_used = (pend[-1] // MOE_ROWS).astype(I32).reshape(1)

    xs = _dispatch(seg_nch, seg_off, seg_row, tile_nch, tflat, lp, jnp.zeros((nb * MOE_ROWS * SUBLANES, LANES), F32),
                   ne=ne, tm=tm, chunk=chunk)
    yb = _moe_experts(block_e.astype(I32), block_valid, n_used, xs, w1, w3, w2, layer=layer)
    out = _combine(seg_nch, seg_off, seg_row, tile_nch, yb, x1.reshape(ttot, d), tokm, g2, lng, lnb,
                   ne=ne, tm=tm, chunk=chunk, n_per_batch=n, alpha=alpha)
    return out.reshape(b, n, d)


def kernel(x, c, ctx, c_ctx, ada_w, ada_b, ln_g, ln_b, ev_w_in, ev_conv_w, ev_conv_b, ev_gate_a_w,
           ev_gate_a_b, ev_gate_x_w, ev_gate_x_b, ev_lru_lambda, ev_da_lambda, ev_da_subln, ev_w_out,
           od_w_out, od_b_out, moe_wg, moe_bg, moe_wf, moe_bf, moe_w1, moe_w3, moe_w2):
    bsz, n_lat, d = x.shape
    depth = ada_w.shape[0]
    alpha = (2.0 * depth) ** 0.25
    lw = ev_conv_w.shape[-1]
    hd = ev_da_lambda.shape[-1]
    vd = ev_da_subln.shape[-1]
    aw = (ev_w_in.shape[-1] - 2 * lw) // 3
    heads = aw // vd
    fnet_groups = 4

    rows = ((bsz + 1 + SUBLANES - 1) // SUBLANES) * SUBLANES
    cond = jnp.zeros((rows, d), F32).at[0:bsz].set(c).at[bsz].set(c_ctx)
    ada = _ada_terms(cond, ada_w, ada_b).reshape(depth, rows, 6, d)

    def lat_term(l, k):
        return ada[l, 0:bsz, k, :].reshape(bsz, 1, d)

    def ctx_term(l, k):
        return jnp.broadcast_to(ada[l, bsz, k, :].reshape(1, 1, d), (bsz, 1, d))

    for l in range(depth):
        ctx_live = any(m % 2 == 0 for m in range(l + 1, depth))
        assert not ctx_live, "context stream update is only needed for depth > 2"
        sh1, sc1, g1, sh2, sc2, g2 = [lat_term(l, k) for k in range(6)]
        if l % 2 == 0:
            e = l // 2
            lam_init = 0.8 - 0.6 * math.exp(-0.3 * l)
            w_in_b = ev_w_in[e].astype(BF16)
            cos_t, sin_t = _rope_tables(n_lat, hd)
            qscale = hd ** -0.5
            g_l, xr_l, q_l, k_l, v_l = _project_even(x, sh1, sc1, w_in_b, cos_t, sin_t, rope=True,
                                                     lw=lw, aw=aw, qscale=qscale, tm=512)
            n_ctx = ctx.shape[1]
            _, xr_c, _, k_c, v_c = _project_even(ctx, ctx_term(l, 0), ctx_term(l, 1), w_in_b,
                                                 cos_t[0:n_ctx], sin_t[0:n_ctx], rope=False,
                                                 lw=lw, aw=aw, qscale=qscale, tm=256)
            o_l = _diff_attention(q_l, k_c, k_l, v_c, v_l, ev_da_lambda[e], ev_da_subln[e],
                                  heads=heads, hd=hd, lam_init=lam_init, tq=512)
            gwid = 256
            y = None
            for dirn, rev in ((0, False), (1, True)):
                wa_bd = _block_diag_groups(ev_gate_a_w[e, dirn], gwid)
                wx_bd = _block_diag_groups(ev_gate_x_w[e, dirn], gwid)
                common = (ev_conv_w[e], ev_conv_b[e], wa_bd, ev_gate_a_b[e, dirn], wx_bd,
                          ev_gate_x_b[e, dirn], ev_lru_lambda[e, dirn])
                h_zero = jnp.zeros((bsz, 1, lw), F32)
                _, h_fin = _rglru_dir(xr_c, None, *common, h_zero, reverse=rev, tn=256)
                y, _ = _rglru_dir(xr_l, y, *common, h_fin, reverse=rev, tn=256)
            x1, t = _outproj_even(y, g_l, o_l, x, ev_w_out[e].astype(BF16), g1, ln_g[l, 0], ln_b[l, 0],
                                  sh2, sc2, alpha=alpha, tm=512)
        else:
            o = l // 2
            gd = d // fnet_groups
            tc, ts, tab = _dft_tables(n_lat, gd)
            zr, zi = _chan_dft(x, sh1, sc1, tab, groups=fnet_groups, tm=512)
            wv = _tok_dft(tc, ts, zr, zi, tk=256)
            norm = 1.0 / math.sqrt(float(n_lat * gd))
            x1, t = _outproj_odd(wv, x, od_w_out[o].astype(BF16), od_b_out[o], g1, ln_g[l, 0], ln_b[l, 0],
                                 sh2, sc2, alpha=alpha, norm=norm, tm=512)
        x = _hier_moe_layer(x1, t, g2, ln_g[l, 1], ln_b[l, 1], moe_wg[l], moe_bg[l], moe_wf[l], moe_bf[l],
                            moe_w1, moe_w3, moe_w2, layer=l, alpha=alpha)
    return x
```

```python
import functools
import math

import numpy as np
import jax
import jax.numpy as jnp
from jax import lax
from jax.experimental import pallas as pl
from jax.experimental.pallas import tpu as pltpu

F32 = jnp.float32
BF16 = jnp.bfloat16
I32 = jnp.int32

LN_EPS = 1e-6
LRU_C = 8.0
ROPE_BASE = 10000.0
GRID_W = 64
CONV_W = 4
LANES = 128
SUBLANES = 8
MOE_ROWS = 256
MOE_TILE = 256
MOE_CHUNK = 16
VMEM_LIMIT = 56 * 1024 * 1024


def _cp(sem, vmem=None):
    return pltpu.CompilerParams(dimension_semantics=sem, vmem_limit_bytes=vmem)


def _ln(x):
    mu = jnp.mean(x, axis=-1, keepdims=True)
    xc = x - mu
    var = jnp.mean(xc * xc, axis=-1, keepdims=True)
    return xc * lax.rsqrt(var + LN_EPS)


def _silu(x):
    return x * jax.nn.sigmoid(x)


PACK_SUB = 4


def _tile_row(r, mult):
    if isinstance(r, int):
        return r * PACK_SUB
    return pl.multiple_of(r * PACK_SUB, mult)


def _pack_rows(val):
    half = val.shape[1] // 2
    lo = lax.bitcast_convert_type(val[:, 0:half], jnp.uint32) >> 16
    hi = lax.bitcast_convert_type(val[:, half:], jnp.uint32) & jnp.uint32(0xFFFF0000)
    return lo | hi


def _unpack_rows(words):
    lo = lax.bitcast_convert_type(words << 16, F32)
    hi = lax.bitcast_convert_type(words & jnp.uint32(0xFFFF0000), F32)
    return jnp.concatenate([lo, hi], axis=1).astype(BF16)


def _rows_from_tiles(ref, nrows):
    return jnp.concatenate([ref[pl.ds(k, nrows, stride=PACK_SUB), :] for k in range(PACK_SUB)], axis=1)


def _rows_to_tiles(ref, val):
    nrows = val.shape[0]
    for k in range(PACK_SUB):
        ref[pl.ds(k, nrows, stride=PACK_SUB), :] = val[:, k * LANES:(k + 1) * LANES]


def _ada_kernel(c_ref, w_ref, b_ref, o_ref):
    s = _silu(c_ref[...]).astype(BF16)
    o_ref[...] = jnp.dot(s, w_ref[...].astype(BF16), preferred_element_type=F32) + b_ref[...]


def _ada_terms(cond, ada_w, ada_b):
    nl, d, d6 = ada_w.shape
    r = cond.shape[0]
    tn = 1024
    return pl.pallas_call(
        _ada_kernel,
        out_shape=jax.ShapeDtypeStruct((nl, r, d6), F32),
        grid=(nl, d6 // tn),
        in_specs=[pl.BlockSpec((r, d), lambda l, j: (0, 0)),
                  pl.BlockSpec((None, d, tn), lambda l, j: (l, 0, j)),
                  pl.BlockSpec((None, 1, tn), lambda l, j: (l, 0, j))],
        out_specs=pl.BlockSpec((None, r, tn), lambda l, j: (l, 0, j)),
        compiler_params=_cp(("parallel", "parallel")),
        name="ada_terms",
    )(cond, ada_w, ada_b.reshape(nl, 1, d6))


def _rope_apply(x, cos, sin_signed):
    tm = x.shape[0]
    lane = lax.broadcasted_iota(I32, (tm, LANES), 1)
    first_half = (lane % 32) < 16
    outs = []
    for j in range(x.shape[1] // LANES):
        xh = x[:, j * LANES:(j + 1) * LANES]
        partner = jnp.where(first_half, pltpu.roll(xh, LANES - 16, 1), pltpu.roll(xh, 16, 1))
        outs.append(xh * cos + partner * sin_signed)
    return jnp.concatenate(outs, axis=1)


def _proj_kernel(x_ref, sh_ref, sc_ref, w_ref, cos_ref, sin_ref,
                 g_ref, xr_ref, q_ref, k_ref, v_ref, *, rope, lw, aw, qscale):
    h = _ln(x_ref[...]) * (1.0 + sc_ref[...]) + sh_ref[...]
    hb = h.astype(BF16)

    def mm(c0, c1):
        return jnp.dot(hb, w_ref[:, c0:c1], preferred_element_type=F32)

    g_ref[...] = mm(0, lw).astype(BF16)
    xr_ref[...] = mm(lw, 2 * lw)
    q = mm(2 * lw, 2 * lw + aw)
    k = mm(2 * lw + aw, 2 * lw + 2 * aw)
    if rope:
        q = _rope_apply(q, cos_ref[...], sin_ref[...])
        k = _rope_apply(k, cos_ref[...], sin_ref[...])
    q_ref[...] = (q * qscale).astype(BF16)
    k_ref[...] = k.astype(BF16)
    v_ref[...] = mm(2 * lw + 2 * aw, 2 * lw + 3 * aw).astype(BF16)


def _project_even(x, shift, scale, w_in_b, cos_t, sin_t, *, rope, lw, aw, qscale, tm):
    b, n, d = x.shape
    tm = min(tm, n)
    nin = w_in_b.shape[1]
    tok = lambda bi, i: (bi, i, 0)
    per_b = lambda bi, i: (bi, 0, 0)
    outs = (jax.ShapeDtypeStruct((b, n, lw), BF16), jax.ShapeDtypeStruct((b, n, lw), F32),
            jax.ShapeDtypeStruct((b, n, aw), BF16), jax.ShapeDtypeStruct((b, n, aw), BF16),
            jax.ShapeDtypeStruct((b, n, aw), BF16))
    return pl.pallas_call(
        functools.partial(_proj_kernel, rope=rope, lw=lw, aw=aw, qscale=qscale),
        out_shape=outs,
        grid=(b, n // tm),
        in_specs=[pl.BlockSpec((None, tm, d), tok),
                  pl.BlockSpec((None, 1, d), per_b),
                  pl.BlockSpec((None, 1, d), per_b),
                  pl.BlockSpec((d, nin), lambda bi, i: (0, 0)),
                  pl.BlockSpec((tm, LANES), lambda bi, i: (i, 0)),
                  pl.BlockSpec((tm, LANES), lambda bi, i: (i, 0))],
        out_specs=(pl.BlockSpec((None, tm, lw), tok), pl.BlockSpec((None, tm, lw), tok),
                   pl.BlockSpec((None, tm, aw), tok), pl.BlockSpec((None, tm, aw), tok),
                   pl.BlockSpec((None, tm, aw), tok)),
        compiler_params=_cp(("parallel", "parallel"), VMEM_LIMIT),
        name="proj_even_rope" if rope else "proj_even_ctx",
    )(x, shift, scale, w_in_b, cos_t, sin_t)


def _rope_tables(n_tok, head_dim):
    t = jnp.arange(n_tok)
    row = (t // GRID_W).astype(F32)
    col = (t % GRID_W).astype(F32)
    nf = head_dim // 4
    freqs = ROPE_BASE ** (-jnp.arange(nf, dtype=F32) / nf)
    lane = np.arange(LANES)
    within = lane % head_dim
    axis = within // (2 * nf)
    half = (within % (2 * nf)) // nf
    f = within % nf
    pos = jnp.where(jnp.asarray(axis)[None, :] == 0, row[:, None], col[:, None])
    ang = pos * freqs[jnp.asarray(f)][None, :]
    sign = jnp.asarray(np.where(half == 0, -1.0, 1.0), F32)[None, :]
    return jnp.cos(ang).astype(F32), (jnp.sin(ang) * sign).astype(F32)


def _attn_kernel(q_ref, kc_ref, kl_ref, vc_ref, vl_ref, dl_ref, gain_ref, o_ref, kbuf, vbuf, sbuf, abuf, cbuf,
                 *, nc, nl, hd, lam_init, rows):
    @pl.when((pl.program_id(0) == 0) & (pl.program_id(1) == 0))
    def _():
        sbuf[...] = jnp.zeros_like(sbuf)
        abuf[...] = jnp.zeros_like(abuf)
        cbuf[...] = jnp.zeros_like(cbuf)

    kbuf[0:nc, :] = kc_ref[...]
    kbuf[nc:nc + nl, :] = kl_ref[...]
    vbuf[0:nc, :] = vc_ref[...]
    vbuf[nc:nc + nl, :] = vl_ref[...]

    lf = dl_ref[...]
    lam = (jnp.exp(jnp.sum(lf[0:1] * lf[1:2], axis=1, keepdims=True))
           - jnp.exp(jnp.sum(lf[2:3] * lf[3:4], axis=1, keepdims=True)) + lam_init)
    gain = gain_ref[...] * (1.0 - lam_init)
    n_sub = nl // rows
    lane = lax.broadcasted_iota(I32, (rows, 2 * hd), 1)
    nt = (((1,), (1,)), ((), ()))

    def stage_a(j, slot):
        r0 = pl.multiple_of(jnp.minimum(j, n_sub - 1) * rows, rows)
        q = q_ref[pl.ds(r0, rows), :]
        zero = jnp.zeros_like(q)
        kk = kbuf[...]
        sbuf[slot, 0] = lax.dot_general(jnp.where(lane < hd, q, zero), kk, nt, preferred_element_type=F32)
        sbuf[slot, 1] = lax.dot_general(jnp.where(lane >= hd, q, zero), kk, nt, preferred_element_type=F32)

    def stage_b(slot):
        for g in range(rows // SUBLANES):
            rs = slice(g * SUBLANES, (g + 1) * SUBLANES)
            s1 = sbuf[slot, 0, rs, :]
            e1 = jnp.exp2(s1 - jnp.max(s1, axis=1, keepdims=True))
            l1 = jnp.sum(e1, axis=1, keepdims=True)
            s2 = sbuf[slot, 1, rs, :]
            e2 = jnp.exp2(s2 - jnp.max(s2, axis=1, keepdims=True))
            l2 = jnp.sum(e2, axis=1, keepdims=True)
            abuf[slot, rs, :] = e1 - (lam * l1 / l2) * e2
            cbuf[slot, rs, :] = jnp.broadcast_to(1.0 / l1, (SUBLANES, cbuf.shape[2]))

    def stage_c(j, slot):
        r0 = pl.multiple_of(jnp.maximum(j - 2, 0) * rows, rows)
        o = jnp.dot(abuf[slot].astype(BF16), vbuf[...], preferred_element_type=F32) * cbuf[slot]
        o = o * lax.rsqrt(jnp.mean(o * o, axis=1, keepdims=True) + LN_EPS) * gain
        o_ref[pl.ds(r0, rows), :] = o.astype(BF16)

    def body(t, carry):
        j = 2 * t
        stage_a(j, 0)
        stage_b(1)
        stage_c(j, 0)
        stage_a(j + 1, 1)
        stage_b(0)
        stage_c(j + 1, 1)
        return carry

    lax.fori_loop(0, (n_sub + 2) // 2, body, 0)


def _diff_attention(q, k_ctx, k_lat, v_ctx, v_lat, da_lambda, subln, *, heads, hd, lam_init):
    b, n, aw = q.shape
    nc = k_ctx.shape[1]
    vd = aw // heads
    rows = min(128, n)
    assert (n // rows) % 2 == 0
    blk_q = pl.BlockSpec((None, n, vd), lambda bi, h: (bi, 0, h))
    blk_c = pl.BlockSpec((None, nc, vd), lambda bi, h: (bi, 0, h))
    return pl.pallas_call(
        functools.partial(_attn_kernel, nc=nc, nl=n, hd=hd, lam_init=lam_init, rows=rows),
        out_shape=jax.ShapeDtypeStruct((b, n, aw), BF16),
        grid=(b, heads),
        in_specs=[blk_q, blk_c, blk_q, blk_c, blk_q,
                  pl.BlockSpec(da_lambda.shape, lambda bi, h: (0, 0)),
                  pl.BlockSpec((1, vd), lambda bi, h: (0, 0))],
        out_specs=blk_q,
        scratch_shapes=[pltpu.VMEM((nc + n, vd), BF16), pltpu.VMEM((nc + n, vd), BF16),
                        pltpu.VMEM((2, 2, rows, nc + n), F32), pltpu.VMEM((2, rows, nc + n), F32),
                        pltpu.VMEM((2, rows, vd), F32)],
        compiler_params=_cp(("arbitrary", "arbitrary"), VMEM_LIMIT),
        name="diff_attention",
    )(q, k_ctx, k_lat, v_ctx, v_lat, da_lambda, subln.reshape(1, vd))


def _lru_kernel(*refs, reverse, add_prev, nt, groups):
    if add_prev:
        (xp_ref, xc_ref, xn_ref, yprev_ref, cw_ref, cb_ref, wa_ref, ba_ref, wx_ref, bx_ref,
         lam_ref, h0_ref, y_ref, hf_ref, carry) = refs
    else:
        (xp_ref, xc_ref, xn_ref, cw_ref, cb_ref, wa_ref, ba_ref, wx_ref, bx_ref,
         lam_ref, h0_ref, y_ref, hf_ref, carry) = refs
        yprev_ref = None
    i = pl.program_id(1)
    ti = (nt - 1 - i) if reverse else i
    tn, w = xc_ref.shape

    @pl.when(i == 0)
    def _():
        carry[...] = h0_ref[...]

    prev = jnp.where(ti > 0, xp_ref[...], 0.0)
    nxt = jnp.where(ti < nt - 1, xn_ref[...], 0.0)
    ext = jnp.concatenate([prev, xc_ref[...], nxt], axis=0)
    left = CONV_W // 2
    xc = cb_ref[...]
    for k in range(CONV_W):
        off = SUBLANES - left + k
        xc = xc + ext[off:off + tn, :] * cw_ref[k:k + 1, :]

    xb = xc.astype(BF16)
    gw = w // groups

    def gate(w_ref, b_ref):
        parts = [jnp.dot(xb[:, g * gw:(g + 1) * gw], w_ref[g], preferred_element_type=F32)
                 for g in range(groups)]
        return jax.nn.sigmoid(jnp.concatenate(parts, axis=1) + b_ref[...])

    r = gate(wa_ref, ba_ref)
    ig = gate(wx_ref, bx_ref)
    log_a = (-LRU_C * jax.nn.softplus(-lam_ref[...])) * r
    a = jnp.exp(log_a)
    bcoef = jnp.sqrt(-jnp.tanh(log_a) * (a * a + 1.0)) * ig * xc

    row = lax.broadcasted_iota(I32, (tn, w), 0)
    d = 1
    while d < tn:
        if reverse:
            a_sh = pltpu.roll(a, tn - d, 0)
            b_sh = pltpu.roll(bcoef, tn - d, 0)
            live = row < tn - d
        else:
            a_sh = pltpu.roll(a, d, 0)
            b_sh = pltpu.roll(bcoef, d, 0)
            live = row >= d
        bcoef = jnp.where(live, a * b_sh + bcoef, bcoef)
        a = jnp.where(live, a * a_sh, a)
        d *= 2
    h = a * carry[...] + bcoef
    carry[...] = h[0:1, :] if reverse else h[tn - 1:tn, :]
    hf_ref[...] = carry[...]
    if add_prev:
        y_ref[...] = yprev_ref[...] + h
    else:
        y_ref[...] = h


def _rglru_dir(xr, y_prev, conv_w, conv_b, wa_bd, ba, wx_bd, bx, lam, h0, *, reverse, tn):
    b, n, w = xr.shape
    tn = min(tn, n)
    nt = n // tn
    groups = wa_bd.shape[0]
    nb8 = n // SUBLANES
    per8 = tn // SUBLANES

    def tmap(i):
        return (nt - 1 - i) if reverse else i

    cur = pl.BlockSpec((None, tn, w), lambda bi, i: (bi, tmap(i), 0))
    halo_p = pl.BlockSpec((None, SUBLANES, w), lambda bi, i: (bi, jnp.maximum(tmap(i) * per8 - 1, 0), 0))
    halo_n = pl.BlockSpec((None, SUBLANES, w), lambda bi, i: (bi, jnp.minimum((tmap(i) + 1) * per8, nb8 - 1), 0))
    row_w = pl.BlockSpec((1, w), lambda bi, i: (0, 0))
    per_b = pl.BlockSpec((None, 1, w), lambda bi, i: (bi, 0, 0))
    gate_w = pl.BlockSpec(wa_bd.shape, lambda bi, i: (0, 0, 0))
    add_prev = y_prev is not None
    in_specs = [halo_p, cur, halo_n] + ([cur] if add_prev else []) + [
        pl.BlockSpec((CONV_W, w), lambda bi, i: (0, 0)), row_w, gate_w, row_w, gate_w, row_w, row_w, per_b]
    args = [xr, xr, xr] + ([y_prev] if add_prev else []) + [
        conv_w, conv_b.reshape(1, w), wa_bd, ba.reshape(1, w), wx_bd, bx.reshape(1, w),
        lam.reshape(1, w), h0]
    return pl.pallas_call(
        functools.partial(_lru_kernel, reverse=reverse, add_prev=add_prev, nt=nt, groups=groups),
        out_shape=(jax.ShapeDtypeStruct((b, n, w), F32), jax.ShapeDtypeStruct((b, 1, w), F32)),
        grid=(b, nt),
        in_specs=in_specs,
        out_specs=(cur, per_b),
        scratch_shapes=[pltpu.VMEM((1, w), F32)],
        compiler_params=_cp(("parallel", "arbitrary"), VMEM_LIMIT),
        name="rglru_rev" if reverse else "rglru_fwd",
    )(*args)


def _block_diag_groups(wh, group_width):
    heads, blk, _ = wh.shape
    per = group_width // blk
    groups = heads // per
    whg = wh.reshape(groups, per, blk, blk)
    eye = jnp.eye(per, dtype=wh.dtype)
    bd = jnp.einsum('gpij,pq->gpiqj', whg, eye).reshape(groups, group_width, group_width)
    return bd.astype(BF16)


def _residual_ln_mod(x, y, g1, lng, lnb, sh2, sc2, alpha):
    x1 = _ln(alpha * x + g1 * y) * lng + lnb
    t = _ln(x1) * (1.0 + sc2) + sh2
    return x1, t


def _outproj_even_kernel(r_ref, g_ref, o_ref, x_ref, w_ref, g1_ref, lng_ref, lnb_ref, sh2_ref, sc2_ref,
                         x1_ref, t_ref, *, lw, alpha):
    z = (r_ref[...] * jax.nn.gelu(g_ref[...].astype(F32))).astype(BF16)
    y = (jnp.dot(z, w_ref[0:lw, :], preferred_element_type=F32)
         + jnp.dot(o_ref[...], w_ref[lw:, :], preferred_element_type=F32))
    x1, t = _residual_ln_mod(x_ref[...], y, g1_ref[...], lng_ref[...], lnb_ref[...],
                             sh2_ref[...], sc2_ref[...], alpha)
    x1_ref[...] = x1
    t_ref[...] = t.astype(BF16)


def _outproj_even(r, g, o, x, w_out_b, g1, lng, lnb, sh2, sc2, *, alpha, tm):
    b, n, d = x.shape
    lw = r.shape[2]
    aw = o.shape[2]
    tm = min(tm, n)
    tok = lambda bi, i: (bi, i, 0)
    per_b = pl.BlockSpec((None, 1, d), lambda bi, i: (bi, 0, 0))
    row = pl.BlockSpec((1, d), lambda bi, i: (0, 0))
    return pl.pallas_call(
        functools.partial(_outproj_even_kernel, lw=lw, alpha=alpha),
        out_shape=(jax.ShapeDtypeStruct((b, n, d), F32), jax.ShapeDtypeStruct((b, n, d), BF16)),
        grid=(b, n // tm),
        in_specs=[pl.BlockSpec((None, tm, lw), tok), pl.BlockSpec((None, tm, lw), tok),
                  pl.BlockSpec((None, tm, aw), tok), pl.BlockSpec((None, tm, d), tok),
                  pl.BlockSpec(w_out_b.shape, lambda bi, i: (0, 0)),
                  per_b, row, row, per_b, per_b],
        out_specs=(pl.BlockSpec((None, tm, d), tok), pl.BlockSpec((None, tm, d), tok)),
        compiler_params=_cp(("parallel", "parallel"), VMEM_LIMIT),
        name="outproj_even",
    )(r, g, o, x, w_out_b, g1, lng.reshape(1, d), lnb.reshape(1, d), sh2, sc2)


def _chan_dft_kernel(x_ref, sh_ref, sc_ref, tab_ref, zr_ref, zi_ref, *, groups, gd):
    h = (_ln(x_ref[...]) * (1.0 + sc_ref[...]) + sh_ref[...]).astype(BF16)
    for g in range(groups):
        z = jnp.dot(h[:, g * gd:(g + 1) * gd], tab_ref[...], preferred_element_type=F32)
        zr_ref[:, g * gd:(g + 1) * gd] = z[:, 0:gd].astype(BF16)
        zi_ref[:, g * gd:(g + 1) * gd] = z[:, gd:2 * gd].astype(BF16)


def _chan_dft(x, shift, scale, tab, *, groups, tm):
    b, n, d = x.shape
    gd = d // groups
    tm = min(tm, n)
    tok = lambda bi, i: (bi, i, 0)
    per_b = pl.BlockSpec((None, 1, d), lambda bi, i: (bi, 0, 0))
    return pl.pallas_call(
        functools.partial(_chan_dft_kernel, groups=groups, gd=gd),
        out_shape=(jax.ShapeDtypeStruct((b, n, d), BF16), jax.ShapeDtypeStruct((b, n, d), BF16)),
        grid=(b, n // tm),
        in_specs=[pl.BlockSpec((None, tm, d), tok), per_b, per_b,
                  pl.BlockSpec(tab.shape, lambda bi, i: (0, 0))],
        out_specs=(pl.BlockSpec((None, tm, d), tok), pl.BlockSpec((None, tm, d), tok)),
        compiler_params=_cp(("parallel", "parallel"), VMEM_LIMIT),
        name="chan_dft",
    )(x, shift, scale, tab)


def _tok_dft_kernel(tc_ref, ts_ref, zr_ref, zi_ref, o_ref):
    acc = jnp.dot(tc_ref[...], zr_ref[...], preferred_element_type=F32)
    acc = acc - jnp.dot(ts_ref[...], zi_ref[...], preferred_element_type=F32)
    o_ref[...] = acc.astype(BF16)


def _tok_dft(tc, ts, zr, zi, *, tk):
    b, n, d = zr.shape
    tk = min(tk, n)
    lhs = pl.BlockSpec((tk, n), lambda bi, j: (j, 0))
    rhs = pl.BlockSpec((None, n, d), lambda bi, j: (bi, 0, 0))
    return pl.pallas_call(
        _tok_dft_kernel,
        out_shape=jax.ShapeDtypeStruct((b, n, d), BF16),
        grid=(b, n // tk),
        in_specs=[lhs, lhs, rhs, rhs],
        out_specs=pl.BlockSpec((None, tk, d), lambda bi, j: (bi, j, 0)),
        compiler_params=_cp(("parallel", "parallel"), VMEM_LIMIT),
        name="tok_dft",
    )(tc, ts, zr, zi)


def _dft_tables(n, gd):
    sub = 64 if n % 64 == 0 else 1
    t = np.arange(n, dtype=np.float64)
    a = np.arange(n // sub, dtype=np.float64)
    bb = np.arange(sub, dtype=np.float64)
    ang_a = 2.0 * np.pi * ((np.outer(a, t) * sub) % n) / n
    ang_b = 2.0 * np.pi * (np.outer(bb, t) % n) / n
    pc, ps = jnp.asarray(np.cos(ang_a), F32)[:, None, :], jnp.asarray(np.sin(ang_a), F32)[:, None, :]
    qc, qs = jnp.asarray(np.cos(ang_b), F32)[None, :, :], jnp.asarray(np.sin(ang_b), F32)[None, :, :]
    tc = (pc * qc - ps * qs).reshape(n, n).astype(BF16)
    ts = (ps * qc + pc * qs).reshape(n, n).astype(BF16)
    c = np.arange(gd, dtype=np.float64)
    ang_c = 2.0 * np.pi * (np.outer(c, c) % gd) / gd
    tab = jnp.asarray(np.concatenate([np.cos(ang_c), np.sin(ang_c)], axis=1), F32).astype(BF16)
    return tc, ts, tab


def _outproj_odd_kernel(wv_ref, x_ref, w_ref, b_ref, g1_ref, lng_ref, lnb_ref, sh2_ref, sc2_ref,
                        x1_ref, t_ref, *, alpha, norm):
    y = jnp.dot(wv_ref[...], w_ref[...], preferred_element_type=F32) * norm + b_ref[...]
    x1, t = _residual_ln_mod(x_ref[...], y, g1_ref[...], lng_ref[...], lnb_ref[...],
                             sh2_ref[...], sc2_ref[...], alpha)
    x1_ref[...] = x1
    t_ref[...] = t.astype(BF16)


def _outproj_odd(wv, x, w_b, bias, g1, lng, lnb, sh2, sc2, *, alpha, norm, tm):
    b, n, d = x.shape
    tm = min(tm, n)
    tok = lambda bi, i: (bi, i, 0)
    per_b = pl.BlockSpec((None, 1, d), lambda bi, i: (bi, 0, 0))
    row = pl.BlockSpec((1, d), lambda bi, i: (0, 0))
    return pl.pallas_call(
        functools.partial(_outproj_odd_kernel, alpha=alpha, norm=norm),
        out_shape=(jax.ShapeDtypeStruct((b, n, d), F32), jax.ShapeDtypeStruct((b, n, d), BF16)),
        grid=(b, n // tm),
        in_specs=[pl.BlockSpec((None, tm, d), tok), pl.BlockSpec((None, tm, d), tok),
                  pl.BlockSpec(w_b.shape, lambda bi, i: (0, 0)), row,
                  per_b, row, row, per_b, per_b],
        out_specs=(pl.BlockSpec((None, tm, d), tok), pl.BlockSpec((None, tm, d), tok)),
        compiler_params=_cp(("parallel", "parallel"), VMEM_LIMIT),
        name="outproj_odd",
    )(wv, x, w_b, bias.reshape(1, d), g1, lng.reshape(1, d), lnb.reshape(1, d), sh2, sc2)


def _route_kernel(t_ref, w_ref, b_ref, up_ref, lp_ref, tokm_ref, cnt_ref, *, ng, epg, chunk):
    tm = t_ref.shape[0]
    ne = ng * epg
    logits = jnp.dot(t_ref[...].astype(BF16), w_ref[...], preferred_element_type=F32) + b_ref[...]
    lt = logits.T
    best = lt[0:1, :]
    bi = jnp.zeros((1, tm), I32)
    for k in range(1, ng):
        gk = lt[k:k + 1, :]
        upd = gk > best
        bi = jnp.where(upd, k, bi)
        best = jnp.where(upd, gk, best)
    den = jnp.zeros((1, tm), F32)
    for k in range(ng):
        den = den + jnp.exp(lt[k:k + 1, :] - best)
    p_g = 1.0 / den
    fsel = lt[SUBLANES:SUBLANES + epg, :]
    for k in range(1, ng):
        fsel = jnp.where(bi == k, lt[SUBLANES + k * epg:SUBLANES + (k + 1) * epg, :], fsel)
    neg = jnp.full((1, tm), -jnp.inf, F32)
    m1, m2 = neg, neg
    i1 = jnp.zeros((1, tm), I32)
    i2 = jnp.zeros((1, tm), I32)
    for j in range(epg):
        v = fsel[j:j + 1, :]
        gt1 = v > m1
        gt2 = v > m2
        m2 = jnp.where(gt1, m1, jnp.where(gt2, v, m2))
        i2 = jnp.where(gt1, i1, jnp.where(gt2, j, i2))
        m1 = jnp.where(gt1, v, m1)
        i1 = jnp.where(gt1, j, i1)
    e21 = jnp.exp(m2 - m1)
    w1 = p_g / (1.0 + e21)
    w2 = p_g * e21 / (1.0 + e21)
    e1 = bi * epg + i1
    e2 = bi * epg + i2

    e = jnp.concatenate([e1, e2], axis=1)
    rows = lax.broadcasted_iota(I32, (ne, 2 * tm), 0)
    onehot = jnp.where(rows == e, 1.0, 0.0)
    before = jnp.dot(onehot.astype(BF16), up_ref[...], preferred_element_type=F32)
    tot = jnp.sum(onehot, axis=1, keepdims=True)
    slots = jnp.floor((tot + (chunk - 1.0)) * (1.0 / chunk)) * chunk
    slots_b = jnp.broadcast_to(slots, (ne, LANES))
    rowe = lax.broadcasted_iota(I32, (ne, LANES), 0)
    incl = slots_b
    d = 1
    while d < ne:
        incl = incl + jnp.where(rowe >= d, pltpu.roll(incl, d, 0), 0.0)
        d *= 2
    seg_off = jnp.tile(incl - slots_b, (1, 2 * tm // LANES))
    lpos = jnp.sum(onehot * (before + seg_off), axis=0, keepdims=True)
    lp0 = lpos[:, 0:tm]
    lp1 = lpos[:, tm:2 * tm]
    row8 = lax.broadcasted_iota(I32, (SUBLANES, tm), 0)
    lp_ref[...] = jnp.where(row8 == 0, lp0, jnp.where(row8 == 1, lp1, 0.0)).astype(I32)
    rowl = lax.broadcasted_iota(I32, (LANES, tm), 0)
    tokm = jnp.where(rowl == 0, w1, jnp.where(rowl == 1, w2, jnp.where(rowl == 2, lp0, jnp.where(rowl == 3, lp1, 0.0))))
    tokm_ref[...] = tokm.T
    cnt_ref[...] = jnp.broadcast_to(tot, (ne, LANES))


def _route(t, wcat_b, bcat, upper, *, ng, epg, tm, chunk):
    tt, d = t.shape
    ne = ng * epg
    nt = tt // tm
    return pl.pallas_call(
        functools.partial(_route_kernel, ng=ng, epg=epg, chunk=chunk),
        out_shape=(jax.ShapeDtypeStruct((SUBLANES, tt), I32), jax.ShapeDtypeStruct((tt, LANES), F32),
                   jax.ShapeDtypeStruct((ne, nt * LANES), F32)),
        grid=(nt,),
        in_specs=[pl.BlockSpec((tm, d), lambda i: (i, 0)),
                  pl.BlockSpec((d, LANES), lambda i: (0, 0)),
                  pl.BlockSpec((1, LANES), lambda i: (0, 0)),
                  pl.BlockSpec(upper.shape, lambda i: (0, 0))],
        out_specs=(pl.BlockSpec((SUBLANES, tm), lambda i: (0, i)),
                   pl.BlockSpec((tm, LANES), lambda i: (i, 0)),
                   pl.BlockSpec((ne, LANES), lambda i: (0, i))),
        compiler_params=_cp(("parallel",), VMEM_LIMIT),
        name="route_sort",
    )(t, wcat_b, bcat, upper)


def _dispatch_kernel(nch_ref, off_ref, dst_ref, tot_ref, t_ref, lp_ref, xs_in_ref, xs_ref, stage, sems,
                     *, ne, chunk, nsteps):
    del xs_in_ref
    i = pl.program_id(0)
    slot = i % 2
    sp = stage.shape[1] // PACK_SUB
    tm = t_ref.shape[0]
    crow = chunk * PACK_SUB

    lp = lp_ref[...]
    prow = lax.broadcasted_iota(I32, (sp, tm), 0)
    perm = jnp.where(prow == lp[0:1, :], 1.0, jnp.where(prow == lp[1:2, :], 1.0, 0.0)).astype(BF16)
    srt = jnp.dot(perm, t_ref[...], preferred_element_type=F32)
    _rows_to_tiles(stage.at[slot], _pack_rows(srt))

    def chunk_copy(sl, src, dst):
        return pltpu.make_async_copy(stage.at[sl, pl.ds(_tile_row(src, crow), crow)],
                                     xs_ref.at[pl.ds(_tile_row(dst, SUBLANES), crow)], sems.at[sl])

    def drain(sl, n):
        def w(c, carry):
            chunk_copy(sl, 0, 0).wait()
            return carry
        lax.fori_loop(0, n, w, 0)

    @pl.when(i >= 1)
    def _():
        drain(1 - slot, tot_ref[jnp.maximum(i - 1, 0)])

    for e in range(ne):
        src0 = pl.multiple_of(off_ref[i * ne + e], chunk)
        dst0 = dst_ref[i * ne + e]

        def issue(c, carry, src0=src0, dst0=dst0):
            chunk_copy(slot, src0 + c * chunk, dst0 + c * chunk).start()
            return carry
        lax.fori_loop(0, nch_ref[i * ne + e], issue, 0)

    @pl.when(i == nsteps - 1)
    def _():
        drain(slot, tot_ref[i])


def _dispatch(seg_nch, seg_off, seg_dst, tile_nch, t, lp, *, n_rows, ne, tm, chunk):
    ttot, d = t.shape
    nsteps = ttot // tm
    sp = 2 * tm + ne * chunk
    xs_init = jnp.zeros((n_rows * PACK_SUB, LANES), jnp.uint32)
    grid_spec = pltpu.PrefetchScalarGridSpec(
        num_scalar_prefetch=4,
        grid=(nsteps,),
        in_specs=[pl.BlockSpec((tm, d), lambda i, *_: (i, 0)),
                  pl.BlockSpec((SUBLANES, tm), lambda i, *_: (0, i)),
                  pl.BlockSpec(memory_space=pl.ANY)],
        out_specs=pl.BlockSpec(memory_space=pl.ANY),
        scratch_shapes=[pltpu.VMEM((2, sp * PACK_SUB, LANES), jnp.uint32), pltpu.SemaphoreType.DMA((2,))],
    )
    assert d == 2 * PACK_SUB * LANES
    return pl.pallas_call(
        functools.partial(_dispatch_kernel, ne=ne, chunk=chunk, nsteps=nsteps),
        out_shape=jax.ShapeDtypeStruct(xs_init.shape, xs_init.dtype),
        grid_spec=grid_spec,
        input_output_aliases={6: 0},
        compiler_params=_cp(("arbitrary",), VMEM_LIMIT),
        name="moe_dispatch",
    )(seg_nch, seg_off, seg_dst, tile_nch, t, lp, xs_init)


def _moe_kernel(be_ref, nu_ref, xs_ref, w1_ref, w3_ref, w2_ref, y_ref, w13b, w2b, *, ff):
    i = pl.program_id(0)

    @pl.when(i < nu_ref[0])
    def _():
        prev = be_ref[jnp.maximum(i - 1, 0)]

        @pl.when((i == 0) | (be_ref[i] != prev))
        def _():
            w13b[:, 0:ff] = w1_ref[...].astype(BF16)
            w13b[:, ff:2 * ff] = w3_ref[...].astype(BF16)
            w2b[...] = w2_ref[...].astype(BF16)

        x = _unpack_rows(_rows_from_tiles(xs_ref, MOE_ROWS))
        h = jnp.dot(x, w13b[...], preferred_element_type=F32)
        hid = (_silu(h[:, 0:ff]) * h[:, ff:2 * ff]).astype(BF16)
        y = jnp.dot(hid, w2b[...], preferred_element_type=F32)
        _rows_to_tiles(y_ref, _pack_rows(y.astype(BF16).astype(F32)))

    @pl.when(i >= nu_ref[0])
    def _():
        y_ref[...] = jnp.zeros_like(y_ref)


def _moe_experts(block_e, n_used, xs, w1, w3, w2, *, layer):
    d = w1.shape[-2]
    ff = w1.shape[-1]
    blk = MOE_ROWS * PACK_SUB
    nb = xs.shape[0] // blk
    wmap = lambda i, be, nu: (layer, be[i], 0, 0)
    grid_spec = pltpu.PrefetchScalarGridSpec(
        num_scalar_prefetch=2,
        grid=(nb,),
        in_specs=[pl.BlockSpec((blk, LANES), lambda i, be, nu: (i, 0)),
                  pl.BlockSpec((None, None, d, ff), wmap),
                  pl.BlockSpec((None, None, d, ff), wmap),
                  pl.BlockSpec((None, None, ff, d), wmap)],
        out_specs=pl.BlockSpec((blk, LANES), lambda i, be, nu: (i, 0)),
        scratch_shapes=[pltpu.VMEM((d, 2 * ff), BF16), pltpu.VMEM((ff, d), BF16)],
    )
    return pl.pallas_call(
        functools.partial(_moe_kernel, ff=ff),
        out_shape=jax.ShapeDtypeStruct(xs.shape, xs.dtype),
        grid_spec=grid_spec,
        compiler_params=_cp(("arbitrary",), VMEM_LIMIT),
        name="moe_experts",
    )(block_e, n_used, xs, w1, w3, w2)


def _combine_kernel(nch_ref, off_ref, src_ref, tot_ref, yb_ref, x_ref, tokm_ref, g2_ref, lng_ref, lnb_ref,
                    o_ref, stage, sems, *, ne, chunk, nsteps, alpha):
    i = pl.program_id(0)
    slot = i % 2
    sp = stage.shape[1] // PACK_SUB
    tm = x_ref.shape[0]
    crow = chunk * PACK_SUB

    def chunk_copy(sl, src, dst):
        return pltpu.make_async_copy(yb_ref.at[pl.ds(_tile_row(src, SUBLANES), crow)],
                                     stage.at[sl, pl.ds(_tile_row(dst, crow), crow)], sems.at[sl])

    def issue_tile(step, sl):
        for e in range(ne):
            src0 = src_ref[step * ne + e]
            dst0 = pl.multiple_of(off_ref[step * ne + e], chunk)

            def issue(c, carry, src0=src0, dst0=dst0):
                chunk_copy(sl, src0 + c * chunk, dst0 + c * chunk).start()
                return carry
            lax.fori_loop(0, nch_ref[step * ne + e], issue, 0)

    @pl.when(i == 0)
    def _():
        stage[...] = jnp.zeros_like(stage)
        issue_tile(0, 0)

    @pl.when(i + 1 < nsteps)
    def _():
        issue_tile(jnp.minimum(i + 1, nsteps - 1), 1 - slot)

    def w(c, carry):
        chunk_copy(slot, 0, 0).wait()
        return carry
    lax.fori_loop(0, tot_ref[i], w, 0)

    tk = tokm_ref[...]
    pos = lax.broadcasted_iota(I32, (tm, sp), 1).astype(F32)
    st = _unpack_rows(_rows_from_tiles(stage.at[slot], sp))
    g0 = jnp.dot(jnp.where(pos == tk[:, 2:3], 1.0, 0.0).astype(BF16), st, preferred_element_type=F32)
    g1 = jnp.dot(jnp.where(pos == tk[:, 3:4], 1.0, 0.0).astype(BF16), st, preferred_element_type=F32)
    m = tk[:, 0:1] * g0 + tk[:, 1:2] * g1
    o_ref[...] = _ln(alpha * x_ref[...] + g2_ref[...] * m) * lng_ref[...] + lnb_ref[...]


def _combine(seg_nch, seg_off, seg_src, tile_nch, yb, x, tokm, g2, lng, lnb, *, ne, tm, chunk, n_per_batch, alpha):
    ttot, d = x.shape
    nsteps = ttot // tm
    per = n_per_batch // tm
    sp = 2 * tm + ne * chunk
    grid_spec = pltpu.PrefetchScalarGridSpec(
        num_scalar_prefetch=4,
        grid=(nsteps,),
        in_specs=[pl.BlockSpec(memory_space=pl.ANY),
                  pl.BlockSpec((tm, d), lambda i, *_: (i, 0)),
                  pl.BlockSpec((tm, LANES), lambda i, *_: (i, 0)),
                  pl.BlockSpec((None, 1, d), lambda i, *_: (i // per, 0, 0)),
                  pl.BlockSpec((1, d), lambda i, *_: (0, 0)),
                  pl.BlockSpec((1, d), lambda i, *_: (0, 0))],
        out_specs=pl.BlockSpec((tm, d), lambda i, *_: (i, 0)),
        scratch_shapes=[pltpu.VMEM((2, sp * PACK_SUB, LANES), yb.dtype), pltpu.SemaphoreType.DMA((2,))],
    )
    return pl.pallas_call(
        functools.partial(_combine_kernel, ne=ne, chunk=chunk, nsteps=nsteps, alpha=alpha),
        out_shape=jax.ShapeDtypeStruct((ttot, d), F32),
        grid_spec=grid_spec,
        compiler_params=_cp(("arbitrary",), VMEM_LIMIT),
        name="moe_combine",
    )(seg_nch, seg_off, seg_src, tile_nch, yb, x, tokm, g2, lng.reshape(1, d), lnb.reshape(1, d))


def _hier_moe_layer(x1, t, g2, lng, lnb, wg, bg, wf, bf, w1, w3, w2, *, layer, alpha):
    b, n, d = x1.shape
    ttot = b * n
    ng = wg.shape[1]
    ne = wf.shape[1]
    epg = ne // ng
    tm = min(MOE_TILE, n)
    chunk = MOE_CHUNK
    nt = ttot // tm
    tflat = t.reshape(ttot, d)
    wcat = jnp.zeros((d, LANES), F32).at[:, 0:ng].set(wg).at[:, SUBLANES:SUBLANES + ne].set(wf).astype(BF16)
    bcat = jnp.zeros((1, LANES), F32).at[0, 0:ng].set(bg).at[0, SUBLANES:SUBLANES + ne].set(bf)
    ar = jnp.arange(2 * tm, dtype=I32)
    upper = (ar[:, None] < ar[None, :]).astype(BF16)
    lp, tokm, cnt = _route(tflat, wcat, bcat, upper, ng=ng, epg=epg, tm=tm, chunk=chunk)

    tile_cnt = cnt.reshape(ne, nt, LANES)[:, :, 0].T.astype(I32)
    seg_len = ((tile_cnt + 1) // 2) * 2
    counts = jnp.sum(seg_len, axis=0)
    padded = ((counts + chunk + MOE_ROWS - 1) // MOE_ROWS) * MOE_ROWS
    pend = jnp.cumsum(padded)
    pstart = pend - padded
    base = jnp.cumsum(seg_len, axis=0) - seg_len
    seg_row = (pstart[None, :] + base).astype(I32).reshape(nt * ne)
    nch = (tile_cnt + chunk - 1) // chunk
    seg_nch = nch.astype(I32).reshape(nt * ne)
    seg_off = ((jnp.cumsum(nch, axis=1) - nch) * chunk).astype(I32).reshape(nt * ne)
    tile_nch = jnp.sum(nch, axis=1).astype(I32)
    nb = -(-(2 * ttot + nt * ne + ne * (chunk + MOE_ROWS)) // MOE_ROWS)
    bstart = jnp.arange(nb, dtype=I32) * MOE_ROWS
    block_e = jnp.minimum(jnp.sum((pend[None, :] <= bstart[:, None]).astype(I32), axis=1), ne - 1)
    n_used = (pend[-1] // MOE_ROWS).astype(I32).reshape(1)

    xs = _dispatch(seg_nch, seg_off, seg_row, tile_nch, tflat, lp, n_rows=nb * MOE_ROWS, ne=ne, tm=tm, chunk=chunk)
    yb = _moe_experts(block_e.astype(I32), n_used, xs, w1, w3, w2, layer=layer)
    out = _combine(seg_nch, seg_off, seg_row, tile_nch, yb, x1.reshape(ttot, d), tokm, g2, lng, lnb,
                   ne=ne, tm=tm, chunk=chunk, n_per_batch=n, alpha=alpha)
    return out.reshape(b, n, d)


def kernel(x, c, ctx, c_ctx, ada_w, ada_b, ln_g, ln_b, ev_w_in, ev_conv_w, ev_conv_b, ev_gate_a_w,
           ev_gate_a_b, ev_gate_x_w, ev_gate_x_b, ev_lru_lambda, ev_da_lambda, ev_da_subln, ev_w_out,
           od_w_out, od_b_out, moe_wg, moe_bg, moe_wf, moe_bf, moe_w1, moe_w3, moe_w2):
    bsz, n_lat, d = x.shape
    depth = ada_w.shape[0]
    alpha = (2.0 * depth) ** 0.25
    lw = ev_conv_w.shape[-1]
    hd = ev_da_lambda.shape[-1]
    vd = ev_da_subln.shape[-1]
    aw = (ev_w_in.shape[-1] - 2 * lw) // 3
    heads = aw // vd
    fnet_groups = 4

    rows = ((bsz + 1 + SUBLANES - 1) // SUBLANES) * SUBLANES
    cond = jnp.zeros((rows, d), F32).at[0:bsz].set(c).at[bsz].set(c_ctx)
    ada = _ada_terms(cond, ada_w, ada_b).reshape(depth, rows, 6, d)

    def lat_term(l, k):
        return ada[l, 0:bsz, k, :].reshape(bsz, 1, d)

    def ctx_term(l, k):
        return jnp.broadcast_to(ada[l, bsz, k, :].reshape(1, 1, d), (bsz, 1, d))

    for l in range(depth):
        ctx_live = any(m % 2 == 0 for m in range(l + 1, depth))
        assert not ctx_live, "context stream update is only needed for depth > 2"
        sh1, sc1, g1, sh2, sc2, g2 = [lat_term(l, k) for k in range(6)]
        if l % 2 == 0:
            e = l // 2
            lam_init = 0.8 - 0.6 * math.exp(-0.3 * l)
            w_in_b = ev_w_in[e].astype(BF16)
            cos_t, sin_t = _rope_tables(n_lat, hd)
            qscale = hd ** -0.5 * math.log2(math.e)
            g_l, xr_l, q_l, k_l, v_l = _project_even(x, sh1, sc1, w_in_b, cos_t, sin_t, rope=True,
                                                     lw=lw, aw=aw, qscale=qscale, tm=512)
            n_ctx = ctx.shape[1]
            _, xr_c, _, k_c, v_c = _project_even(ctx, ctx_term(l, 0), ctx_term(l, 1), w_in_b,
                                                 cos_t[0:n_ctx], sin_t[0:n_ctx], rope=False,
                                                 lw=lw, aw=aw, qscale=qscale, tm=256)
            o_l = _diff_attention(q_l, k_c, k_l, v_c, v_l, ev_da_lambda[e], ev_da_subln[e],
                                  heads=heads, hd=hd, lam_init=lam_init)
            gwid = 256
            y = None
            for dirn, rev in ((0, False), (1, True)):
                wa_bd = _block_diag_groups(ev_gate_a_w[e, dirn], gwid)
                wx_bd = _block_diag_groups(ev_gate_x_w[e, dirn], gwid)
                common = (ev_conv_w[e], ev_conv_b[e], wa_bd, ev_gate_a_b[e, dirn], wx_bd,
                          ev_gate_x_b[e, dirn], ev_lru_lambda[e, dirn])
                h_zero = jnp.zeros((bsz, 1, lw), F32)
                _, h_fin = _rglru_dir(xr_c, None, *common, h_zero, reverse=rev, tn=256)
                y, _ = _rglru_dir(xr_l, y, *common, h_fin, reverse=rev, tn=256)
            x1, t = _outproj_even(y, g_l, o_l, x, ev_w_out[e].astype(BF16), g1, ln_g[l, 0], ln_b[l, 0],
                                  sh2, sc2, alpha=alpha, tm=512)
        else:
            o = l // 2
            gd = d // fnet_groups
            tc, ts, tab = _dft_tables(n_lat, gd)
            zr, zi = _chan_dft(x, sh1, sc1, tab, groups=fnet_groups, tm=512)
            wv = _tok_dft(tc, ts, zr, zi, tk=256)
            norm = 1.0 / math.sqrt(float(n_lat * gd))
            x1, t = _outproj_odd(wv, x, od_w_out[o].astype(BF16), od_b_out[o], g1, ln_g[l, 0], ln_b[l, 0],
                                 sh2, sc2, alpha=alpha, norm=norm, tm=512)
        x = _hier_moe_layer(x1, t, g2, ln_g[l, 1], ln_b[l, 1], moe_wg[l], moe_bg[l], moe_wf[l], moe_bf[l],
                            moe_w1, moe_w3, moe_w2, layer=l, alpha=alpha)
    return x
```

```python
import functools
import math

import numpy as np
import jax
import jax.numpy as jnp
from jax import lax
from jax.experimental import pallas as pl
from jax.experimental.pallas import tpu as pltpu

F32 = jnp.float32
BF16 = jnp.bfloat16
I32 = jnp.int32

LN_EPS = 1e-6
LRU_C = 8.0
ROPE_BASE = 10000.0
GRID_W = 64
CONV_W = 4
LANES = 128
SUBLANES = 8
MOE_ROWS = 512
MOE_TILE = 256
MOE_CHUNK = 16
VMEM_LIMIT = 56 * 1024 * 1024


def _cp(sem, vmem=None):
    return pltpu.CompilerParams(dimension_semantics=sem, vmem_limit_bytes=vmem)


def _ln(x):
    mu = jnp.mean(x, axis=-1, keepdims=True)
    xc = x - mu
    var = jnp.mean(xc * xc, axis=-1, keepdims=True)
    return xc * lax.rsqrt(var + LN_EPS)


def _silu(x):
    return x * jax.nn.sigmoid(x)


PACK_SUB = 4


def _tile_row(r, mult):
    if isinstance(r, int):
        return r * PACK_SUB
    return pl.multiple_of(r * PACK_SUB, mult)


def _pack_rows(val):
    half = val.shape[1] // 2
    lo = lax.bitcast_convert_type(val[:, 0:half], jnp.uint32) >> 16
    hi = lax.bitcast_convert_type(val[:, half:], jnp.uint32) & jnp.uint32(0xFFFF0000)
    return lo | hi


def _unpack_rows(words):
    lo = lax.bitcast_convert_type(words << 16, F32)
    hi = lax.bitcast_convert_type(words & jnp.uint32(0xFFFF0000), F32)
    return jnp.concatenate([lo, hi], axis=1).astype(BF16)


def _rows_from_tiles(ref, nrows):
    return jnp.concatenate([ref[pl.ds(k, nrows, stride=PACK_SUB), :] for k in range(PACK_SUB)], axis=1)


def _rows_to_tiles(ref, val):
    nrows = val.shape[0]
    for k in range(PACK_SUB):
        ref[pl.ds(k, nrows, stride=PACK_SUB), :] = val[:, k * LANES:(k + 1) * LANES]


def _ada_kernel(c_ref, w_ref, b_ref, o_ref):
    s = _silu(c_ref[...]).astype(BF16)
    o_ref[...] = jnp.dot(s, w_ref[...].astype(BF16), preferred_element_type=F32) + b_ref[...]


def _ada_terms(cond, ada_w, ada_b):
    nl, d, d6 = ada_w.shape
    r = cond.shape[0]
    tn = 1024
    return pl.pallas_call(
        _ada_kernel,
        out_shape=jax.ShapeDtypeStruct((nl, r, d6), F32),
        grid=(nl, d6 // tn),
        in_specs=[pl.BlockSpec((r, d), lambda l, j: (0, 0)),
                  pl.BlockSpec((None, d, tn), lambda l, j: (l, 0, j)),
                  pl.BlockSpec((None, 1, tn), lambda l, j: (l, 0, j))],
        out_specs=pl.BlockSpec((None, r, tn), lambda l, j: (l, 0, j)),
        compiler_params=_cp(("parallel", "parallel")),
        name="ada_terms",
    )(cond, ada_w, ada_b.reshape(nl, 1, d6))


def _rope_apply(x, cos, sin_signed):
    tm = x.shape[0]
    lane = lax.broadcasted_iota(I32, (tm, LANES), 1)
    first_half = (lane % 32) < 16
    outs = []
    for j in range(x.shape[1] // LANES):
        xh = x[:, j * LANES:(j + 1) * LANES]
        partner = jnp.where(first_half, pltpu.roll(xh, LANES - 16, 1), pltpu.roll(xh, 16, 1))
        outs.append(xh * cos + partner * sin_signed)
    return jnp.concatenate(outs, axis=1)


def _proj_kernel(x_ref, sh_ref, sc_ref, w_ref, cos_ref, sin_ref,
                 g_ref, xr_ref, q_ref, k_ref, v_ref, *, rope, lw, aw, qscale):
    h = _ln(x_ref[...]) * (1.0 + sc_ref[...]) + sh_ref[...]
    hb = h.astype(BF16)

    def mm(c0, c1):
        return jnp.dot(hb, w_ref[:, c0:c1], preferred_element_type=F32)

    g_ref[...] = mm(0, lw).astype(BF16)
    xr_ref[...] = mm(lw, 2 * lw)
    q = mm(2 * lw, 2 * lw + aw)
    k = mm(2 * lw + aw, 2 * lw + 2 * aw)
    if rope:
        q = _rope_apply(q, cos_ref[...], sin_ref[...])
        k = _rope_apply(k, cos_ref[...], sin_ref[...])
    q_ref[...] = (q * qscale).astype(BF16)
    k_ref[...] = k.astype(BF16)
    v_ref[...] = mm(2 * lw + 2 * aw, 2 * lw + 3 * aw).astype(BF16)


def _project_even(x, shift, scale, w_in_b, cos_t, sin_t, *, rope, lw, aw, qscale, tm):
    b, n, d = x.shape
    tm = min(tm, n)
    nin = w_in_b.shape[1]
    tok = lambda bi, i: (bi, i, 0)
    per_b = lambda bi, i: (bi, 0, 0)
    outs = (jax.ShapeDtypeStruct((b, n, lw), BF16), jax.ShapeDtypeStruct((b, n, lw), F32),
            jax.ShapeDtypeStruct((b, n, aw), BF16), jax.ShapeDtypeStruct((b, n, aw), BF16),
            jax.ShapeDtypeStruct((b, n, aw), BF16))
    return pl.pallas_call(
        functools.partial(_proj_kernel, rope=rope, lw=lw, aw=aw, qscale=qscale),
        out_shape=outs,
        grid=(b, n // tm),
        in_specs=[pl.BlockSpec((None, tm, d), tok),
                  pl.BlockSpec((None, 1, d), per_b),
                  pl.BlockSpec((None, 1, d), per_b),
                  pl.BlockSpec((d, nin), lambda bi, i: (0, 0)),
                  pl.BlockSpec((tm, LANES), lambda bi, i: (i, 0)),
                  pl.BlockSpec((tm, LANES), lambda bi, i: (i, 0))],
        out_specs=(pl.BlockSpec((None, tm, lw), tok), pl.BlockSpec((None, tm, lw), tok),
                   pl.BlockSpec((None, tm, aw), tok), pl.BlockSpec((None, tm, aw), tok),
                   pl.BlockSpec((None, tm, aw), tok)),
        compiler_params=_cp(("parallel", "parallel"), VMEM_LIMIT),
        name="proj_even_rope" if rope else "proj_even_ctx",
    )(x, shift, scale, w_in_b, cos_t, sin_t)


def _rope_tables(n_tok, head_dim):
    t = jnp.arange(n_tok)
    row = (t // GRID_W).astype(F32)
    col = (t % GRID_W).astype(F32)
    nf = head_dim // 4
    freqs = ROPE_BASE ** (-jnp.arange(nf, dtype=F32) / nf)
    lane = np.arange(LANES)
    within = lane % head_dim
    axis = within // (2 * nf)
    half = (within % (2 * nf)) // nf
    f = within % nf
    pos = jnp.where(jnp.asarray(axis)[None, :] == 0, row[:, None], col[:, None])
    ang = pos * freqs[jnp.asarray(f)][None, :]
    sign = jnp.asarray(np.where(half == 0, -1.0, 1.0), F32)[None, :]
    return jnp.cos(ang).astype(F32), (jnp.sin(ang) * sign).astype(F32)


def _attn_kernel(q_ref, kc_ref, kl_ref, vc_ref, vl_ref, dl_ref, gain_ref, o_ref, kbuf, vbuf, sbuf, ebuf, abuf,
                 cbuf, *, nc, nl, hd, lam_init, rows):
    @pl.when((pl.program_id(0) == 0) & (pl.program_id(1) == 0))
    def _():
        sbuf[...] = jnp.zeros_like(sbuf)
        abuf[...] = jnp.zeros_like(abuf)
        cbuf[...] = jnp.zeros_like(cbuf)

    kbuf[0:nc, :] = kc_ref[...]
    kbuf[nc:nc + nl, :] = kl_ref[...]
    vbuf[0:nc, :] = vc_ref[...]
    vbuf[nc:nc + nl, :] = vl_ref[...]

    lf = dl_ref[...]
    lam = (jnp.exp(jnp.sum(lf[0:1] * lf[1:2], axis=1, keepdims=True))
           - jnp.exp(jnp.sum(lf[2:3] * lf[3:4], axis=1, keepdims=True)) + lam_init)
    gain = gain_ref[...] * (1.0 - lam_init)
    n_sub = nl // rows
    lane = lax.broadcasted_iota(I32, (rows, 2 * hd), 1)
    nt = (((1,), (1,)), ((), ()))

    def stage_a(j, slot):
        r0 = pl.multiple_of(jnp.minimum(j, n_sub - 1) * rows, rows)
        q = q_ref[pl.ds(r0, rows), :]
        zero = jnp.zeros_like(q)
        kk = kbuf[...]
        sbuf[slot, 0] = lax.dot_general(jnp.where(lane < hd, q, zero), kk, nt, preferred_element_type=F32)
        sbuf[slot, 1] = lax.dot_general(jnp.where(lane >= hd, q, zero), kk, nt, preferred_element_type=F32)

    def stage_b(slot):
        cols = [slice(c, c + LANES) for c in range(0, nc + nl, LANES)]
        ls = []
        for k in range(2):
            pm = sbuf[slot, k, :, cols[0]]
            for cs in cols[1:]:
                pm = jnp.maximum(pm, sbuf[slot, k, :, cs])
            m = jnp.max(pm, axis=1, keepdims=True)
            acc = jnp.zeros((rows, LANES), F32)
            for cs in cols:
                e = jnp.exp2(sbuf[slot, k, :, cs] - m)
                acc = acc + e
                ebuf[slot, k, :, cs] = e.astype(BF16)
            ls.append(jnp.sum(acc, axis=1, keepdims=True))
        ratio = (lam * ls[0] / ls[1]).astype(BF16)
        abuf[slot] = ebuf[slot, 0] - ratio * ebuf[slot, 1]
        cbuf[slot] = jnp.broadcast_to(1.0 / ls[0], cbuf.shape[1:])

    def stage_c(j, slot):
        r0 = pl.multiple_of(jnp.maximum(j - 2, 0) * rows, rows)
        o = jnp.dot(abuf[slot], vbuf[...], preferred_element_type=F32) * cbuf[slot]
        o = o * lax.rsqrt(jnp.mean(o * o, axis=1, keepdims=True) + LN_EPS) * gain
        o_ref[pl.ds(r0, rows), :] = o.astype(BF16)

    def body(t, carry):
        j = 2 * t
        stage_a(j, 0)
        stage_b(1)
        stage_c(j, 0)
        stage_a(j + 1, 1)
        stage_b(0)
        stage_c(j + 1, 1)
        return carry

    lax.fori_loop(0, (n_sub + 2) // 2, body, 0)


def _diff_attention(q, k_ctx, k_lat, v_ctx, v_lat, da_lambda, subln, *, heads, hd, lam_init):
    b, n, aw = q.shape
    nc = k_ctx.shape[1]
    vd = aw // heads
    rows = min(128, n)
    assert (n // rows) % 2 == 0
    blk_q = pl.BlockSpec((None, n, vd), lambda bi, h: (bi, 0, h))
    blk_c = pl.BlockSpec((None, nc, vd), lambda bi, h: (bi, 0, h))
    return pl.pallas_call(
        functools.partial(_attn_kernel, nc=nc, nl=n, hd=hd, lam_init=lam_init, rows=rows),
        out_shape=jax.ShapeDtypeStruct((b, n, aw), BF16),
        grid=(b, heads),
        in_specs=[blk_q, blk_c, blk_q, blk_c, blk_q,
                  pl.BlockSpec(da_lambda.shape, lambda bi, h: (0, 0)),
                  pl.BlockSpec((1, vd), lambda bi, h: (0, 0))],
        out_specs=blk_q,
        scratch_shapes=[pltpu.VMEM((nc + n, vd), BF16), pltpu.VMEM((nc + n, vd), BF16),
                        pltpu.VMEM((2, 2, rows, nc + n), F32), pltpu.VMEM((2, 2, rows, nc + n), BF16),
                        pltpu.VMEM((2, rows, nc + n), BF16), pltpu.VMEM((2, rows, vd), F32)],
        compiler_params=_cp(("arbitrary", "arbitrary"), VMEM_LIMIT),
        name="diff_attention",
    )(q, k_ctx, k_lat, v_ctx, v_lat, da_lambda, subln.reshape(1, vd))


def _lru_kernel(*refs, reverse, add_prev, nt, groups):
    if add_prev:
        (xp_ref, xc_ref, xn_ref, yprev_ref, cw_ref, cb_ref, wa_ref, ba_ref, wx_ref, bx_ref,
         lam_ref, h0_ref, y_ref, hf_ref, carry) = refs
    else:
        (xp_ref, xc_ref, xn_ref, cw_ref, cb_ref, wa_ref, ba_ref, wx_ref, bx_ref,
         lam_ref, h0_ref, y_ref, hf_ref, carry) = refs
        yprev_ref = None
    i = pl.program_id(1)
    ti = (nt - 1 - i) if reverse else i
    tn, w = xc_ref.shape

    @pl.when(i == 0)
    def _():
        carry[...] = h0_ref[...]

    prev = jnp.where(ti > 0, xp_ref[...], 0.0)
    nxt = jnp.where(ti < nt - 1, xn_ref[...], 0.0)
    ext = jnp.concatenate([prev, xc_ref[...], nxt], axis=0)
    left = CONV_W // 2
    xc = cb_ref[...]
    for k in range(CONV_W):
        off = SUBLANES - left + k
        xc = xc + ext[off:off + tn, :] * cw_ref[k:k + 1, :]

    xb = xc.astype(BF16)
    gw = w // groups

    def gate(w_ref, b_ref):
        parts = [jnp.dot(xb[:, g * gw:(g + 1) * gw], w_ref[g], preferred_element_type=F32)
                 for g in range(groups)]
        return jax.nn.sigmoid(jnp.concatenate(parts, axis=1) + b_ref[...])

    r = gate(wa_ref, ba_ref)
    ig = gate(wx_ref, bx_ref)
    log_a = (-LRU_C * jax.nn.softplus(-lam_ref[...])) * r
    a = jnp.exp(log_a)
    bcoef = jnp.sqrt(-jnp.tanh(log_a) * (a * a + 1.0)) * ig * xc

    row = lax.broadcasted_iota(I32, (tn, w), 0)
    d = 1
    while d < tn:
        if reverse:
            a_sh = pltpu.roll(a, tn - d, 0)
            b_sh = pltpu.roll(bcoef, tn - d, 0)
            live = row < tn - d
        else:
            a_sh = pltpu.roll(a, d, 0)
            b_sh = pltpu.roll(bcoef, d, 0)
            live = row >= d
        bcoef = jnp.where(live, a * b_sh + bcoef, bcoef)
        a = jnp.where(live, a * a_sh, a)
        d *= 2
    h = a * carry[...] + bcoef
    carry[...] = h[0:1, :] if reverse else h[tn - 1:tn, :]
    hf_ref[...] = carry[...]
    if add_prev:
        y_ref[...] = yprev_ref[...] + h
    else:
        y_ref[...] = h


def _rglru_dir(xr, y_prev, conv_w, conv_b, wa_bd, ba, wx_bd, bx, lam, h0, *, reverse, tn):
    b, n, w = xr.shape
    tn = min(tn, n)
    nt = n // tn
    groups = wa_bd.shape[0]
    nb8 = n // SUBLANES
    per8 = tn // SUBLANES

    def tmap(i):
        return (nt - 1 - i) if reverse else i

    cur = pl.BlockSpec((None, tn, w), lambda bi, i: (bi, tmap(i), 0))
    halo_p = pl.BlockSpec((None, SUBLANES, w), lambda bi, i: (bi, jnp.maximum(tmap(i) * per8 - 1, 0), 0))
    halo_n = pl.BlockSpec((None, SUBLANES, w), lambda bi, i: (bi, jnp.minimum((tmap(i) + 1) * per8, nb8 - 1), 0))
    row_w = pl.BlockSpec((1, w), lambda bi, i: (0, 0))
    per_b = pl.BlockSpec((None, 1, w), lambda bi, i: (bi, 0, 0))
    gate_w = pl.BlockSpec(wa_bd.shape, lambda bi, i: (0, 0, 0))
    add_prev = y_prev is not None
    in_specs = [halo_p, cur, halo_n] + ([cur] if add_prev else []) + [
        pl.BlockSpec((CONV_W, w), lambda bi, i: (0, 0)), row_w, gate_w, row_w, gate_w, row_w, row_w, per_b]
    args = [xr, xr, xr] + ([y_prev] if add_prev else []) + [
        conv_w, conv_b.reshape(1, w), wa_bd, ba.reshape(1, w), wx_bd, bx.reshape(1, w),
        lam.reshape(1, w), h0]
    return pl.pallas_call(
        functools.partial(_lru_kernel, reverse=reverse, add_prev=add_prev, nt=nt, groups=groups),
        out_shape=(jax.ShapeDtypeStruct((b, n, w), F32), jax.ShapeDtypeStruct((b, 1, w), F32)),
        grid=(b, nt),
        in_specs=in_specs,
        out_specs=(cur, per_b),
        scratch_shapes=[pltpu.VMEM((1, w), F32)],
        compiler_params=_cp(("parallel", "arbitrary"), VMEM_LIMIT),
        name="rglru_rev" if reverse else "rglru_fwd",
    )(*args)


def _block_diag_groups(wh, group_width):
    heads, blk, _ = wh.shape
    per = group_width // blk
    groups = heads // per
    whg = wh.reshape(groups, per, blk, blk)
    eye = jnp.eye(per, dtype=wh.dtype)
    bd = jnp.einsum('gpij,pq->gpiqj', whg, eye).reshape(groups, group_width, group_width)
    return bd.astype(BF16)


def _residual_ln_mod(x, y, g1, lng, lnb, sh2, sc2, alpha):
    x1 = _ln(alpha * x + g1 * y) * lng + lnb
    t = _ln(x1) * (1.0 + sc2) + sh2
    return x1, t


def _outproj_even_kernel(r_ref, g_ref, o_ref, x_ref, w_ref, g1_ref, lng_ref, lnb_ref, sh2_ref, sc2_ref,
                         x1_ref, t_ref, *, lw, alpha):
    z = (r_ref[...] * jax.nn.gelu(g_ref[...].astype(F32))).astype(BF16)
    y = (jnp.dot(z, w_ref[0:lw, :], preferred_element_type=F32)
         + jnp.dot(o_ref[...], w_ref[lw:, :], preferred_element_type=F32))
    x1, t = _residual_ln_mod(x_ref[...], y, g1_ref[...], lng_ref[...], lnb_ref[...],
                             sh2_ref[...], sc2_ref[...], alpha)
    x1_ref[...] = x1
    t_ref[...] = t.astype(BF16)


def _outproj_even(r, g, o, x, w_out_b, g1, lng, lnb, sh2, sc2, *, alpha, tm):
    b, n, d = x.shape
    lw = r.shape[2]
    aw = o.shape[2]
    tm = min(tm, n)
    tok = lambda bi, i: (bi, i, 0)
    per_b = pl.BlockSpec((None, 1, d), lambda bi, i: (bi, 0, 0))
    row = pl.BlockSpec((1, d), lambda bi, i: (0, 0))
    return pl.pallas_call(
        functools.partial(_outproj_even_kernel, lw=lw, alpha=alpha),
        out_shape=(jax.ShapeDtypeStruct((b, n, d), F32), jax.ShapeDtypeStruct((b, n, d), BF16)),
        grid=(b, n // tm),
        in_specs=[pl.BlockSpec((None, tm, lw), tok), pl.BlockSpec((None, tm, lw), tok),
                  pl.BlockSpec((None, tm, aw), tok), pl.BlockSpec((None, tm, d), tok),
                  pl.BlockSpec(w_out_b.shape, lambda bi, i: (0, 0)),
                  per_b, row, row, per_b, per_b],
        out_specs=(pl.BlockSpec((None, tm, d), tok), pl.BlockSpec((None, tm, d), tok)),
        compiler_params=_cp(("parallel", "parallel"), VMEM_LIMIT),
        name="outproj_even",
    )(r, g, o, x, w_out_b, g1, lng.reshape(1, d), lnb.reshape(1, d), sh2, sc2)


def _chan_dft_kernel(x_ref, sh_ref, sc_ref, tab_ref, zr_ref, zi_ref, *, groups, gd):
    h = (_ln(x_ref[...]) * (1.0 + sc_ref[...]) + sh_ref[...]).astype(BF16)
    for g in range(groups):
        z = jnp.dot(h[:, g * gd:(g + 1) * gd], tab_ref[...], preferred_element_type=F32)
        zr_ref[:, g * gd:(g + 1) * gd] = z[:, 0:gd].astype(BF16)
        zi_ref[:, g * gd:(g + 1) * gd] = z[:, gd:2 * gd].astype(BF16)


def _chan_dft(x, shift, scale, tab, *, groups, tm):
    b, n, d = x.shape
    gd = d // groups
    tm = min(tm, n)
    tok = lambda bi, i: (bi, i, 0)
    per_b = pl.BlockSpec((None, 1, d), lambda bi, i: (bi, 0, 0))
    return pl.pallas_call(
        functools.partial(_chan_dft_kernel, groups=groups, gd=gd),
        out_shape=(jax.ShapeDtypeStruct((b, n, d), BF16), jax.ShapeDtypeStruct((b, n, d), BF16)),
        grid=(b, n // tm),
        in_specs=[pl.BlockSpec((None, tm, d), tok), per_b, per_b,
                  pl.BlockSpec(tab.shape, lambda bi, i: (0, 0))],
        out_specs=(pl.BlockSpec((None, tm, d), tok), pl.BlockSpec((None, tm, d), tok)),
        compiler_params=_cp(("parallel", "parallel"), VMEM_LIMIT),
        name="chan_dft",
    )(x, shift, scale, tab)


FFT_C = 64


def _fft_pitch(group):
    p = -(-group // SUBLANES)
    return SUBLANES * (p if p % 2 else p + 1)


def _tok_fft_kernel(zr_ref, zi_ref, m1_ref, m3_ref, tc_ref, ts_ref, o_ref, zsr, zsi, asr, asi, ob, *, nr):
    c_len = FFT_C
    pz = zsr.shape[0] // nr
    pa = asr.shape[0] // c_len
    for r in range(nr):
        zsr[pz * r:pz * r + c_len, :] = zr_ref[c_len * r:c_len * (r + 1), :].astype(F32)
        zsi[pz * r:pz * r + c_len, :] = zi_ref[c_len * r:c_len * (r + 1), :].astype(F32)
    m1 = m1_ref[...]
    for c in range(c_len):
        x2 = jnp.concatenate([zsr[pl.ds(c, nr, stride=pz), :], zsi[pl.ds(c, nr, stride=pz), :]], axis=0)
        a2 = jnp.dot(m1, x2.astype(BF16), preferred_element_type=F32)
        ar, ai = a2[0:nr], a2[nr:2 * nr]
        tcv = tc_ref[c * nr:(c + 1) * nr, :]
        tsv = ts_ref[c * nr:(c + 1) * nr, :]
        asr[pa * c:pa * c + nr, :] = ar * tcv + ai * tsv
        asi[pa * c:pa * c + nr, :] = ai * tcv - ar * tsv
    m3 = m3_ref[...]
    for k1 in range(nr):
        y2 = jnp.concatenate([asr[pl.ds(k1, c_len, stride=pa), :], asi[pl.ds(k1, c_len, stride=pa), :]], axis=0)
        ob[pl.ds(k1, c_len, stride=pa), :] = jnp.dot(m3, y2.astype(BF16), preferred_element_type=F32)
    for k2 in range(c_len):
        o_ref[nr * k2:nr * (k2 + 1), :] = ob[pa * k2:pa * k2 + nr, :].astype(BF16)


def _tok_fft(zr, zi):
    b, n, d = zr.shape
    nr = n // FFT_C
    pz = _fft_pitch(FFT_C)
    pa = _fft_pitch(nr)
    kr = np.arange(nr, dtype=np.float64)
    ang_r = 2.0 * np.pi * (np.outer(kr, kr) % nr) / nr
    cr, sr = np.cos(ang_r), np.sin(ang_r)
    m1 = jnp.asarray(np.block([[cr, -sr], [-sr, -cr]]), F32).astype(BF16)
    kc = np.arange(FFT_C, dtype=np.float64)
    ang_c = 2.0 * np.pi * (np.outer(kc, kc) % FFT_C) / FFT_C
    m3 = jnp.asarray(np.concatenate([np.cos(ang_c), np.sin(ang_c)], axis=1), F32).astype(BF16)
    ang_t = 2.0 * np.pi * (np.outer(kc, kr) % n) / n
    tc = jnp.broadcast_to(jnp.asarray(np.cos(ang_t).reshape(FFT_C * nr, 1), F32), (FFT_C * nr, LANES))
    ts = jnp.broadcast_to(jnp.asarray(np.sin(ang_t).reshape(FFT_C * nr, 1), F32), (FFT_C * nr, LANES))
    slab = pl.BlockSpec((None, n, LANES), lambda bi, l: (bi, 0, l))
    const = lambda a: pl.BlockSpec(a.shape, lambda bi, l: (0, 0))
    return pl.pallas_call(
        functools.partial(_tok_fft_kernel, nr=nr),
        out_shape=jax.ShapeDtypeStruct((b, n, d), BF16),
        grid=(b, d // LANES),
        in_specs=[slab, slab, const(m1), const(m3), const(tc), const(ts)],
        out_specs=slab,
        scratch_shapes=[pltpu.VMEM((nr * pz, LANES), F32), pltpu.VMEM((nr * pz, LANES), F32),
                        pltpu.VMEM((FFT_C * pa, LANES), F32), pltpu.VMEM((FFT_C * pa, LANES), F32),
                        pltpu.VMEM((FFT_C * pa, LANES), F32)],
        compiler_params=_cp(("parallel", "parallel"), VMEM_LIMIT),
        name="tok_fft",
    )(zr, zi, m1, m3, tc, ts)


def _chan_dft_table(gd):
    c = np.arange(gd, dtype=np.float64)
    ang_c = 2.0 * np.pi * (np.outer(c, c) % gd) / gd
    return jnp.asarray(np.concatenate([np.cos(ang_c), np.sin(ang_c)], axis=1), F32).astype(BF16)


def _outproj_odd_kernel(wv_ref, x_ref, w_ref, b_ref, g1_ref, lng_ref, lnb_ref, sh2_ref, sc2_ref,
                        x1_ref, t_ref, *, alpha, norm):
    y = jnp.dot(wv_ref[...], w_ref[...], preferred_element_type=F32) * norm + b_ref[...]
    x1, t = _residual_ln_mod(x_ref[...], y, g1_ref[...], lng_ref[...], lnb_ref[...],
                             sh2_ref[...], sc2_ref[...], alpha)
    x1_ref[...] = x1
    t_ref[...] = t.astype(BF16)


def _outproj_odd(wv, x, w_b, bias, g1, lng, lnb, sh2, sc2, *, alpha, norm, tm):
    b, n, d = x.shape
    tm = min(tm, n)
    tok = lambda bi, i: (bi, i, 0)
    per_b = pl.BlockSpec((None, 1, d), lambda bi, i: (bi, 0, 0))
    row = pl.BlockSpec((1, d), lambda bi, i: (0, 0))
    return pl.pallas_call(
        functools.partial(_outproj_odd_kernel, alpha=alpha, norm=norm),
        out_shape=(jax.ShapeDtypeStruct((b, n, d), F32), jax.ShapeDtypeStruct((b, n, d), BF16)),
        grid=(b, n // tm),
        in_specs=[pl.BlockSpec((None, tm, d), tok), pl.BlockSpec((None, tm, d), tok),
                  pl.BlockSpec(w_b.shape, lambda bi, i: (0, 0)), row,
                  per_b, row, row, per_b, per_b],
        out_specs=(pl.BlockSpec((None, tm, d), tok), pl.BlockSpec((None, tm, d), tok)),
        compiler_params=_cp(("parallel", "parallel"), VMEM_LIMIT),
        name="outproj_odd",
    )(wv, x, w_b, bias.reshape(1, d), g1, lng.reshape(1, d), lnb.reshape(1, d), sh2, sc2)


def _route_kernel(t_ref, w_ref, b_ref, up_ref, lp_ref, tokm_ref, cnt_ref, *, ng, epg, chunk):
    tm = t_ref.shape[0]
    ne = ng * epg
    logits = jnp.dot(t_ref[...].astype(BF16), w_ref[...], preferred_element_type=F32) + b_ref[...]
    lt = logits.T
    best = lt[0:1, :]
    bi = jnp.zeros((1, tm), I32)
    for k in range(1, ng):
        gk = lt[k:k + 1, :]
        upd = gk > best
        bi = jnp.where(upd, k, bi)
        best = jnp.where(upd, gk, best)
    den = jnp.zeros((1, tm), F32)
    for k in range(ng):
        den = den + jnp.exp(lt[k:k + 1, :] - best)
    p_g = 1.0 / den
    fsel = lt[SUBLANES:SUBLANES + epg, :]
    for k in range(1, ng):
        fsel = jnp.where(bi == k, lt[SUBLANES + k * epg:SUBLANES + (k + 1) * epg, :], fsel)
    neg = jnp.full((1, tm), -jnp.inf, F32)
    m1, m2 = neg, neg
    i1 = jnp.zeros((1, tm), I32)
    i2 = jnp.zeros((1, tm), I32)
    for j in range(epg):
        v = fsel[j:j + 1, :]
        gt1 = v > m1
        gt2 = v > m2
        m2 = jnp.where(gt1, m1, jnp.where(gt2, v, m2))
        i2 = jnp.where(gt1, i1, jnp.where(gt2, j, i2))
        m1 = jnp.where(gt1, v, m1)
        i1 = jnp.where(gt1, j, i1)
    e21 = jnp.exp(m2 - m1)
    w1 = p_g / (1.0 + e21)
    w2 = p_g * e21 / (1.0 + e21)
    e1 = bi * epg + i1
    e2 = bi * epg + i2

    e = jnp.concatenate([e1, e2], axis=1)
    rows = lax.broadcasted_iota(I32, (ne, 2 * tm), 0)
    onehot = jnp.where(rows == e, 1.0, 0.0)
    before = jnp.dot(onehot.astype(BF16), up_ref[...], preferred_element_type=F32)
    tot = jnp.sum(onehot, axis=1, keepdims=True)
    slots = jnp.floor((tot + (chunk - 1.0)) * (1.0 / chunk)) * chunk
    slots_b = jnp.broadcast_to(slots, (ne, LANES))
    rowe = lax.broadcasted_iota(I32, (ne, LANES), 0)
    incl = slots_b
    d = 1
    while d < ne:
        incl = incl + jnp.where(rowe >= d, pltpu.roll(incl, d, 0), 0.0)
        d *= 2
    seg_off = jnp.tile(incl - slots_b, (1, 2 * tm // LANES))
    lpos = jnp.sum(onehot * (before + seg_off), axis=0, keepdims=True)
    lp0 = lpos[:, 0:tm]
    lp1 = lpos[:, tm:2 * tm]
    row8 = lax.broadcasted_iota(I32, (SUBLANES, tm), 0)
    lp_ref[...] = jnp.where(row8 == 0, lp0, jnp.where(row8 == 1, lp1, 0.0)).astype(I32)
    rowl = lax.broadcasted_iota(I32, (LANES, tm), 0)
    tokm = jnp.where(rowl == 0, w1, jnp.where(rowl == 1, w2, jnp.where(rowl == 2, lp0, jnp.where(rowl == 3, lp1, 0.0))))
    tokm_ref[...] = tokm.T
    cnt_ref[...] = jnp.broadcast_to(tot, (ne, LANES))


def _route(t, wcat_b, bcat, upper, *, ng, epg, tm, chunk):
    tt, d = t.shape
    ne = ng * epg
    nt = tt // tm
    return pl.pallas_call(
        functools.partial(_route_kernel, ng=ng, epg=epg, chunk=chunk),
        out_shape=(jax.ShapeDtypeStruct((SUBLANES, tt), I32), jax.ShapeDtypeStruct((tt, LANES), F32),
                   jax.ShapeDtypeStruct((ne, nt * LANES), F32)),
        grid=(nt,),
        in_specs=[pl.BlockSpec((tm, d), lambda i: (i, 0)),
                  pl.BlockSpec((d, LANES), lambda i: (0, 0)),
                  pl.BlockSpec((1, LANES), lambda i: (0, 0)),
                  pl.BlockSpec(upper.shape, lambda i: (0, 0))],
        out_specs=(pl.BlockSpec((SUBLANES, tm), lambda i: (0, i)),
                   pl.BlockSpec((tm, LANES), lambda i: (i, 0)),
                   pl.BlockSpec((ne, LANES), lambda i: (0, i))),
        compiler_params=_cp(("parallel",), VMEM_LIMIT),
        name="route_sort",
    )(t, wcat_b, bcat, upper)


def _dispatch_kernel(nch_ref, off_ref, dst_ref, tot_ref, t_ref, lp_ref, xs_in_ref, xs_ref, stage, sems,
                     *, ne, chunk, nsteps):
    del xs_in_ref
    i = pl.program_id(0)
    slot = i % 2
    sp = stage.shape[1] // PACK_SUB
    tm = t_ref.shape[0]
    crow = chunk * PACK_SUB

    lp = lp_ref[...]
    prow = lax.broadcasted_iota(I32, (sp, tm), 0)
    perm = jnp.where(prow == lp[0:1, :], 1.0, jnp.where(prow == lp[1:2, :], 1.0, 0.0)).astype(BF16)
    srt = jnp.dot(perm, t_ref[...], preferred_element_type=F32)
    _rows_to_tiles(stage.at[slot], _pack_rows(srt))

    def chunk_copy(sl, src, dst):
        return pltpu.make_async_copy(stage.at[sl, pl.ds(_tile_row(src, crow), crow)],
                                     xs_ref.at[pl.ds(_tile_row(dst, SUBLANES), crow)], sems.at[sl])

    def drain(sl, n):
        def w(c, carry):
            chunk_copy(sl, 0, 0).wait()
            return carry
        lax.fori_loop(0, n, w, 0)

    @pl.when(i >= 1)
    def _():
        drain(1 - slot, tot_ref[jnp.maximum(i - 1, 0)])

    for e in range(ne):
        src0 = pl.multiple_of(off_ref[i * ne + e], chunk)
        dst0 = dst_ref[i * ne + e]

        def issue(c, carry, src0=src0, dst0=dst0):
            chunk_copy(slot, src0 + c * chunk, dst0 + c * chunk).start()
            return carry
        lax.fori_loop(0, nch_ref[i * ne + e], issue, 0)

    @pl.when(i == nsteps - 1)
    def _():
        drain(slot, tot_ref[i])


def _dispatch(seg_nch, seg_off, seg_dst, tile_nch, t, lp, *, n_rows, ne, tm, chunk):
    ttot, d = t.shape
    nsteps = ttot // tm
    sp = 2 * tm + ne * chunk
    xs_init = jnp.zeros((n_rows * PACK_SUB, LANES), jnp.uint32)
    grid_spec = pltpu.PrefetchScalarGridSpec(
        num_scalar_prefetch=4,
        grid=(nsteps,),
        in_specs=[pl.BlockSpec((tm, d), lambda i, *_: (i, 0)),
                  pl.BlockSpec((SUBLANES, tm), lambda i, *_: (0, i)),
                  pl.BlockSpec(memory_space=pl.ANY)],
        out_specs=pl.BlockSpec(memory_space=pl.ANY),
        scratch_shapes=[pltpu.VMEM((2, sp * PACK_SUB, LANES), jnp.uint32), pltpu.SemaphoreType.DMA((2,))],
    )
    assert d == 2 * PACK_SUB * LANES
    return pl.pallas_call(
        functools.partial(_dispatch_kernel, ne=ne, chunk=chunk, nsteps=nsteps),
        out_shape=jax.ShapeDtypeStruct(xs_init.shape, xs_init.dtype),
        grid_spec=grid_spec,
        input_output_aliases={6: 0},
        compiler_params=_cp(("arbitrary",), VMEM_LIMIT),
        name="moe_dispatch",
    )(seg_nch, seg_off, seg_dst, tile_nch, t, lp, xs_init)


def _moe_kernel(be_ref, nu_ref, xs_ref, w1_ref, w3_ref, w2_ref, y_ref, w13b, w2b, *, ff):
    i = pl.program_id(0)

    @pl.when(i < nu_ref[0])
    def _():
        prev = be_ref[jnp.maximum(i - 1, 0)]

        @pl.when((i == 0) | (be_ref[i] != prev))
        def _():
            w13b[:, 0:ff] = w1_ref[...].astype(BF16)
            w13b[:, ff:2 * ff] = w3_ref[...].astype(BF16)
            w2b[...] = w2_ref[...].astype(BF16)

        x = _unpack_rows(_rows_from_tiles(xs_ref, MOE_ROWS))
        h = jnp.dot(x, w13b[...], preferred_element_type=F32)
        hid = (_silu(h[:, 0:ff]) * h[:, ff:2 * ff]).astype(BF16)
        y = jnp.dot(hid, w2b[...], preferred_element_type=F32)
        _rows_to_tiles(y_ref, _pack_rows(y.astype(BF16).astype(F32)))

    @pl.when(i >= nu_ref[0])
    def _():
        y_ref[...] = jnp.zeros_like(y_ref)


def _moe_experts(block_e, n_used, xs, w1, w3, w2, *, layer):
    d = w1.shape[-2]
    ff = w1.shape[-1]
    blk = MOE_ROWS * PACK_SUB
    nb = xs.shape[0] // blk
    wmap = lambda i, be, nu: (layer, be[i], 0, 0)
    grid_spec = pltpu.PrefetchScalarGridSpec(
        num_scalar_prefetch=2,
        grid=(nb,),
        in_specs=[pl.BlockSpec((blk, LANES), lambda i, be, nu: (i, 0)),
                  pl.BlockSpec((None, None, d, ff), wmap),
                  pl.BlockSpec((None, None, d, ff), wmap),
                  pl.BlockSpec((None, None, ff, d), wmap)],
        out_specs=pl.BlockSpec((blk, LANES), lambda i, be, nu: (i, 0)),
        scratch_shapes=[pltpu.VMEM((d, 2 * ff), BF16), pltpu.VMEM((ff, d), BF16)],
    )
    return pl.pallas_call(
        functools.partial(_moe_kernel, ff=ff),
        out_shape=jax.ShapeDtypeStruct(xs.shape, xs.dtype),
        grid_spec=grid_spec,
        compiler_params=_cp(("arbitrary",), VMEM_LIMIT),
        name="moe_experts",
    )(block_e, n_used, xs, w1, w3, w2)


def _combine_kernel(nch_ref, off_ref, src_ref, tot_ref, yb_ref, x_ref, tokm_ref, g2_ref, lng_ref, lnb_ref,
                    o_ref, stage, sems, *, ne, chunk, nsteps, alpha):
    i = pl.program_id(0)
    slot = i % 2
    sp = stage.shape[1] // PACK_SUB
    tm = x_ref.shape[0]
    crow = chunk * PACK_SUB

    def chunk_copy(sl, src, dst):
        return pltpu.make_async_copy(yb_ref.at[pl.ds(_tile_row(src, SUBLANES), crow)],
                                     stage.at[sl, pl.ds(_tile_row(dst, crow), crow)], sems.at[sl])

    def issue_tile(step, sl):
        for e in range(ne):
            src0 = src_ref[step * ne + e]
            dst0 = pl.multiple_of(off_ref[step * ne + e], chunk)

            def issue(c, carry, src0=src0, dst0=dst0):
                chunk_copy(sl, src0 + c * chunk, dst0 + c * chunk).start()
                return carry
            lax.fori_loop(0, nch_ref[step * ne + e], issue, 0)

    @pl.when(i == 0)
    def _():
        stage[...] = jnp.zeros_like(stage)
        issue_tile(0, 0)

    @pl.when(i + 1 < nsteps)
    def _():
        issue_tile(jnp.minimum(i + 1, nsteps - 1), 1 - slot)

    def w(c, carry):
        chunk_copy(slot, 0, 0).wait()
        return carry
    lax.fori_loop(0, tot_ref[i], w, 0)

    tk = tokm_ref[...]
    pos = lax.broadcasted_iota(I32, (tm, sp), 1).astype(F32)
    st = _unpack_rows(_rows_from_tiles(stage.at[slot], sp))
    g0 = jnp.dot(jnp.where(pos == tk[:, 2:3], 1.0, 0.0).astype(BF16), st, preferred_element_type=F32)
    g1 = jnp.dot(jnp.where(pos == tk[:, 3:4], 1.0, 0.0).astype(BF16), st, preferred_element_type=F32)
    m = tk[:, 0:1] * g0 + tk[:, 1:2] * g1
    o_ref[...] = _ln(alpha * x_ref[...] + g2_ref[...] * m) * lng_ref[...] + lnb_ref[...]


def _combine(seg_nch, seg_off, seg_src, tile_nch, yb, x, tokm, g2, lng, lnb, *, ne, tm, chunk, n_per_batch, alpha):
    ttot, d = x.shape
    nsteps = ttot // tm
    per = n_per_batch // tm
    sp = 2 * tm + ne * chunk
    grid_spec = pltpu.PrefetchScalarGridSpec(
        num_scalar_prefetch=4,
        grid=(nsteps,),
        in_specs=[pl.BlockSpec(memory_space=pl.ANY),
                  pl.BlockSpec((tm, d), lambda i, *_: (i, 0)),
                  pl.BlockSpec((tm, LANES), lambda i, *_: (i, 0)),
                  pl.BlockSpec((None, 1, d), lambda i, *_: (i // per, 0, 0)),
                  pl.BlockSpec((1, d), lambda i, *_: (0, 0)),
                  pl.BlockSpec((1, d), lambda i, *_: (0, 0))],
        out_specs=pl.BlockSpec((tm, d), lambda i, *_: (i, 0)),
        scratch_shapes=[pltpu.VMEM((2, sp * PACK_SUB, LANES), yb.dtype), pltpu.SemaphoreType.DMA((2,))],
    )
    return pl.pallas_call(
        functools.partial(_combine_kernel, ne=ne, chunk=chunk, nsteps=nsteps, alpha=alpha),
        out_shape=jax.ShapeDtypeStruct((ttot, d), F32),
        grid_spec=grid_spec,
        compiler_params=_cp(("arbitrary",), VMEM_LIMIT),
        name="moe_combine",
    )(seg_nch, seg_off, seg_src, tile_nch, yb, x, tokm, g2, lng.reshape(1, d), lnb.reshape(1, d))


def _hier_moe_layer(x1, t, g2, lng, lnb, wg, bg, wf, bf, w1, w3, w2, *, layer, alpha):
    b, n, d = x1.shape
    ttot = b * n
    ng = wg.shape[1]
    ne = wf.shape[1]
    epg = ne // ng
    tm = min(MOE_TILE, n)
    chunk = MOE_CHUNK
    nt = ttot // tm
    tflat = t.reshape(ttot, d)
    wcat = jnp.zeros((d, LANES), F32).at[:, 0:ng].set(wg).at[:, SUBLANES:SUBLANES + ne].set(wf).astype(BF16)
    bcat = jnp.zeros((1, LANES), F32).at[0, 0:ng].set(bg).at[0, SUBLANES:SUBLANES + ne].set(bf)
    ar = jnp.arange(2 * tm, dtype=I32)
    upper = (ar[:, None] < ar[None, :]).astype(BF16)
    lp, tokm, cnt = _route(tflat, wcat, bcat, upper, ng=ng, epg=epg, tm=tm, chunk=chunk)

    tile_cnt = cnt.reshape(ne, nt, LANES)[:, :, 0].T.astype(I32)
    seg_len = ((tile_cnt + 1) // 2) * 2
    counts = jnp.sum(seg_len, axis=0)
    padded = ((counts + chunk + MOE_ROWS - 1) // MOE_ROWS) * MOE_ROWS
    pend = jnp.cumsum(padded)
    pstart = pend - padded
    base = jnp.cumsum(seg_len, axis=0) - seg_len
    seg_row = (pstart[None, :] + base).astype(I32).reshape(nt * ne)
    nch = (tile_cnt + chunk - 1) // chunk
    seg_nch = nch.astype(I32).reshape(nt * ne)
    seg_off = ((jnp.cumsum(nch, axis=1) - nch) * chunk).astype(I32).reshape(nt * ne)
    tile_nch = jnp.sum(nch, axis=1).astype(I32)
    nb = -(-(2 * ttot + nt * ne + ne * (chunk + MOE_ROWS)) // MOE_ROWS)
    bstart = jnp.arange(nb, dtype=I32) * MOE_ROWS
    block_e = jnp.minimum(jnp.sum((pend[None, :] <= bstart[:, None]).astype(I32), axis=1), ne - 1)
    n_used = (pend[-1] // MOE_ROWS).astype(I32).reshape(1)

    xs = _dispatch(seg_nch, seg_off, seg_row, tile_nch, tflat, lp, n_rows=nb * MOE_ROWS, ne=ne, tm=tm, chunk=chunk)
    yb = _moe_experts(block_e.astype(I32), n_used, xs, w1, w3, w2, layer=layer)
    out = _combine(seg_nch, seg_off, seg_row, tile_nch, yb, x1.reshape(ttot, d), tokm, g2, lng, lnb,
                   ne=ne, tm=tm, chunk=chunk, n_per_batch=n, alpha=alpha)
    return out.reshape(b, n, d)


def kernel(x, c, ctx, c_ctx, ada_w, ada_b, ln_g, ln_b, ev_w_in, ev_conv_w, ev_conv_b, ev_gate_a_w,
           ev_gate_a_b, ev_gate_x_w, ev_gate_x_b, ev_lru_lambda, ev_da_lambda, ev_da_subln, ev_w_out,
           od_w_out, od_b_out, moe_wg, moe_bg, moe_wf, moe_bf, moe_w1, moe_w3, moe_w2):
    bsz, n_lat, d = x.shape
    depth = ada_w.shape[0]
    alpha = (2.0 * depth) ** 0.25
    lw = ev_conv_w.shape[-1]
    hd = ev_da_lambda.shape[-1]
    vd = ev_da_subln.shape[-1]
    aw = (ev_w_in.shape[-1] - 2 * lw) // 3
    heads = aw // vd
    fnet_groups = 4

    rows = ((bsz + 1 + SUBLANES - 1) // SUBLANES) * SUBLANES
    cond = jnp.zeros((rows, d), F32).at[0:bsz].set(c).at[bsz].set(c_ctx)
    ada = _ada_terms(cond, ada_w, ada_b).reshape(depth, rows, 6, d)

    def lat_term(l, k):
        return ada[l, 0:bsz, k, :].reshape(bsz, 1, d)

    def ctx_term(l, k):
        return jnp.broadcast_to(ada[l, bsz, k, :].reshape(1, 1, d), (bsz, 1, d))

    for l in range(depth):
        ctx_live = any(m % 2 == 0 for m in range(l + 1, depth))
        assert not ctx_live, "context stream update is only needed for depth > 2"
        sh1, sc1, g1, sh2, sc2, g2 = [lat_term(l, k) for k in range(6)]
        if l % 2 == 0:
            e = l // 2
            lam_init = 0.8 - 0.6 * math.exp(-0.3 * l)
            w_in_b = ev_w_in[e].astype(BF16)
            cos_t, sin_t = _rope_tables(n_lat, hd)
            qscale = hd ** -0.5 * math.log2(math.e)
            g_l, xr_l, q_l, k_l, v_l = _project_even(x, sh1, sc1, w_in_b, cos_t, sin_t, rope=True,
                                                     lw=lw, aw=aw, qscale=qscale, tm=512)
            n_ctx = ctx.shape[1]
            _, xr_c, _, k_c, v_c = _project_even(ctx, ctx_term(l, 0), ctx_term(l, 1), w_in_b,
                                                 cos_t[0:n_ctx], sin_t[0:n_ctx], rope=False,
                                                 lw=lw, aw=aw, qscale=qscale, tm=256)
            o_l = _diff_attention(q_l, k_c, k_l, v_c, v_l, ev_da_lambda[e], ev_da_subln[e],
                                  heads=heads, hd=hd, lam_init=lam_init)
            gwid = 256
            y = None
            for dirn, rev in ((0, False), (1, True)):
                wa_bd = _block_diag_groups(ev_gate_a_w[e, dirn], gwid)
                wx_bd = _block_diag_groups(ev_gate_x_w[e, dirn], gwid)
                common = (ev_conv_w[e], ev_conv_b[e], wa_bd, ev_gate_a_b[e, dirn], wx_bd,
                          ev_gate_x_b[e, dirn], ev_lru_lambda[e, dirn])
                h_zero = jnp.zeros((bsz, 1, lw), F32)
                _, h_fin = _rglru_dir(xr_c, None, *common, h_zero, reverse=rev, tn=256)
                y, _ = _rglru_dir(xr_l, y, *common, h_fin, reverse=rev, tn=256)
            x1, t = _outproj_even(y, g_l, o_l, x, ev_w_out[e].astype(BF16), g1, ln_g[l, 0], ln_b[l, 0],
                                  sh2, sc2, alpha=alpha, tm=512)
        else:
            o = l // 2
            gd = d // fnet_groups
            zr, zi = _chan_dft(x, sh1, sc1, _chan_dft_table(gd), groups=fnet_groups, tm=512)
            wv = _tok_fft(zr, zi)
            norm = 1.0 / math.sqrt(float(n_lat * gd))
            x1, t = _outproj_odd(wv, x, od_w_out[o].astype(BF16), od_b_out[o], g1, ln_g[l, 0], ln_b[l, 0],
                                 sh2, sc2, alpha=alpha, norm=norm, tm=512)
        x = _hier_moe_layer(x1, t, g2, ln_g[l, 1], ln_b[l, 1], moe_wg[l], moe_bg[l], moe_wf[l], moe_bf[l],
                            moe_w1, moe_w3, moe_w2, layer=l, alpha=alpha)
    return x
```

```python
import functools
import math

import numpy as np
import jax
import jax.numpy as jnp
from jax import lax
from jax.experimental import pallas as pl
from jax.experimental.pallas import tpu as pltpu

F32 = jnp.float32
BF16 = jnp.bfloat16
I32 = jnp.int32

LN_EPS = 1e-6
LRU_C = 8.0
ROPE_BASE = 10000.0
GRID_W = 64
CONV_W = 4
LANES = 128
SUBLANES = 8
MOE_ROWS = 512
MOE_TILE = 512
MOE_CHUNK = 16
VMEM_LIMIT = 56 * 1024 * 1024


def _cp(sem, vmem=None):
    return pltpu.CompilerParams(dimension_semantics=sem, vmem_limit_bytes=vmem)


def _ln(x):
    mu = jnp.mean(x, axis=-1, keepdims=True)
    xc = x - mu
    var = jnp.mean(xc * xc, axis=-1, keepdims=True)
    return xc * lax.rsqrt(var + LN_EPS)


def _silu(x):
    return x * jax.nn.sigmoid(x)


PACK_SUB = 4


def _tile_row(r, mult):
    if isinstance(r, int):
        return r * PACK_SUB
    return pl.multiple_of(r * PACK_SUB, mult)


def _pack_rows(val):
    half = val.shape[1] // 2
    lo = lax.bitcast_convert_type(val[:, 0:half], jnp.uint32) >> 16
    hi = lax.bitcast_convert_type(val[:, half:], jnp.uint32) & jnp.uint32(0xFFFF0000)
    return lo | hi


def _unpack_rows(words):
    lo = lax.bitcast_convert_type(words << 16, F32)
    hi = lax.bitcast_convert_type(words & jnp.uint32(0xFFFF0000), F32)
    return jnp.concatenate([lo, hi], axis=1).astype(BF16)


def _rows_from_tiles(ref, nrows):
    return jnp.concatenate([ref[pl.ds(k, nrows, stride=PACK_SUB), :] for k in range(PACK_SUB)], axis=1)


def _rows_to_tiles(ref, val):
    nrows = val.shape[0]
    for k in range(PACK_SUB):
        ref[pl.ds(k, nrows, stride=PACK_SUB), :] = val[:, k * LANES:(k + 1) * LANES]


def _ada_kernel(c_ref, w_ref, b_ref, o_ref):
    s = _silu(c_ref[...]).astype(BF16)
    o_ref[...] = jnp.dot(s, w_ref[...].astype(BF16), preferred_element_type=F32) + b_ref[...]


def _ada_terms(cond, ada_w, ada_b):
    nl, d, d6 = ada_w.shape
    r = cond.shape[0]
    tn = 1024
    return pl.pallas_call(
        _ada_kernel,
        out_shape=jax.ShapeDtypeStruct((nl, r, d6), F32),
        grid=(nl, d6 // tn),
        in_specs=[pl.BlockSpec((r, d), lambda l, j: (0, 0)),
                  pl.BlockSpec((None, d, tn), lambda l, j: (l, 0, j)),
                  pl.BlockSpec((None, 1, tn), lambda l, j: (l, 0, j))],
        out_specs=pl.BlockSpec((None, r, tn), lambda l, j: (l, 0, j)),
        compiler_params=_cp(("parallel", "parallel")),
        name="ada_terms",
    )(cond, ada_w, ada_b.reshape(nl, 1, d6))


def _rope_apply(x, cos, sin_signed):
    tm = x.shape[0]
    lane = lax.broadcasted_iota(I32, (tm, LANES), 1)
    first_half = (lane % 32) < 16
    outs = []
    for j in range(x.shape[1] // LANES):
        xh = x[:, j * LANES:(j + 1) * LANES]
        partner = jnp.where(first_half, pltpu.roll(xh, LANES - 16, 1), pltpu.roll(xh, 16, 1))
        outs.append(xh * cos + partner * sin_signed)
    return jnp.concatenate(outs, axis=1)


def _proj_kernel(x_ref, sh_ref, sc_ref, w_ref, cos_ref, sin_ref,
                 g_ref, xr_ref, q_ref, k_ref, v_ref, *, rope, lw, aw, qscale):
    h = _ln(x_ref[...]) * (1.0 + sc_ref[...]) + sh_ref[...]
    hb = h.astype(BF16)

    def mm(c0, c1):
        return jnp.dot(hb, w_ref[:, c0:c1], preferred_element_type=F32)

    g_ref[...] = mm(0, lw).astype(BF16)
    xr_ref[...] = mm(lw, 2 * lw)
    q = mm(2 * lw, 2 * lw + aw)
    k = mm(2 * lw + aw, 2 * lw + 2 * aw)
    if rope:
        q = _rope_apply(q, cos_ref[...], sin_ref[...])
        k = _rope_apply(k, cos_ref[...], sin_ref[...])
    q_ref[...] = (q * qscale).astype(BF16)
    k_ref[...] = k.astype(BF16)
    v_ref[...] = mm(2 * lw + 2 * aw, 2 * lw + 3 * aw).astype(BF16)


def _project_even(x, shift, scale, w_in_b, cos_t, sin_t, *, rope, lw, aw, qscale, tm):
    b, n, d = x.shape
    tm = min(tm, n)
    nin = w_in_b.shape[1]
    tok = lambda bi, i: (bi, i, 0)
    per_b = lambda bi, i: (bi, 0, 0)
    outs = (jax.ShapeDtypeStruct((b, n, lw), BF16), jax.ShapeDtypeStruct((b, n, lw), F32),
            jax.ShapeDtypeStruct((b, n, aw), BF16), jax.ShapeDtypeStruct((b, n, aw), BF16),
            jax.ShapeDtypeStruct((b, n, aw), BF16))
    return pl.pallas_call(
        functools.partial(_proj_kernel, rope=rope, lw=lw, aw=aw, qscale=qscale),
        out_shape=outs,
        grid=(b, n // tm),
        in_specs=[pl.BlockSpec((None, tm, d), tok),
                  pl.BlockSpec((None, 1, d), per_b),
                  pl.BlockSpec((None, 1, d), per_b),
                  pl.BlockSpec((d, nin), lambda bi, i: (0, 0)),
                  pl.BlockSpec((tm, LANES), lambda bi, i: (i, 0)),
                  pl.BlockSpec((tm, LANES), lambda bi, i: (i, 0))],
        out_specs=(pl.BlockSpec((None, tm, lw), tok), pl.BlockSpec((None, tm, lw), tok),
                   pl.BlockSpec((None, tm, aw), tok), pl.BlockSpec((None, tm, aw), tok),
                   pl.BlockSpec((None, tm, aw), tok)),
        compiler_params=_cp(("parallel", "parallel"), VMEM_LIMIT),
        name="proj_even_rope" if rope else "proj_even_ctx",
    )(x, shift, scale, w_in_b, cos_t, sin_t)


def _rope_tables(n_tok, head_dim):
    t = jnp.arange(n_tok)
    row = (t // GRID_W).astype(F32)
    col = (t % GRID_W).astype(F32)
    nf = head_dim // 4
    freqs = ROPE_BASE ** (-jnp.arange(nf, dtype=F32) / nf)
    lane = np.arange(LANES)
    within = lane % head_dim
    axis = within // (2 * nf)
    half = (within % (2 * nf)) // nf
    f = within % nf
    pos = jnp.where(jnp.asarray(axis)[None, :] == 0, row[:, None], col[:, None])
    ang = pos * freqs[jnp.asarray(f)][None, :]
    sign = jnp.asarray(np.where(half == 0, -1.0, 1.0), F32)[None, :]
    return jnp.cos(ang).astype(F32), (jnp.sin(ang) * sign).astype(F32)


def _attn_kernel(q_ref, kc_ref, kl_ref, vc_ref, vl_ref, dl_ref, gain_ref, o_ref, kbuf, vbuf, sbuf, ebuf, abuf,
                 cbuf, *, nc, nl, hd, lam_init, rows):
    @pl.when((pl.program_id(0) == 0) & (pl.program_id(1) == 0))
    def _():
        sbuf[...] = jnp.zeros_like(sbuf)
        abuf[...] = jnp.zeros_like(abuf)
        cbuf[...] = jnp.zeros_like(cbuf)

    kbuf[0:nc, :] = kc_ref[...]
    kbuf[nc:nc + nl, :] = kl_ref[...]
    vbuf[0:nc, :] = vc_ref[...]
    vbuf[nc:nc + nl, :] = vl_ref[...]

    lf = dl_ref[...]
    lam = (jnp.exp(jnp.sum(lf[0:1] * lf[1:2], axis=1, keepdims=True))
           - jnp.exp(jnp.sum(lf[2:3] * lf[3:4], axis=1, keepdims=True)) + lam_init)
    gain = gain_ref[...] * (1.0 - lam_init)
    n_sub = nl // rows
    lane = lax.broadcasted_iota(I32, (rows, 2 * hd), 1)
    nt = (((1,), (1,)), ((), ()))

    def stage_a(j, slot):
        r0 = pl.multiple_of(jnp.minimum(j, n_sub - 1) * rows, rows)
        q = q_ref[pl.ds(r0, rows), :]
        zero = jnp.zeros_like(q)
        kk = kbuf[...]
        sbuf[slot, 0] = lax.dot_general(jnp.where(lane < hd, q, zero), kk, nt, preferred_element_type=F32)
        sbuf[slot, 1] = lax.dot_general(jnp.where(lane >= hd, q, zero), kk, nt, preferred_element_type=F32)

    def stage_b(slot):
        cols = [slice(c, c + LANES) for c in range(0, nc + nl, LANES)]
        ls = []
        for k in range(2):
            pm = sbuf[slot, k, :, cols[0]]
            for cs in cols[1:]:
                pm = jnp.maximum(pm, sbuf[slot, k, :, cs])
            m = jnp.max(pm, axis=1, keepdims=True)
            acc = jnp.zeros((rows, LANES), F32)
            for cs in cols:
                e = jnp.exp2(sbuf[slot, k, :, cs] - m)
                acc = acc + e
                ebuf[slot, k, :, cs] = e.astype(BF16)
            ls.append(jnp.sum(acc, axis=1, keepdims=True))
        ratio = (lam * ls[0] / ls[1]).astype(BF16)
        abuf[slot] = ebuf[slot, 0] - ratio * ebuf[slot, 1]
        cbuf[slot] = jnp.broadcast_to(1.0 / ls[0], cbuf.shape[1:])

    def stage_c(j, slot):
        r0 = pl.multiple_of(jnp.maximum(j - 2, 0) * rows, rows)
        o = jnp.dot(abuf[slot], vbuf[...], preferred_element_type=F32) * cbuf[slot]
        o = o * lax.rsqrt(jnp.mean(o * o, axis=1, keepdims=True) + LN_EPS) * gain
        o_ref[pl.ds(r0, rows), :] = o.astype(BF16)

    def body(t, carry):
        j = 2 * t
        stage_a(j, 0)
        stage_b(1)
        stage_c(j, 0)
        stage_a(j + 1, 1)
        stage_b(0)
        stage_c(j + 1, 1)
        return carry

    lax.fori_loop(0, (n_sub + 2) // 2, body, 0)


def _diff_attention(q, k_ctx, k_lat, v_ctx, v_lat, da_lambda, subln, *, heads, hd, lam_init):
    b, n, aw = q.shape
    nc = k_ctx.shape[1]
    vd = aw // heads
    rows = min(128, n)
    assert (n // rows) % 2 == 0
    blk_q = pl.BlockSpec((None, n, vd), lambda bi, h: (bi, 0, h))
    blk_c = pl.BlockSpec((None, nc, vd), lambda bi, h: (bi, 0, h))
    return pl.pallas_call(
        functools.partial(_attn_kernel, nc=nc, nl=n, hd=hd, lam_init=lam_init, rows=rows),
        out_shape=jax.ShapeDtypeStruct((b, n, aw), BF16),
        grid=(b, heads),
        in_specs=[blk_q, blk_c, blk_q, blk_c, blk_q,
                  pl.BlockSpec(da_lambda.shape, lambda bi, h: (0, 0)),
                  pl.BlockSpec((1, vd), lambda bi, h: (0, 0))],
        out_specs=blk_q,
        scratch_shapes=[pltpu.VMEM((nc + n, vd), BF16), pltpu.VMEM((nc + n, vd), BF16),
                        pltpu.VMEM((2, 2, rows, nc + n), F32), pltpu.VMEM((2, 2, rows, nc + n), BF16),
                        pltpu.VMEM((2, rows, nc + n), BF16), pltpu.VMEM((2, rows, vd), F32)],
        compiler_params=_cp(("arbitrary", "arbitrary"), VMEM_LIMIT),
        name="diff_attention",
    )(q, k_ctx, k_lat, v_ctx, v_lat, da_lambda, subln.reshape(1, vd))


def _lru_kernel(*refs, reverse, add_prev, nt, groups):
    if add_prev:
        (xp_ref, xc_ref, xn_ref, yprev_ref, cw_ref, cb_ref, wa_ref, ba_ref, wx_ref, bx_ref,
         lam_ref, h0_ref, y_ref, hf_ref, carry, ext) = refs
    else:
        (xp_ref, xc_ref, xn_ref, cw_ref, cb_ref, wa_ref, ba_ref, wx_ref, bx_ref,
         lam_ref, h0_ref, y_ref, hf_ref, carry, ext) = refs
        yprev_ref = None
    i = pl.program_id(1)
    ti = (nt - 1 - i) if reverse else i
    tn, w = xc_ref.shape

    @pl.when(i == 0)
    def _():
        carry[...] = h0_ref[...]

    ext[0:SUBLANES, :] = jnp.where(ti > 0, xp_ref[...], 0.0)
    ext[SUBLANES:SUBLANES + tn, :] = xc_ref[...]
    ext[SUBLANES + tn:2 * SUBLANES + tn, :] = jnp.where(ti < nt - 1, xn_ref[...], 0.0)
    left = CONV_W // 2
    xc = cb_ref[...]
    for k in range(CONV_W):
        off = SUBLANES - left + k
        xc = xc + ext[off:off + tn, :] * cw_ref[k:k + 1, :]

    xb = xc.astype(BF16)
    gw = w // groups

    def gate(w_ref, b_ref):
        parts = [jnp.dot(xb[:, g * gw:(g + 1) * gw], w_ref[g], preferred_element_type=F32)
                 for g in range(groups)]
        return jax.nn.sigmoid(jnp.concatenate(parts, axis=1) + b_ref[...])

    r = gate(wa_ref, ba_ref)
    ig = gate(wx_ref, bx_ref)
    log_a = (-LRU_C * jax.nn.softplus(-lam_ref[...])) * r
    a = jnp.exp(log_a)
    bcoef = jnp.sqrt(-jnp.tanh(log_a) * (a * a + 1.0)) * ig * xc

    ngroups = tn // SUBLANES
    a = a.reshape(ngroups, SUBLANES, w)
    bcoef = bcoef.reshape(ngroups, SUBLANES, w)
    row = lax.broadcasted_iota(I32, (ngroups, SUBLANES, w), 1)
    d = 1
    while d < SUBLANES:
        shift = (SUBLANES - d) if reverse else d
        a_sh = pltpu.roll(a, shift, 1)
        b_sh = pltpu.roll(bcoef, shift, 1)
        live = (row < SUBLANES - d) if reverse else (row >= d)
        bcoef = jnp.where(live, a * b_sh + bcoef, bcoef)
        a = jnp.where(live, a * a_sh, a)
        d *= 2
    hc = carry[...]
    for g in (range(ngroups - 1, -1, -1) if reverse else range(ngroups)):
        rs = slice(g * SUBLANES, (g + 1) * SUBLANES)
        h = a[g] * hc + bcoef[g]
        hc = h[0:1, :] if reverse else h[SUBLANES - 1:SUBLANES, :]
        y_ref[rs, :] = (yprev_ref[rs, :] + h) if add_prev else h
    carry[...] = hc
    hf_ref[...] = hc


def _rglru_dir(xr, y_prev, conv_w, conv_b, wa_bd, ba, wx_bd, bx, lam, h0, *, reverse, tn):
    b, n, w = xr.shape
    tn = min(tn, n)
    nt = n // tn
    groups = wa_bd.shape[0]
    nb8 = n // SUBLANES
    per8 = tn // SUBLANES

    def tmap(i):
        return (nt - 1 - i) if reverse else i

    cur = pl.BlockSpec((None, tn, w), lambda bi, i: (bi, tmap(i), 0))
    halo_p = pl.BlockSpec((None, SUBLANES, w), lambda bi, i: (bi, jnp.maximum(tmap(i) * per8 - 1, 0), 0))
    halo_n = pl.BlockSpec((None, SUBLANES, w), lambda bi, i: (bi, jnp.minimum((tmap(i) + 1) * per8, nb8 - 1), 0))
    row_w = pl.BlockSpec((1, w), lambda bi, i: (0, 0))
    per_b = pl.BlockSpec((None, 1, w), lambda bi, i: (bi, 0, 0))
    gate_w = pl.BlockSpec(wa_bd.shape, lambda bi, i: (0, 0, 0))
    add_prev = y_prev is not None
    in_specs = [halo_p, cur, halo_n] + ([cur] if add_prev else []) + [
        pl.BlockSpec((CONV_W, w), lambda bi, i: (0, 0)), row_w, gate_w, row_w, gate_w, row_w, row_w, per_b]
    args = [xr, xr, xr] + ([y_prev] if add_prev else []) + [
        conv_w, conv_b.reshape(1, w), wa_bd, ba.reshape(1, w), wx_bd, bx.reshape(1, w),
        lam.reshape(1, w), h0]
    return pl.pallas_call(
        functools.partial(_lru_kernel, reverse=reverse, add_prev=add_prev, nt=nt, groups=groups),
        out_shape=(jax.ShapeDtypeStruct((b, n, w), F32), jax.ShapeDtypeStruct((b, 1, w), F32)),
        grid=(b, nt),
        in_specs=in_specs,
        out_specs=(cur, per_b),
        scratch_shapes=[pltpu.VMEM((1, w), F32), pltpu.VMEM((tn + 2 * SUBLANES, w), F32)],
        compiler_params=_cp(("parallel", "arbitrary"), VMEM_LIMIT),
        name="rglru_rev" if reverse else "rglru_fwd",
    )(*args)


def _block_diag_groups(wh, group_width):
    heads, blk, _ = wh.shape
    per = group_width // blk
    groups = heads // per
    whg = wh.reshape(groups, per, blk, blk)
    eye = jnp.eye(per, dtype=wh.dtype)
    bd = jnp.einsum('gpij,pq->gpiqj', whg, eye).reshape(groups, group_width, group_width)
    return bd.astype(BF16)


def _residual_ln_mod(x, y, g1, lng, lnb, sh2, sc2, alpha):
    x1 = _ln(alpha * x + g1 * y) * lng + lnb
    t = _ln(x1) * (1.0 + sc2) + sh2
    return x1, t


def _outproj_even_kernel(r_ref, g_ref, o_ref, x_ref, w_ref, g1_ref, lng_ref, lnb_ref, sh2_ref, sc2_ref,
                         x1_ref, t_ref, *, lw, alpha):
    z = (r_ref[...] * jax.nn.gelu(g_ref[...].astype(F32))).astype(BF16)
    y = (jnp.dot(z, w_ref[0:lw, :], preferred_element_type=F32)
         + jnp.dot(o_ref[...], w_ref[lw:, :], preferred_element_type=F32))
    x1, t = _residual_ln_mod(x_ref[...], y, g1_ref[...], lng_ref[...], lnb_ref[...],
                             sh2_ref[...], sc2_ref[...], alpha)
    x1_ref[...] = x1
    t_ref[...] = t.astype(BF16)


def _outproj_even(r, g, o, x, w_out_b, g1, lng, lnb, sh2, sc2, *, alpha, tm):
    b, n, d = x.shape
    lw = r.shape[2]
    aw = o.shape[2]
    tm = min(tm, n)
    tok = lambda bi, i: (bi, i, 0)
    per_b = pl.BlockSpec((None, 1, d), lambda bi, i: (bi, 0, 0))
    row = pl.BlockSpec((1, d), lambda bi, i: (0, 0))
    return pl.pallas_call(
        functools.partial(_outproj_even_kernel, lw=lw, alpha=alpha),
        out_shape=(jax.ShapeDtypeStruct((b, n, d), F32), jax.ShapeDtypeStruct((b, n, d), BF16)),
        grid=(b, n // tm),
        in_specs=[pl.BlockSpec((None, tm, lw), tok), pl.BlockSpec((None, tm, lw), tok),
                  pl.BlockSpec((None, tm, aw), tok), pl.BlockSpec((None, tm, d), tok),
                  pl.BlockSpec(w_out_b.shape, lambda bi, i: (0, 0)),
                  per_b, row, row, per_b, per_b],
        out_specs=(pl.BlockSpec((None, tm, d), tok), pl.BlockSpec((None, tm, d), tok)),
        compiler_params=_cp(("parallel", "parallel"), VMEM_LIMIT),
        name="outproj_even",
    )(r, g, o, x, w_out_b, g1, lng.reshape(1, d), lnb.reshape(1, d), sh2, sc2)


def _chan_dft_kernel(x_ref, sh_ref, sc_ref, tab_ref, zr_ref, zi_ref, *, groups, gd):
    h = (_ln(x_ref[...]) * (1.0 + sc_ref[...]) + sh_ref[...]).astype(BF16)
    for g in range(groups):
        z = jnp.dot(h[:, g * gd:(g + 1) * gd], tab_ref[...], preferred_element_type=F32)
        zr_ref[:, g * gd:(g + 1) * gd] = z[:, 0:gd].astype(BF16)
        zi_ref[:, g * gd:(g + 1) * gd] = z[:, gd:2 * gd].astype(BF16)


def _chan_dft(x, shift, scale, tab, *, groups, tm):
    b, n, d = x.shape
    gd = d // groups
    tm = min(tm, n)
    tok = lambda bi, i: (bi, i, 0)
    per_b = pl.BlockSpec((None, 1, d), lambda bi, i: (bi, 0, 0))
    return pl.pallas_call(
        functools.partial(_chan_dft_kernel, groups=groups, gd=gd),
        out_shape=(jax.ShapeDtypeStruct((b, n, d), BF16), jax.ShapeDtypeStruct((b, n, d), BF16)),
        grid=(b, n // tm),
        in_specs=[pl.BlockSpec((None, tm, d), tok), per_b, per_b,
                  pl.BlockSpec(tab.shape, lambda bi, i: (0, 0))],
        out_specs=(pl.BlockSpec((None, tm, d), tok), pl.BlockSpec((None, tm, d), tok)),
        compiler_params=_cp(("parallel", "parallel"), VMEM_LIMIT),
        name="chan_dft",
    )(x, shift, scale, tab)


FFT_C = 64


def _fft_pitch(group):
    p = -(-group // SUBLANES)
    return SUBLANES * (p if p % 2 else p + 1)


def _tok_fft_kernel(zr_ref, zi_ref, m1_ref, m3_ref, tc_ref, ts_ref, o_ref, zsr, zsi, asr, asi, ob, *, nr):
    c_len = FFT_C
    pz = zsr.shape[0] // nr
    pa = asr.shape[0] // c_len
    for r in range(nr):
        zsr[pz * r:pz * r + c_len, :] = zr_ref[c_len * r:c_len * (r + 1), :].astype(F32)
        zsi[pz * r:pz * r + c_len, :] = zi_ref[c_len * r:c_len * (r + 1), :].astype(F32)
    m1 = m1_ref[...]
    for c in range(c_len):
        x2 = jnp.concatenate([zsr[pl.ds(c, nr, stride=pz), :], zsi[pl.ds(c, nr, stride=pz), :]], axis=0)
        a2 = jnp.dot(m1, x2.astype(BF16), preferred_element_type=F32)
        ar, ai = a2[0:nr], a2[nr:2 * nr]
        tcv = tc_ref[c * nr:(c + 1) * nr, :]
        tsv = ts_ref[c * nr:(c + 1) * nr, :]
        asr[pa * c:pa * c + nr, :] = ar * tcv + ai * tsv
        asi[pa * c:pa * c + nr, :] = ai * tcv - ar * tsv
    m3 = m3_ref[...]
    for k1 in range(nr):
        y2 = jnp.concatenate([asr[pl.ds(k1, c_len, stride=pa), :], asi[pl.ds(k1, c_len, stride=pa), :]], axis=0)
        ob[pl.ds(k1, c_len, stride=pa), :] = jnp.dot(m3, y2.astype(BF16), preferred_element_type=F32)
    for k2 in range(c_len):
        o_ref[nr * k2:nr * (k2 + 1), :] = ob[pa * k2:pa * k2 + nr, :].astype(BF16)


def _tok_fft(zr, zi):
    b, n, d = zr.shape
    nr = n // FFT_C
    pz = _fft_pitch(FFT_C)
    pa = _fft_pitch(nr)
    kr = np.arange(nr, dtype=np.float64)
    ang_r = 2.0 * np.pi * (np.outer(kr, kr) % nr) / nr
    cr, sr = np.cos(ang_r), np.sin(ang_r)
    m1 = jnp.asarray(np.block([[cr, -sr], [-sr, -cr]]), F32).astype(BF16)
    kc = np.arange(FFT_C, dtype=np.float64)
    ang_c = 2.0 * np.pi * (np.outer(kc, kc) % FFT_C) / FFT_C
    m3 = jnp.asarray(np.concatenate([np.cos(ang_c), np.sin(ang_c)], axis=1), F32).astype(BF16)
    ang_t = 2.0 * np.pi * (np.outer(kc, kr) % n) / n
    tc = jnp.broadcast_to(jnp.asarray(np.cos(ang_t).reshape(FFT_C * nr, 1), F32), (FFT_C * nr, LANES))
    ts = jnp.broadcast_to(jnp.asarray(np.sin(ang_t).reshape(FFT_C * nr, 1), F32), (FFT_C * nr, LANES))
    slab = pl.BlockSpec((None, n, LANES), lambda bi, l: (bi, 0, l))
    const = lambda a: pl.BlockSpec(a.shape, lambda bi, l: (0, 0))
    return pl.pallas_call(
        functools.partial(_tok_fft_kernel, nr=nr),
        out_shape=jax.ShapeDtypeStruct((b, n, d), BF16),
        grid=(b, d // LANES),
        in_specs=[slab, slab, const(m1), const(m3), const(tc), const(ts)],
        out_specs=slab,
        scratch_shapes=[pltpu.VMEM((nr * pz, LANES), F32), pltpu.VMEM((nr * pz, LANES), F32),
                        pltpu.VMEM((FFT_C * pa, LANES), F32), pltpu.VMEM((FFT_C * pa, LANES), F32),
                        pltpu.VMEM((FFT_C * pa, LANES), F32)],
        compiler_params=_cp(("parallel", "parallel"), VMEM_LIMIT),
        name="tok_fft",
    )(zr, zi, m1, m3, tc, ts)


def _chan_dft_table(gd):
    c = np.arange(gd, dtype=np.float64)
    ang_c = 2.0 * np.pi * (np.outer(c, c) % gd) / gd
    return jnp.asarray(np.concatenate([np.cos(ang_c), np.sin(ang_c)], axis=1), F32).astype(BF16)


def _outproj_odd_kernel(wv_ref, x_ref, w_ref, b_ref, g1_ref, lng_ref, lnb_ref, sh2_ref, sc2_ref,
                        x1_ref, t_ref, *, alpha, norm):
    y = jnp.dot(wv_ref[...], w_ref[...], preferred_element_type=F32) * norm + b_ref[...]
    x1, t = _residual_ln_mod(x_ref[...], y, g1_ref[...], lng_ref[...], lnb_ref[...],
                             sh2_ref[...], sc2_ref[...], alpha)
    x1_ref[...] = x1
    t_ref[...] = t.astype(BF16)


def _outproj_odd(wv, x, w_b, bias, g1, lng, lnb, sh2, sc2, *, alpha, norm, tm):
    b, n, d = x.shape
    tm = min(tm, n)
    tok = lambda bi, i: (bi, i, 0)
    per_b = pl.BlockSpec((None, 1, d), lambda bi, i: (bi, 0, 0))
    row = pl.BlockSpec((1, d), lambda bi, i: (0, 0))
    return pl.pallas_call(
        functools.partial(_outproj_odd_kernel, alpha=alpha, norm=norm),
        out_shape=(jax.ShapeDtypeStruct((b, n, d), F32), jax.ShapeDtypeStruct((b, n, d), BF16)),
        grid=(b, n // tm),
        in_specs=[pl.BlockSpec((None, tm, d), tok), pl.BlockSpec((None, tm, d), tok),
                  pl.BlockSpec(w_b.shape, lambda bi, i: (0, 0)), row,
                  per_b, row, row, per_b, per_b],
        out_specs=(pl.BlockSpec((None, tm, d), tok), pl.BlockSpec((None, tm, d), tok)),
        compiler_params=_cp(("parallel", "parallel"), VMEM_LIMIT),
        name="outproj_odd",
    )(wv, x, w_b, bias.reshape(1, d), g1, lng.reshape(1, d), lnb.reshape(1, d), sh2, sc2)


def _route_kernel(t_ref, w_ref, b_ref, up_ref, lp_ref, tokm_ref, cnt_ref, *, ng, epg, chunk):
    tm = t_ref.shape[0]
    ne = ng * epg
    logits = jnp.dot(t_ref[...].astype(BF16), w_ref[...], preferred_element_type=F32) + b_ref[...]
    lt = logits.T
    best = lt[0:1, :]
    bi = jnp.zeros((1, tm), I32)
    for k in range(1, ng):
        gk = lt[k:k + 1, :]
        upd = gk > best
        bi = jnp.where(upd, k, bi)
        best = jnp.where(upd, gk, best)
    den = jnp.zeros((1, tm), F32)
    for k in range(ng):
        den = den + jnp.exp(lt[k:k + 1, :] - best)
    p_g = 1.0 / den
    fsel = lt[SUBLANES:SUBLANES + epg, :]
    for k in range(1, ng):
        fsel = jnp.where(bi == k, lt[SUBLANES + k * epg:SUBLANES + (k + 1) * epg, :], fsel)
    neg = jnp.full((1, tm), -jnp.inf, F32)
    m1, m2 = neg, neg
    i1 = jnp.zeros((1, tm), I32)
    i2 = jnp.zeros((1, tm), I32)
    for j in range(epg):
        v = fsel[j:j + 1, :]
        gt1 = v > m1
        gt2 = v > m2
        m2 = jnp.where(gt1, m1, jnp.where(gt2, v, m2))
        i2 = jnp.where(gt1, i1, jnp.where(gt2, j, i2))
        m1 = jnp.where(gt1, v, m1)
        i1 = jnp.where(gt1, j, i1)
    e21 = jnp.exp(m2 - m1)
    w1 = p_g / (1.0 + e21)
    w2 = p_g * e21 / (1.0 + e21)
    e1 = bi * epg + i1
    e2 = bi * epg + i2

    e = jnp.concatenate([e1, e2], axis=1)
    rows = lax.broadcasted_iota(I32, (ne, 2 * tm), 0)
    onehot = jnp.where(rows == e, 1.0, 0.0)
    before = jnp.dot(onehot.astype(BF16), up_ref[...], preferred_element_type=F32)
    tot = jnp.sum(onehot, axis=1, keepdims=True)
    slots = jnp.floor((tot + (chunk - 1.0)) * (1.0 / chunk)) * chunk
    slots_b = jnp.broadcast_to(slots, (ne, LANES))
    rowe = lax.broadcasted_iota(I32, (ne, LANES), 0)
    incl = slots_b
    d = 1
    while d < ne:
        incl = incl + jnp.where(rowe >= d, pltpu.roll(incl, d, 0), 0.0)
        d *= 2
    seg_off = jnp.tile(incl - slots_b, (1, 2 * tm // LANES))
    lpos = jnp.sum(onehot * (before + seg_off), axis=0, keepdims=True)
    lp0 = lpos[:, 0:tm]
    lp1 = lpos[:, tm:2 * tm]
    row8 = lax.broadcasted_iota(I32, (SUBLANES, tm), 0)
    lp_ref[...] = jnp.where(row8 == 0, lp0, jnp.where(row8 == 1, lp1, 0.0)).astype(I32)
    rowl = lax.broadcasted_iota(I32, (LANES, tm), 0)
    tokm = jnp.where(rowl == 0, w1, jnp.where(rowl == 1, w2, jnp.where(rowl == 2, lp0, jnp.where(rowl == 3, lp1, 0.0))))
    tokm_ref[...] = tokm.T
    cnt_ref[...] = jnp.broadcast_to(tot, (ne, LANES))


def _route(t, wcat_b, bcat, upper, *, ng, epg, tm, chunk):
    tt, d = t.shape
    ne = ng * epg
    nt = tt // tm
    return pl.pallas_call(
        functools.partial(_route_kernel, ng=ng, epg=epg, chunk=chunk),
        out_shape=(jax.ShapeDtypeStruct((SUBLANES, tt), I32), jax.ShapeDtypeStruct((tt, LANES), F32),
                   jax.ShapeDtypeStruct((ne, nt * LANES), F32)),
        grid=(nt,),
        in_specs=[pl.BlockSpec((tm, d), lambda i: (i, 0)),
                  pl.BlockSpec((d, LANES), lambda i: (0, 0)),
                  pl.BlockSpec((1, LANES), lambda i: (0, 0)),
                  pl.BlockSpec(upper.shape, lambda i: (0, 0))],
        out_specs=(pl.BlockSpec((SUBLANES, tm), lambda i: (0, i)),
                   pl.BlockSpec((tm, LANES), lambda i: (i, 0)),
                   pl.BlockSpec((ne, LANES), lambda i: (0, i))),
        compiler_params=_cp(("parallel",), VMEM_LIMIT),
        name="route_sort",
    )(t, wcat_b, bcat, upper)


def _dispatch_kernel(row_ref, tot_ref, t_ref, lp_ref, xs_in_ref, xs_ref, stage, sems, *, qmax, chunk, nsteps):
    del xs_in_ref
    i = pl.program_id(0)
    slot = i % 2
    sp = stage.shape[1] // PACK_SUB
    tm = t_ref.shape[0]
    crow = chunk * PACK_SUB

    lp = lp_ref[...]
    prow = lax.broadcasted_iota(I32, (sp, tm), 0)
    perm = jnp.where(prow == lp[0:1, :], 1.0, jnp.where(prow == lp[1:2, :], 1.0, 0.0)).astype(BF16)
    srt = jnp.dot(perm, t_ref[...], preferred_element_type=F32)
    _rows_to_tiles(stage.at[slot], _pack_rows(srt))

    def chunk_copy(sl, src, dst):
        return pltpu.make_async_copy(stage.at[sl, pl.ds(_tile_row(src, crow), crow)],
                                     xs_ref.at[pl.ds(_tile_row(dst, SUBLANES), crow)], sems.at[sl])

    def drain(sl, n):
        def w(c, carry):
            chunk_copy(sl, 0, 0).wait()
            return carry
        lax.fori_loop(0, n, w, 0)

    @pl.when(i >= 1)
    def _():
        drain(1 - slot, tot_ref[jnp.maximum(i - 1, 0)])

    def issue(q, carry):
        chunk_copy(slot, pl.multiple_of(q * chunk, chunk), row_ref[i * qmax + q]).start()
        return carry
    lax.fori_loop(0, tot_ref[i], issue, 0)

    @pl.when(i == nsteps - 1)
    def _():
        drain(slot, tot_ref[i])


def _dispatch(chunk_row, tile_nch, t, lp, *, n_rows, ne, tm, chunk):
    ttot, d = t.shape
    nsteps = ttot // tm
    sp = 2 * tm + ne * chunk
    xs_init = jnp.zeros((n_rows * PACK_SUB, LANES), jnp.uint32)
    grid_spec = pltpu.PrefetchScalarGridSpec(
        num_scalar_prefetch=2,
        grid=(nsteps,),
        in_specs=[pl.BlockSpec((tm, d), lambda i, *_: (i, 0)),
                  pl.BlockSpec((SUBLANES, tm), lambda i, *_: (0, i)),
                  pl.BlockSpec(memory_space=pl.ANY)],
        out_specs=pl.BlockSpec(memory_space=pl.ANY),
        scratch_shapes=[pltpu.VMEM((2, sp * PACK_SUB, LANES), jnp.uint32), pltpu.SemaphoreType.DMA((2,))],
    )
    assert d == 2 * PACK_SUB * LANES
    return pl.pallas_call(
        functools.partial(_dispatch_kernel, qmax=chunk_row.shape[0] // nsteps, chunk=chunk, nsteps=nsteps),
        out_shape=jax.ShapeDtypeStruct(xs_init.shape, xs_init.dtype),
        grid_spec=grid_spec,
        input_output_aliases={4: 0},
        compiler_params=_cp(("arbitrary",), VMEM_LIMIT),
        name="moe_dispatch",
    )(chunk_row, tile_nch, t, lp, xs_init)


def _moe_kernel(be_ref, nu_ref, xs_ref, w1_ref, w3_ref, w2_ref, y_ref, w13b, w2b, *, ff):
    i = pl.program_id(0)

    @pl.when(i < nu_ref[0])
    def _():
        prev = be_ref[jnp.maximum(i - 1, 0)]

        @pl.when((i == 0) | (be_ref[i] != prev))
        def _():
            w13b[:, 0:ff] = w1_ref[...].astype(BF16)
            w13b[:, ff:2 * ff] = w3_ref[...].astype(BF16)
            w2b[...] = w2_ref[...].astype(BF16)

        x = _unpack_rows(_rows_from_tiles(xs_ref, MOE_ROWS))
        h = jnp.dot(x, w13b[...], preferred_element_type=F32)
        hid = (_silu(h[:, 0:ff]) * h[:, ff:2 * ff]).astype(BF16)
        y = jnp.dot(hid, w2b[...], preferred_element_type=F32)
        _rows_to_tiles(y_ref, _pack_rows(y.astype(BF16).astype(F32)))

    @pl.when(i >= nu_ref[0])
    def _():
        y_ref[...] = jnp.zeros_like(y_ref)


def _moe_experts(block_e, n_used, xs, w1, w3, w2, *, layer):
    d = w1.shape[-2]
    ff = w1.shape[-1]
    blk = MOE_ROWS * PACK_SUB
    nb = xs.shape[0] // blk
    wmap = lambda i, be, nu: (layer, be[i], 0, 0)
    grid_spec = pltpu.PrefetchScalarGridSpec(
        num_scalar_prefetch=2,
        grid=(nb,),
        in_specs=[pl.BlockSpec((blk, LANES), lambda i, be, nu: (i, 0)),
                  pl.BlockSpec((None, None, d, ff), wmap),
                  pl.BlockSpec((None, None, d, ff), wmap),
                  pl.BlockSpec((None, None, ff, d), wmap)],
        out_specs=pl.BlockSpec((blk, LANES), lambda i, be, nu: (i, 0)),
        scratch_shapes=[pltpu.VMEM((d, 2 * ff), BF16), pltpu.VMEM((ff, d), BF16)],
    )
    return pl.pallas_call(
        functools.partial(_moe_kernel, ff=ff),
        out_shape=jax.ShapeDtypeStruct(xs.shape, xs.dtype),
        grid_spec=grid_spec,
        compiler_params=_cp(("arbitrary",), VMEM_LIMIT),
        name="moe_experts",
    )(block_e, n_used, xs, w1, w3, w2)


def _combine_kernel(row_ref, tot_ref, yb_ref, x_ref, tokm_ref, g2_ref, lng_ref, lnb_ref,
                    o_ref, stage, sems, *, qmax, chunk, nsteps, alpha):
    i = pl.program_id(0)
    slot = i % 2
    sp = stage.shape[1] // PACK_SUB
    tm = x_ref.shape[0]
    crow = chunk * PACK_SUB

    def chunk_copy(sl, src, dst):
        return pltpu.make_async_copy(yb_ref.at[pl.ds(_tile_row(src, SUBLANES), crow)],
                                     stage.at[sl, pl.ds(_tile_row(dst, crow), crow)], sems.at[sl])

    def issue_tile(step, sl):
        def issue(q, carry):
            chunk_copy(sl, row_ref[step * qmax + q], pl.multiple_of(q * chunk, chunk)).start()
            return carry
        lax.fori_loop(0, tot_ref[step], issue, 0)

    @pl.when(i == 0)
    def _():
        stage[...] = jnp.zeros_like(stage)
        issue_tile(0, 0)

    @pl.when(i + 1 < nsteps)
    def _():
        issue_tile(jnp.minimum(i + 1, nsteps - 1), 1 - slot)

    def w(c, carry):
        chunk_copy(slot, 0, 0).wait()
        return carry
    lax.fori_loop(0, tot_ref[i], w, 0)

    tk = tokm_ref[...]
    pos = lax.broadcasted_iota(I32, (tm, sp), 1).astype(F32)
    st = _unpack_rows(_rows_from_tiles(stage.at[slot], sp))
    g0 = jnp.dot(jnp.where(pos == tk[:, 2:3], 1.0, 0.0).astype(BF16), st, preferred_element_type=F32)
    g1 = jnp.dot(jnp.where(pos == tk[:, 3:4], 1.0, 0.0).astype(BF16), st, preferred_element_type=F32)
    m = tk[:, 0:1] * g0 + tk[:, 1:2] * g1
    o_ref[...] = _ln(alpha * x_ref[...] + g2_ref[...] * m) * lng_ref[...] + lnb_ref[...]


def _combine(chunk_row, tile_nch, yb, x, tokm, g2, lng, lnb, *, ne, tm, chunk, n_per_batch, alpha):
    ttot, d = x.shape
    nsteps = ttot // tm
    per = n_per_batch // tm
    sp = 2 * tm + ne * chunk
    grid_spec = pltpu.PrefetchScalarGridSpec(
        num_scalar_prefetch=2,
        grid=(nsteps,),
        in_specs=[pl.BlockSpec(memory_space=pl.ANY),
                  pl.BlockSpec((tm, d), lambda i, *_: (i, 0)),
                  pl.BlockSpec((tm, LANES), lambda i, *_: (i, 0)),
                  pl.BlockSpec((None, 1, d), lambda i, *_: (i // per, 0, 0)),
                  pl.BlockSpec((1, d), lambda i, *_: (0, 0)),
                  pl.BlockSpec((1, d), lambda i, *_: (0, 0))],
        out_specs=pl.BlockSpec((tm, d), lambda i, *_: (i, 0)),
        scratch_shapes=[pltpu.VMEM((2, sp * PACK_SUB, LANES), yb.dtype), pltpu.SemaphoreType.DMA((2,))],
    )
    return pl.pallas_call(
        functools.partial(_combine_kernel, qmax=chunk_row.shape[0] // nsteps, chunk=chunk, nsteps=nsteps,
                          alpha=alpha),
        out_shape=jax.ShapeDtypeStruct((ttot, d), F32),
        grid_spec=grid_spec,
        compiler_params=_cp(("arbitrary",), VMEM_LIMIT),
        name="moe_combine",
    )(chunk_row, tile_nch, yb, x, tokm, g2, lng.reshape(1, d), lnb.reshape(1, d))


def _hier_moe_layer(x1, t, g2, lng, lnb, wg, bg, wf, bf, w1, w3, w2, *, layer, alpha):
    b, n, d = x1.shape
    ttot = b * n
    ng = wg.shape[1]
    ne = wf.shape[1]
    epg = ne // ng
    tm = min(MOE_TILE, n)
    chunk = MOE_CHUNK
    nt = ttot // tm
    tflat = t.reshape(ttot, d)
    wcat = jnp.zeros((d, LANES), F32).at[:, 0:ng].set(wg).at[:, SUBLANES:SUBLANES + ne].set(wf).astype(BF16)
    bcat = jnp.zeros((1, LANES), F32).at[0, 0:ng].set(bg).at[0, SUBLANES:SUBLANES + ne].set(bf)
    ar = jnp.arange(2 * tm, dtype=I32)
    upper = (ar[:, None] < ar[None, :]).astype(BF16)
    lp, tokm, cnt = _route(tflat, wcat, bcat, upper, ng=ng, epg=epg, tm=tm, chunk=chunk)

    tile_cnt = cnt.reshape(ne, nt, LANES)[:, :, 0].T.astype(I32)
    seg_len = ((tile_cnt + 1) // 2) * 2
    counts = jnp.sum(seg_len, axis=0)
    padded = ((counts + chunk + MOE_ROWS - 1) // MOE_ROWS) * MOE_ROWS
    pend = jnp.cumsum(padded)
    pstart = pend - padded
    base = jnp.cumsum(seg_len, axis=0) - seg_len
    seg_row = pstart[None, :] + base
    nch = (tile_cnt + chunk - 1) // chunk
    nch_end = jnp.cumsum(nch, axis=1)
    tile_nch = nch_end[:, -1].astype(I32)
    qmax = 2 * tm // chunk + ne
    qs = jnp.arange(qmax, dtype=I32)
    e_q = jnp.minimum(jnp.sum((nch_end[:, None, :] <= qs[None, :, None]).astype(I32), axis=2), ne - 1)
    c_q = qs[None, :] - jnp.take_along_axis(nch_end - nch, e_q, axis=1)
    chunk_row = (jnp.take_along_axis(seg_row, e_q, axis=1) + c_q * chunk).astype(I32).reshape(nt * qmax)
    nb = -(-(2 * ttot + nt * ne + ne * (chunk + MOE_ROWS)) // MOE_ROWS)
    bstart = jnp.arange(nb, dtype=I32) * MOE_ROWS
    block_e = jnp.minimum(jnp.sum((pend[None, :] <= bstart[:, None]).astype(I32), axis=1), ne - 1)
    n_used = (pend[-1] // MOE_ROWS).astype(I32).reshape(1)

    xs = _dispatch(chunk_row, tile_nch, tflat, lp, n_rows=nb * MOE_ROWS, ne=ne, tm=tm, chunk=chunk)
    yb = _moe_experts(block_e.astype(I32), n_used, xs, w1, w3, w2, layer=layer)
    out = _combine(chunk_row, tile_nch, yb, x1.reshape(ttot, d), tokm, g2, lng, lnb,
                   ne=ne, tm=tm, chunk=chunk, n_per_batch=n, alpha=alpha)
    return out.reshape(b, n, d)


def kernel(x, c, ctx, c_ctx, ada_w, ada_b, ln_g, ln_b, ev_w_in, ev_conv_w, ev_conv_b, ev_gate_a_w,
           ev_gate_a_b, ev_gate_x_w, ev_gate_x_b, ev_lru_lambda, ev_da_lambda, ev_da_subln, ev_w_out,
           od_w_out, od_b_out, moe_wg, moe_bg, moe_wf, moe_bf, moe_w1, moe_w3, moe_w2):
    bsz, n_lat, d = x.shape
    depth = ada_w.shape[0]
    alpha = (2.0 * depth) ** 0.25
    lw = ev_conv_w.shape[-1]
    hd = ev_da_lambda.shape[-1]
    vd = ev_da_subln.shape[-1]
    aw = (ev_w_in.shape[-1] - 2 * lw) // 3
    heads = aw // vd
    fnet_groups = 4

    rows = ((bsz + 1 + SUBLANES - 1) // SUBLANES) * SUBLANES
    cond = jnp.zeros((rows, d), F32).at[0:bsz].set(c).at[bsz].set(c_ctx)
    ada = _ada_terms(cond, ada_w, ada_b).reshape(depth, rows, 6, d)

    def lat_term(l, k):
        return ada[l, 0:bsz, k, :].reshape(bsz, 1, d)

    def ctx_term(l, k):
        return jnp.broadcast_to(ada[l, bsz, k, :].reshape(1, 1, d), (bsz, 1, d))

    for l in range(depth):
        ctx_live = any(m % 2 == 0 for m in range(l + 1, depth))
        assert not ctx_live, "context stream update is only needed for depth > 2"
        sh1, sc1, g1, sh2, sc2, g2 = [lat_term(l, k) for k in range(6)]
        if l % 2 == 0:
            e = l // 2
            lam_init = 0.8 - 0.6 * math.exp(-0.3 * l)
            w_in_b = ev_w_in[e].astype(BF16)
            cos_t, sin_t = _rope_tables(n_lat, hd)
            qscale = hd ** -0.5 * math.log2(math.e)
            g_l, xr_l, q_l, k_l, v_l = _project_even(x, sh1, sc1, w_in_b, cos_t, sin_t, rope=True,
                                                     lw=lw, aw=aw, qscale=qscale, tm=512)
            n_ctx = ctx.shape[1]
            _, xr_c, _, k_c, v_c = _project_even(ctx, ctx_term(l, 0), ctx_term(l, 1), w_in_b,
                                                 cos_t[0:n_ctx], sin_t[0:n_ctx], rope=False,
                                                 lw=lw, aw=aw, qscale=qscale, tm=256)
            o_l = _diff_attention(q_l, k_c, k_l, v_c, v_l, ev_da_lambda[e], ev_da_subln[e],
                                  heads=heads, hd=hd, lam_init=lam_init)
            gwid = 256
            y = None
            for dirn, rev in ((0, False), (1, True)):
                wa_bd = _block_diag_groups(ev_gate_a_w[e, dirn], gwid)
                wx_bd = _block_diag_groups(ev_gate_x_w[e, dirn], gwid)
                common = (ev_conv_w[e], ev_conv_b[e], wa_bd, ev_gate_a_b[e, dirn], wx_bd,
                          ev_gate_x_b[e, dirn], ev_lru_lambda[e, dirn])
                h_zero = jnp.zeros((bsz, 1, lw), F32)
                _, h_fin = _rglru_dir(xr_c, None, *common, h_zero, reverse=rev, tn=256)
                y, _ = _rglru_dir(xr_l, y, *common, h_fin, reverse=rev, tn=256)
            x1, t = _outproj_even(y, g_l, o_l, x, ev_w_out[e].astype(BF16), g1, ln_g[l, 0], ln_b[l, 0],
                                  sh2, sc2, alpha=alpha, tm=512)
        else:
            o = l // 2
            gd = d // fnet_groups
            zr, zi = _chan_dft(x, sh1, sc1, _chan_dft_table(gd), groups=fnet_groups, tm=512)
            wv = _tok_fft(zr, zi)
            norm = 1.0 / math.sqrt(float(n_lat * gd))
            x1, t = _outproj_odd(wv, x, od_w_out[o].astype(BF16), od_b_out[o], g1, ln_g[l, 0], ln_b[l, 0],
                                 sh2, sc2, alpha=alpha, norm=norm, tm=512)
        x = _hier_moe_layer(x1, t, g2, ln_g[l, 1], ln_b[l, 1], moe_wg[l], moe_bg[l], moe_wf[l], moe_bf[l],
                            moe_w1, moe_w3, moe_w2, layer=l, alpha=alpha)
    return x
```

```python
import functools
import math

import numpy as np
import jax
import jax.numpy as jnp
from jax import lax
from jax.experimental import pallas as pl
from jax.experimental.pallas import tpu as pltpu

F32 = jnp.float32
BF16 = jnp.bfloat16
I32 = jnp.int32

LN_EPS = 1e-6
LRU_C = 8.0
ROPE_BASE = 10000.0
GRID_W = 64
CONV_W = 4
LANES = 128
SUBLANES = 8
MOE_ROWS = 512
MOE_TILE = 256
MOE_CHUNK = 8
VMEM_LIMIT = 56 * 1024 * 1024


def _cp(sem, vmem=None):
    return pltpu.CompilerParams(dimension_semantics=sem, vmem_limit_bytes=vmem)


def _ln(x):
    mu = jnp.mean(x, axis=-1, keepdims=True)
    xc = x - mu
    var = jnp.mean(xc * xc, axis=-1, keepdims=True)
    return xc * lax.rsqrt(var + LN_EPS)


def _silu(x):
    return x * jax.nn.sigmoid(x)


PACK_SUB = 4


def _tile_row(r, mult):
    if isinstance(r, int):
        return r * PACK_SUB
    return pl.multiple_of(r * PACK_SUB, mult)


def _pack_rows(val):
    half = val.shape[1] // 2
    lo = lax.bitcast_convert_type(val[:, 0:half], jnp.uint32) >> 16
    hi = lax.bitcast_convert_type(val[:, half:], jnp.uint32) & jnp.uint32(0xFFFF0000)
    return lo | hi


def _unpack_rows(words):
    lo = lax.bitcast_convert_type(words << 16, F32)
    hi = lax.bitcast_convert_type(words & jnp.uint32(0xFFFF0000), F32)
    return jnp.concatenate([lo, hi], axis=1).astype(BF16)


def _rows_from_tiles(ref, nrows):
    return jnp.concatenate([ref[pl.ds(k, nrows, stride=PACK_SUB), :] for k in range(PACK_SUB)], axis=1)


def _rows_to_tiles(ref, val):
    nrows = val.shape[0]
    for k in range(PACK_SUB):
        ref[pl.ds(k, nrows, stride=PACK_SUB), :] = val[:, k * LANES:(k + 1) * LANES]


def _ada_kernel(c_ref, w_ref, b_ref, o_ref):
    s = _silu(c_ref[...]).astype(BF16)
    o_ref[...] = jnp.dot(s, w_ref[...].astype(BF16), preferred_element_type=F32) + b_ref[...]


def _ada_terms(cond, ada_w, ada_b):
    nl, d, d6 = ada_w.shape
    r = cond.shape[0]
    tn = 1024
    return pl.pallas_call(
        _ada_kernel,
        out_shape=jax.ShapeDtypeStruct((nl, r, d6), F32),
        grid=(nl, d6 // tn),
        in_specs=[pl.BlockSpec((r, d), lambda l, j: (0, 0)),
                  pl.BlockSpec((None, d, tn), lambda l, j: (l, 0, j)),
                  pl.BlockSpec((None, 1, tn), lambda l, j: (l, 0, j))],
        out_specs=pl.BlockSpec((None, r, tn), lambda l, j: (l, 0, j)),
        compiler_params=_cp(("parallel", "parallel")),
        name="ada_terms",
    )(cond, ada_w, ada_b.reshape(nl, 1, d6))


def _rope_apply(x, cos, sin_signed):
    tm = x.shape[0]
    lane = lax.broadcasted_iota(I32, (tm, LANES), 1)
    first_half = (lane % 32) < 16
    outs = []
    for j in range(x.shape[1] // LANES):
        xh = x[:, j * LANES:(j + 1) * LANES]
        partner = jnp.where(first_half, pltpu.roll(xh, LANES - 16, 1), pltpu.roll(xh, 16, 1))
        outs.append(xh * cos + partner * sin_signed)
    return jnp.concatenate(outs, axis=1)


def _proj_kernel(x_ref, sh_ref, sc_ref, w_ref, cos_ref, sin_ref,
                 g_ref, xr_ref, q_ref, k_ref, v_ref, *, rope, lw, aw, qscale):
    h = _ln(x_ref[...]) * (1.0 + sc_ref[...]) + sh_ref[...]
    hb = h.astype(BF16)

    def mm(c0, c1):
        return jnp.dot(hb, w_ref[:, c0:c1], preferred_element_type=F32)

    g_ref[...] = mm(0, lw).astype(BF16)
    xr_ref[...] = mm(lw, 2 * lw)
    q = mm(2 * lw, 2 * lw + aw)
    k = mm(2 * lw + aw, 2 * lw + 2 * aw)
    if rope:
        q = _rope_apply(q, cos_ref[...], sin_ref[...])
        k = _rope_apply(k, cos_ref[...], sin_ref[...])
    q_ref[...] = (q * qscale).astype(BF16)
    k_ref[...] = k.astype(BF16)
    v_ref[...] = mm(2 * lw + 2 * aw, 2 * lw + 3 * aw).astype(BF16)


def _project_even(x, shift, scale, w_in_b, cos_t, sin_t, *, rope, lw, aw, qscale, tm):
    b, n, d = x.shape
    tm = min(tm, n)
    nin = w_in_b.shape[1]
    tok = lambda bi, i: (bi, i, 0)
    per_b = lambda bi, i: (bi, 0, 0)
    outs = (jax.ShapeDtypeStruct((b, n, lw), BF16), jax.ShapeDtypeStruct((b, n, lw), F32),
            jax.ShapeDtypeStruct((b, n, aw), BF16), jax.ShapeDtypeStruct((b, n, aw), BF16),
            jax.ShapeDtypeStruct((b, n, aw), BF16))
    return pl.pallas_call(
        functools.partial(_proj_kernel, rope=rope, lw=lw, aw=aw, qscale=qscale),
        out_shape=outs,
        grid=(b, n // tm),
        in_specs=[pl.BlockSpec((None, tm, d), tok),
                  pl.BlockSpec((None, 1, d), per_b),
                  pl.BlockSpec((None, 1, d), per_b),
                  pl.BlockSpec((d, nin), lambda bi, i: (0, 0)),
                  pl.BlockSpec((tm, LANES), lambda bi, i: (i, 0)),
                  pl.BlockSpec((tm, LANES), lambda bi, i: (i, 0))],
        out_specs=(pl.BlockSpec((None, tm, lw), tok), pl.BlockSpec((None, tm, lw), tok),
                   pl.BlockSpec((None, tm, aw), tok), pl.BlockSpec((None, tm, aw), tok),
                   pl.BlockSpec((None, tm, aw), tok)),
        compiler_params=_cp(("parallel", "parallel"), VMEM_LIMIT),
        name="proj_even_rope" if rope else "proj_even_ctx",
    )(x, shift, scale, w_in_b, cos_t, sin_t)


def _rope_tables(n_tok, head_dim):
    t = jnp.arange(n_tok)
    row = (t // GRID_W).astype(F32)
    col = (t % GRID_W).astype(F32)
    nf = head_dim // 4
    freqs = ROPE_BASE ** (-jnp.arange(nf, dtype=F32) / nf)
    lane = np.arange(LANES)
    within = lane % head_dim
    axis = within // (2 * nf)
    half = (within % (2 * nf)) // nf
    f = within % nf
    pos = jnp.where(jnp.asarray(axis)[None, :] == 0, row[:, None], col[:, None])
    ang = pos * freqs[jnp.asarray(f)][None, :]
    sign = jnp.asarray(np.where(half == 0, -1.0, 1.0), F32)[None, :]
    return jnp.cos(ang).astype(F32), (jnp.sin(ang) * sign).astype(F32)


def _attn_kernel(q_ref, kc_ref, kl_ref, vc_ref, vl_ref, dl_ref, gain_ref, o_ref, kbuf, vbuf, sbuf, ebuf, abuf,
                 cbuf, *, nc, nl, hd, lam_init, rows):
    @pl.when((pl.program_id(0) == 0) & (pl.program_id(1) == 0))
    def _():
        sbuf[...] = jnp.zeros_like(sbuf)
        abuf[...] = jnp.zeros_like(abuf)
        cbuf[...] = jnp.zeros_like(cbuf)

    kbuf[0:nc, :] = kc_ref[...]
    kbuf[nc:nc + nl, :] = kl_ref[...]
    vbuf[0:nc, :] = vc_ref[...]
    vbuf[nc:nc + nl, :] = vl_ref[...]

    lf = dl_ref[...]
    lam = (jnp.exp(jnp.sum(lf[0:1] * lf[1:2], axis=1, keepdims=True))
           - jnp.exp(jnp.sum(lf[2:3] * lf[3:4], axis=1, keepdims=True)) + lam_init)
    gain = gain_ref[...] * (1.0 - lam_init)
    n_sub = nl // rows
    lane = lax.broadcasted_iota(I32, (rows, 2 * hd), 1)
    nt = (((1,), (1,)), ((), ()))

    def stage_a(j, slot):
        r0 = pl.multiple_of(jnp.minimum(j, n_sub - 1) * rows, rows)
        q = q_ref[pl.ds(r0, rows), :]
        zero = jnp.zeros_like(q)
        kk = kbuf[...]
        sbuf[slot, 0] = lax.dot_general(jnp.where(lane < hd, q, zero), kk, nt, preferred_element_type=F32)
        sbuf[slot, 1] = lax.dot_general(jnp.where(lane >= hd, q, zero), kk, nt, preferred_element_type=F32)

    def stage_b(slot):
        cols = [slice(c, c + LANES) for c in range(0, nc + nl, LANES)]
        ls = []
        for k in range(2):
            pm = sbuf[slot, k, :, cols[0]]
            for cs in cols[1:]:
                pm = jnp.maximum(pm, sbuf[slot, k, :, cs])
            m = jnp.max(pm, axis=1, keepdims=True)
            acc = jnp.zeros((rows, LANES), F32)
            for cs in cols:
                e = jnp.exp2(sbuf[slot, k, :, cs] - m)
                acc = acc + e
                ebuf[slot, k, :, cs] = e.astype(BF16)
            ls.append(jnp.sum(acc, axis=1, keepdims=True))
        ratio = (lam * ls[0] / ls[1]).astype(BF16)
        abuf[slot] = ebuf[slot, 0] - ratio * ebuf[slot, 1]
        cbuf[slot] = jnp.broadcast_to(1.0 / ls[0], cbuf.shape[1:])

    def stage_c(j, slot):
        r0 = pl.multiple_of(jnp.maximum(j - 2, 0) * rows, rows)
        o = jnp.dot(abuf[slot], vbuf[...], preferred_element_type=F32) * cbuf[slot]
        o = o * lax.rsqrt(jnp.mean(o * o, axis=1, keepdims=True) + LN_EPS) * gain
        o_ref[pl.ds(r0, rows), :] = o.astype(BF16)

    def body(t, carry):
        j = 2 * t
        stage_a(j, 0)
        stage_b(1)
        stage_c(j, 0)
        stage_a(j + 1, 1)
        stage_b(0)
        stage_c(j + 1, 1)
        return carry

    lax.fori_loop(0, (n_sub + 2) // 2, body, 0)


def _diff_attention(q, k_ctx, k_lat, v_ctx, v_lat, da_lambda, subln, *, heads, hd, lam_init):
    b, n, aw = q.shape
    nc = k_ctx.shape[1]
    vd = aw // heads
    rows = min(128, n)
    assert (n // rows) % 2 == 0
    blk_q = pl.BlockSpec((None, n, vd), lambda bi, h: (bi, 0, h))
    blk_c = pl.BlockSpec((None, nc, vd), lambda bi, h: (bi, 0, h))
    return pl.pallas_call(
        functools.partial(_attn_kernel, nc=nc, nl=n, hd=hd, lam_init=lam_init, rows=rows),
        out_shape=jax.ShapeDtypeStruct((b, n, aw), BF16),
        grid=(b, heads),
        in_specs=[blk_q, blk_c, blk_q, blk_c, blk_q,
                  pl.BlockSpec(da_lambda.shape, lambda bi, h: (0, 0)),
                  pl.BlockSpec((1, vd), lambda bi, h: (0, 0))],
        out_specs=blk_q,
        scratch_shapes=[pltpu.VMEM((nc + n, vd), BF16), pltpu.VMEM((nc + n, vd), BF16),
                        pltpu.VMEM((2, 2, rows, nc + n), F32), pltpu.VMEM((2, 2, rows, nc + n), BF16),
                        pltpu.VMEM((2, rows, nc + n), BF16), pltpu.VMEM((2, rows, vd), F32)],
        compiler_params=_cp(("arbitrary", "arbitrary"), VMEM_LIMIT),
        name="diff_attention",
    )(q, k_ctx, k_lat, v_ctx, v_lat, da_lambda, subln.reshape(1, vd))


def _lru_kernel(*refs, reverse, add_prev, nt, groups):
    if add_prev:
        (xp_ref, xc_ref, xn_ref, yprev_ref, cw_ref, cb_ref, wa_ref, ba_ref, wx_ref, bx_ref,
         lam_ref, h0_ref, y_ref, hf_ref, carry, ext) = refs
    else:
        (xp_ref, xc_ref, xn_ref, cw_ref, cb_ref, wa_ref, ba_ref, wx_ref, bx_ref,
         lam_ref, h0_ref, y_ref, hf_ref, carry, ext) = refs
        yprev_ref = None
    i = pl.program_id(1)
    ti = (nt - 1 - i) if reverse else i
    tn, w = xc_ref.shape

    @pl.when(i == 0)
    def _():
        carry[...] = h0_ref[...]

    ext[0:SUBLANES, :] = jnp.where(ti > 0, xp_ref[...], 0.0)
    ext[SUBLANES:SUBLANES + tn, :] = xc_ref[...]
    ext[SUBLANES + tn:2 * SUBLANES + tn, :] = jnp.where(ti < nt - 1, xn_ref[...], 0.0)
    left = CONV_W // 2
    xc = cb_ref[...]
    for k in range(CONV_W):
        off = SUBLANES - left + k
        xc = xc + ext[off:off + tn, :] * cw_ref[k:k + 1, :]

    xb = xc.astype(BF16)
    gw = w // groups

    def gate(w_ref, b_ref):
        parts = [jnp.dot(xb[:, g * gw:(g + 1) * gw], w_ref[g], preferred_element_type=F32)
                 for g in range(groups)]
        return jax.nn.sigmoid(jnp.concatenate(parts, axis=1) + b_ref[...])

    r = gate(wa_ref, ba_ref)
    ig = gate(wx_ref, bx_ref)
    log_a = (-LRU_C * jax.nn.softplus(-lam_ref[...])) * r
    a = jnp.exp(log_a)
    bcoef = jnp.sqrt(-jnp.tanh(log_a) * (a * a + 1.0)) * ig * xc

    ngroups = tn // SUBLANES
    a = a.reshape(ngroups, SUBLANES, w)
    bcoef = bcoef.reshape(ngroups, SUBLANES, w)
    row = lax.broadcasted_iota(I32, (ngroups, SUBLANES, w), 1)
    d = 1
    while d < SUBLANES:
        shift = (SUBLANES - d) if reverse else d
        a_sh = pltpu.roll(a, shift, 1)
        b_sh = pltpu.roll(bcoef, shift, 1)
        live = (row < SUBLANES - d) if reverse else (row >= d)
        bcoef = jnp.where(live, a * b_sh + bcoef, bcoef)
        a = jnp.where(live, a * a_sh, a)
        d *= 2
    hc = carry[...]
    for g in (range(ngroups - 1, -1, -1) if reverse else range(ngroups)):
        rs = slice(g * SUBLANES, (g + 1) * SUBLANES)
        h = a[g] * hc + bcoef[g]
        hc = h[0:1, :] if reverse else h[SUBLANES - 1:SUBLANES, :]
        y_ref[rs, :] = (yprev_ref[rs, :] + h) if add_prev else h
    carry[...] = hc
    hf_ref[...] = hc


def _rglru_dir(xr, y_prev, conv_w, conv_b, wa_bd, ba, wx_bd, bx, lam, h0, *, reverse, tn):
    b, n, w = xr.shape
    tn = min(tn, n)
    nt = n // tn
    groups = wa_bd.shape[0]
    nb8 = n // SUBLANES
    per8 = tn // SUBLANES

    def tmap(i):
        return (nt - 1 - i) if reverse else i

    cur = pl.BlockSpec((None, tn, w), lambda bi, i: (bi, tmap(i), 0))
    halo_p = pl.BlockSpec((None, SUBLANES, w), lambda bi, i: (bi, jnp.maximum(tmap(i) * per8 - 1, 0), 0))
    halo_n = pl.BlockSpec((None, SUBLANES, w), lambda bi, i: (bi, jnp.minimum((tmap(i) + 1) * per8, nb8 - 1), 0))
    row_w = pl.BlockSpec((1, w), lambda bi, i: (0, 0))
    per_b = pl.BlockSpec((None, 1, w), lambda bi, i: (bi, 0, 0))
    gate_w = pl.BlockSpec(wa_bd.shape, lambda bi, i: (0, 0, 0))
    add_prev = y_prev is not None
    in_specs = [halo_p, cur, halo_n] + ([cur] if add_prev else []) + [
        pl.BlockSpec((CONV_W, w), lambda bi, i: (0, 0)), row_w, gate_w, row_w, gate_w, row_w, row_w, per_b]
    args = [xr, xr, xr] + ([y_prev] if add_prev else []) + [
        conv_w, conv_b.reshape(1, w), wa_bd, ba.reshape(1, w), wx_bd, bx.reshape(1, w),
        lam.reshape(1, w), h0]
    return pl.pallas_call(
        functools.partial(_lru_kernel, reverse=reverse, add_prev=add_prev, nt=nt, groups=groups),
        out_shape=(jax.ShapeDtypeStruct((b, n, w), F32), jax.ShapeDtypeStruct((b, 1, w), F32)),
        grid=(b, nt),
        in_specs=in_specs,
        out_specs=(cur, per_b),
        scratch_shapes=[pltpu.VMEM((1, w), F32), pltpu.VMEM((tn + 2 * SUBLANES, w), F32)],
        compiler_params=_cp(("parallel", "arbitrary"), VMEM_LIMIT),
        name="rglru_rev" if reverse else "rglru_fwd",
    )(*args)


def _block_diag_groups(wh, group_width):
    heads, blk, _ = wh.shape
    per = group_width // blk
    groups = heads // per
    whg = wh.reshape(groups, per, blk, blk)
    eye = jnp.eye(per, dtype=wh.dtype)
    bd = jnp.einsum('gpij,pq->gpiqj', whg, eye).reshape(groups, group_width, group_width)
    return bd.astype(BF16)


def _residual_ln_mod(x, y, g1, lng, lnb, sh2, sc2, alpha):
    x1 = _ln(alpha * x + g1 * y) * lng + lnb
    t = _ln(x1) * (1.0 + sc2) + sh2
    return x1, t


def _outproj_even_kernel(r_ref, g_ref, o_ref, x_ref, w_ref, g1_ref, lng_ref, lnb_ref, sh2_ref, sc2_ref,
                         x1_ref, t_ref, *, lw, alpha):
    z = (r_ref[...] * jax.nn.gelu(g_ref[...].astype(F32))).astype(BF16)
    y = (jnp.dot(z, w_ref[0:lw, :], preferred_element_type=F32)
         + jnp.dot(o_ref[...], w_ref[lw:, :], preferred_element_type=F32))
    x1, t = _residual_ln_mod(x_ref[...], y, g1_ref[...], lng_ref[...], lnb_ref[...],
                             sh2_ref[...], sc2_ref[...], alpha)
    x1_ref[...] = x1
    t_ref[...] = t.astype(BF16)


def _outproj_even(r, g, o, x, w_out_b, g1, lng, lnb, sh2, sc2, *, alpha, tm):
    b, n, d = x.shape
    lw = r.shape[2]
    aw = o.shape[2]
    tm = min(tm, n)
    tok = lambda bi, i: (bi, i, 0)
    per_b = pl.BlockSpec((None, 1, d), lambda bi, i: (bi, 0, 0))
    row = pl.BlockSpec((1, d), lambda bi, i: (0, 0))
    return pl.pallas_call(
        functools.partial(_outproj_even_kernel, lw=lw, alpha=alpha),
        out_shape=(jax.ShapeDtypeStruct((b, n, d), F32), jax.ShapeDtypeStruct((b, n, d), BF16)),
        grid=(b, n // tm),
        in_specs=[pl.BlockSpec((None, tm, lw), tok), pl.BlockSpec((None, tm, lw), tok),
                  pl.BlockSpec((None, tm, aw), tok), pl.BlockSpec((None, tm, d), tok),
                  pl.BlockSpec(w_out_b.shape, lambda bi, i: (0, 0)),
                  per_b, row, row, per_b, per_b],
        out_specs=(pl.BlockSpec((None, tm, d), tok), pl.BlockSpec((None, tm, d), tok)),
        compiler_params=_cp(("parallel", "parallel"), VMEM_LIMIT),
        name="outproj_even",
    )(r, g, o, x, w_out_b, g1, lng.reshape(1, d), lnb.reshape(1, d), sh2, sc2)


def _chan_dft_kernel(x_ref, sh_ref, sc_ref, tab_ref, zr_ref, zi_ref, *, groups, gd):
    h = (_ln(x_ref[...]) * (1.0 + sc_ref[...]) + sh_ref[...]).astype(BF16)
    for g in range(groups):
        z = jnp.dot(h[:, g * gd:(g + 1) * gd], tab_ref[...], preferred_element_type=F32)
        zr_ref[:, g * gd:(g + 1) * gd] = z[:, 0:gd].astype(BF16)
        zi_ref[:, g * gd:(g + 1) * gd] = z[:, gd:2 * gd].astype(BF16)


def _chan_dft(x, shift, scale, tab, *, groups, tm):
    b, n, d = x.shape
    gd = d // groups
    tm = min(tm, n)
    tok = lambda bi, i: (bi, i, 0)
    per_b = pl.BlockSpec((None, 1, d), lambda bi, i: (bi, 0, 0))
    return pl.pallas_call(
        functools.partial(_chan_dft_kernel, groups=groups, gd=gd),
        out_shape=(jax.ShapeDtypeStruct((b, n, d), BF16), jax.ShapeDtypeStruct((b, n, d), BF16)),
        grid=(b, n // tm),
        in_specs=[pl.BlockSpec((None, tm, d), tok), per_b, per_b,
                  pl.BlockSpec(tab.shape, lambda bi, i: (0, 0))],
        out_specs=(pl.BlockSpec((None, tm, d), tok), pl.BlockSpec((None, tm, d), tok)),
        compiler_params=_cp(("parallel", "parallel"), VMEM_LIMIT),
        name="chan_dft",
    )(x, shift, scale, tab)


FFT_C = 64


def _fft_pitch(group):
    p = -(-group // SUBLANES)
    return SUBLANES * (p if p % 2 else p + 1)


def _tok_fft_kernel(zr_ref, zi_ref, m1_ref, m3_ref, tc_ref, ts_ref, o_ref, zsr, zsi, asr, asi, ob, *, nr):
    c_len = FFT_C
    pz = zsr.shape[0] // nr
    pa = asr.shape[0] // c_len
    for r in range(nr):
        zsr[pz * r:pz * r + c_len, :] = zr_ref[c_len * r:c_len * (r + 1), :].astype(F32)
        zsi[pz * r:pz * r + c_len, :] = zi_ref[c_len * r:c_len * (r + 1), :].astype(F32)
    m1 = m1_ref[...]
    for c in range(c_len):
        x2 = jnp.concatenate([zsr[pl.ds(c, nr, stride=pz), :], zsi[pl.ds(c, nr, stride=pz), :]], axis=0)
        a2 = jnp.dot(m1, x2.astype(BF16), preferred_element_type=F32)
        ar, ai = a2[0:nr], a2[nr:2 * nr]
        tcv = tc_ref[c * nr:(c + 1) * nr, :]
        tsv = ts_ref[c * nr:(c + 1) * nr, :]
        asr[pa * c:pa * c + nr, :] = ar * tcv + ai * tsv
        asi[pa * c:pa * c + nr, :] = ai * tcv - ar * tsv
    m3 = m3_ref[...]
    for k1 in range(nr):
        y2 = jnp.concatenate([asr[pl.ds(k1, c_len, stride=pa), :], asi[pl.ds(k1, c_len, stride=pa), :]], axis=0)
        ob[pl.ds(k1, c_len, stride=pa), :] = jnp.dot(m3, y2.astype(BF16), preferred_element_type=F32)
    for k2 in range(c_len):
        o_ref[nr * k2:nr * (k2 + 1), :] = ob[pa * k2:pa * k2 + nr, :].astype(BF16)


def _tok_fft(zr, zi):
    b, n, d = zr.shape
    nr = n // FFT_C
    pz = _fft_pitch(FFT_C)
    pa = _fft_pitch(nr)
    kr = np.arange(nr, dtype=np.float64)
    ang_r = 2.0 * np.pi * (np.outer(kr, kr) % nr) / nr
    cr, sr = np.cos(ang_r), np.sin(ang_r)
    m1 = jnp.asarray(np.block([[cr, -sr], [-sr, -cr]]), F32).astype(BF16)
    kc = np.arange(FFT_C, dtype=np.float64)
    ang_c = 2.0 * np.pi * (np.outer(kc, kc) % FFT_C) / FFT_C
    m3 = jnp.asarray(np.concatenate([np.cos(ang_c), np.sin(ang_c)], axis=1), F32).astype(BF16)
    ang_t = 2.0 * np.pi * (np.outer(kc, kr) % n) / n
    tc = jnp.broadcast_to(jnp.asarray(np.cos(ang_t).reshape(FFT_C * nr, 1), F32), (FFT_C * nr, LANES))
    ts = jnp.broadcast_to(jnp.asarray(np.sin(ang_t).reshape(FFT_C * nr, 1), F32), (FFT_C * nr, LANES))
    slab = pl.BlockSpec((None, n, LANES), lambda bi, l: (bi, 0, l))
    const = lambda a: pl.BlockSpec(a.shape, lambda bi, l: (0, 0))
    return pl.pallas_call(
        functools.partial(_tok_fft_kernel, nr=nr),
        out_shape=jax.ShapeDtypeStruct((b, n, d), BF16),
        grid=(b, d // LANES),
        in_specs=[slab, slab, const(m1), const(m3), const(tc), const(ts)],
        out_specs=slab,
        scratch_shapes=[pltpu.VMEM((nr * pz, LANES), F32), pltpu.VMEM((nr * pz, LANES), F32),
                        pltpu.VMEM((FFT_C * pa, LANES), F32), pltpu.VMEM((FFT_C * pa, LANES), F32),
                        pltpu.VMEM((FFT_C * pa, LANES), F32)],
        compiler_params=_cp(("parallel", "parallel"), VMEM_LIMIT),
        name="tok_fft",
    )(zr, zi, m1, m3, tc, ts)


def _chan_dft_table(gd):
    c = np.arange(gd, dtype=np.float64)
    ang_c = 2.0 * np.pi * (np.outer(c, c) % gd) / gd
    return jnp.asarray(np.concatenate([np.cos(ang_c), np.sin(ang_c)], axis=1), F32).astype(BF16)


def _outproj_odd_kernel(wv_ref, x_ref, w_ref, b_ref, g1_ref, lng_ref, lnb_ref, sh2_ref, sc2_ref,
                        x1_ref, t_ref, *, alpha, norm):
    y = jnp.dot(wv_ref[...], w_ref[...], preferred_element_type=F32) * norm + b_ref[...]
    x1, t = _residual_ln_mod(x_ref[...], y, g1_ref[...], lng_ref[...], lnb_ref[...],
                             sh2_ref[...], sc2_ref[...], alpha)
    x1_ref[...] = x1
    t_ref[...] = t.astype(BF16)


def _outproj_odd(wv, x, w_b, bias, g1, lng, lnb, sh2, sc2, *, alpha, norm, tm):
    b, n, d = x.shape
    tm = min(tm, n)
    tok = lambda bi, i: (bi, i, 0)
    per_b = pl.BlockSpec((None, 1, d), lambda bi, i: (bi, 0, 0))
    row = pl.BlockSpec((1, d), lambda bi, i: (0, 0))
    return pl.pallas_call(
        functools.partial(_outproj_odd_kernel, alpha=alpha, norm=norm),
        out_shape=(jax.ShapeDtypeStruct((b, n, d), F32), jax.ShapeDtypeStruct((b, n, d), BF16)),
        grid=(b, n // tm),
        in_specs=[pl.BlockSpec((None, tm, d), tok), pl.BlockSpec((None, tm, d), tok),
                  pl.BlockSpec(w_b.shape, lambda bi, i: (0, 0)), row,
                  per_b, row, row, per_b, per_b],
        out_specs=(pl.BlockSpec((None, tm, d), tok), pl.BlockSpec((None, tm, d), tok)),
        compiler_params=_cp(("parallel", "parallel"), VMEM_LIMIT),
        name="outproj_odd",
    )(wv, x, w_b, bias.reshape(1, d), g1, lng.reshape(1, d), lnb.reshape(1, d), sh2, sc2)


def _route_kernel(t_ref, w_ref, b_ref, up_ref, lp_ref, tokm_ref, cnt_ref, *, ng, epg, chunk):
    tm = t_ref.shape[0]
    ne = ng * epg
    logits = jnp.dot(t_ref[...].astype(BF16), w_ref[...], preferred_element_type=F32) + b_ref[...]
    lt = logits.T
    best = lt[0:1, :]
    bi = jnp.zeros((1, tm), I32)
    for k in range(1, ng):
        gk = lt[k:k + 1, :]
        upd = gk > best
        bi = jnp.where(upd, k, bi)
        best = jnp.where(upd, gk, best)
    den = jnp.zeros((1, tm), F32)
    for k in range(ng):
        den = den + jnp.exp(lt[k:k + 1, :] - best)
    p_g = 1.0 / den
    fsel = lt[SUBLANES:SUBLANES + epg, :]
    for k in range(1, ng):
        fsel = jnp.where(bi == k, lt[SUBLANES + k * epg:SUBLANES + (k + 1) * epg, :], fsel)
    neg = jnp.full((1, tm), -jnp.inf, F32)
    m1, m2 = neg, neg
    i1 = jnp.zeros((1, tm), I32)
    i2 = jnp.zeros((1, tm), I32)
    for j in range(epg):
        v = fsel[j:j + 1, :]
        gt1 = v > m1
        gt2 = v > m2
        m2 = jnp.where(gt1, m1, jnp.where(gt2, v, m2))
        i2 = jnp.where(gt1, i1, jnp.where(gt2, j, i2))
        m1 = jnp.where(gt1, v, m1)
        i1 = jnp.where(gt1, j, i1)
    e21 = jnp.exp(m2 - m1)
    w1 = p_g / (1.0 + e21)
    w2 = p_g * e21 / (1.0 + e21)
    e1 = bi * epg + i1
    e2 = bi * epg + i2

    e = jnp.concatenate([e1, e2], axis=1)
    rows = lax.broadcasted_iota(I32, (ne, 2 * tm), 0)
    onehot = jnp.where(rows == e, 1.0, 0.0)
    before = jnp.dot(onehot.astype(BF16), up_ref[...], preferred_element_type=F32)
    tot = jnp.sum(onehot, axis=1, keepdims=True)
    slots = jnp.floor((tot + (chunk - 1.0)) * (1.0 / chunk)) * chunk
    slots_b = jnp.broadcast_to(slots, (ne, LANES))
    rowe = lax.broadcasted_iota(I32, (ne, LANES), 0)
    incl = slots_b
    d = 1
    while d < ne:
        incl = incl + jnp.where(rowe >= d, pltpu.roll(incl, d, 0), 0.0)
        d *= 2
    seg_off = jnp.tile(incl - slots_b, (1, 2 * tm // LANES))
    lpos = jnp.sum(onehot * (before + seg_off), axis=0, keepdims=True)
    lp0 = lpos[:, 0:tm]
    lp1 = lpos[:, tm:2 * tm]
    row8 = lax.broadcasted_iota(I32, (SUBLANES, tm), 0)
    lp_ref[...] = jnp.where(row8 == 0, lp0, jnp.where(row8 == 1, lp1, 0.0)).astype(I32)
    rowl = lax.broadcasted_iota(I32, (LANES, tm), 0)
    tokm = jnp.where(rowl == 0, w1, jnp.where(rowl == 1, w2, jnp.where(rowl == 2, lp0, jnp.where(rowl == 3, lp1, 0.0))))
    tokm_ref[...] = tokm.T
    cnt_ref[...] = jnp.broadcast_to(tot, (ne, LANES))


def _route(t, wcat_b, bcat, upper, *, ng, epg, tm, chunk):
    tt, d = t.shape
    ne = ng * epg
    nt = tt // tm
    return pl.pallas_call(
        functools.partial(_route_kernel, ng=ng, epg=epg, chunk=chunk),
        out_shape=(jax.ShapeDtypeStruct((SUBLANES, tt), I32), jax.ShapeDtypeStruct((tt, LANES), F32),
                   jax.ShapeDtypeStruct((ne, nt * LANES), F32)),
        grid=(nt,),
        in_specs=[pl.BlockSpec((tm, d), lambda i: (i, 0)),
                  pl.BlockSpec((d, LANES), lambda i: (0, 0)),
                  pl.BlockSpec((1, LANES), lambda i: (0, 0)),
                  pl.BlockSpec(upper.shape, lambda i: (0, 0))],
        out_specs=(pl.BlockSpec((SUBLANES, tm), lambda i: (0, i)),
                   pl.BlockSpec((tm, LANES), lambda i: (i, 0)),
                   pl.BlockSpec((ne, LANES), lambda i: (0, i))),
        compiler_params=_cp(("parallel",), VMEM_LIMIT),
        name="route_sort",
    )(t, wcat_b, bcat, upper)


def _dispatch_kernel(row_ref, tot_ref, t_ref, lp_ref, xs_in_ref, xs_ref, stage, sems, *, qmax, chunk, nsteps):
    del xs_in_ref
    i = pl.program_id(0)
    slot = i % 2
    sp = stage.shape[1] // PACK_SUB
    tm = t_ref.shape[0]
    crow = chunk * PACK_SUB

    lp = lp_ref[...]
    prow = lax.broadcasted_iota(I32, (sp, tm), 0)
    perm = jnp.where(prow == lp[0:1, :], 1.0, jnp.where(prow == lp[1:2, :], 1.0, 0.0)).astype(BF16)
    srt = jnp.dot(perm, t_ref[...], preferred_element_type=F32)
    _rows_to_tiles(stage.at[slot], _pack_rows(srt))

    def chunk_copy(sl, src, dst):
        return pltpu.make_async_copy(stage.at[sl, pl.ds(_tile_row(src, crow), crow)],
                                     xs_ref.at[pl.ds(_tile_row(dst, SUBLANES), crow)], sems.at[sl])

    def drain(sl, n):
        def w(c, carry):
            chunk_copy(sl, 0, 0).wait()
            return carry
        lax.fori_loop(0, n, w, 0)

    @pl.when(i >= 1)
    def _():
        drain(1 - slot, tot_ref[jnp.maximum(i - 1, 0)])

    def issue(q, carry):
        chunk_copy(slot, pl.multiple_of(q * chunk, chunk), row_ref[i * qmax + q]).start()
        return carry
    lax.fori_loop(0, tot_ref[i], issue, 0)

    @pl.when(i == nsteps - 1)
    def _():
        drain(slot, tot_ref[i])


def _dispatch(chunk_row, tile_nch, t, lp, *, n_rows, ne, tm, chunk):
    ttot, d = t.shape
    nsteps = ttot // tm
    sp = 2 * tm + ne * chunk
    xs_init = jnp.zeros((n_rows * PACK_SUB, LANES), jnp.uint32)
    grid_spec = pltpu.PrefetchScalarGridSpec(
        num_scalar_prefetch=2,
        grid=(nsteps,),
        in_specs=[pl.BlockSpec((tm, d), lambda i, *_: (i, 0)),
                  pl.BlockSpec((SUBLANES, tm), lambda i, *_: (0, i)),
                  pl.BlockSpec(memory_space=pl.ANY)],
        out_specs=pl.BlockSpec(memory_space=pl.ANY),
        scratch_shapes=[pltpu.VMEM((2, sp * PACK_SUB, LANES), jnp.uint32), pltpu.SemaphoreType.DMA((2,))],
    )
    assert d == 2 * PACK_SUB * LANES
    return pl.pallas_call(
        functools.partial(_dispatch_kernel, qmax=chunk_row.shape[0] // nsteps, chunk=chunk, nsteps=nsteps),
        out_shape=jax.ShapeDtypeStruct(xs_init.shape, xs_init.dtype),
        grid_spec=grid_spec,
        input_output_aliases={4: 0},
        compiler_params=_cp(("arbitrary",), VMEM_LIMIT),
        name="moe_dispatch",
    )(chunk_row, tile_nch, t, lp, xs_init)


def _moe_kernel(be_ref, nu_ref, xs_ref, w1_ref, w3_ref, w2_ref, y_ref, w13b, w2b, *, ff):
    i = pl.program_id(0)

    @pl.when(i < nu_ref[0])
    def _():
        prev = be_ref[jnp.maximum(i - 1, 0)]

        @pl.when((i == 0) | (be_ref[i] != prev))
        def _():
            w13b[:, 0:ff] = w1_ref[...].astype(BF16)
            w13b[:, ff:2 * ff] = w3_ref[...].astype(BF16)
            w2b[...] = w2_ref[...].astype(BF16)

        x = _unpack_rows(_rows_from_tiles(xs_ref, MOE_ROWS))
        h = jnp.dot(x, w13b[...], preferred_element_type=F32)
        hid = (_silu(h[:, 0:ff]) * h[:, ff:2 * ff]).astype(BF16)
        y = jnp.dot(hid, w2b[...], preferred_element_type=F32)
        _rows_to_tiles(y_ref, _pack_rows(y.astype(BF16).astype(F32)))

    @pl.when(i >= nu_ref[0])
    def _():
        y_ref[...] = jnp.zeros_like(y_ref)


def _moe_experts(block_e, n_used, xs, w1, w3, w2, *, layer):
    d = w1.shape[-2]
    ff = w1.shape[-1]
    blk = MOE_ROWS * PACK_SUB
    nb = xs.shape[0] // blk
    wmap = lambda i, be, nu: (layer, be[i], 0, 0)
    grid_spec = pltpu.PrefetchScalarGridSpec(
        num_scalar_prefetch=2,
        grid=(nb,),
        in_specs=[pl.BlockSpec((blk, LANES), lambda i, be, nu: (i, 0)),
                  pl.BlockSpec((None, None, d, ff), wmap),
                  pl.BlockSpec((None, None, d, ff), wmap),
                  pl.BlockSpec((None, None, ff, d), wmap)],
        out_specs=pl.BlockSpec((blk, LANES), lambda i, be, nu: (i, 0)),
        scratch_shapes=[pltpu.VMEM((d, 2 * ff), BF16), pltpu.VMEM((ff, d), BF16)],
    )
    return pl.pallas_call(
        functools.partial(_moe_kernel, ff=ff),
        out_shape=jax.ShapeDtypeStruct(xs.shape, xs.dtype),
        grid_spec=grid_spec,
        compiler_params=_cp(("arbitrary",), VMEM_LIMIT),
        name="moe_experts",
    )(block_e, n_used, xs, w1, w3, w2)


def _combine_kernel(row_ref, tot_ref, yb_ref, x_ref, tokm_ref, g2_ref, lng_ref, lnb_ref,
                    o_ref, stage, sems, *, qmax, chunk, nsteps, alpha):
    i = pl.program_id(0)
    slot = i % 2
    sp = stage.shape[1] // PACK_SUB
    tm = x_ref.shape[0]
    crow = chunk * PACK_SUB

    def chunk_copy(sl, src, dst):
        return pltpu.make_async_copy(yb_ref.at[pl.ds(_tile_row(src, SUBLANES), crow)],
                                     stage.at[sl, pl.ds(_tile_row(dst, crow), crow)], sems.at[sl])

    def issue_tile(step, sl):
        def issue(q, carry):
            chunk_copy(sl, row_ref[step * qmax + q], pl.multiple_of(q * chunk, chunk)).start()
            return carry
        lax.fori_loop(0, tot_ref[step], issue, 0)

    @pl.when(i == 0)
    def _():
        stage[...] = jnp.zeros_like(stage)
        issue_tile(0, 0)

    @pl.when(i + 1 < nsteps)
    def _():
        issue_tile(jnp.minimum(i + 1, nsteps - 1), 1 - slot)

    def w(c, carry):
        chunk_copy(slot, 0, 0).wait()
        return carry
    lax.fori_loop(0, tot_ref[i], w, 0)

    tk = tokm_ref[...]
    pos = lax.broadcasted_iota(I32, (tm, sp), 1).astype(F32)
    st = _unpack_rows(_rows_from_tiles(stage.at[slot], sp))
    gsel = jnp.where(pos == tk[:, 2:3], tk[:, 0:1], jnp.where(pos == tk[:, 3:4], tk[:, 1:2], 0.0))
    m = jnp.dot(gsel.astype(BF16), st, preferred_element_type=F32)
    o_ref[...] = _ln(alpha * x_ref[...] + g2_ref[...] * m) * lng_ref[...] + lnb_ref[...]


def _combine(chunk_row, tile_nch, yb, x, tokm, g2, lng, lnb, *, ne, tm, chunk, n_per_batch, alpha):
    ttot, d = x.shape
    nsteps = ttot // tm
    per = n_per_batch // tm
    sp = 2 * tm + ne * chunk
    grid_spec = pltpu.PrefetchScalarGridSpec(
        num_scalar_prefetch=2,
        grid=(nsteps,),
        in_specs=[pl.BlockSpec(memory_space=pl.ANY),
                  pl.BlockSpec((tm, d), lambda i, *_: (i, 0)),
                  pl.BlockSpec((tm, LANES), lambda i, *_: (i, 0)),
                  pl.BlockSpec((None, 1, d), lambda i, *_: (i // per, 0, 0)),
                  pl.BlockSpec((1, d), lambda i, *_: (0, 0)),
                  pl.BlockSpec((1, d), lambda i, *_: (0, 0))],
        out_specs=pl.BlockSpec((tm, d), lambda i, *_: (i, 0)),
        scratch_shapes=[pltpu.VMEM((2, sp * PACK_SUB, LANES), yb.dtype), pltpu.SemaphoreType.DMA((2,))],
    )
    return pl.pallas_call(
        functools.partial(_combine_kernel, qmax=chunk_row.shape[0] // nsteps, chunk=chunk, nsteps=nsteps,
                          alpha=alpha),
        out_shape=jax.ShapeDtypeStruct((ttot, d), F32),
        grid_spec=grid_spec,
        compiler_params=_cp(("arbitrary",), VMEM_LIMIT),
        name="moe_combine",
    )(chunk_row, tile_nch, yb, x, tokm, g2, lng.reshape(1, d), lnb.reshape(1, d))


def _hier_moe_layer(x1, t, g2, lng, lnb, wg, bg, wf, bf, w1, w3, w2, *, layer, alpha):
    b, n, d = x1.shape
    ttot = b * n
    ng = wg.shape[1]
    ne = wf.shape[1]
    epg = ne // ng
    tm = min(MOE_TILE, n)
    chunk = MOE_CHUNK
    nt = ttot // tm
    tflat = t.reshape(ttot, d)
    wcat = jnp.zeros((d, LANES), F32).at[:, 0:ng].set(wg).at[:, SUBLANES:SUBLANES + ne].set(wf).astype(BF16)
    bcat = jnp.zeros((1, LANES), F32).at[0, 0:ng].set(bg).at[0, SUBLANES:SUBLANES + ne].set(bf)
    ar = jnp.arange(2 * tm, dtype=I32)
    upper = (ar[:, None] < ar[None, :]).astype(BF16)
    lp, tokm, cnt = _route(tflat, wcat, bcat, upper, ng=ng, epg=epg, tm=tm, chunk=chunk)

    tile_cnt = cnt.reshape(ne, nt, LANES)[:, :, 0].T.astype(I32)
    seg_len = ((tile_cnt + 1) // 2) * 2
    counts = jnp.sum(seg_len, axis=0)
    padded = ((counts + chunk + MOE_ROWS - 1) // MOE_ROWS) * MOE_ROWS
    pend = jnp.cumsum(padded)
    pstart = pend - padded
    base = jnp.cumsum(seg_len, axis=0) - seg_len
    seg_row = pstart[None, :] + base
    nch = (tile_cnt + chunk - 1) // chunk
    nch_end = jnp.cumsum(nch, axis=1)
    tile_nch = nch_end[:, -1].astype(I32)
    qmax = 2 * tm // chunk + ne
    qs = jnp.arange(qmax, dtype=I32)
    e_q = jnp.minimum(jnp.sum((nch_end[:, None, :] <= qs[None, :, None]).astype(I32), axis=2), ne - 1)
    pick = e_q[:, :, None] == jnp.arange(ne, dtype=I32)[None, None, :]
    first_q = jnp.sum(jnp.where(pick, (nch_end - nch)[:, None, :], 0), axis=2)
    first_row = jnp.sum(jnp.where(pick, seg_row[:, None, :], 0), axis=2)
    chunk_row = (first_row + (qs[None, :] - first_q) * chunk).astype(I32).reshape(nt * qmax)
    nb = -(-(2 * ttot + nt * ne + ne * (chunk + MOE_ROWS)) // MOE_ROWS)
    bstart = jnp.arange(nb, dtype=I32) * MOE_ROWS
    block_e = jnp.minimum(jnp.sum((pend[None, :] <= bstart[:, None]).astype(I32), axis=1), ne - 1)
    n_used = (pend[-1] // MOE_ROWS).astype(I32).reshape(1)

    xs = _dispatch(chunk_row, tile_nch, tflat, lp, n_rows=nb * MOE_ROWS, ne=ne, tm=tm, chunk=chunk)
    yb = _moe_experts(block_e.astype(I32), n_used, xs, w1, w3, w2, layer=layer)
    out = _combine(chunk_row, tile_nch, yb, x1.reshape(ttot, d), tokm, g2, lng, lnb,
                   ne=ne, tm=tm, chunk=chunk, n_per_batch=n, alpha=alpha)
    return out.reshape(b, n, d)


def kernel(x, c, ctx, c_ctx, ada_w, ada_b, ln_g, ln_b, ev_w_in, ev_conv_w, ev_conv_b, ev_gate_a_w,
           ev_gate_a_b, ev_gate_x_w, ev_gate_x_b, ev_lru_lambda, ev_da_lambda, ev_da_subln, ev_w_out,
           od_w_out, od_b_out, moe_wg, moe_bg, moe_wf, moe_bf, moe_w1, moe_w3, moe_w2):
    bsz, n_lat, d = x.shape
    depth = ada_w.shape[0]
    alpha = (2.0 * depth) ** 0.25
    lw = ev_conv_w.shape[-1]
    hd = ev_da_lambda.shape[-1]
    vd = ev_da_subln.shape[-1]
    aw = (ev_w_in.shape[-1] - 2 * lw) // 3
    heads = aw // vd
    fnet_groups = 4

    rows = ((bsz + 1 + SUBLANES - 1) // SUBLANES) * SUBLANES
    cond = jnp.zeros((rows, d), F32).at[0:bsz].set(c).at[bsz].set(c_ctx)
    ada = _ada_terms(cond, ada_w, ada_b).reshape(depth, rows, 6, d)

    def lat_term(l, k):
        return ada[l, 0:bsz, k, :].reshape(bsz, 1, d)

    def ctx_term(l, k):
        return jnp.broadcast_to(ada[l, bsz, k, :].reshape(1, 1, d), (bsz, 1, d))

    for l in range(depth):
        ctx_live = any(m % 2 == 0 for m in range(l + 1, depth))
        assert not ctx_live, "context stream update is only needed for depth > 2"
        sh1, sc1, g1, sh2, sc2, g2 = [lat_term(l, k) for k in range(6)]
        if l % 2 == 0:
            e = l // 2
            lam_init = 0.8 - 0.6 * math.exp(-0.3 * l)
            w_in_b = ev_w_in[e].astype(BF16)
            cos_t, sin_t = _rope_tables(n_lat, hd)
            qscale = hd ** -0.5 * math.log2(math.e)
            g_l, xr_l, q_l, k_l, v_l = _project_even(x, sh1, sc1, w_in_b, cos_t, sin_t, rope=True,
                                                     lw=lw, aw=aw, qscale=qscale, tm=512)
            n_ctx = ctx.shape[1]
            _, xr_c, _, k_c, v_c = _project_even(ctx, ctx_term(l, 0), ctx_term(l, 1), w_in_b,
                                                 cos_t[0:n_ctx], sin_t[0:n_ctx], rope=False,
                                                 lw=lw, aw=aw, qscale=qscale, tm=256)
            o_l = _diff_attention(q_l, k_c, k_l, v_c, v_l, ev_da_lambda[e], ev_da_subln[e],
                                  heads=heads, hd=hd, lam_init=lam_init)
            gwid = 256
            y = None
            for dirn, rev in ((0, False), (1, True)):
                wa_bd = _block_diag_groups(ev_gate_a_w[e, dirn], gwid)
                wx_bd = _block_diag_groups(ev_gate_x_w[e, dirn], gwid)
                common = (ev_conv_w[e], ev_conv_b[e], wa_bd, ev_gate_a_b[e, dirn], wx_bd,
                          ev_gate_x_b[e, dirn], ev_lru_lambda[e, dirn])
                h_zero = jnp.zeros((bsz, 1, lw), F32)
                _, h_fin = _rglru_dir(xr_c, None, *common, h_zero, reverse=rev, tn=256)
                y, _ = _rglru_dir(xr_l, y, *common, h_fin, reverse=rev, tn=256)
            x1, t = _outproj_even(y, g_l, o_l, x, ev_w_out[e].astype(BF16), g1, ln_g[l, 0], ln_b[l, 0],
                                  sh2, sc2, alpha=alpha, tm=512)
        else:
            o = l // 2
            gd = d // fnet_groups
            zr, zi = _chan_dft(x, sh1, sc1, _chan_dft_table(gd), groups=fnet_groups, tm=512)
            wv = _tok_fft(zr, zi)
            norm = 1.0 / math.sqrt(float(n_lat * gd))
            x1, t = _outproj_odd(wv, x, od_w_out[o].astype(BF16), od_b_out[o], g1, ln_g[l, 0], ln_b[l, 0],
                                 sh2, sc2, alpha=alpha, norm=norm, tm=512)
        x = _hier_moe_layer(x1, t, g2, ln_g[l, 1], ln_b[l, 1], moe_wg[l], moe_bg[l], moe_wf[l], moe_bf[l],
                            moe_w1, moe_w3, moe_w2, layer=l, alpha=alpha)
    return x
```

```python
import functools
import math

import numpy as np
import jax
import jax.numpy as jnp
from jax import lax
from jax.experimental import pallas as pl
from jax.experimental.pallas import tpu as pltpu

F32 = jnp.float32
BF16 = jnp.bfloat16
I32 = jnp.int32

LN_EPS = 1e-6
LRU_C = 8.0
ROPE_BASE = 10000.0
GRID_W = 64
CONV_W = 4
LANES = 128
SUBLANES = 8
MOE_ROWS = 512
MOE_TILE = 256
MOE_CHUNK = 8
VMEM_LIMIT = 56 * 1024 * 1024


def _cp(sem, vmem=None):
    return pltpu.CompilerParams(dimension_semantics=sem, vmem_limit_bytes=vmem)


def _ln(x):
    mu = jnp.mean(x, axis=-1, keepdims=True)
    xc = x - mu
    var = jnp.mean(xc * xc, axis=-1, keepdims=True)
    return xc * lax.rsqrt(var + LN_EPS)


def _silu(x):
    return x * jax.nn.sigmoid(x)


PACK_SUB = 4


def _tile_row(r, mult):
    if isinstance(r, int):
        return r * PACK_SUB
    return pl.multiple_of(r * PACK_SUB, mult)


def _pack_rows(val):
    half = val.shape[1] // 2
    lo = lax.bitcast_convert_type(val[:, 0:half], jnp.uint32) >> 16
    hi = lax.bitcast_convert_type(val[:, half:], jnp.uint32) & jnp.uint32(0xFFFF0000)
    return lo | hi


def _unpack_rows(words):
    lo = lax.bitcast_convert_type(words << 16, F32)
    hi = lax.bitcast_convert_type(words & jnp.uint32(0xFFFF0000), F32)
    return jnp.concatenate([lo, hi], axis=1).astype(BF16)


def _wait_chunks(copy_of, n, nmax):
    b = 1
    while b <= nmax:
        @pl.when((n & b) != 0)
        def _(b=b):
            copy_of(b).wait()
        b *= 2


def _rows_from_tiles(ref, nrows):
    return jnp.concatenate([ref[pl.ds(k, nrows, stride=PACK_SUB), :] for k in range(PACK_SUB)], axis=1)


def _rows_to_tiles(ref, val):
    nrows = val.shape[0]
    for k in range(PACK_SUB):
        ref[pl.ds(k, nrows, stride=PACK_SUB), :] = val[:, k * LANES:(k + 1) * LANES]


def _ada_kernel(c_ref, w_ref, b_ref, o_ref):
    s = _silu(c_ref[...]).astype(BF16)
    o_ref[...] = jnp.dot(s, w_ref[...].astype(BF16), preferred_element_type=F32) + b_ref[...]


def _ada_terms(cond, ada_w, ada_b):
    nl, d, d6 = ada_w.shape
    r = cond.shape[0]
    tn = 1024
    return pl.pallas_call(
        _ada_kernel,
        out_shape=jax.ShapeDtypeStruct((nl, r, d6), F32),
        grid=(nl, d6 // tn),
        in_specs=[pl.BlockSpec((r, d), lambda l, j: (0, 0)),
                  pl.BlockSpec((None, d, tn), lambda l, j: (l, 0, j)),
                  pl.BlockSpec((None, 1, tn), lambda l, j: (l, 0, j))],
        out_specs=pl.BlockSpec((None, r, tn), lambda l, j: (l, 0, j)),
        compiler_params=_cp(("parallel", "parallel")),
        name="ada_terms",
    )(cond, ada_w, ada_b.reshape(nl, 1, d6))


def _rope_apply(x, cos, sin_signed):
    tm = x.shape[0]
    lane = lax.broadcasted_iota(I32, (tm, LANES), 1)
    first_half = (lane % 32) < 16
    outs = []
    for j in range(x.shape[1] // LANES):
        xh = x[:, j * LANES:(j + 1) * LANES]
        partner = jnp.where(first_half, pltpu.roll(xh, LANES - 16, 1), pltpu.roll(xh, 16, 1))
        outs.append(xh * cos + partner * sin_signed)
    return jnp.concatenate(outs, axis=1)


def _proj_kernel(x_ref, sh_ref, sc_ref, w_ref, cos_ref, sin_ref,
                 g_ref, xr_ref, q_ref, k_ref, v_ref, *, rope, lw, aw, qscale):
    h = _ln(x_ref[...]) * (1.0 + sc_ref[...]) + sh_ref[...]
    hb = h.astype(BF16)

    def mm(c0, c1):
        return jnp.dot(hb, w_ref[:, c0:c1], preferred_element_type=F32)

    g_ref[...] = mm(0, lw).astype(BF16)
    xr_ref[...] = mm(lw, 2 * lw)
    q = mm(2 * lw, 2 * lw + aw)
    k = mm(2 * lw + aw, 2 * lw + 2 * aw)
    if rope:
        q = _rope_apply(q, cos_ref[...], sin_ref[...])
        k = _rope_apply(k, cos_ref[...], sin_ref[...])
    q_ref[...] = (q * qscale).astype(BF16)
    k_ref[...] = k.astype(BF16)
    v_ref[...] = mm(2 * lw + 2 * aw, 2 * lw + 3 * aw).astype(BF16)


def _project_even(x, shift, scale, w_in_b, cos_t, sin_t, *, rope, lw, aw, qscale, tm):
    b, n, d = x.shape
    tm = min(tm, n)
    nin = w_in_b.shape[1]
    tok = lambda bi, i: (bi, i, 0)
    per_b = lambda bi, i: (bi, 0, 0)
    outs = (jax.ShapeDtypeStruct((b, n, lw), BF16), jax.ShapeDtypeStruct((b, n, lw), F32),
            jax.ShapeDtypeStruct((b, n, aw), BF16), jax.ShapeDtypeStruct((b, n, aw), BF16),
            jax.ShapeDtypeStruct((b, n, aw), BF16))
    return pl.pallas_call(
        functools.partial(_proj_kernel, rope=rope, lw=lw, aw=aw, qscale=qscale),
        out_shape=outs,
        grid=(b, n // tm),
        in_specs=[pl.BlockSpec((None, tm, d), tok),
                  pl.BlockSpec((None, 1, d), per_b),
                  pl.BlockSpec((None, 1, d), per_b),
                  pl.BlockSpec((d, nin), lambda bi, i: (0, 0)),
                  pl.BlockSpec((tm, LANES), lambda bi, i: (i, 0)),
                  pl.BlockSpec((tm, LANES), lambda bi, i: (i, 0))],
        out_specs=(pl.BlockSpec((None, tm, lw), tok), pl.BlockSpec((None, tm, lw), tok),
                   pl.BlockSpec((None, tm, aw), tok), pl.BlockSpec((None, tm, aw), tok),
                   pl.BlockSpec((None, tm, aw), tok)),
        compiler_params=_cp(("parallel", "parallel"), VMEM_LIMIT),
        name="proj_even_rope" if rope else "proj_even_ctx",
    )(x, shift, scale, w_in_b, cos_t, sin_t)


def _rope_tables(n_tok, head_dim):
    t = jnp.arange(n_tok)
    row = (t // GRID_W).astype(F32)
    col = (t % GRID_W).astype(F32)
    nf = head_dim // 4
    freqs = ROPE_BASE ** (-jnp.arange(nf, dtype=F32) / nf)
    lane = np.arange(LANES)
    within = lane % head_dim
    axis = within // (2 * nf)
    half = (within % (2 * nf)) // nf
    f = within % nf
    pos = jnp.where(jnp.asarray(axis)[None, :] == 0, row[:, None], col[:, None])
    ang = pos * freqs[jnp.asarray(f)][None, :]
    sign = jnp.asarray(np.where(half == 0, -1.0, 1.0), F32)[None, :]
    return jnp.cos(ang).astype(F32), (jnp.sin(ang) * sign).astype(F32)


def _attn_kernel(q_ref, kc_ref, kl_ref, vc_ref, vl_ref, dl_ref, gain_ref, o_ref, kbuf, vbuf, sbuf, ebuf, abuf,
                 cbuf, *, nc, nl, hd, lam_init, rows):
    @pl.when((pl.program_id(0) == 0) & (pl.program_id(1) == 0))
    def _():
        sbuf[...] = jnp.zeros_like(sbuf)
        abuf[...] = jnp.zeros_like(abuf)
        cbuf[...] = jnp.zeros_like(cbuf)

    kbuf[0:nc, :] = kc_ref[...]
    kbuf[nc:nc + nl, :] = kl_ref[...]
    vbuf[0:nc, :] = vc_ref[...]
    vbuf[nc:nc + nl, :] = vl_ref[...]

    lf = dl_ref[...]
    lam = (jnp.exp(jnp.sum(lf[0:1] * lf[1:2], axis=1, keepdims=True))
           - jnp.exp(jnp.sum(lf[2:3] * lf[3:4], axis=1, keepdims=True)) + lam_init)
    gain = gain_ref[...] * (1.0 - lam_init)
    n_sub = nl // rows
    lane = lax.broadcasted_iota(I32, (rows, 2 * hd), 1)
    nt = (((1,), (1,)), ((), ()))

    def stage_a(j, slot):
        r0 = pl.multiple_of(jnp.minimum(j, n_sub - 1) * rows, rows)
        q = q_ref[pl.ds(r0, rows), :]
        zero = jnp.zeros_like(q)
        kk = kbuf[...]
        sbuf[slot, 0] = lax.dot_general(jnp.where(lane < hd, q, zero), kk, nt, preferred_element_type=F32)
        sbuf[slot, 1] = lax.dot_general(jnp.where(lane >= hd, q, zero), kk, nt, preferred_element_type=F32)

    def stage_b(slot):
        cols = [slice(c, c + LANES) for c in range(0, nc + nl, LANES)]
        ls = []
        for k in range(2):
            pm = sbuf[slot, k, :, cols[0]]
            for cs in cols[1:]:
                pm = jnp.maximum(pm, sbuf[slot, k, :, cs])
            m = jnp.max(pm, axis=1, keepdims=True)
            acc = jnp.zeros((rows, LANES), F32)
            for cs in cols:
                e = jnp.exp2(sbuf[slot, k, :, cs] - m)
                acc = acc + e
                ebuf[slot, k, :, cs] = e.astype(BF16)
            ls.append(jnp.sum(acc, axis=1, keepdims=True))
        ratio = (lam * ls[0] / ls[1]).astype(BF16)
        abuf[slot] = ebuf[slot, 0] - ratio * ebuf[slot, 1]
        cbuf[slot] = jnp.broadcast_to(1.0 / ls[0], cbuf.shape[1:])

    def stage_c(j, slot):
        r0 = pl.multiple_of(jnp.maximum(j - 2, 0) * rows, rows)
        o = jnp.dot(abuf[slot], vbuf[...], preferred_element_type=F32) * cbuf[slot]
        o = o * lax.rsqrt(jnp.mean(o * o, axis=1, keepdims=True) + LN_EPS) * gain
        o_ref[pl.ds(r0, rows), :] = o.astype(BF16)

    def body(t, carry):
        j = 2 * t
        stage_a(j, 0)
        stage_b(1)
        stage_c(j, 0)
        stage_a(j + 1, 1)
        stage_b(0)
        stage_c(j + 1, 1)
        return carry

    lax.fori_loop(0, (n_sub + 2) // 2, body, 0)


def _diff_attention(q, k_ctx, k_lat, v_ctx, v_lat, da_lambda, subln, *, heads, hd, lam_init):
    b, n, aw = q.shape
    nc = k_ctx.shape[1]
    vd = aw // heads
    rows = min(128, n)
    assert (n // rows) % 2 == 0
    blk_q = pl.BlockSpec((None, n, vd), lambda bi, h: (bi, 0, h))
    blk_c = pl.BlockSpec((None, nc, vd), lambda bi, h: (bi, 0, h))
    return pl.pallas_call(
        functools.partial(_attn_kernel, nc=nc, nl=n, hd=hd, lam_init=lam_init, rows=rows),
        out_shape=jax.ShapeDtypeStruct((b, n, aw), BF16),
        grid=(b, heads),
        in_specs=[blk_q, blk_c, blk_q, blk_c, blk_q,
                  pl.BlockSpec(da_lambda.shape, lambda bi, h: (0, 0)),
                  pl.BlockSpec((1, vd), lambda bi, h: (0, 0))],
        out_specs=blk_q,
        scratch_shapes=[pltpu.VMEM((nc + n, vd), BF16), pltpu.VMEM((nc + n, vd), BF16),
                        pltpu.VMEM((2, 2, rows, nc + n), F32), pltpu.VMEM((2, 2, rows, nc + n), BF16),
                        pltpu.VMEM((2, rows, nc + n), BF16), pltpu.VMEM((2, rows, vd), F32)],
        compiler_params=_cp(("arbitrary", "arbitrary"), VMEM_LIMIT),
        name="diff_attention",
    )(q, k_ctx, k_lat, v_ctx, v_lat, da_lambda, subln.reshape(1, vd))


def _lru_kernel(*refs, reverse, add_prev, nt, groups):
    if add_prev:
        (xp_ref, xc_ref, xn_ref, yprev_ref, cw_ref, cb_ref, wa_ref, ba_ref, wx_ref, bx_ref,
         lam_ref, h0_ref, y_ref, hf_ref, carry, ext) = refs
    else:
        (xp_ref, xc_ref, xn_ref, cw_ref, cb_ref, wa_ref, ba_ref, wx_ref, bx_ref,
         lam_ref, h0_ref, y_ref, hf_ref, carry, ext) = refs
        yprev_ref = None
    i = pl.program_id(1)
    ti = (nt - 1 - i) if reverse else i
    tn, w = xc_ref.shape

    @pl.when(i == 0)
    def _():
        carry[...] = h0_ref[...]

    ext[0:SUBLANES, :] = jnp.where(ti > 0, xp_ref[...], 0.0)
    ext[SUBLANES:SUBLANES + tn, :] = xc_ref[...]
    ext[SUBLANES + tn:2 * SUBLANES + tn, :] = jnp.where(ti < nt - 1, xn_ref[...], 0.0)
    left = CONV_W // 2
    xc = cb_ref[...]
    for k in range(CONV_W):
        off = SUBLANES - left + k
        xc = xc + ext[off:off + tn, :] * cw_ref[k:k + 1, :]

    xb = xc.astype(BF16)
    gw = w // groups

    def gate(w_ref, b_ref):
        parts = [jnp.dot(xb[:, g * gw:(g + 1) * gw], w_ref[g], preferred_element_type=F32)
                 for g in range(groups)]
        return jax.nn.sigmoid(jnp.concatenate(parts, axis=1) + b_ref[...])

    r = gate(wa_ref, ba_ref)
    ig = gate(wx_ref, bx_ref)
    log_a = (-LRU_C * jax.nn.softplus(-lam_ref[...])) * r
    a = jnp.exp(log_a)
    bcoef = jnp.sqrt(-jnp.tanh(log_a) * (a * a + 1.0)) * ig * xc

    ngroups = tn // SUBLANES
    a = a.reshape(ngroups, SUBLANES, w)
    bcoef = bcoef.reshape(ngroups, SUBLANES, w)
    row = lax.broadcasted_iota(I32, (ngroups, SUBLANES, w), 1)
    d = 1
    while d < SUBLANES:
        shift = (SUBLANES - d) if reverse else d
        a_sh = pltpu.roll(a, shift, 1)
        b_sh = pltpu.roll(bcoef, shift, 1)
        live = (row < SUBLANES - d) if reverse else (row >= d)
        bcoef = jnp.where(live, a * b_sh + bcoef, bcoef)
        a = jnp.where(live, a * a_sh, a)
        d *= 2
    hc = carry[...]
    for g in (range(ngroups - 1, -1, -1) if reverse else range(ngroups)):
        rs = slice(g * SUBLANES, (g + 1) * SUBLANES)
        h = a[g] * hc + bcoef[g]
        hc = h[0:1, :] if reverse else h[SUBLANES - 1:SUBLANES, :]
        y_ref[rs, :] = (yprev_ref[rs, :] + h) if add_prev else h
    carry[...] = hc
    hf_ref[...] = hc


def _rglru_dir(xr, y_prev, conv_w, conv_b, wa_bd, ba, wx_bd, bx, lam, h0, *, reverse, tn):
    b, n, w = xr.shape
    tn = min(tn, n)
    nt = n // tn
    groups = wa_bd.shape[0]
    nb8 = n // SUBLANES
    per8 = tn // SUBLANES

    def tmap(i):
        return (nt - 1 - i) if reverse else i

    cur = pl.BlockSpec((None, tn, w), lambda bi, i: (bi, tmap(i), 0))
    halo_p = pl.BlockSpec((None, SUBLANES, w), lambda bi, i: (bi, jnp.maximum(tmap(i) * per8 - 1, 0), 0))
    halo_n = pl.BlockSpec((None, SUBLANES, w), lambda bi, i: (bi, jnp.minimum((tmap(i) + 1) * per8, nb8 - 1), 0))
    row_w = pl.BlockSpec((1, w), lambda bi, i: (0, 0))
    per_b = pl.BlockSpec((None, 1, w), lambda bi, i: (bi, 0, 0))
    gate_w = pl.BlockSpec(wa_bd.shape, lambda bi, i: (0, 0, 0))
    add_prev = y_prev is not None
    in_specs = [halo_p, cur, halo_n] + ([cur] if add_prev else []) + [
        pl.BlockSpec((CONV_W, w), lambda bi, i: (0, 0)), row_w, gate_w, row_w, gate_w, row_w, row_w, per_b]
    args = [xr, xr, xr] + ([y_prev] if add_prev else []) + [
        conv_w, conv_b.reshape(1, w), wa_bd, ba.reshape(1, w), wx_bd, bx.reshape(1, w),
        lam.reshape(1, w), h0]
    return pl.pallas_call(
        functools.partial(_lru_kernel, reverse=reverse, add_prev=add_prev, nt=nt, groups=groups),
        out_shape=(jax.ShapeDtypeStruct((b, n, w), F32), jax.ShapeDtypeStruct((b, 1, w), F32)),
        grid=(b, nt),
        in_specs=in_specs,
        out_specs=(cur, per_b),
        scratch_shapes=[pltpu.VMEM((1, w), F32), pltpu.VMEM((tn + 2 * SUBLANES, w), F32)],
        compiler_params=_cp(("parallel", "arbitrary"), VMEM_LIMIT),
        name="rglru_rev" if reverse else "rglru_fwd",
    )(*args)


def _block_diag_groups(wh, group_width):
    heads, blk, _ = wh.shape
    per = group_width // blk
    groups = heads // per
    whg = wh.reshape(groups, per, blk, blk)
    eye = jnp.eye(per, dtype=wh.dtype)
    bd = jnp.einsum('gpij,pq->gpiqj', whg, eye).reshape(groups, group_width, group_width)
    return bd.astype(BF16)


def _residual_ln_mod(x, y, g1, lng, lnb, sh2, sc2, alpha):
    x1 = _ln(alpha * x + g1 * y) * lng + lnb
    t = _ln(x1) * (1.0 + sc2) + sh2
    return x1, t


def _outproj_even_kernel(r_ref, g_ref, o_ref, x_ref, w_ref, g1_ref, lng_ref, lnb_ref, sh2_ref, sc2_ref,
                         x1_ref, t_ref, *, lw, alpha):
    z = (r_ref[...] * jax.nn.gelu(g_ref[...].astype(F32))).astype(BF16)
    y = (jnp.dot(z, w_ref[0:lw, :], preferred_element_type=F32)
         + jnp.dot(o_ref[...], w_ref[lw:, :], preferred_element_type=F32))
    x1, t = _residual_ln_mod(x_ref[...], y, g1_ref[...], lng_ref[...], lnb_ref[...],
                             sh2_ref[...], sc2_ref[...], alpha)
    x1_ref[...] = x1
    t_ref[...] = t.astype(BF16)


def _outproj_even(r, g, o, x, w_out_b, g1, lng, lnb, sh2, sc2, *, alpha, tm):
    b, n, d = x.shape
    lw = r.shape[2]
    aw = o.shape[2]
    tm = min(tm, n)
    tok = lambda bi, i: (bi, i, 0)
    per_b = pl.BlockSpec((None, 1, d), lambda bi, i: (bi, 0, 0))
    row = pl.BlockSpec((1, d), lambda bi, i: (0, 0))
    return pl.pallas_call(
        functools.partial(_outproj_even_kernel, lw=lw, alpha=alpha),
        out_shape=(jax.ShapeDtypeStruct((b, n, d), F32), jax.ShapeDtypeStruct((b, n, d), BF16)),
        grid=(b, n // tm),
        in_specs=[pl.BlockSpec((None, tm, lw), tok), pl.BlockSpec((None, tm, lw), tok),
                  pl.BlockSpec((None, tm, aw), tok), pl.BlockSpec((None, tm, d), tok),
                  pl.BlockSpec(w_out_b.shape, lambda bi, i: (0, 0)),
                  per_b, row, row, per_b, per_b],
        out_specs=(pl.BlockSpec((None, tm, d), tok), pl.BlockSpec((None, tm, d), tok)),
        compiler_params=_cp(("parallel", "parallel"), VMEM_LIMIT),
        name="outproj_even",
    )(r, g, o, x, w_out_b, g1, lng.reshape(1, d), lnb.reshape(1, d), sh2, sc2)


def _chan_dft_kernel(x_ref, sh_ref, sc_ref, tab_ref, zr_ref, zi_ref, *, groups, gd):
    h = (_ln(x_ref[...]) * (1.0 + sc_ref[...]) + sh_ref[...]).astype(BF16)
    for g in range(groups):
        z = jnp.dot(h[:, g * gd:(g + 1) * gd], tab_ref[...], preferred_element_type=F32)
        zr_ref[:, g * gd:(g + 1) * gd] = z[:, 0:gd].astype(BF16)
        zi_ref[:, g * gd:(g + 1) * gd] = z[:, gd:2 * gd].astype(BF16)


def _chan_dft(x, shift, scale, tab, *, groups, tm):
    b, n, d = x.shape
    gd = d // groups
    tm = min(tm, n)
    tok = lambda bi, i: (bi, i, 0)
    per_b = pl.BlockSpec((None, 1, d), lambda bi, i: (bi, 0, 0))
    return pl.pallas_call(
        functools.partial(_chan_dft_kernel, groups=groups, gd=gd),
        out_shape=(jax.ShapeDtypeStruct((b, n, d), BF16), jax.ShapeDtypeStruct((b, n, d), BF16)),
        grid=(b, n // tm),
        in_specs=[pl.BlockSpec((None, tm, d), tok), per_b, per_b,
                  pl.BlockSpec(tab.shape, lambda bi, i: (0, 0))],
        out_specs=(pl.BlockSpec((None, tm, d), tok), pl.BlockSpec((None, tm, d), tok)),
        compiler_params=_cp(("parallel", "parallel"), VMEM_LIMIT),
        name="chan_dft",
    )(x, shift, scale, tab)


FFT_C = 64


def _fft_pitch(group):
    p = -(-group // SUBLANES)
    return SUBLANES * (p if p % 2 else p + 1)


def _tok_fft_kernel(zr_ref, zi_ref, m1_ref, m3_ref, tc_ref, ts_ref, o_ref, zsr, zsi, asr, asi, ob, *, nr):
    c_len = FFT_C
    pz = zsr.shape[0] // nr
    pa = asr.shape[0] // c_len
    for r in range(nr):
        zsr[pz * r:pz * r + c_len, :] = zr_ref[c_len * r:c_len * (r + 1), :].astype(F32)
        zsi[pz * r:pz * r + c_len, :] = zi_ref[c_len * r:c_len * (r + 1), :].astype(F32)
    m1 = m1_ref[...]
    for c in range(c_len):
        x2 = jnp.concatenate([zsr[pl.ds(c, nr, stride=pz), :], zsi[pl.ds(c, nr, stride=pz), :]], axis=0)
        a2 = jnp.dot(m1, x2.astype(BF16), preferred_element_type=F32)
        ar, ai = a2[0:nr], a2[nr:2 * nr]
        tcv = tc_ref[c * nr:(c + 1) * nr, :]
        tsv = ts_ref[c * nr:(c + 1) * nr, :]
        asr[pa * c:pa * c + nr, :] = ar * tcv + ai * tsv
        asi[pa * c:pa * c + nr, :] = ai * tcv - ar * tsv
    m3 = m3_ref[...]
    for k1 in range(nr):
        y2 = jnp.concatenate([asr[pl.ds(k1, c_len, stride=pa), :], asi[pl.ds(k1, c_len, stride=pa), :]], axis=0)
        ob[pl.ds(k1, c_len, stride=pa), :] = jnp.dot(m3, y2.astype(BF16), preferred_element_type=F32)
    for k2 in range(c_len):
        o_ref[nr * k2:nr * (k2 + 1), :] = ob[pa * k2:pa * k2 + nr, :].astype(BF16)


def _tok_fft(zr, zi):
    b, n, d = zr.shape
    nr = n // FFT_C
    pz = _fft_pitch(FFT_C)
    pa = _fft_pitch(nr)
    kr = np.arange(nr, dtype=np.float64)
    ang_r = 2.0 * np.pi * (np.outer(kr, kr) % nr) / nr
    cr, sr = np.cos(ang_r), np.sin(ang_r)
    m1 = jnp.asarray(np.block([[cr, -sr], [-sr, -cr]]), F32).astype(BF16)
    kc = np.arange(FFT_C, dtype=np.float64)
    ang_c = 2.0 * np.pi * (np.outer(kc, kc) % FFT_C) / FFT_C
    m3 = jnp.asarray(np.concatenate([np.cos(ang_c), np.sin(ang_c)], axis=1), F32).astype(BF16)
    ang_t = 2.0 * np.pi * (np.outer(kc, kr) % n) / n
    tc = jnp.broadcast_to(jnp.asarray(np.cos(ang_t).reshape(FFT_C * nr, 1), F32), (FFT_C * nr, LANES))
    ts = jnp.broadcast_to(jnp.asarray(np.sin(ang_t).reshape(FFT_C * nr, 1), F32), (FFT_C * nr, LANES))
    slab = pl.BlockSpec((None, n, LANES), lambda bi, l: (bi, 0, l))
    const = lambda a: pl.BlockSpec(a.shape, lambda bi, l: (0, 0))
    return pl.pallas_call(
        functools.partial(_tok_fft_kernel, nr=nr),
        out_shape=jax.ShapeDtypeStruct((b, n, d), BF16),
        grid=(b, d // LANES),
        in_specs=[slab, slab, const(m1), const(m3), const(tc), const(ts)],
        out_specs=slab,
        scratch_shapes=[pltpu.VMEM((nr * pz, LANES), F32), pltpu.VMEM((nr * pz, LANES), F32),
                        pltpu.VMEM((FFT_C * pa, LANES), F32), pltpu.VMEM((FFT_C * pa, LANES), F32),
                        pltpu.VMEM((FFT_C * pa, LANES), F32)],
        compiler_params=_cp(("parallel", "parallel"), VMEM_LIMIT),
        name="tok_fft",
    )(zr, zi, m1, m3, tc, ts)


def _chan_dft_table(gd):
    c = np.arange(gd, dtype=np.float64)
    ang_c = 2.0 * np.pi * (np.outer(c, c) % gd) / gd
    return jnp.asarray(np.concatenate([np.cos(ang_c), np.sin(ang_c)], axis=1), F32).astype(BF16)


def _outproj_odd_kernel(wv_ref, x_ref, w_ref, b_ref, g1_ref, lng_ref, lnb_ref, sh2_ref, sc2_ref,
                        x1_ref, t_ref, *, alpha, norm):
    y = jnp.dot(wv_ref[...], w_ref[...], preferred_element_type=F32) * norm + b_ref[...]
    x1, t = _residual_ln_mod(x_ref[...], y, g1_ref[...], lng_ref[...], lnb_ref[...],
                             sh2_ref[...], sc2_ref[...], alpha)
    x1_ref[...] = x1
    t_ref[...] = t.astype(BF16)


def _outproj_odd(wv, x, w_b, bias, g1, lng, lnb, sh2, sc2, *, alpha, norm, tm):
    b, n, d = x.shape
    tm = min(tm, n)
    tok = lambda bi, i: (bi, i, 0)
    per_b = pl.BlockSpec((None, 1, d), lambda bi, i: (bi, 0, 0))
    row = pl.BlockSpec((1, d), lambda bi, i: (0, 0))
    return pl.pallas_call(
        functools.partial(_outproj_odd_kernel, alpha=alpha, norm=norm),
        out_shape=(jax.ShapeDtypeStruct((b, n, d), F32), jax.ShapeDtypeStruct((b, n, d), BF16)),
        grid=(b, n // tm),
        in_specs=[pl.BlockSpec((None, tm, d), tok), pl.BlockSpec((None, tm, d), tok),
                  pl.BlockSpec(w_b.shape, lambda bi, i: (0, 0)), row,
                  per_b, row, row, per_b, per_b],
        out_specs=(pl.BlockSpec((None, tm, d), tok), pl.BlockSpec((None, tm, d), tok)),
        compiler_params=_cp(("parallel", "parallel"), VMEM_LIMIT),
        name="outproj_odd",
    )(wv, x, w_b, bias.reshape(1, d), g1, lng.reshape(1, d), lnb.reshape(1, d), sh2, sc2)


def _route_kernel(t_ref, w_ref, b_ref, up_ref, lp_ref, tokm_ref, cnt_ref, *, ng, epg, chunk):
    tm = t_ref.shape[0]
    ne = ng * epg
    logits = jnp.dot(t_ref[...].astype(BF16), w_ref[...], preferred_element_type=F32) + b_ref[...]
    lt = logits.T
    best = lt[0:1, :]
    bi = jnp.zeros((1, tm), I32)
    for k in range(1, ng):
        gk = lt[k:k + 1, :]
        upd = gk > best
        bi = jnp.where(upd, k, bi)
        best = jnp.where(upd, gk, best)
    den = jnp.zeros((1, tm), F32)
    for k in range(ng):
        den = den + jnp.exp(lt[k:k + 1, :] - best)
    p_g = 1.0 / den
    fsel = lt[SUBLANES:SUBLANES + epg, :]
    for k in range(1, ng):
        fsel = jnp.where(bi == k, lt[SUBLANES + k * epg:SUBLANES + (k + 1) * epg, :], fsel)
    neg = jnp.full((1, tm), -jnp.inf, F32)
    m1, m2 = neg, neg
    i1 = jnp.zeros((1, tm), I32)
    i2 = jnp.zeros((1, tm), I32)
    for j in range(epg):
        v = fsel[j:j + 1, :]
        gt1 = v > m1
        gt2 = v > m2
        m2 = jnp.where(gt1, m1, jnp.where(gt2, v, m2))
        i2 = jnp.where(gt1, i1, jnp.where(gt2, j, i2))
        m1 = jnp.where(gt1, v, m1)
        i1 = jnp.where(gt1, j, i1)
    e21 = jnp.exp(m2 - m1)
    w1 = p_g / (1.0 + e21)
    w2 = p_g * e21 / (1.0 + e21)
    e1 = bi * epg + i1
    e2 = bi * epg + i2

    e = jnp.concatenate([e1, e2], axis=1)
    rows = lax.broadcasted_iota(I32, (ne, 2 * tm), 0)
    onehot = jnp.where(rows == e, 1.0, 0.0)
    before = jnp.dot(onehot.astype(BF16), up_ref[...], preferred_element_type=F32)
    tot = jnp.sum(onehot, axis=1, keepdims=True)
    slots = jnp.floor((tot + (chunk - 1.0)) * (1.0 / chunk)) * chunk
    slots_b = jnp.broadcast_to(slots, (ne, LANES))
    rowe = lax.broadcasted_iota(I32, (ne, LANES), 0)
    incl = slots_b
    d = 1
    while d < ne:
        incl = incl + jnp.where(rowe >= d, pltpu.roll(incl, d, 0), 0.0)
        d *= 2
    seg_off = jnp.tile(incl - slots_b, (1, 2 * tm // LANES))
    lpos = jnp.sum(onehot * (before + seg_off), axis=0, keepdims=True)
    lp0 = lpos[:, 0:tm]
    lp1 = lpos[:, tm:2 * tm]
    row8 = lax.broadcasted_iota(I32, (SUBLANES, tm), 0)
    lp_ref[...] = jnp.where(row8 == 0, lp0, jnp.where(row8 == 1, lp1, 0.0)).astype(I32)
    rowl = lax.broadcasted_iota(I32, (LANES, tm), 0)
    tokm = jnp.where(rowl == 0, w1, jnp.where(rowl == 1, w2, jnp.where(rowl == 2, lp0, jnp.where(rowl == 3, lp1, 0.0))))
    tokm_ref[...] = tokm.T
    cnt_ref[...] = jnp.broadcast_to(tot, (ne, LANES))


def _route(t, wcat_b, bcat, upper, *, ng, epg, tm, chunk):
    tt, d = t.shape
    ne = ng * epg
    nt = tt // tm
    return pl.pallas_call(
        functools.partial(_route_kernel, ng=ng, epg=epg, chunk=chunk),
        out_shape=(jax.ShapeDtypeStruct((SUBLANES, tt), I32), jax.ShapeDtypeStruct((tt, LANES), F32),
                   jax.ShapeDtypeStruct((ne, nt * LANES), F32)),
        grid=(nt,),
        in_specs=[pl.BlockSpec((tm, d), lambda i: (i, 0)),
                  pl.BlockSpec((d, LANES), lambda i: (0, 0)),
                  pl.BlockSpec((1, LANES), lambda i: (0, 0)),
                  pl.BlockSpec(upper.shape, lambda i: (0, 0))],
        out_specs=(pl.BlockSpec((SUBLANES, tm), lambda i: (0, i)),
                   pl.BlockSpec((tm, LANES), lambda i: (i, 0)),
                   pl.BlockSpec((ne, LANES), lambda i: (0, i))),
        compiler_params=_cp(("parallel",), VMEM_LIMIT),
        name="route_sort",
    )(t, wcat_b, bcat, upper)


def _dispatch_kernel(row_ref, tot_ref, t_ref, lp_ref, xs_in_ref, xs_ref, stage, sems, *, qmax, chunk, nsteps):
    del xs_in_ref
    i = pl.program_id(0)
    slot = i % 2
    sp = stage.shape[1] // PACK_SUB
    tm = t_ref.shape[0]
    crow = chunk * PACK_SUB

    lp = lp_ref[...]
    prow = lax.broadcasted_iota(I32, (sp, tm), 0)
    perm = jnp.where(prow == lp[0:1, :], 1.0, jnp.where(prow == lp[1:2, :], 1.0, 0.0)).astype(BF16)
    srt = jnp.dot(perm, t_ref[...], preferred_element_type=F32)
    _rows_to_tiles(stage.at[slot], _pack_rows(srt))

    def chunk_copy(sl, src, dst):
        return pltpu.make_async_copy(stage.at[sl, pl.ds(_tile_row(src, crow), crow)],
                                     xs_ref.at[pl.ds(_tile_row(dst, SUBLANES), crow)], sems.at[sl])

    def drain(sl, n):
        def copy_of(k):
            return pltpu.make_async_copy(stage.at[sl, pl.ds(0, k * crow)], xs_ref.at[pl.ds(0, k * crow)], sems.at[sl])
        _wait_chunks(copy_of, n, qmax)

    @pl.when(i >= 1)
    def _():
        drain(1 - slot, tot_ref[jnp.maximum(i - 1, 0)])

    def issue(q, carry):
        chunk_copy(slot, pl.multiple_of(q * chunk, chunk), row_ref[i * qmax + q]).start()
        return carry
    lax.fori_loop(0, tot_ref[i], issue, 0)

    @pl.when(i == nsteps - 1)
    def _():
        drain(slot, tot_ref[i])


def _dispatch(chunk_row, tile_nch, t, lp, *, n_rows, ne, tm, chunk):
    ttot, d = t.shape
    nsteps = ttot // tm
    sp = 2 * tm + ne * chunk
    xs_init = jnp.zeros((n_rows * PACK_SUB, LANES), jnp.uint32)
    grid_spec = pltpu.PrefetchScalarGridSpec(
        num_scalar_prefetch=2,
        grid=(nsteps,),
        in_specs=[pl.BlockSpec((tm, d), lambda i, *_: (i, 0)),
                  pl.BlockSpec((SUBLANES, tm), lambda i, *_: (0, i)),
                  pl.BlockSpec(memory_space=pl.ANY)],
        out_specs=pl.BlockSpec(memory_space=pl.ANY),
        scratch_shapes=[pltpu.VMEM((2, sp * PACK_SUB, LANES), jnp.uint32), pltpu.SemaphoreType.DMA((2,))],
    )
    assert d == 2 * PACK_SUB * LANES
    return pl.pallas_call(
        functools.partial(_dispatch_kernel, qmax=chunk_row.shape[0] // nsteps, chunk=chunk, nsteps=nsteps),
        out_shape=jax.ShapeDtypeStruct(xs_init.shape, xs_init.dtype),
        grid_spec=grid_spec,
        input_output_aliases={4: 0},
        compiler_params=_cp(("arbitrary",), VMEM_LIMIT),
        name="moe_dispatch",
    )(chunk_row, tile_nch, t, lp, xs_init)


def _moe_kernel(be_ref, nu_ref, xs_ref, w1_ref, w3_ref, w2_ref, y_ref, w13b, w2b, *, ff):
    i = pl.program_id(0)

    @pl.when(i < nu_ref[0])
    def _():
        prev = be_ref[jnp.maximum(i - 1, 0)]

        @pl.when((i == 0) | (be_ref[i] != prev))
        def _():
            w13b[:, 0:ff] = w1_ref[...].astype(BF16)
            w13b[:, ff:2 * ff] = w3_ref[...].astype(BF16)
            w2b[...] = w2_ref[...].astype(BF16)

        x = _unpack_rows(_rows_from_tiles(xs_ref, MOE_ROWS))
        h = jnp.dot(x, w13b[...], preferred_element_type=F32)
        hid = (_silu(h[:, 0:ff]) * h[:, ff:2 * ff]).astype(BF16)
        y = jnp.dot(hid, w2b[...], preferred_element_type=F32)
        _rows_to_tiles(y_ref, _pack_rows(y.astype(BF16).astype(F32)))

    @pl.when(i >= nu_ref[0])
    def _():
        y_ref[...] = jnp.zeros_like(y_ref)


def _moe_experts(block_e, n_used, xs, w1, w3, w2, *, layer):
    d = w1.shape[-2]
    ff = w1.shape[-1]
    blk = MOE_ROWS * PACK_SUB
    nb = xs.shape[0] // blk
    wmap = lambda i, be, nu: (layer, be[i], 0, 0)
    grid_spec = pltpu.PrefetchScalarGridSpec(
        num_scalar_prefetch=2,
        grid=(nb,),
        in_specs=[pl.BlockSpec((blk, LANES), lambda i, be, nu: (i, 0)),
                  pl.BlockSpec((None, None, d, ff), wmap),
                  pl.BlockSpec((None, None, d, ff), wmap),
                  pl.BlockSpec((None, None, ff, d), wmap)],
        out_specs=pl.BlockSpec((blk, LANES), lambda i, be, nu: (i, 0)),
        scratch_shapes=[pltpu.VMEM((d, 2 * ff), BF16), pltpu.VMEM((ff, d), BF16)],
    )
    return pl.pallas_call(
        functools.partial(_moe_kernel, ff=ff),
        out_shape=jax.ShapeDtypeStruct(xs.shape, xs.dtype),
        grid_spec=grid_spec,
        compiler_params=_cp(("arbitrary",), VMEM_LIMIT),
        name="moe_experts",
    )(block_e, n_used, xs, w1, w3, w2)


def _combine_kernel(row_ref, tot_ref, yb_ref, x_ref, tokm_ref, g2_ref, lng_ref, lnb_ref,
                    o_ref, stage, sems, *, qmax, chunk, nsteps, alpha):
    i = pl.program_id(0)
    slot = i % 2
    sp = stage.shape[1] // PACK_SUB
    tm = x_ref.shape[0]
    crow = chunk * PACK_SUB

    def chunk_copy(sl, src, dst):
        return pltpu.make_async_copy(yb_ref.at[pl.ds(_tile_row(src, SUBLANES), crow)],
                                     stage.at[sl, pl.ds(_tile_row(dst, crow), crow)], sems.at[sl])

    def issue_tile(step, sl):
        def issue(q, carry):
            chunk_copy(sl, row_ref[step * qmax + q], pl.multiple_of(q * chunk, chunk)).start()
            return carry
        lax.fori_loop(0, tot_ref[step], issue, 0)

    @pl.when(i == 0)
    def _():
        stage[...] = jnp.zeros_like(stage)
        issue_tile(0, 0)

    @pl.when(i + 1 < nsteps)
    def _():
        issue_tile(jnp.minimum(i + 1, nsteps - 1), 1 - slot)

    def copy_of(k):
        return pltpu.make_async_copy(yb_ref.at[pl.ds(0, k * crow)], stage.at[slot, pl.ds(0, k * crow)], sems.at[slot])
    _wait_chunks(copy_of, tot_ref[i], qmax)

    tk = tokm_ref[...]
    pos = lax.broadcasted_iota(I32, (tm, sp), 1).astype(F32)
    st = _unpack_rows(_rows_from_tiles(stage.at[slot], sp))
    gsel = jnp.where(pos == tk[:, 2:3], tk[:, 0:1], jnp.where(pos == tk[:, 3:4], tk[:, 1:2], 0.0))
    m = jnp.dot(gsel.astype(BF16), st, preferred_element_type=F32)
    o_ref[...] = _ln(alpha * x_ref[...] + g2_ref[...] * m) * lng_ref[...] + lnb_ref[...]


def _combine(chunk_row, tile_nch, yb, x, tokm, g2, lng, lnb, *, ne, tm, chunk, n_per_batch, alpha):
    ttot, d = x.shape
    nsteps = ttot // tm
    per = n_per_batch // tm
    sp = 2 * tm + ne * chunk
    grid_spec = pltpu.PrefetchScalarGridSpec(
        num_scalar_prefetch=2,
        grid=(nsteps,),
        in_specs=[pl.BlockSpec(memory_space=pl.ANY),
                  pl.BlockSpec((tm, d), lambda i, *_: (i, 0)),
                  pl.BlockSpec((tm, LANES), lambda i, *_: (i, 0)),
                  pl.BlockSpec((None, 1, d), lambda i, *_: (i // per, 0, 0)),
                  pl.BlockSpec((1, d), lambda i, *_: (0, 0)),
                  pl.BlockSpec((1, d), lambda i, *_: (0, 0))],
        out_specs=pl.BlockSpec((tm, d), lambda i, *_: (i, 0)),
        scratch_shapes=[pltpu.VMEM((2, sp * PACK_SUB, LANES), yb.dtype), pltpu.SemaphoreType.DMA((2,))],
    )
    return pl.pallas_call(
        functools.partial(_combine_kernel, qmax=chunk_row.shape[0] // nsteps, chunk=chunk, nsteps=nsteps,
                          alpha=alpha),
        out_shape=jax.ShapeDtypeStruct((ttot, d), F32),
        grid_spec=grid_spec,
        compiler_params=_cp(("arbitrary",), VMEM_LIMIT),
        name="moe_combine",
    )(chunk_row, tile_nch, yb, x, tokm, g2, lng.reshape(1, d), lnb.reshape(1, d))


def _hier_moe_layer(x1, t, g2, lng, lnb, wg, bg, wf, bf, w1, w3, w2, *, layer, alpha):
    b, n, d = x1.shape
    ttot = b * n
    ng = wg.shape[1]
    ne = wf.shape[1]
    epg = ne // ng
    tm = min(MOE_TILE, n)
    chunk = MOE_CHUNK
    nt = ttot // tm
    tflat = t.reshape(ttot, d)
    wcat = jnp.zeros((d, LANES), F32).at[:, 0:ng].set(wg).at[:, SUBLANES:SUBLANES + ne].set(wf).astype(BF16)
    bcat = jnp.zeros((1, LANES), F32).at[0, 0:ng].set(bg).at[0, SUBLANES:SUBLANES + ne].set(bf)
    ar = jnp.arange(2 * tm, dtype=I32)
    upper = (ar[:, None] < ar[None, :]).astype(BF16)
    lp, tokm, cnt = _route(tflat, wcat, bcat, upper, ng=ng, epg=epg, tm=tm, chunk=chunk)

    tile_cnt = cnt.reshape(ne, nt, LANES)[:, :, 0].T.astype(I32)
    seg_len = ((tile_cnt + 1) // 2) * 2
    counts = jnp.sum(seg_len, axis=0)
    padded = ((counts + chunk + MOE_ROWS - 1) // MOE_ROWS) * MOE_ROWS
    pend = jnp.cumsum(padded)
    pstart = pend - padded
    base = jnp.cumsum(seg_len, axis=0) - seg_len
    seg_row = pstart[None, :] + base
    nch = (tile_cnt + chunk - 1) // chunk
    nch_end = jnp.cumsum(nch, axis=1)
    tile_nch = nch_end[:, -1].astype(I32)
    qmax = 2 * tm // chunk + ne
    qs = jnp.arange(qmax, dtype=I32)
    e_q = jnp.minimum(jnp.sum((nch_end[:, None, :] <= qs[None, :, None]).astype(I32), axis=2), ne - 1)
    pick = e_q[:, :, None] == jnp.arange(ne, dtype=I32)[None, None, :]
    first_q = jnp.sum(jnp.where(pick, (nch_end - nch)[:, None, :], 0), axis=2)
    first_row = jnp.sum(jnp.where(pick, seg_row[:, None, :], 0), axis=2)
    chunk_row = (first_row + (qs[None, :] - first_q) * chunk).astype(I32).reshape(nt * qmax)
    nb = -(-(2 * ttot + nt * ne + ne * (chunk + MOE_ROWS)) // MOE_ROWS)
    bstart = jnp.arange(nb, dtype=I32) * MOE_ROWS
    block_e = jnp.minimum(jnp.sum((pend[None, :] <= bstart[:, None]).astype(I32), axis=1), ne - 1)
    n_used = (pend[-1] // MOE_ROWS).astype(I32).reshape(1)

    xs = _dispatch(chunk_row, tile_nch, tflat, lp, n_rows=nb * MOE_ROWS, ne=ne, tm=tm, chunk=chunk)
    yb = _moe_experts(block_e.astype(I32), n_used, xs, w1, w3, w2, layer=layer)
    out = _combine(chunk_row, tile_nch, yb, x1.reshape(ttot, d), tokm, g2, lng, lnb,
                   ne=ne, tm=tm, chunk=chunk, n_per_batch=n, alpha=alpha)
    return out.reshape(b, n, d)


def kernel(x, c, ctx, c_ctx, ada_w, ada_b, ln_g, ln_b, ev_w_in, ev_conv_w, ev_conv_b, ev_gate_a_w,
           ev_gate_a_b, ev_gate_x_w, ev_gate_x_b, ev_lru_lambda, ev_da_lambda, ev_da_subln, ev_w_out,
           od_w_out, od_b_out, moe_wg, moe_bg, moe_wf, moe_bf, moe_w1, moe_w3, moe_w2):
    bsz, n_lat, d = x.shape
    depth = ada_w.shape[0]
    alpha = (2.0 * depth) ** 0.25
    lw = ev_conv_w.shape[-1]
    hd = ev_da_lambda.shape[-1]
    vd = ev_da_subln.shape[-1]
    aw = (ev_w_in.shape[-1] - 2 * lw) // 3
    heads = aw // vd
    fnet_groups = 4

    rows = ((bsz + 1 + SUBLANES - 1) // SUBLANES) * SUBLANES
    cond = jnp.zeros((rows, d), F32).at[0:bsz].set(c).at[bsz].set(c_ctx)
    ada = _ada_terms(cond, ada_w, ada_b).reshape(depth, rows, 6, d)

    def lat_term(l, k):
        return ada[l, 0:bsz, k, :].reshape(bsz, 1, d)

    def ctx_term(l, k):
        return jnp.broadcast_to(ada[l, bsz, k, :].reshape(1, 1, d), (bsz, 1, d))

    for l in range(depth):
        ctx_live = any(m % 2 == 0 for m in range(l + 1, depth))
        assert not ctx_live, "context stream update is only needed for depth > 2"
        sh1, sc1, g1, sh2, sc2, g2 = [lat_term(l, k) for k in range(6)]
        if l % 2 == 0:
            e = l // 2
            lam_init = 0.8 - 0.6 * math.exp(-0.3 * l)
            w_in_b = ev_w_in[e].astype(BF16)
            cos_t, sin_t = _rope_tables(n_lat, hd)
            qscale = hd ** -0.5 * math.log2(math.e)
            g_l, xr_l, q_l, k_l, v_l = _project_even(x, sh1, sc1, w_in_b, cos_t, sin_t, rope=True,
                                                     lw=lw, aw=aw, qscale=qscale, tm=512)
            n_ctx = ctx.shape[1]
            _, xr_c, _, k_c, v_c = _project_even(ctx, ctx_term(l, 0), ctx_term(l, 1), w_in_b,
                                                 cos_t[0:n_ctx], sin_t[0:n_ctx], rope=False,
                                                 lw=lw, aw=aw, qscale=qscale, tm=256)
            o_l = _diff_attention(q_l, k_c, k_l, v_c, v_l, ev_da_lambda[e], ev_da_subln[e],
                                  heads=heads, hd=hd, lam_init=lam_init)
            gwid = 256
            y = None
            for dirn, rev in ((0, False), (1, True)):
                wa_bd = _block_diag_groups(ev_gate_a_w[e, dirn], gwid)
                wx_bd = _block_diag_groups(ev_gate_x_w[e, dirn], gwid)
                common = (ev_conv_w[e], ev_conv_b[e], wa_bd, ev_gate_a_b[e, dirn], wx_bd,
                          ev_gate_x_b[e, dirn], ev_lru_lambda[e, dirn])
                h_zero = jnp.zeros((bsz, 1, lw), F32)
                _, h_fin = _rglru_dir(xr_c, None, *common, h_zero, reverse=rev, tn=256)
                y, _ = _rglru_dir(xr_l, y, *common, h_fin, reverse=rev, tn=256)
            x1, t = _outproj_even(y, g_l, o_l, x, ev_w_out[e].astype(BF16), g1, ln_g[l, 0], ln_b[l, 0],
                                  sh2, sc2, alpha=alpha, tm=512)
        else:
            o = l // 2
            gd = d // fnet_groups
            zr, zi = _chan_dft(x, sh1, sc1, _chan_dft_table(gd), groups=fnet_groups, tm=512)
            wv = _tok_fft(zr, zi)
            norm = 1.0 / math.sqrt(float(n_lat * gd))
            x1, t = _outproj_odd(wv, x, od_w_out[o].astype(BF16), od_b_out[o], g1, ln_g[l, 0], ln_b[l, 0],
                                 sh2, sc2, alpha=alpha, norm=norm, tm=512)
        x = _hier_moe_layer(x1, t, g2, ln_g[l, 1], ln_b[l, 1], moe_wg[l], moe_bg[l], moe_wf[l], moe_bf[l],
                            moe_w1, moe_w3, moe_w2, layer=l, alpha=alpha)
    return x
```

```python
import functools
import math

import numpy as np
import jax
import jax.numpy as jnp
from jax import lax
from jax.experimental import pallas as pl
from jax.experimental.pallas import tpu as pltpu

F32 = jnp.float32
BF16 = jnp.bfloat16
I32 = jnp.int32

LN_EPS = 1e-6
LRU_C = 8.0
ROPE_BASE = 10000.0
GRID_W = 64
CONV_W = 4
LANES = 128
SUBLANES = 8
MOE_ROWS = 512
MOE_TILE = 256
MOE_CHUNK = 8
VMEM_LIMIT = 56 * 1024 * 1024


def _cp(sem, vmem=None):
    return pltpu.CompilerParams(dimension_semantics=sem, vmem_limit_bytes=vmem)


def _ln(x):
    mu = jnp.mean(x, axis=-1, keepdims=True)
    xc = x - mu
    var = jnp.mean(xc * xc, axis=-1, keepdims=True)
    return xc * lax.rsqrt(var + LN_EPS)


def _silu(x):
    return x * jax.nn.sigmoid(x)


PACK_SUB = 4


def _tile_row(r, mult):
    if isinstance(r, int):
        return r * PACK_SUB
    return pl.multiple_of(r * PACK_SUB, mult)


def _pack_rows(val):
    half = val.shape[1] // 2
    lo = lax.bitcast_convert_type(val[:, 0:half], jnp.uint32) >> 16
    hi = lax.bitcast_convert_type(val[:, half:], jnp.uint32) & jnp.uint32(0xFFFF0000)
    return lo | hi


def _unpack_rows(words):
    lo = lax.bitcast_convert_type(words << 16, F32)
    hi = lax.bitcast_convert_type(words & jnp.uint32(0xFFFF0000), F32)
    return jnp.concatenate([lo, hi], axis=1).astype(BF16)


def _wait_chunks(copy_of, n, nmax):
    b = 1
    while b <= nmax:
        @pl.when((n & b) != 0)
        def _(b=b):
            copy_of(b).wait()
        b *= 2


ISSUE_UNROLL = 4


def _for_each_chunk(n, start_one):
    groups = n // ISSUE_UNROLL

    def group(g, carry):
        for u in range(ISSUE_UNROLL):
            start_one(g * ISSUE_UNROLL + u)
        return carry
    lax.fori_loop(0, groups, group, 0)

    def single(q, carry):
        start_one(q)
        return carry
    lax.fori_loop(groups * ISSUE_UNROLL, n, single, 0)


def _rows_from_tiles(ref, nrows):
    return jnp.concatenate([ref[pl.ds(k, nrows, stride=PACK_SUB), :] for k in range(PACK_SUB)], axis=1)


def _rows_to_tiles(ref, val):
    nrows = val.shape[0]
    for k in range(PACK_SUB):
        ref[pl.ds(k, nrows, stride=PACK_SUB), :] = val[:, k * LANES:(k + 1) * LANES]


def _ada_kernel(c_ref, w_ref, b_ref, o_ref):
    s = _silu(c_ref[...]).astype(BF16)
    o_ref[...] = jnp.dot(s, w_ref[...].astype(BF16), preferred_element_type=F32) + b_ref[...]


def _ada_terms(cond, ada_w, ada_b):
    nl, d, d6 = ada_w.shape
    r = cond.shape[0]
    tn = 1024
    return pl.pallas_call(
        _ada_kernel,
        out_shape=jax.ShapeDtypeStruct((nl, r, d6), F32),
        grid=(nl, d6 // tn),
        in_specs=[pl.BlockSpec((r, d), lambda l, j: (0, 0)),
                  pl.BlockSpec((None, d, tn), lambda l, j: (l, 0, j)),
                  pl.BlockSpec((None, 1, tn), lambda l, j: (l, 0, j))],
        out_specs=pl.BlockSpec((None, r, tn), lambda l, j: (l, 0, j)),
        compiler_params=_cp(("parallel", "parallel")),
        name="ada_terms",
    )(cond, ada_w, ada_b.reshape(nl, 1, d6))


def _rope_apply(x, cos, sin_signed):
    tm = x.shape[0]
    lane = lax.broadcasted_iota(I32, (tm, LANES), 1)
    first_half = (lane % 32) < 16
    outs = []
    for j in range(x.shape[1] // LANES):
        xh = x[:, j * LANES:(j + 1) * LANES]
        partner = jnp.where(first_half, pltpu.roll(xh, LANES - 16, 1), pltpu.roll(xh, 16, 1))
        outs.append(xh * cos + partner * sin_signed)
    return jnp.concatenate(outs, axis=1)


def _proj_kernel(x_ref, sh_ref, sc_ref, w_ref, cos_ref, sin_ref,
                 g_ref, xr_ref, q_ref, k_ref, v_ref, *, rope, lw, aw, qscale):
    h = _ln(x_ref[...]) * (1.0 + sc_ref[...]) + sh_ref[...]
    hb = h.astype(BF16)

    def mm(c0, c1):
        return jnp.dot(hb, w_ref[:, c0:c1], preferred_element_type=F32)

    g_ref[...] = mm(0, lw).astype(BF16)
    xr_ref[...] = mm(lw, 2 * lw)
    q = mm(2 * lw, 2 * lw + aw)
    k = mm(2 * lw + aw, 2 * lw + 2 * aw)
    if rope:
        q = _rope_apply(q, cos_ref[...], sin_ref[...])
        k = _rope_apply(k, cos_ref[...], sin_ref[...])
    q_ref[...] = (q * qscale).astype(BF16)
    k_ref[...] = k.astype(BF16)
    v_ref[...] = mm(2 * lw + 2 * aw, 2 * lw + 3 * aw).astype(BF16)


def _project_even(x, shift, scale, w_in_b, cos_t, sin_t, *, rope, lw, aw, qscale, tm):
    b, n, d = x.shape
    tm = min(tm, n)
    nin = w_in_b.shape[1]
    tok = lambda bi, i: (bi, i, 0)
    per_b = lambda bi, i: (bi, 0, 0)
    outs = (jax.ShapeDtypeStruct((b, n, lw), BF16), jax.ShapeDtypeStruct((b, n, lw), F32),
            jax.ShapeDtypeStruct((b, n, aw), BF16), jax.ShapeDtypeStruct((b, n, aw), BF16),
            jax.ShapeDtypeStruct((b, n, aw), BF16))
    return pl.pallas_call(
        functools.partial(_proj_kernel, rope=rope, lw=lw, aw=aw, qscale=qscale),
        out_shape=outs,
        grid=(b, n // tm),
        in_specs=[pl.BlockSpec((None, tm, d), tok),
                  pl.BlockSpec((None, 1, d), per_b),
                  pl.BlockSpec((None, 1, d), per_b),
                  pl.BlockSpec((d, nin), lambda bi, i: (0, 0)),
                  pl.BlockSpec((tm, LANES), lambda bi, i: (i, 0)),
                  pl.BlockSpec((tm, LANES), lambda bi, i: (i, 0))],
        out_specs=(pl.BlockSpec((None, tm, lw), tok), pl.BlockSpec((None, tm, lw), tok),
                   pl.BlockSpec((None, tm, aw), tok), pl.BlockSpec((None, tm, aw), tok),
                   pl.BlockSpec((None, tm, aw), tok)),
        compiler_params=_cp(("parallel", "parallel"), VMEM_LIMIT),
        name="proj_even_rope" if rope else "proj_even_ctx",
    )(x, shift, scale, w_in_b, cos_t, sin_t)


def _rope_tables(n_tok, head_dim):
    t = jnp.arange(n_tok)
    row = (t // GRID_W).astype(F32)
    col = (t % GRID_W).astype(F32)
    nf = head_dim // 4
    freqs = ROPE_BASE ** (-jnp.arange(nf, dtype=F32) / nf)
    lane = np.arange(LANES)
    within = lane % head_dim
    axis = within // (2 * nf)
    half = (within % (2 * nf)) // nf
    f = within % nf
    pos = jnp.where(jnp.asarray(axis)[None, :] == 0, row[:, None], col[:, None])
    ang = pos * freqs[jnp.asarray(f)][None, :]
    sign = jnp.asarray(np.where(half == 0, -1.0, 1.0), F32)[None, :]
    return jnp.cos(ang).astype(F32), (jnp.sin(ang) * sign).astype(F32)


def _attn_kernel(q_ref, kc_ref, kl_ref, vc_ref, vl_ref, dl_ref, gain_ref, o_ref, kbuf, vbuf, sbuf, ebuf, abuf,
                 cbuf, *, nc, nl, hd, lam_init, rows):
    @pl.when((pl.program_id(0) == 0) & (pl.program_id(1) == 0))
    def _():
        sbuf[...] = jnp.zeros_like(sbuf)
        abuf[...] = jnp.zeros_like(abuf)
        cbuf[...] = jnp.zeros_like(cbuf)

    kbuf[0:nc, :] = kc_ref[...]
    kbuf[nc:nc + nl, :] = kl_ref[...]
    vbuf[0:nc, :] = vc_ref[...]
    vbuf[nc:nc + nl, :] = vl_ref[...]

    lf = dl_ref[...]
    lam = (jnp.exp(jnp.sum(lf[0:1] * lf[1:2], axis=1, keepdims=True))
           - jnp.exp(jnp.sum(lf[2:3] * lf[3:4], axis=1, keepdims=True)) + lam_init)
    gain = gain_ref[...] * (1.0 - lam_init)
    n_sub = nl // rows
    lane = lax.broadcasted_iota(I32, (rows, 2 * hd), 1)
    nt = (((1,), (1,)), ((), ()))

    def stage_a(j, slot):
        r0 = pl.multiple_of(jnp.minimum(j, n_sub - 1) * rows, rows)
        q = q_ref[pl.ds(r0, rows), :]
        zero = jnp.zeros_like(q)
        kk = kbuf[...]
        sbuf[slot, 0] = lax.dot_general(jnp.where(lane < hd, q, zero), kk, nt, preferred_element_type=F32)
        sbuf[slot, 1] = lax.dot_general(jnp.where(lane >= hd, q, zero), kk, nt, preferred_element_type=F32)

    def stage_b(slot):
        cols = [slice(c, c + LANES) for c in range(0, nc + nl, LANES)]
        ls = []
        for k in range(2):
            pm = sbuf[slot, k, :, cols[0]]
            for cs in cols[1:]:
                pm = jnp.maximum(pm, sbuf[slot, k, :, cs])
            m = jnp.max(pm, axis=1, keepdims=True)
            acc = jnp.zeros((rows, LANES), F32)
            for cs in cols:
                e = jnp.exp2(sbuf[slot, k, :, cs] - m)
                acc = acc + e
                ebuf[slot, k, :, cs] = e.astype(BF16)
            ls.append(jnp.sum(acc, axis=1, keepdims=True))
        ratio = (lam * ls[0] / ls[1]).astype(BF16)
        abuf[slot] = ebuf[slot, 0] - ratio * ebuf[slot, 1]
        cbuf[slot] = jnp.broadcast_to(1.0 / ls[0], cbuf.shape[1:])

    def stage_c(j, slot):
        r0 = pl.multiple_of(jnp.maximum(j - 2, 0) * rows, rows)
        o = jnp.dot(abuf[slot], vbuf[...], preferred_element_type=F32) * cbuf[slot]
        o = o * lax.rsqrt(jnp.mean(o * o, axis=1, keepdims=True) + LN_EPS) * gain
        o_ref[pl.ds(r0, rows), :] = o.astype(BF16)

    def body(t, carry):
        j = 2 * t
        stage_a(j, 0)
        stage_b(1)
        stage_c(j, 0)
        stage_a(j + 1, 1)
        stage_b(0)
        stage_c(j + 1, 1)
        return carry

    lax.fori_loop(0, (n_sub + 2) // 2, body, 0)


def _diff_attention(q, k_ctx, k_lat, v_ctx, v_lat, da_lambda, subln, *, heads, hd, lam_init):
    b, n, aw = q.shape
    nc = k_ctx.shape[1]
    vd = aw // heads
    rows = min(128, n)
    assert (n // rows) % 2 == 0
    blk_q = pl.BlockSpec((None, n, vd), lambda bi, h: (bi, 0, h))
    blk_c = pl.BlockSpec((None, nc, vd), lambda bi, h: (bi, 0, h))
    return pl.pallas_call(
        functools.partial(_attn_kernel, nc=nc, nl=n, hd=hd, lam_init=lam_init, rows=rows),
        out_shape=jax.ShapeDtypeStruct((b, n, aw), BF16),
        grid=(b, heads),
        in_specs=[blk_q, blk_c, blk_q, blk_c, blk_q,
                  pl.BlockSpec(da_lambda.shape, lambda bi, h: (0, 0)),
                  pl.BlockSpec((1, vd), lambda bi, h: (0, 0))],
        out_specs=blk_q,
        scratch_shapes=[pltpu.VMEM((nc + n, vd), BF16), pltpu.VMEM((nc + n, vd), BF16),
                        pltpu.VMEM((2, 2, rows, nc + n), F32), pltpu.VMEM((2, 2, rows, nc + n), BF16),
                        pltpu.VMEM((2, rows, nc + n), BF16), pltpu.VMEM((2, rows, vd), F32)],
        compiler_params=_cp(("arbitrary", "arbitrary"), VMEM_LIMIT),
        name="diff_attention",
    )(q, k_ctx, k_lat, v_ctx, v_lat, da_lambda, subln.reshape(1, vd))


def _lru_kernel(*refs, reverse, add_prev, nt, groups):
    if add_prev:
        (xp_ref, xc_ref, xn_ref, yprev_ref, cw_ref, cb_ref, wa_ref, ba_ref, wx_ref, bx_ref,
         lam_ref, h0_ref, y_ref, hf_ref, carry, ext) = refs
    else:
        (xp_ref, xc_ref, xn_ref, cw_ref, cb_ref, wa_ref, ba_ref, wx_ref, bx_ref,
         lam_ref, h0_ref, y_ref, hf_ref, carry, ext) = refs
        yprev_ref = None
    i = pl.program_id(1)
    ti = (nt - 1 - i) if reverse else i
    tn, w = xc_ref.shape

    @pl.when(i == 0)
    def _():
        carry[...] = h0_ref[...]

    ext[0:SUBLANES, :] = jnp.where(ti > 0, xp_ref[...], 0.0)
    ext[SUBLANES:SUBLANES + tn, :] = xc_ref[...]
    ext[SUBLANES + tn:2 * SUBLANES + tn, :] = jnp.where(ti < nt - 1, xn_ref[...], 0.0)
    left = CONV_W // 2
    xc = cb_ref[...]
    for k in range(CONV_W):
        off = SUBLANES - left + k
        xc = xc + ext[off:off + tn, :] * cw_ref[k:k + 1, :]

    xb = xc.astype(BF16)
    gw = w // groups

    def gate(w_ref, b_ref):
        parts = [jnp.dot(xb[:, g * gw:(g + 1) * gw], w_ref[g], preferred_element_type=F32)
                 for g in range(groups)]
        return jax.nn.sigmoid(jnp.concatenate(parts, axis=1) + b_ref[...])

    r = gate(wa_ref, ba_ref)
    ig = gate(wx_ref, bx_ref)
    log_a = (-LRU_C * jax.nn.softplus(-lam_ref[...])) * r
    a = jnp.exp(log_a)
    bcoef = jnp.sqrt(-jnp.tanh(log_a) * (a * a + 1.0)) * ig * xc

    ngroups = tn // SUBLANES
    a = a.reshape(ngroups, SUBLANES, w)
    bcoef = bcoef.reshape(ngroups, SUBLANES, w)
    row = lax.broadcasted_iota(I32, (ngroups, SUBLANES, w), 1)
    d = 1
    while d < SUBLANES:
        shift = (SUBLANES - d) if reverse else d
        a_sh = pltpu.roll(a, shift, 1)
        b_sh = pltpu.roll(bcoef, shift, 1)
        live = (row < SUBLANES - d) if reverse else (row >= d)
        bcoef = jnp.where(live, a * b_sh + bcoef, bcoef)
        a = jnp.where(live, a * a_sh, a)
        d *= 2
    hc = carry[...]
    npairs = ngroups // 2
    for p in (range(npairs - 1, -1, -1) if reverse else range(npairs)):
        hs = {}
        for g in ((2 * p + 1, 2 * p) if reverse else (2 * p, 2 * p + 1)):
            h = a[g] * hc + bcoef[g]
            hc = h[0:1, :] if reverse else h[SUBLANES - 1:SUBLANES, :]
            hs[g] = h
        rs = slice(2 * p * SUBLANES, (2 * p + 2) * SUBLANES)
        h2 = jnp.concatenate([hs[2 * p], hs[2 * p + 1]], axis=0)
        if add_prev:
            h2 = yprev_ref[rs, :].astype(F32) + h2
        y_ref[rs, :] = h2.astype(y_ref.dtype)
    carry[...] = hc
    hf_ref[...] = hc


def _rglru_dir(xr, y_prev, conv_w, conv_b, wa_bd, ba, wx_bd, bx, lam, h0, *, reverse, tn):
    b, n, w = xr.shape
    tn = min(tn, n)
    nt = n // tn
    groups = wa_bd.shape[0]
    nb8 = n // SUBLANES
    per8 = tn // SUBLANES

    def tmap(i):
        return (nt - 1 - i) if reverse else i

    cur = pl.BlockSpec((None, tn, w), lambda bi, i: (bi, tmap(i), 0))
    halo_p = pl.BlockSpec((None, SUBLANES, w), lambda bi, i: (bi, jnp.maximum(tmap(i) * per8 - 1, 0), 0))
    halo_n = pl.BlockSpec((None, SUBLANES, w), lambda bi, i: (bi, jnp.minimum((tmap(i) + 1) * per8, nb8 - 1), 0))
    row_w = pl.BlockSpec((1, w), lambda bi, i: (0, 0))
    per_b = pl.BlockSpec((None, 1, w), lambda bi, i: (bi, 0, 0))
    gate_w = pl.BlockSpec(wa_bd.shape, lambda bi, i: (0, 0, 0))
    add_prev = y_prev is not None
    in_specs = [halo_p, cur, halo_n] + ([cur] if add_prev else []) + [
        pl.BlockSpec((CONV_W, w), lambda bi, i: (0, 0)), row_w, gate_w, row_w, gate_w, row_w, row_w, per_b]
    args = [xr, xr, xr] + ([y_prev] if add_prev else []) + [
        conv_w, conv_b.reshape(1, w), wa_bd, ba.reshape(1, w), wx_bd, bx.reshape(1, w),
        lam.reshape(1, w), h0]
    return pl.pallas_call(
        functools.partial(_lru_kernel, reverse=reverse, add_prev=add_prev, nt=nt, groups=groups),
        out_shape=(jax.ShapeDtypeStruct((b, n, w), BF16), jax.ShapeDtypeStruct((b, 1, w), F32)),
        grid=(b, nt),
        in_specs=in_specs,
        out_specs=(cur, per_b),
        scratch_shapes=[pltpu.VMEM((1, w), F32), pltpu.VMEM((tn + 2 * SUBLANES, w), F32)],
        compiler_params=_cp(("parallel", "arbitrary"), VMEM_LIMIT),
        name="rglru_rev" if reverse else "rglru_fwd",
    )(*args)


def _block_diag_groups(wh, group_width):
    heads, blk, _ = wh.shape
    per = group_width // blk
    groups = heads // per
    whg = wh.reshape(groups, per, blk, blk)
    eye = jnp.eye(per, dtype=wh.dtype)
    bd = jnp.einsum('gpij,pq->gpiqj', whg, eye).reshape(groups, group_width, group_width)
    return bd.astype(BF16)


def _residual_ln_mod(x, y, g1, lng, lnb, sh2, sc2, alpha):
    x1 = _ln(alpha * x + g1 * y) * lng + lnb
    t = _ln(x1) * (1.0 + sc2) + sh2
    return x1, t


def _outproj_even_kernel(r_ref, g_ref, o_ref, x_ref, w_ref, g1_ref, lng_ref, lnb_ref, sh2_ref, sc2_ref,
                         x1_ref, t_ref, *, lw, alpha):
    z = (r_ref[...].astype(F32) * jax.nn.gelu(g_ref[...].astype(F32))).astype(BF16)
    y = (jnp.dot(z, w_ref[0:lw, :], preferred_element_type=F32)
         + jnp.dot(o_ref[...], w_ref[lw:, :], preferred_element_type=F32))
    x1, t = _residual_ln_mod(x_ref[...], y, g1_ref[...], lng_ref[...], lnb_ref[...],
                             sh2_ref[...], sc2_ref[...], alpha)
    x1_ref[...] = x1
    t_ref[...] = t.astype(BF16)


def _outproj_even(r, g, o, x, w_out_b, g1, lng, lnb, sh2, sc2, *, alpha, tm):
    b, n, d = x.shape
    lw = r.shape[2]
    aw = o.shape[2]
    tm = min(tm, n)
    tok = lambda bi, i: (bi, i, 0)
    per_b = pl.BlockSpec((None, 1, d), lambda bi, i: (bi, 0, 0))
    row = pl.BlockSpec((1, d), lambda bi, i: (0, 0))
    return pl.pallas_call(
        functools.partial(_outproj_even_kernel, lw=lw, alpha=alpha),
        out_shape=(jax.ShapeDtypeStruct((b, n, d), F32), jax.ShapeDtypeStruct((b, n, d), BF16)),
        grid=(b, n // tm),
        in_specs=[pl.BlockSpec((None, tm, lw), tok), pl.BlockSpec((None, tm, lw), tok),
                  pl.BlockSpec((None, tm, aw), tok), pl.BlockSpec((None, tm, d), tok),
                  pl.BlockSpec(w_out_b.shape, lambda bi, i: (0, 0)),
                  per_b, row, row, per_b, per_b],
        out_specs=(pl.BlockSpec((None, tm, d), tok), pl.BlockSpec((None, tm, d), tok)),
        compiler_params=_cp(("parallel", "parallel"), VMEM_LIMIT),
        name="outproj_even",
    )(r, g, o, x, w_out_b, g1, lng.reshape(1, d), lnb.reshape(1, d), sh2, sc2)


def _chan_dft_kernel(x_ref, sh_ref, sc_ref, tab_ref, zr_ref, zi_ref, *, groups, gd):
    h = (_ln(x_ref[...]) * (1.0 + sc_ref[...]) + sh_ref[...]).astype(BF16)
    for g in range(groups):
        z = jnp.dot(h[:, g * gd:(g + 1) * gd], tab_ref[...], preferred_element_type=F32)
        zr_ref[:, g * gd:(g + 1) * gd] = z[:, 0:gd].astype(BF16)
        zi_ref[:, g * gd:(g + 1) * gd] = z[:, gd:2 * gd].astype(BF16)


def _chan_dft(x, shift, scale, tab, *, groups, tm):
    b, n, d = x.shape
    gd = d // groups
    tm = min(tm, n)
    tok = lambda bi, i: (bi, i, 0)
    per_b = pl.BlockSpec((None, 1, d), lambda bi, i: (bi, 0, 0))
    return pl.pallas_call(
        functools.partial(_chan_dft_kernel, groups=groups, gd=gd),
        out_shape=(jax.ShapeDtypeStruct((b, n, d), BF16), jax.ShapeDtypeStruct((b, n, d), BF16)),
        grid=(b, n // tm),
        in_specs=[pl.BlockSpec((None, tm, d), tok), per_b, per_b,
                  pl.BlockSpec(tab.shape, lambda bi, i: (0, 0))],
        out_specs=(pl.BlockSpec((None, tm, d), tok), pl.BlockSpec((None, tm, d), tok)),
        compiler_params=_cp(("parallel", "parallel"), VMEM_LIMIT),
        name="chan_dft",
    )(x, shift, scale, tab)


FFT_C = 64


def _fft_pitch(group):
    p = -(-group // SUBLANES)
    return SUBLANES * (p if p % 2 else p + 1)


def _tok_fft_kernel(zr_ref, zi_ref, m1_ref, m3_ref, tc_ref, ts_ref, o_ref, zsr, zsi, asr, asi, ob, *, nr):
    c_len = FFT_C
    pz = zsr.shape[0] // nr
    pa = asr.shape[0] // c_len
    for r in range(nr):
        zsr[pz * r:pz * r + c_len, :] = zr_ref[c_len * r:c_len * (r + 1), :].astype(F32)
        zsi[pz * r:pz * r + c_len, :] = zi_ref[c_len * r:c_len * (r + 1), :].astype(F32)
    m1 = m1_ref[...]
    for c in range(c_len):
        x2 = jnp.concatenate([zsr[pl.ds(c, nr, stride=pz), :], zsi[pl.ds(c, nr, stride=pz), :]], axis=0)
        a2 = jnp.dot(m1, x2.astype(BF16), preferred_element_type=F32)
        ar, ai = a2[0:nr], a2[nr:2 * nr]
        tcv = tc_ref[c * nr:(c + 1) * nr, :]
        tsv = ts_ref[c * nr:(c + 1) * nr, :]
        asr[pa * c:pa * c + nr, :] = ar * tcv + ai * tsv
        asi[pa * c:pa * c + nr, :] = ai * tcv - ar * tsv
    m3 = m3_ref[...]
    for k1 in range(nr):
        y2 = jnp.concatenate([asr[pl.ds(k1, c_len, stride=pa), :], asi[pl.ds(k1, c_len, stride=pa), :]], axis=0)
        ob[pl.ds(k1, c_len, stride=pa), :] = jnp.dot(m3, y2.astype(BF16), preferred_element_type=F32)
    for k2 in range(c_len):
        o_ref[nr * k2:nr * (k2 + 1), :] = ob[pa * k2:pa * k2 + nr, :].astype(BF16)


def _tok_fft(zr, zi):
    b, n, d = zr.shape
    nr = n // FFT_C
    pz = _fft_pitch(FFT_C)
    pa = _fft_pitch(nr)
    kr = np.arange(nr, dtype=np.float64)
    ang_r = 2.0 * np.pi * (np.outer(kr, kr) % nr) / nr
    cr, sr = np.cos(ang_r), np.sin(ang_r)
    m1 = jnp.asarray(np.block([[cr, -sr], [-sr, -cr]]), F32).astype(BF16)
    kc = np.arange(FFT_C, dtype=np.float64)
    ang_c = 2.0 * np.pi * (np.outer(kc, kc) % FFT_C) / FFT_C
    m3 = jnp.asarray(np.concatenate([np.cos(ang_c), np.sin(ang_c)], axis=1), F32).astype(BF16)
    ang_t = 2.0 * np.pi * (np.outer(kc, kr) % n) / n
    tc = jnp.broadcast_to(jnp.asarray(np.cos(ang_t).reshape(FFT_C * nr, 1), F32), (FFT_C * nr, LANES))
    ts = jnp.broadcast_to(jnp.asarray(np.sin(ang_t).reshape(FFT_C * nr, 1), F32), (FFT_C * nr, LANES))
    slab = pl.BlockSpec((None, n, LANES), lambda bi, l: (bi, 0, l))
    const = lambda a: pl.BlockSpec(a.shape, lambda bi, l: (0, 0))
    return pl.pallas_call(
        functools.partial(_tok_fft_kernel, nr=nr),
        out_shape=jax.ShapeDtypeStruct((b, n, d), BF16),
        grid=(b, d // LANES),
        in_specs=[slab, slab, const(m1), const(m3), const(tc), const(ts)],
        out_specs=slab,
        scratch_shapes=[pltpu.VMEM((nr * pz, LANES), F32), pltpu.VMEM((nr * pz, LANES), F32),
                        pltpu.VMEM((FFT_C * pa, LANES), F32), pltpu.VMEM((FFT_C * pa, LANES), F32),
                        pltpu.VMEM((FFT_C * pa, LANES), F32)],
        compiler_params=_cp(("parallel", "parallel"), VMEM_LIMIT),
        name="tok_fft",
    )(zr, zi, m1, m3, tc, ts)


def _chan_dft_table(gd):
    c = np.arange(gd, dtype=np.float64)
    ang_c = 2.0 * np.pi * (np.outer(c, c) % gd) / gd
    return jnp.asarray(np.concatenate([np.cos(ang_c), np.sin(ang_c)], axis=1), F32).astype(BF16)


def _outproj_odd_kernel(wv_ref, x_ref, w_ref, b_ref, g1_ref, lng_ref, lnb_ref, sh2_ref, sc2_ref,
                        x1_ref, t_ref, *, alpha, norm):
    y = jnp.dot(wv_ref[...], w_ref[...], preferred_element_type=F32) * norm + b_ref[...]
    x1, t = _residual_ln_mod(x_ref[...], y, g1_ref[...], lng_ref[...], lnb_ref[...],
                             sh2_ref[...], sc2_ref[...], alpha)
    x1_ref[...] = x1
    t_ref[...] = t.astype(BF16)


def _outproj_odd(wv, x, w_b, bias, g1, lng, lnb, sh2, sc2, *, alpha, norm, tm):
    b, n, d = x.shape
    tm = min(tm, n)
    tok = lambda bi, i: (bi, i, 0)
    per_b = pl.BlockSpec((None, 1, d), lambda bi, i: (bi, 0, 0))
    row = pl.BlockSpec((1, d), lambda bi, i: (0, 0))
    return pl.pallas_call(
        functools.partial(_outproj_odd_kernel, alpha=alpha, norm=norm),
        out_shape=(jax.ShapeDtypeStruct((b, n, d), F32), jax.ShapeDtypeStruct((b, n, d), BF16)),
        grid=(b, n // tm),
        in_specs=[pl.BlockSpec((None, tm, d), tok), pl.BlockSpec((None, tm, d), tok),
                  pl.BlockSpec(w_b.shape, lambda bi, i: (0, 0)), row,
                  per_b, row, row, per_b, per_b],
        out_specs=(pl.BlockSpec((None, tm, d), tok), pl.BlockSpec((None, tm, d), tok)),
        compiler_params=_cp(("parallel", "parallel"), VMEM_LIMIT),
        name="outproj_odd",
    )(wv, x, w_b, bias.reshape(1, d), g1, lng.reshape(1, d), lnb.reshape(1, d), sh2, sc2)


def _route_kernel(t_ref, w_ref, b_ref, up_ref, lp_ref, tokm_ref, cnt_ref, *, ng, epg, chunk):
    tm = t_ref.shape[0]
    ne = ng * epg
    logits = jnp.dot(t_ref[...].astype(BF16), w_ref[...], preferred_element_type=F32) + b_ref[...]
    lt = logits.T
    best = lt[0:1, :]
    bi = jnp.zeros((1, tm), I32)
    for k in range(1, ng):
        gk = lt[k:k + 1, :]
        upd = gk > best
        bi = jnp.where(upd, k, bi)
        best = jnp.where(upd, gk, best)
    den = jnp.zeros((1, tm), F32)
    for k in range(ng):
        den = den + jnp.exp(lt[k:k + 1, :] - best)
    p_g = 1.0 / den
    fsel = lt[SUBLANES:SUBLANES + epg, :]
    for k in range(1, ng):
        fsel = jnp.where(bi == k, lt[SUBLANES + k * epg:SUBLANES + (k + 1) * epg, :], fsel)
    neg = jnp.full((1, tm), -jnp.inf, F32)
    m1, m2 = neg, neg
    i1 = jnp.zeros((1, tm), I32)
    i2 = jnp.zeros((1, tm), I32)
    for j in range(epg):
        v = fsel[j:j + 1, :]
        gt1 = v > m1
        gt2 = v > m2
        m2 = jnp.where(gt1, m1, jnp.where(gt2, v, m2))
        i2 = jnp.where(gt1, i1, jnp.where(gt2, j, i2))
        m1 = jnp.where(gt1, v, m1)
        i1 = jnp.where(gt1, j, i1)
    e21 = jnp.exp(m2 - m1)
    w1 = p_g / (1.0 + e21)
    w2 = p_g * e21 / (1.0 + e21)
    e1 = bi * epg + i1
    e2 = bi * epg + i2

    e = jnp.concatenate([e1, e2], axis=1)
    rows = lax.broadcasted_iota(I32, (ne, 2 * tm), 0)
    onehot = jnp.where(rows == e, 1.0, 0.0)
    before = jnp.dot(onehot.astype(BF16), up_ref[...], preferred_element_type=F32)
    tot = jnp.sum(onehot, axis=1, keepdims=True)
    slots = jnp.floor((tot + (chunk - 1.0)) * (1.0 / chunk)) * chunk
    slots_b = jnp.broadcast_to(slots, (ne, LANES))
    rowe = lax.broadcasted_iota(I32, (ne, LANES), 0)
    incl = slots_b
    d = 1
    while d < ne:
        incl = incl + jnp.where(rowe >= d, pltpu.roll(incl, d, 0), 0.0)
        d *= 2
    seg_off = jnp.tile(incl - slots_b, (1, 2 * tm // LANES))
    lpos = jnp.sum(onehot * (before + seg_off), axis=0, keepdims=True)
    lp0 = lpos[:, 0:tm]
    lp1 = lpos[:, tm:2 * tm]
    row8 = lax.broadcasted_iota(I32, (SUBLANES, tm), 0)
    lp_ref[...] = jnp.where(row8 == 0, lp0, jnp.where(row8 == 1, lp1, 0.0)).astype(I32)
    rowl = lax.broadcasted_iota(I32, (LANES, tm), 0)
    tokm = jnp.where(rowl == 0, w1, jnp.where(rowl == 1, w2, jnp.where(rowl == 2, lp0, jnp.where(rowl == 3, lp1, 0.0))))
    tokm_ref[...] = tokm.T
    cnt_ref[...] = jnp.broadcast_to(tot, (ne, LANES))


def _route(t, wcat_b, bcat, upper, *, ng, epg, tm, chunk):
    tt, d = t.shape
    ne = ng * epg
    nt = tt // tm
    return pl.pallas_call(
        functools.partial(_route_kernel, ng=ng, epg=epg, chunk=chunk),
        out_shape=(jax.ShapeDtypeStruct((SUBLANES, tt), I32), jax.ShapeDtypeStruct((tt, LANES), F32),
                   jax.ShapeDtypeStruct((ne, nt * LANES), F32)),
        grid=(nt,),
        in_specs=[pl.BlockSpec((tm, d), lambda i: (i, 0)),
                  pl.BlockSpec((d, LANES), lambda i: (0, 0)),
                  pl.BlockSpec((1, LANES), lambda i: (0, 0)),
                  pl.BlockSpec(upper.shape, lambda i: (0, 0))],
        out_specs=(pl.BlockSpec((SUBLANES, tm), lambda i: (0, i)),
                   pl.BlockSpec((tm, LANES), lambda i: (i, 0)),
                   pl.BlockSpec((ne, LANES), lambda i: (0, i))),
        compiler_params=_cp(("parallel",), VMEM_LIMIT),
        name="route_sort",
    )(t, wcat_b, bcat, upper)


def _dispatch_kernel(row_ref, tot_ref, t_ref, lp_ref, xs_in_ref, xs_ref, stage, sems, *, qmax, chunk, nsteps):
    del xs_in_ref
    i = pl.program_id(0)
    slot = i % 2
    sp = stage.shape[1] // PACK_SUB
    tm = t_ref.shape[0]
    crow = chunk * PACK_SUB

    lp = lp_ref[...]
    prow = lax.broadcasted_iota(I32, (sp, tm), 0)
    perm = jnp.where(prow == lp[0:1, :], 1.0, jnp.where(prow == lp[1:2, :], 1.0, 0.0)).astype(BF16)
    srt = jnp.dot(perm, t_ref[...], preferred_element_type=F32)
    _rows_to_tiles(stage.at[slot], _pack_rows(srt))

    def chunk_copy(sl, src, dst):
        return pltpu.make_async_copy(stage.at[sl, pl.ds(_tile_row(src, crow), crow)],
                                     xs_ref.at[pl.ds(_tile_row(dst, SUBLANES), crow)], sems.at[sl])

    def drain(sl, n):
        def copy_of(k):
            return pltpu.make_async_copy(stage.at[sl, pl.ds(0, k * crow)], xs_ref.at[pl.ds(0, k * crow)], sems.at[sl])
        _wait_chunks(copy_of, n, qmax)

    @pl.when(i >= 1)
    def _():
        drain(1 - slot, tot_ref[jnp.maximum(i - 1, 0)])

    _for_each_chunk(tot_ref[i], lambda q: chunk_copy(slot, pl.multiple_of(q * chunk, chunk),
                                                     row_ref[i * qmax + q]).start())

    @pl.when(i == nsteps - 1)
    def _():
        drain(slot, tot_ref[i])


def _dispatch(chunk_row, tile_nch, t, lp, *, n_rows, ne, tm, chunk):
    ttot, d = t.shape
    nsteps = ttot // tm
    sp = 2 * tm + ne * chunk
    xs_init = jnp.zeros((n_rows * PACK_SUB, LANES), jnp.uint32)
    grid_spec = pltpu.PrefetchScalarGridSpec(
        num_scalar_prefetch=2,
        grid=(nsteps,),
        in_specs=[pl.BlockSpec((tm, d), lambda i, *_: (i, 0)),
                  pl.BlockSpec((SUBLANES, tm), lambda i, *_: (0, i)),
                  pl.BlockSpec(memory_space=pl.ANY)],
        out_specs=pl.BlockSpec(memory_space=pl.ANY),
        scratch_shapes=[pltpu.VMEM((2, sp * PACK_SUB, LANES), jnp.uint32), pltpu.SemaphoreType.DMA((2,))],
    )
    assert d == 2 * PACK_SUB * LANES
    return pl.pallas_call(
        functools.partial(_dispatch_kernel, qmax=chunk_row.shape[0] // nsteps, chunk=chunk, nsteps=nsteps),
        out_shape=jax.ShapeDtypeStruct(xs_init.shape, xs_init.dtype),
        grid_spec=grid_spec,
        input_output_aliases={4: 0},
        compiler_params=_cp(("arbitrary",), VMEM_LIMIT),
        name="moe_dispatch",
    )(chunk_row, tile_nch, t, lp, xs_init)


def _moe_kernel(be_ref, nu_ref, xs_ref, w1_ref, w3_ref, w2_ref, y_ref, w13b, w2b, *, ff):
    i = pl.program_id(0)

    @pl.when(i < nu_ref[0])
    def _():
        prev = be_ref[jnp.maximum(i - 1, 0)]

        @pl.when((i == 0) | (be_ref[i] != prev))
        def _():
            w13b[:, 0:ff] = w1_ref[...].astype(BF16)
            w13b[:, ff:2 * ff] = w3_ref[...].astype(BF16)
            w2b[...] = w2_ref[...].astype(BF16)

        x = _unpack_rows(_rows_from_tiles(xs_ref, MOE_ROWS))
        h = jnp.dot(x, w13b[...], preferred_element_type=F32)
        hid = (_silu(h[:, 0:ff]) * h[:, ff:2 * ff]).astype(BF16)
        y = jnp.dot(hid, w2b[...], preferred_element_type=F32)
        _rows_to_tiles(y_ref, _pack_rows(y.astype(BF16).astype(F32)))

    @pl.when(i >= nu_ref[0])
    def _():
        y_ref[...] = jnp.zeros_like(y_ref)


def _moe_experts(block_e, n_used, xs, w1, w3, w2, *, layer):
    d = w1.shape[-2]
    ff = w1.shape[-1]
    blk = MOE_ROWS * PACK_SUB
    nb = xs.shape[0] // blk
    wmap = lambda i, be, nu: (layer, be[i], 0, 0)
    grid_spec = pltpu.PrefetchScalarGridSpec(
        num_scalar_prefetch=2,
        grid=(nb,),
        in_specs=[pl.BlockSpec((blk, LANES), lambda i, be, nu: (i, 0)),
                  pl.BlockSpec((None, None, d, ff), wmap),
                  pl.BlockSpec((None, None, d, ff), wmap),
                  pl.BlockSpec((None, None, ff, d), wmap)],
        out_specs=pl.BlockSpec((blk, LANES), lambda i, be, nu: (i, 0)),
        scratch_shapes=[pltpu.VMEM((d, 2 * ff), BF16), pltpu.VMEM((ff, d), BF16)],
    )
    return pl.pallas_call(
        functools.partial(_moe_kernel, ff=ff),
        out_shape=jax.ShapeDtypeStruct(xs.shape, xs.dtype),
        grid_spec=grid_spec,
        compiler_params=_cp(("arbitrary",), VMEM_LIMIT),
        name="moe_experts",
    )(block_e, n_used, xs, w1, w3, w2)


def _combine_kernel(row_ref, tot_ref, yb_ref, x_ref, tokm_ref, g2_ref, lng_ref, lnb_ref,
                    o_ref, stage, sems, *, qmax, chunk, nsteps, alpha):
    i = pl.program_id(0)
    slot = i % 2
    sp = stage.shape[1] // PACK_SUB
    tm = x_ref.shape[0]
    crow = chunk * PACK_SUB

    def chunk_copy(sl, src, dst):
        return pltpu.make_async_copy(yb_ref.at[pl.ds(_tile_row(src, SUBLANES), crow)],
                                     stage.at[sl, pl.ds(_tile_row(dst, crow), crow)], sems.at[sl])

    def issue_tile(step, sl):
        _for_each_chunk(tot_ref[step], lambda q: chunk_copy(sl, row_ref[step * qmax + q],
                                                            pl.multiple_of(q * chunk, chunk)).start())

    @pl.when(i == 0)
    def _():
        stage[...] = jnp.zeros_like(stage)
        issue_tile(0, 0)

    @pl.when(i + 1 < nsteps)
    def _():
        issue_tile(jnp.minimum(i + 1, nsteps - 1), 1 - slot)

    def copy_of(k):
        return pltpu.make_async_copy(yb_ref.at[pl.ds(0, k * crow)], stage.at[slot, pl.ds(0, k * crow)], sems.at[slot])
    _wait_chunks(copy_of, tot_ref[i], qmax)

    tk = tokm_ref[...]
    pos = lax.broadcasted_iota(I32, (tm, sp), 1).astype(F32)
    st = _unpack_rows(_rows_from_tiles(stage.at[slot], sp))
    gsel = jnp.where(pos == tk[:, 2:3], tk[:, 0:1], jnp.where(pos == tk[:, 3:4], tk[:, 1:2], 0.0))
    m = jnp.dot(gsel.astype(BF16), st, preferred_element_type=F32)
    o_ref[...] = _ln(alpha * x_ref[...] + g2_ref[...] * m) * lng_ref[...] + lnb_ref[...]


def _combine(chunk_row, tile_nch, yb, x, tokm, g2, lng, lnb, *, ne, tm, chunk, n_per_batch, alpha):
    ttot, d = x.shape
    nsteps = ttot // tm
    per = n_per_batch // tm
    sp = 2 * tm + ne * chunk
    grid_spec = pltpu.PrefetchScalarGridSpec(
        num_scalar_prefetch=2,
        grid=(nsteps,),
        in_specs=[pl.BlockSpec(memory_space=pl.ANY),
                  pl.BlockSpec((tm, d), lambda i, *_: (i, 0)),
                  pl.BlockSpec((tm, LANES), lambda i, *_: (i, 0)),
                  pl.BlockSpec((None, 1, d), lambda i, *_: (i // per, 0, 0)),
                  pl.BlockSpec((1, d), lambda i, *_: (0, 0)),
                  pl.BlockSpec((1, d), lambda i, *_: (0, 0))],
        out_specs=pl.BlockSpec((tm, d), lambda i, *_: (i, 0)),
        scratch_shapes=[pltpu.VMEM((2, sp * PACK_SUB, LANES), yb.dtype), pltpu.SemaphoreType.DMA((2,))],
    )
    return pl.pallas_call(
        functools.partial(_combine_kernel, qmax=chunk_row.shape[0] // nsteps, chunk=chunk, nsteps=nsteps,
                          alpha=alpha),
        out_shape=jax.ShapeDtypeStruct((ttot, d), F32),
        grid_spec=grid_spec,
        compiler_params=_cp(("arbitrary",), VMEM_LIMIT),
        name="moe_combine",
    )(chunk_row, tile_nch, yb, x, tokm, g2, lng.reshape(1, d), lnb.reshape(1, d))


def _hier_moe_layer(x1, t, g2, lng, lnb, wg, bg, wf, bf, w1, w3, w2, *, layer, alpha):
    b, n, d = x1.shape
    ttot = b * n
    ng = wg.shape[1]
    ne = wf.shape[1]
    epg = ne // ng
    tm = min(MOE_TILE, n)
    chunk = MOE_CHUNK
    nt = ttot // tm
    tflat = t.reshape(ttot, d)
    wcat = jnp.zeros((d, LANES), F32).at[:, 0:ng].set(wg).at[:, SUBLANES:SUBLANES + ne].set(wf).astype(BF16)
    bcat = jnp.zeros((1, LANES), F32).at[0, 0:ng].set(bg).at[0, SUBLANES:SUBLANES + ne].set(bf)
    ar = jnp.arange(2 * tm, dtype=I32)
    upper = (ar[:, None] < ar[None, :]).astype(BF16)
    lp, tokm, cnt = _route(tflat, wcat, bcat, upper, ng=ng, epg=epg, tm=tm, chunk=chunk)

    tile_cnt = cnt.reshape(ne, nt, LANES)[:, :, 0].T.astype(I32)
    seg_len = ((tile_cnt + 1) // 2) * 2
    counts = jnp.sum(seg_len, axis=0)
    padded = ((counts + chunk + MOE_ROWS - 1) // MOE_ROWS) * MOE_ROWS
    pend = jnp.cumsum(padded)
    pstart = pend - padded
    base = jnp.cumsum(seg_len, axis=0) - seg_len
    seg_row = pstart[None, :] + base
    nch = (tile_cnt + chunk - 1) // chunk
    nch_end = jnp.cumsum(nch, axis=1)
    tile_nch = nch_end[:, -1].astype(I32)
    qmax = 2 * tm // chunk + ne
    qs = jnp.arange(qmax, dtype=I32)
    e_q = jnp.minimum(jnp.sum((nch_end[:, None, :] <= qs[None, :, None]).astype(I32), axis=2), ne - 1)
    pick = e_q[:, :, None] == jnp.arange(ne, dtype=I32)[None, None, :]
    first_q = jnp.sum(jnp.where(pick, (nch_end - nch)[:, None, :], 0), axis=2)
    first_row = jnp.sum(jnp.where(pick, seg_row[:, None, :], 0), axis=2)
    chunk_row = (first_row + (qs[None, :] - first_q) * chunk).astype(I32).reshape(nt * qmax)
    nb = -(-(2 * ttot + nt * ne + ne * (chunk + MOE_ROWS)) // MOE_ROWS)
    bstart = jnp.arange(nb, dtype=I32) * MOE_ROWS
    block_e = jnp.minimum(jnp.sum((pend[None, :] <= bstart[:, None]).astype(I32), axis=1), ne - 1)
    n_used = (pend[-1] // MOE_ROWS).astype(I32).reshape(1)

    xs = _dispatch(chunk_row, tile_nch, tflat, lp, n_rows=nb * MOE_ROWS, ne=ne, tm=tm, chunk=chunk)
    yb = _moe_experts(block_e.astype(I32), n_used, xs, w1, w3, w2, layer=layer)
    out = _combine(chunk_row, tile_nch, yb, x1.reshape(ttot, d), tokm, g2, lng, lnb,
                   ne=ne, tm=tm, chunk=chunk, n_per_batch=n, alpha=alpha)
    return out.reshape(b, n, d)


def kernel(x, c, ctx, c_ctx, ada_w, ada_b, ln_g, ln_b, ev_w_in, ev_conv_w, ev_conv_b, ev_gate_a_w,
           ev_gate_a_b, ev_gate_x_w, ev_gate_x_b, ev_lru_lambda, ev_da_lambda, ev_da_subln, ev_w_out,
           od_w_out, od_b_out, moe_wg, moe_bg, moe_wf, moe_bf, moe_w1, moe_w3, moe_w2):
    bsz, n_lat, d = x.shape
    depth = ada_w.shape[0]
    alpha = (2.0 * depth) ** 0.25
    lw = ev_conv_w.shape[-1]
    hd = ev_da_lambda.shape[-1]
    vd = ev_da_subln.shape[-1]
    aw = (ev_w_in.shape[-1] - 2 * lw) // 3
    heads = aw // vd
    fnet_groups = 4

    rows = ((bsz + 1 + SUBLANES - 1) // SUBLANES) * SUBLANES
    cond = jnp.zeros((rows, d), F32).at[0:bsz].set(c).at[bsz].set(c_ctx)
    ada = _ada_terms(cond, ada_w, ada_b).reshape(depth, rows, 6, d)

    def lat_term(l, k):
        return ada[l, 0:bsz, k, :].reshape(bsz, 1, d)

    def ctx_term(l, k):
        return jnp.broadcast_to(ada[l, bsz, k, :].reshape(1, 1, d), (bsz, 1, d))

    for l in range(depth):
        ctx_live = any(m % 2 == 0 for m in range(l + 1, depth))
        assert not ctx_live, "context stream update is only needed for depth > 2"
        sh1, sc1, g1, sh2, sc2, g2 = [lat_term(l, k) for k in range(6)]
        if l % 2 == 0:
            e = l // 2
            lam_init = 0.8 - 0.6 * math.exp(-0.3 * l)
            w_in_b = ev_w_in[e].astype(BF16)
            cos_t, sin_t = _rope_tables(n_lat, hd)
            qscale = hd ** -0.5 * math.log2(math.e)
            g_l, xr_l, q_l, k_l, v_l = _project_even(x, sh1, sc1, w_in_b, cos_t, sin_t, rope=True,
                                                     lw=lw, aw=aw, qscale=qscale, tm=512)
            n_ctx = ctx.shape[1]
            _, xr_c, _, k_c, v_c = _project_even(ctx, ctx_term(l, 0), ctx_term(l, 1), w_in_b,
                                                 cos_t[0:n_ctx], sin_t[0:n_ctx], rope=False,
                                                 lw=lw, aw=aw, qscale=qscale, tm=256)
            o_l = _diff_attention(q_l, k_c, k_l, v_c, v_l, ev_da_lambda[e], ev_da_subln[e],
                                  heads=heads, hd=hd, lam_init=lam_init)
            gwid = 256
            y = None
            for dirn, rev in ((0, False), (1, True)):
                wa_bd = _block_diag_groups(ev_gate_a_w[e, dirn], gwid)
                wx_bd = _block_diag_groups(ev_gate_x_w[e, dirn], gwid)
                common = (ev_conv_w[e], ev_conv_b[e], wa_bd, ev_gate_a_b[e, dirn], wx_bd,
                          ev_gate_x_b[e, dirn], ev_lru_lambda[e, dirn])
                h_zero = jnp.zeros((bsz, 1, lw), F32)
                _, h_fin = _rglru_dir(xr_c, None, *common, h_zero, reverse=rev, tn=256)
                y, _ = _rglru_dir(xr_l, y, *common, h_fin, reverse=rev, tn=256)
            x1, t = _outproj_even(y, g_l, o_l, x, ev_w_out[e].astype(BF16), g1, ln_g[l, 0], ln_b[l, 0],
                                  sh2, sc2, alpha=alpha, tm=512)
        else:
            o = l // 2
            gd = d // fnet_groups
            zr, zi = _chan_dft(x, sh1, sc1, _chan_dft_table(gd), groups=fnet_groups, tm=512)
            wv = _tok_fft(zr, zi)
            norm = 1.0 / math.sqrt(float(n_lat * gd))
            x1, t = _outproj_odd(wv, x, od_w_out[o].astype(BF16), od_b_out[o], g1, ln_g[l, 0], ln_b[l, 0],
                                 sh2, sc2, alpha=alpha, norm=norm, tm=512)
        x = _hier_moe_layer(x1, t, g2, ln_g[l, 1], ln_b[l, 1], moe_wg[l], moe_bg[l], moe_wf[l], moe_bf[l],
                            moe_w1, moe_w3, moe_w2, layer=l, alpha=alpha)
    return x
```

```python
import functools
import math

import numpy as np
import jax
import jax.numpy as jnp
from jax import lax
from jax.experimental import pallas as pl
from jax.experimental.pallas import tpu as pltpu

F32 = jnp.float32
BF16 = jnp.bfloat16
I32 = jnp.int32

LN_EPS = 1e-6
LRU_C = 8.0
ROPE_BASE = 10000.0
GRID_W = 64
CONV_W = 4
LANES = 128
SUBLANES = 8
MOE_ROWS = 512
MOE_TILE = 256
MOE_CHUNK = 8
VMEM_LIMIT = 56 * 1024 * 1024


def _cp(sem, vmem=None):
    return pltpu.CompilerParams(dimension_semantics=sem, vmem_limit_bytes=vmem)


def _ln(x):
    mu = jnp.mean(x, axis=-1, keepdims=True)
    xc = x - mu
    var = jnp.mean(xc * xc, axis=-1, keepdims=True)
    return xc * lax.rsqrt(var + LN_EPS)


def _silu(x):
    return x * jax.nn.sigmoid(x)


PACK_SUB = 4


def _tile_row(r, mult):
    if isinstance(r, int):
        return r * PACK_SUB
    return pl.multiple_of(r * PACK_SUB, mult)


def _pack_rows(val):
    half = val.shape[1] // 2
    lo = lax.bitcast_convert_type(val[:, 0:half], jnp.uint32) >> 16
    hi = lax.bitcast_convert_type(val[:, half:], jnp.uint32) & jnp.uint32(0xFFFF0000)
    return lo | hi


def _unpack_rows(words):
    lo = lax.bitcast_convert_type(words << 16, F32)
    hi = lax.bitcast_convert_type(words & jnp.uint32(0xFFFF0000), F32)
    return jnp.concatenate([lo, hi], axis=1).astype(BF16)


def _wait_chunks(copy_of, n, nmax):
    b = 1
    while b <= nmax:
        @pl.when((n & b) != 0)
        def _(b=b):
            copy_of(b).wait()
        b *= 2


ISSUE_UNROLL = 4


def _for_each_chunk(n, start_one):
    groups = n // ISSUE_UNROLL

    def group(g, carry):
        for u in range(ISSUE_UNROLL):
            start_one(g * ISSUE_UNROLL + u)
        return carry
    lax.fori_loop(0, groups, group, 0)

    def single(q, carry):
        start_one(q)
        return carry
    lax.fori_loop(groups * ISSUE_UNROLL, n, single, 0)


def _rows_from_tiles(ref, nrows):
    return jnp.concatenate([ref[pl.ds(k, nrows, stride=PACK_SUB), :] for k in range(PACK_SUB)], axis=1)


def _rows_to_tiles(ref, val):
    nrows = val.shape[0]
    for k in range(PACK_SUB):
        ref[pl.ds(k, nrows, stride=PACK_SUB), :] = val[:, k * LANES:(k + 1) * LANES]


def _ada_kernel(c_ref, w_ref, b_ref, o_ref):
    s = _silu(c_ref[...]).astype(BF16)
    o_ref[...] = jnp.dot(s, w_ref[...].astype(BF16), preferred_element_type=F32) + b_ref[...]


def _ada_terms(cond, ada_w, ada_b):
    nl, d, d6 = ada_w.shape
    r = cond.shape[0]
    tn = 1024
    return pl.pallas_call(
        _ada_kernel,
        out_shape=jax.ShapeDtypeStruct((nl, r, d6), F32),
        grid=(nl, d6 // tn),
        in_specs=[pl.BlockSpec((r, d), lambda l, j: (0, 0)),
                  pl.BlockSpec((None, d, tn), lambda l, j: (l, 0, j)),
                  pl.BlockSpec((None, 1, tn), lambda l, j: (l, 0, j))],
        out_specs=pl.BlockSpec((None, r, tn), lambda l, j: (l, 0, j)),
        compiler_params=_cp(("parallel", "parallel")),
        name="ada_terms",
    )(cond, ada_w, ada_b.reshape(nl, 1, d6))


def _rope_apply(x, cos, sin_signed):
    tm = x.shape[0]
    lane = lax.broadcasted_iota(I32, (tm, LANES), 1)
    first_half = (lane % 32) < 16
    outs = []
    for j in range(x.shape[1] // LANES):
        xh = x[:, j * LANES:(j + 1) * LANES]
        partner = jnp.where(first_half, pltpu.roll(xh, LANES - 16, 1), pltpu.roll(xh, 16, 1))
        outs.append(xh * cos + partner * sin_signed)
    return jnp.concatenate(outs, axis=1)


def _proj_kernel(x_ref, sh_ref, sc_ref, w_ref, cos_ref, sin_ref,
                 g_ref, xr_ref, q_ref, k_ref, v_ref, *, rope, lw, aw, qscale):
    h = _ln(x_ref[...]) * (1.0 + sc_ref[...]) + sh_ref[...]
    hb = h.astype(BF16)

    def mm(c0, c1):
        return jnp.dot(hb, w_ref[:, c0:c1], preferred_element_type=F32)

    g_ref[...] = mm(0, lw).astype(BF16)
    xr_ref[...] = mm(lw, 2 * lw)
    q = mm(2 * lw, 2 * lw + aw)
    k = mm(2 * lw + aw, 2 * lw + 2 * aw)
    if rope:
        q = _rope_apply(q, cos_ref[...], sin_ref[...])
        k = _rope_apply(k, cos_ref[...], sin_ref[...])
    q_ref[...] = (q * qscale).astype(BF16)
    k_ref[...] = k.astype(BF16)
    v_ref[...] = mm(2 * lw + 2 * aw, 2 * lw + 3 * aw).astype(BF16)


def _project_even(x, shift, scale, w_in_b, cos_t, sin_t, *, rope, lw, aw, qscale, tm):
    b, n, d = x.shape
    tm = min(tm, n)
    nin = w_in_b.shape[1]
    tok = lambda bi, i: (bi, i, 0)
    per_b = lambda bi, i: (bi, 0, 0)
    outs = (jax.ShapeDtypeStruct((b, n, lw), BF16), jax.ShapeDtypeStruct((b, n, lw), F32),
            jax.ShapeDtypeStruct((b, n, aw), BF16), jax.ShapeDtypeStruct((b, n, aw), BF16),
            jax.ShapeDtypeStruct((b, n, aw), BF16))
    return pl.pallas_call(
        functools.partial(_proj_kernel, rope=rope, lw=lw, aw=aw, qscale=qscale),
        out_shape=outs,
        grid=(b, n // tm),
        in_specs=[pl.BlockSpec((None, tm, d), tok),
                  pl.BlockSpec((None, 1, d), per_b),
                  pl.BlockSpec((None, 1, d), per_b),
                  pl.BlockSpec((d, nin), lambda bi, i: (0, 0)),
                  pl.BlockSpec((tm, LANES), lambda bi, i: (i, 0)),
                  pl.BlockSpec((tm, LANES), lambda bi, i: (i, 0))],
        out_specs=(pl.BlockSpec((None, tm, lw), tok), pl.BlockSpec((None, tm, lw), tok),
                   pl.BlockSpec((None, tm, aw), tok), pl.BlockSpec((None, tm, aw), tok),
                   pl.BlockSpec((None, tm, aw), tok)),
        compiler_params=_cp(("parallel", "parallel"), VMEM_LIMIT),
        name="proj_even_rope" if rope else "proj_even_ctx",
    )(x, shift, scale, w_in_b, cos_t, sin_t)


def _rope_tables(n_tok, head_dim):
    t = jnp.arange(n_tok)
    row = (t // GRID_W).astype(F32)
    col = (t % GRID_W).astype(F32)
    nf = head_dim // 4
    freqs = ROPE_BASE ** (-jnp.arange(nf, dtype=F32) / nf)
    lane = np.arange(LANES)
    within = lane % head_dim
    axis = within // (2 * nf)
    half = (within % (2 * nf)) // nf
    f = within % nf
    pos = jnp.where(jnp.asarray(axis)[None, :] == 0, row[:, None], col[:, None])
    ang = pos * freqs[jnp.asarray(f)][None, :]
    sign = jnp.asarray(np.where(half == 0, -1.0, 1.0), F32)[None, :]
    return jnp.cos(ang).astype(F32), (jnp.sin(ang) * sign).astype(F32)


def _attn_kernel(q_ref, kc_ref, kl_ref, vc_ref, vl_ref, dl_ref, gain_ref, o_ref, kbuf, vbuf, sbuf, ebuf, abuf,
                 cbuf, *, nc, nl, hd, lam_init, rows):
    @pl.when((pl.program_id(0) == 0) & (pl.program_id(1) == 0))
    def _():
        sbuf[...] = jnp.zeros_like(sbuf)
        abuf[...] = jnp.zeros_like(abuf)
        cbuf[...] = jnp.zeros_like(cbuf)

    kbuf[0:nc, :] = kc_ref[...]
    kbuf[nc:nc + nl, :] = kl_ref[...]
    vbuf[0:nc, :] = vc_ref[...]
    vbuf[nc:nc + nl, :] = vl_ref[...]

    lf = dl_ref[...]
    lam = (jnp.exp(jnp.sum(lf[0:1] * lf[1:2], axis=1, keepdims=True))
           - jnp.exp(jnp.sum(lf[2:3] * lf[3:4], axis=1, keepdims=True)) + lam_init)
    gain = gain_ref[...] * (1.0 - lam_init)
    n_sub = nl // rows
    lane = lax.broadcasted_iota(I32, (rows, 2 * hd), 1)
    nt = (((1,), (1,)), ((), ()))

    def stage_a(j, slot):
        r0 = pl.multiple_of(jnp.minimum(j, n_sub - 1) * rows, rows)
        q = q_ref[pl.ds(r0, rows), :]
        zero = jnp.zeros_like(q)
        kk = kbuf[...]
        sbuf[slot, 0] = lax.dot_general(jnp.where(lane < hd, q, zero), kk, nt, preferred_element_type=F32)
        sbuf[slot, 1] = lax.dot_general(jnp.where(lane >= hd, q, zero), kk, nt, preferred_element_type=F32)

    def stage_b(slot):
        cols = [slice(c, c + LANES) for c in range(0, nc + nl, LANES)]
        ls = []
        for k in range(2):
            pm = sbuf[slot, k, :, cols[0]]
            for cs in cols[1:]:
                pm = jnp.maximum(pm, sbuf[slot, k, :, cs])
            m = jnp.max(pm, axis=1, keepdims=True)
            acc = jnp.zeros((rows, LANES), F32)
            for cs in cols:
                e = jnp.exp2(sbuf[slot, k, :, cs] - m)
                acc = acc + e
                ebuf[slot, k, :, cs] = e.astype(BF16)
            ls.append(jnp.sum(acc, axis=1, keepdims=True))
        ratio = (lam * ls[0] / ls[1]).astype(BF16)
        abuf[slot] = ebuf[slot, 0] - ratio * ebuf[slot, 1]
        cbuf[slot] = jnp.broadcast_to(1.0 / ls[0], cbuf.shape[1:])

    def stage_c(j, slot):
        r0 = pl.multiple_of(jnp.maximum(j - 2, 0) * rows, rows)
        o = jnp.dot(abuf[slot], vbuf[...], preferred_element_type=F32) * cbuf[slot]
        o = o * lax.rsqrt(jnp.mean(o * o, axis=1, keepdims=True) + LN_EPS) * gain
        o_ref[pl.ds(r0, rows), :] = o.astype(BF16)

    def body(t, carry):
        j = 2 * t
        stage_a(j, 0)
        stage_b(1)
        stage_c(j, 0)
        stage_a(j + 1, 1)
        stage_b(0)
        stage_c(j + 1, 1)
        return carry

    lax.fori_loop(0, (n_sub + 2) // 2, body, 0)


def _diff_attention(q, k_ctx, k_lat, v_ctx, v_lat, da_lambda, subln, *, heads, hd, lam_init):
    b, n, aw = q.shape
    nc = k_ctx.shape[1]
    vd = aw // heads
    rows = min(128, n)
    assert (n // rows) % 2 == 0
    blk_q = pl.BlockSpec((None, n, vd), lambda bi, h: (bi, 0, h))
    blk_c = pl.BlockSpec((None, nc, vd), lambda bi, h: (bi, 0, h))
    return pl.pallas_call(
        functools.partial(_attn_kernel, nc=nc, nl=n, hd=hd, lam_init=lam_init, rows=rows),
        out_shape=jax.ShapeDtypeStruct((b, n, aw), BF16),
        grid=(b, heads),
        in_specs=[blk_q, blk_c, blk_q, blk_c, blk_q,
                  pl.BlockSpec(da_lambda.shape, lambda bi, h: (0, 0)),
                  pl.BlockSpec((1, vd), lambda bi, h: (0, 0))],
        out_specs=blk_q,
        scratch_shapes=[pltpu.VMEM((nc + n, vd), BF16), pltpu.VMEM((nc + n, vd), BF16),
                        pltpu.VMEM((2, 2, rows, nc + n), F32), pltpu.VMEM((2, 2, rows, nc + n), BF16),
                        pltpu.VMEM((2, rows, nc + n), BF16), pltpu.VMEM((2, rows, vd), F32)],
        compiler_params=_cp(("arbitrary", "arbitrary"), VMEM_LIMIT),
        name="diff_attention",
    )(q, k_ctx, k_lat, v_ctx, v_lat, da_lambda, subln.reshape(1, vd))


def _lru_kernel(*refs, reverse, add_prev, nt, groups):
    if add_prev:
        (xp_ref, xc_ref, xn_ref, yprev_ref, cw_ref, cb_ref, wa_ref, ba_ref, wx_ref, bx_ref,
         lam_ref, h0_ref, y_ref, hf_ref, carry, ext) = refs
    else:
        (xp_ref, xc_ref, xn_ref, cw_ref, cb_ref, wa_ref, ba_ref, wx_ref, bx_ref,
         lam_ref, h0_ref, y_ref, hf_ref, carry, ext) = refs
        yprev_ref = None
    i = pl.program_id(1)
    ti = (nt - 1 - i) if reverse else i
    tn, w = xc_ref.shape

    @pl.when(i == 0)
    def _():
        carry[...] = h0_ref[...]

    ext[0:SUBLANES, :] = jnp.where(ti > 0, xp_ref[...], 0.0)
    ext[SUBLANES:SUBLANES + tn, :] = xc_ref[...]
    ext[SUBLANES + tn:2 * SUBLANES + tn, :] = jnp.where(ti < nt - 1, xn_ref[...], 0.0)
    left = CONV_W // 2
    xc = cb_ref[...]
    for k in range(CONV_W):
        off = SUBLANES - left + k
        xc = xc + ext[off:off + tn, :] * cw_ref[k:k + 1, :]

    xb = xc.astype(BF16)
    gw = w // groups

    def gate(w_ref, b_ref):
        parts = [jnp.dot(xb[:, g * gw:(g + 1) * gw], w_ref[g], preferred_element_type=F32)
                 for g in range(groups)]
        return jax.nn.sigmoid(jnp.concatenate(parts, axis=1) + b_ref[...])

    r = gate(wa_ref, ba_ref)
    ig = gate(wx_ref, bx_ref)
    log_a = (-LRU_C * jax.nn.softplus(-lam_ref[...])) * r
    a = jnp.exp(log_a)
    bcoef = jnp.sqrt(-jnp.tanh(log_a) * (a * a + 1.0)) * ig * xc

    ngroups = tn // SUBLANES
    a = a.reshape(ngroups, SUBLANES, w)
    bcoef = bcoef.reshape(ngroups, SUBLANES, w)
    row = lax.broadcasted_iota(I32, (ngroups, SUBLANES, w), 1)
    d = 1
    while d < SUBLANES:
        shift = (SUBLANES - d) if reverse else d
        a_sh = pltpu.roll(a, shift, 1)
        b_sh = pltpu.roll(bcoef, shift, 1)
        live = (row < SUBLANES - d) if reverse else (row >= d)
        bcoef = jnp.where(live, a * b_sh + bcoef, bcoef)
        a = jnp.where(live, a * a_sh, a)
        d *= 2
    hc = carry[...]
    npairs = ngroups // 2
    for p in (range(npairs - 1, -1, -1) if reverse else range(npairs)):
        hs = {}
        for g in ((2 * p + 1, 2 * p) if reverse else (2 * p, 2 * p + 1)):
            h = a[g] * hc + bcoef[g]
            hc = h[0:1, :] if reverse else h[SUBLANES - 1:SUBLANES, :]
            hs[g] = h
        rs = slice(2 * p * SUBLANES, (2 * p + 2) * SUBLANES)
        h2 = jnp.concatenate([hs[2 * p], hs[2 * p + 1]], axis=0)
        if add_prev:
            h2 = yprev_ref[rs, :].astype(F32) + h2
        y_ref[rs, :] = h2.astype(y_ref.dtype)
    carry[...] = hc
    hf_ref[...] = hc


def _rglru_dir(xr, y_prev, conv_w, conv_b, wa_bd, ba, wx_bd, bx, lam, h0, *, reverse, tn):
    b, n, w = xr.shape
    tn = min(tn, n)
    nt = n // tn
    groups = wa_bd.shape[0]
    nb8 = n // SUBLANES
    per8 = tn // SUBLANES

    def tmap(i):
        return (nt - 1 - i) if reverse else i

    cur = pl.BlockSpec((None, tn, w), lambda bi, i: (bi, tmap(i), 0))
    halo_p = pl.BlockSpec((None, SUBLANES, w), lambda bi, i: (bi, jnp.maximum(tmap(i) * per8 - 1, 0), 0))
    halo_n = pl.BlockSpec((None, SUBLANES, w), lambda bi, i: (bi, jnp.minimum((tmap(i) + 1) * per8, nb8 - 1), 0))
    row_w = pl.BlockSpec((1, w), lambda bi, i: (0, 0))
    per_b = pl.BlockSpec((None, 1, w), lambda bi, i: (bi, 0, 0))
    gate_w = pl.BlockSpec(wa_bd.shape, lambda bi, i: (0, 0, 0))
    add_prev = y_prev is not None
    in_specs = [halo_p, cur, halo_n] + ([cur] if add_prev else []) + [
        pl.BlockSpec((CONV_W, w), lambda bi, i: (0, 0)), row_w, gate_w, row_w, gate_w, row_w, row_w, per_b]
    args = [xr, xr, xr] + ([y_prev] if add_prev else []) + [
        conv_w, conv_b.reshape(1, w), wa_bd, ba.reshape(1, w), wx_bd, bx.reshape(1, w),
        lam.reshape(1, w), h0]
    return pl.pallas_call(
        functools.partial(_lru_kernel, reverse=reverse, add_prev=add_prev, nt=nt, groups=groups),
        out_shape=(jax.ShapeDtypeStruct((b, n, w), BF16), jax.ShapeDtypeStruct((b, 1, w), F32)),
        grid=(b, nt),
        in_specs=in_specs,
        out_specs=(cur, per_b),
        scratch_shapes=[pltpu.VMEM((1, w), F32), pltpu.VMEM((tn + 2 * SUBLANES, w), F32)],
        compiler_params=_cp(("parallel", "arbitrary"), VMEM_LIMIT),
        name="rglru_rev" if reverse else "rglru_fwd",
    )(*args)


def _block_diag_groups(wh, group_width):
    heads, blk, _ = wh.shape
    per = group_width // blk
    groups = heads // per
    whg = wh.reshape(groups, per, blk, blk)
    eye = jnp.eye(per, dtype=wh.dtype)
    bd = jnp.einsum('gpij,pq->gpiqj', whg, eye).reshape(groups, group_width, group_width)
    return bd.astype(BF16)


def _residual_ln_mod(x, y, g1, lng, lnb, sh2, sc2, alpha):
    x1 = _ln(alpha * x + g1 * y) * lng + lnb
    t = _ln(x1) * (1.0 + sc2) + sh2
    return x1, t


def _outproj_even_kernel(r_ref, g_ref, o_ref, x_ref, w_ref, g1_ref, lng_ref, lnb_ref, sh2_ref, sc2_ref,
                         x1_ref, t_ref, *, lw, alpha):
    z = (r_ref[...].astype(F32) * jax.nn.gelu(g_ref[...].astype(F32))).astype(BF16)
    y = (jnp.dot(z, w_ref[0:lw, :], preferred_element_type=F32)
         + jnp.dot(o_ref[...], w_ref[lw:, :], preferred_element_type=F32))
    x1, t = _residual_ln_mod(x_ref[...], y, g1_ref[...], lng_ref[...], lnb_ref[...],
                             sh2_ref[...], sc2_ref[...], alpha)
    x1_ref[...] = x1
    t_ref[...] = t.astype(BF16)


def _outproj_even(r, g, o, x, w_out_b, g1, lng, lnb, sh2, sc2, *, alpha, tm):
    b, n, d = x.shape
    lw = r.shape[2]
    aw = o.shape[2]
    tm = min(tm, n)
    tok = lambda bi, i: (bi, i, 0)
    per_b = pl.BlockSpec((None, 1, d), lambda bi, i: (bi, 0, 0))
    row = pl.BlockSpec((1, d), lambda bi, i: (0, 0))
    return pl.pallas_call(
        functools.partial(_outproj_even_kernel, lw=lw, alpha=alpha),
        out_shape=(jax.ShapeDtypeStruct((b, n, d), F32), jax.ShapeDtypeStruct((b, n, d), BF16)),
        grid=(b, n // tm),
        in_specs=[pl.BlockSpec((None, tm, lw), tok), pl.BlockSpec((None, tm, lw), tok),
                  pl.BlockSpec((None, tm, aw), tok), pl.BlockSpec((None, tm, d), tok),
                  pl.BlockSpec(w_out_b.shape, lambda bi, i: (0, 0)),
                  per_b, row, row, per_b, per_b],
        out_specs=(pl.BlockSpec((None, tm, d), tok), pl.BlockSpec((None, tm, d), tok)),
        compiler_params=_cp(("parallel", "parallel"), VMEM_LIMIT),
        name="outproj_even",
    )(r, g, o, x, w_out_b, g1, lng.reshape(1, d), lnb.reshape(1, d), sh2, sc2)


def _chan_dft_kernel(x_ref, sh_ref, sc_ref, tab_ref, zr_ref, zi_ref, *, groups, gd):
    h = (_ln(x_ref[...]) * (1.0 + sc_ref[...]) + sh_ref[...]).astype(BF16)
    for g in range(groups):
        z = jnp.dot(h[:, g * gd:(g + 1) * gd], tab_ref[...], preferred_element_type=F32)
        zr_ref[:, g * gd:(g + 1) * gd] = z[:, 0:gd].astype(BF16)
        zi_ref[:, g * gd:(g + 1) * gd] = z[:, gd:2 * gd].astype(BF16)


def _chan_dft(x, shift, scale, tab, *, groups, tm):
    b, n, d = x.shape
    gd = d // groups
    tm = min(tm, n)
    tok = lambda bi, i: (bi, i, 0)
    per_b = pl.BlockSpec((None, 1, d), lambda bi, i: (bi, 0, 0))
    return pl.pallas_call(
        functools.partial(_chan_dft_kernel, groups=groups, gd=gd),
        out_shape=(jax.ShapeDtypeStruct((b, n, d), BF16), jax.ShapeDtypeStruct((b, n, d), BF16)),
        grid=(b, n // tm),
        in_specs=[pl.BlockSpec((None, tm, d), tok), per_b, per_b,
                  pl.BlockSpec(tab.shape, lambda bi, i: (0, 0))],
        out_specs=(pl.BlockSpec((None, tm, d), tok), pl.BlockSpec((None, tm, d), tok)),
        compiler_params=_cp(("parallel", "parallel"), VMEM_LIMIT),
        name="chan_dft",
    )(x, shift, scale, tab)


FFT_C = 64


def _fft_pitch(group):
    p = -(-group // SUBLANES)
    return SUBLANES * (p if p % 2 else p + 1)


def _tok_fft_kernel(zr_ref, zi_ref, m1_ref, m3_ref, tc_ref, ts_ref, o_ref, zsr, zsi, asr, asi, ob, *, nr):
    c_len = FFT_C
    pz = zsr.shape[0] // nr
    pa = asr.shape[0] // c_len
    for r in range(nr):
        zsr[pz * r:pz * r + c_len, :] = zr_ref[c_len * r:c_len * (r + 1), :].astype(F32)
        zsi[pz * r:pz * r + c_len, :] = zi_ref[c_len * r:c_len * (r + 1), :].astype(F32)
    m1 = m1_ref[...]
    for c in range(c_len):
        x2 = jnp.concatenate([zsr[pl.ds(c, nr, stride=pz), :], zsi[pl.ds(c, nr, stride=pz), :]], axis=0)
        a2 = jnp.dot(m1, x2.astype(BF16), preferred_element_type=F32)
        ar, ai = a2[0:nr], a2[nr:2 * nr]
        tcv = tc_ref[c * nr:(c + 1) * nr, :]
        tsv = ts_ref[c * nr:(c + 1) * nr, :]
        asr[pa * c:pa * c + nr, :] = ar * tcv + ai * tsv
        asi[pa * c:pa * c + nr, :] = ai * tcv - ar * tsv
    m3 = m3_ref[...]
    for k1 in range(nr):
        y2 = jnp.concatenate([asr[pl.ds(k1, c_len, stride=pa), :], asi[pl.ds(k1, c_len, stride=pa), :]], axis=0)
        ob[pl.ds(k1, c_len, stride=pa), :] = jnp.dot(m3, y2.astype(BF16), preferred_element_type=F32)
    for k2 in range(c_len):
        o_ref[nr * k2:nr * (k2 + 1), :] = ob[pa * k2:pa * k2 + nr, :].astype(BF16)


def _tok_fft(zr, zi):
    b, n, d = zr.shape
    nr = n // FFT_C
    pz = _fft_pitch(FFT_C)
    pa = _fft_pitch(nr)
    kr = np.arange(nr, dtype=np.float64)
    ang_r = 2.0 * np.pi * (np.outer(kr, kr) % nr) / nr
    cr, sr = np.cos(ang_r), np.sin(ang_r)
    m1 = jnp.asarray(np.block([[cr, -sr], [-sr, -cr]]), F32).astype(BF16)
    kc = np.arange(FFT_C, dtype=np.float64)
    ang_c = 2.0 * np.pi * (np.outer(kc, kc) % FFT_C) / FFT_C
    m3 = jnp.asarray(np.concatenate([np.cos(ang_c), np.sin(ang_c)], axis=1), F32).astype(BF16)
    ang_t = 2.0 * np.pi * (np.outer(kc, kr) % n) / n
    tc = jnp.broadcast_to(jnp.asarray(np.cos(ang_t).reshape(FFT_C * nr, 1), F32), (FFT_C * nr, LANES))
    ts = jnp.broadcast_to(jnp.asarray(np.sin(ang_t).reshape(FFT_C * nr, 1), F32), (FFT_C * nr, LANES))
    slab = pl.BlockSpec((None, n, LANES), lambda bi, l: (bi, 0, l))
    const = lambda a: pl.BlockSpec(a.shape, lambda bi, l: (0, 0))
    return pl.pallas_call(
        functools.partial(_tok_fft_kernel, nr=nr),
        out_shape=jax.ShapeDtypeStruct((b, n, d), BF16),
        grid=(b, d // LANES),
        in_specs=[slab, slab, const(m1), const(m3), const(tc), const(ts)],
        out_specs=slab,
        scratch_shapes=[pltpu.VMEM((nr * pz, LANES), F32), pltpu.VMEM((nr * pz, LANES), F32),
                        pltpu.VMEM((FFT_C * pa, LANES), F32), pltpu.VMEM((FFT_C * pa, LANES), F32),
                        pltpu.VMEM((FFT_C * pa, LANES), F32)],
        compiler_params=_cp(("parallel", "parallel"), VMEM_LIMIT),
        name="tok_fft",
    )(zr, zi, m1, m3, tc, ts)


def _chan_dft_table(gd):
    c = np.arange(gd, dtype=np.float64)
    ang_c = 2.0 * np.pi * (np.outer(c, c) % gd) / gd
    return jnp.asarray(np.concatenate([np.cos(ang_c), np.sin(ang_c)], axis=1), F32).astype(BF16)


def _outproj_odd_kernel(wv_ref, x_ref, w_ref, b_ref, g1_ref, lng_ref, lnb_ref, sh2_ref, sc2_ref,
                        x1_ref, t_ref, *, alpha, norm):
    y = jnp.dot(wv_ref[...], w_ref[...], preferred_element_type=F32) * norm + b_ref[...]
    x1, t = _residual_ln_mod(x_ref[...], y, g1_ref[...], lng_ref[...], lnb_ref[...],
                             sh2_ref[...], sc2_ref[...], alpha)
    x1_ref[...] = x1
    t_ref[...] = t.astype(BF16)


def _outproj_odd(wv, x, w_b, bias, g1, lng, lnb, sh2, sc2, *, alpha, norm, tm):
    b, n, d = x.shape
    tm = min(tm, n)
    tok = lambda bi, i: (bi, i, 0)
    per_b = pl.BlockSpec((None, 1, d), lambda bi, i: (bi, 0, 0))
    row = pl.BlockSpec((1, d), lambda bi, i: (0, 0))
    return pl.pallas_call(
        functools.partial(_outproj_odd_kernel, alpha=alpha, norm=norm),
        out_shape=(jax.ShapeDtypeStruct((b, n, d), F32), jax.ShapeDtypeStruct((b, n, d), BF16)),
        grid=(b, n // tm),
        in_specs=[pl.BlockSpec((None, tm, d), tok), pl.BlockSpec((None, tm, d), tok),
                  pl.BlockSpec(w_b.shape, lambda bi, i: (0, 0)), row,
                  per_b, row, row, per_b, per_b],
        out_specs=(pl.BlockSpec((None, tm, d), tok), pl.BlockSpec((None, tm, d), tok)),
        compiler_params=_cp(("parallel", "parallel"), VMEM_LIMIT),
        name="outproj_odd",
    )(wv, x, w_b, bias.reshape(1, d), g1, lng.reshape(1, d), lnb.reshape(1, d), sh2, sc2)


def _route_kernel(t_ref, w_ref, b_ref, up_ref, lp_ref, tokm_ref, cnt_ref, *, ng, epg, chunk):
    tm = t_ref.shape[0]
    ne = ng * epg
    logits = jnp.dot(t_ref[...].astype(BF16), w_ref[...], preferred_element_type=F32) + b_ref[...]
    lt = logits.T
    best = lt[0:1, :]
    bi = jnp.zeros((1, tm), I32)
    for k in range(1, ng):
        gk = lt[k:k + 1, :]
        upd = gk > best
        bi = jnp.where(upd, k, bi)
        best = jnp.where(upd, gk, best)
    den = jnp.zeros((1, tm), F32)
    for k in range(ng):
        den = den + jnp.exp(lt[k:k + 1, :] - best)
    p_g = 1.0 / den
    fsel = lt[SUBLANES:SUBLANES + epg, :]
    for k in range(1, ng):
        fsel = jnp.where(bi == k, lt[SUBLANES + k * epg:SUBLANES + (k + 1) * epg, :], fsel)
    neg = jnp.full((1, tm), -jnp.inf, F32)
    m1, m2 = neg, neg
    i1 = jnp.zeros((1, tm), I32)
    i2 = jnp.zeros((1, tm), I32)
    for j in range(epg):
        v = fsel[j:j + 1, :]
        gt1 = v > m1
        gt2 = v > m2
        m2 = jnp.where(gt1, m1, jnp.where(gt2, v, m2))
        i2 = jnp.where(gt1, i1, jnp.where(gt2, j, i2))
        m1 = jnp.where(gt1, v, m1)
        i1 = jnp.where(gt1, j, i1)
    e21 = jnp.exp(m2 - m1)
    w1 = p_g / (1.0 + e21)
    w2 = p_g * e21 / (1.0 + e21)
    e1 = bi * epg + i1
    e2 = bi * epg + i2

    e = jnp.concatenate([e1, e2], axis=1)
    rows = lax.broadcasted_iota(I32, (ne, 2 * tm), 0)
    onehot = jnp.where(rows == e, 1.0, 0.0)
    before = jnp.dot(onehot.astype(BF16), up_ref[...], preferred_element_type=F32)
    tot = jnp.sum(onehot, axis=1, keepdims=True)
    slots = jnp.floor((tot + (chunk - 1.0)) * (1.0 / chunk)) * chunk
    slots_b = jnp.broadcast_to(slots, (ne, LANES))
    rowe = lax.broadcasted_iota(I32, (ne, LANES), 0)
    incl = slots_b
    d = 1
    while d < ne:
        incl = incl + jnp.where(rowe >= d, pltpu.roll(incl, d, 0), 0.0)
        d *= 2
    seg_off = jnp.tile(incl - slots_b, (1, 2 * tm // LANES))
    lpos = jnp.sum(onehot * (before + seg_off), axis=0, keepdims=True)
    lp0 = lpos[:, 0:tm]
    lp1 = lpos[:, tm:2 * tm]
    row8 = lax.broadcasted_iota(I32, (SUBLANES, tm), 0)
    lp_ref[...] = jnp.where(row8 == 0, lp0, jnp.where(row8 == 1, lp1, 0.0)).astype(I32)
    rowl = lax.broadcasted_iota(I32, (LANES, tm), 0)
    tokm = jnp.where(rowl == 0, w1, jnp.where(rowl == 1, w2, jnp.where(rowl == 2, lp0, jnp.where(rowl == 3, lp1, 0.0))))
    tokm_ref[...] = tokm.T
    cnt_ref[...] = jnp.broadcast_to(tot, (ne, LANES))


def _route(t, wcat_b, bcat, upper, *, ng, epg, tm, chunk):
    tt, d = t.shape
    ne = ng * epg
    nt = tt // tm
    return pl.pallas_call(
        functools.partial(_route_kernel, ng=ng, epg=epg, chunk=chunk),
        out_shape=(jax.ShapeDtypeStruct((SUBLANES, tt), I32), jax.ShapeDtypeStruct((tt, LANES), F32),
                   jax.ShapeDtypeStruct((ne, nt * LANES), F32)),
        grid=(nt,),
        in_specs=[pl.BlockSpec((tm, d), lambda i: (i, 0)),
                  pl.BlockSpec((d, LANES), lambda i: (0, 0)),
                  pl.BlockSpec((1, LANES), lambda i: (0, 0)),
                  pl.BlockSpec(upper.shape, lambda i: (0, 0))],
        out_specs=(pl.BlockSpec((SUBLANES, tm), lambda i: (0, i)),
                   pl.BlockSpec((tm, LANES), lambda i: (i, 0)),
                   pl.BlockSpec((ne, LANES), lambda i: (0, i))),
        compiler_params=_cp(("parallel",), VMEM_LIMIT),
        name="route_sort",
    )(t, wcat_b, bcat, upper)


def _dispatch_kernel(row_ref, tot_ref, t_ref, lp_ref, xs_in_ref, xs_ref, stage, sems, *, qmax, chunk, nsteps):
    del xs_in_ref
    i = pl.program_id(0)
    slot = i % 2
    sp = stage.shape[1] // PACK_SUB
    tm = t_ref.shape[0]
    crow = chunk * PACK_SUB

    lp = lp_ref[...]
    prow = lax.broadcasted_iota(I32, (sp, tm), 0)
    perm = jnp.where(prow == lp[0:1, :], 1.0, jnp.where(prow == lp[1:2, :], 1.0, 0.0)).astype(BF16)
    srt = jnp.dot(perm, t_ref[...], preferred_element_type=F32)
    _rows_to_tiles(stage.at[slot], _pack_rows(srt))

    def chunk_copy(sl, src, dst):
        return pltpu.make_async_copy(stage.at[sl, pl.ds(_tile_row(src, crow), crow)],
                                     xs_ref.at[pl.ds(_tile_row(dst, SUBLANES), crow)], sems.at[sl])

    def drain(sl, n):
        def copy_of(k):
            return pltpu.make_async_copy(stage.at[sl, pl.ds(0, k * crow)], xs_ref.at[pl.ds(0, k * crow)], sems.at[sl])
        _wait_chunks(copy_of, n, qmax)

    @pl.when(i >= 1)
    def _():
        drain(1 - slot, tot_ref[jnp.maximum(i - 1, 0)])

    _for_each_chunk(tot_ref[i], lambda q: chunk_copy(slot, pl.multiple_of(q * chunk, chunk),
                                                     row_ref[i * qmax + q]).start())

    @pl.when(i == nsteps - 1)
    def _():
        drain(slot, tot_ref[i])


def _dispatch(chunk_row, tile_nch, t, lp, *, n_rows, ne, tm, chunk):
    ttot, d = t.shape
    nsteps = ttot // tm
    sp = 2 * tm + ne * chunk
    xs_init = jnp.zeros((n_rows * PACK_SUB, LANES), jnp.uint32)
    grid_spec = pltpu.PrefetchScalarGridSpec(
        num_scalar_prefetch=2,
        grid=(nsteps,),
        in_specs=[pl.BlockSpec((tm, d), lambda i, *_: (i, 0)),
                  pl.BlockSpec((SUBLANES, tm), lambda i, *_: (0, i)),
                  pl.BlockSpec(memory_space=pl.ANY)],
        out_specs=pl.BlockSpec(memory_space=pl.ANY),
        scratch_shapes=[pltpu.VMEM((2, sp * PACK_SUB, LANES), jnp.uint32), pltpu.SemaphoreType.DMA((2,))],
    )
    assert d == 2 * PACK_SUB * LANES
    return pl.pallas_call(
        functools.partial(_dispatch_kernel, qmax=chunk_row.shape[0] // nsteps, chunk=chunk, nsteps=nsteps),
        out_shape=jax.ShapeDtypeStruct(xs_init.shape, xs_init.dtype),
        grid_spec=grid_spec,
        input_output_aliases={4: 0},
        compiler_params=_cp(("arbitrary",), VMEM_LIMIT),
        name="moe_dispatch",
    )(chunk_row, tile_nch, t, lp, xs_init)


def _moe_kernel(be_ref, nu_ref, xs_ref, w1_ref, w3_ref, w2_ref, y_ref, w13b, w2b, *, ff):
    i = pl.program_id(0)

    @pl.when(i < nu_ref[0])
    def _():
        prev = be_ref[jnp.maximum(i - 1, 0)]

        @pl.when((i == 0) | (be_ref[i] != prev))
        def _():
            w13b[:, 0:ff] = w1_ref[...].astype(BF16)
            w13b[:, ff:2 * ff] = w3_ref[...].astype(BF16)
            w2b[...] = w2_ref[...].astype(BF16)

        x = _unpack_rows(_rows_from_tiles(xs_ref, MOE_ROWS))
        h = jnp.dot(x, w13b[...], preferred_element_type=F32)
        hid = (_silu(h[:, 0:ff]) * h[:, ff:2 * ff]).astype(BF16)
        y = jnp.dot(hid, w2b[...], preferred_element_type=F32)
        _rows_to_tiles(y_ref, _pack_rows(y.astype(BF16).astype(F32)))

    @pl.when(i >= nu_ref[0])
    def _():
        y_ref[...] = jnp.zeros_like(y_ref)


def _moe_experts(block_e, n_used, xs, w1, w3, w2, *, layer):
    d = w1.shape[-2]
    ff = w1.shape[-1]
    blk = MOE_ROWS * PACK_SUB
    nb = xs.shape[0] // blk
    wmap = lambda i, be, nu: (layer, be[i], 0, 0)
    grid_spec = pltpu.PrefetchScalarGridSpec(
        num_scalar_prefetch=2,
        grid=(nb,),
        in_specs=[pl.BlockSpec((blk, LANES), lambda i, be, nu: (i, 0)),
                  pl.BlockSpec((None, None, d, ff), wmap),
                  pl.BlockSpec((None, None, d, ff), wmap),
                  pl.BlockSpec((None, None, ff, d), wmap)],
        out_specs=pl.BlockSpec((blk, LANES), lambda i, be, nu: (i, 0)),
        scratch_shapes=[pltpu.VMEM((d, 2 * ff), BF16), pltpu.VMEM((ff, d), BF16)],
    )
    return pl.pallas_call(
        functools.partial(_moe_kernel, ff=ff),
        out_shape=jax.ShapeDtypeStruct(xs.shape, xs.dtype),
        grid_spec=grid_spec,
        compiler_params=_cp(("arbitrary",), VMEM_LIMIT),
        name="moe_experts",
    )(block_e, n_used, xs, w1, w3, w2)


def _combine_kernel(row_ref, tot_ref, yb_ref, x_ref, tokm_ref, g2_ref, lng_ref, lnb_ref,
                    o_ref, stage, sems, *, qmax, chunk, nsteps, alpha):
    i = pl.program_id(0)
    slot = i % 2
    sp = stage.shape[1] // PACK_SUB
    tm = x_ref.shape[0]
    crow = chunk * PACK_SUB

    def chunk_copy(sl, src, dst):
        return pltpu.make_async_copy(yb_ref.at[pl.ds(_tile_row(src, SUBLANES), crow)],
                                     stage.at[sl, pl.ds(_tile_row(dst, crow), crow)], sems.at[sl])

    def issue_tile(step, sl):
        _for_each_chunk(tot_ref[step], lambda q: chunk_copy(sl, row_ref[step * qmax + q],
                                                            pl.multiple_of(q * chunk, chunk)).start())

    @pl.when(i == 0)
    def _():
        stage[...] = jnp.zeros_like(stage)
        issue_tile(0, 0)

    @pl.when(i + 1 < nsteps)
    def _():
        issue_tile(jnp.minimum(i + 1, nsteps - 1), 1 - slot)

    def copy_of(k):
        return pltpu.make_async_copy(yb_ref.at[pl.ds(0, k * crow)], stage.at[slot, pl.ds(0, k * crow)], sems.at[slot])
    _wait_chunks(copy_of, tot_ref[i], qmax)

    tk = tokm_ref[...]
    pos = lax.broadcasted_iota(I32, (tm, sp), 1).astype(F32)
    st = _unpack_rows(_rows_from_tiles(stage.at[slot], sp))
    gsel = jnp.where(pos == tk[:, 2:3], tk[:, 0:1], jnp.where(pos == tk[:, 3:4], tk[:, 1:2], 0.0))
    m = jnp.dot(gsel.astype(BF16), st, preferred_element_type=F32)
    o_ref[...] = _ln(alpha * x_ref[...] + g2_ref[...] * m) * lng_ref[...] + lnb_ref[...]


def _combine(chunk_row, tile_nch, yb, x, tokm, g2, lng, lnb, *, ne, tm, chunk, n_per_batch, alpha):
    ttot, d = x.shape
    nsteps = ttot // tm
    per = n_per_batch // tm
    sp = 2 * tm + ne * chunk
    grid_spec = pltpu.PrefetchScalarGridSpec(
        num_scalar_prefetch=2,
        grid=(nsteps,),
        in_specs=[pl.BlockSpec(memory_space=pl.ANY),
                  pl.BlockSpec((tm, d), lambda i, *_: (i, 0)),
                  pl.BlockSpec((tm, LANES), lambda i, *_: (i, 0)),
                  pl.BlockSpec((None, 1, d), lambda i, *_: (i // per, 0, 0)),
                  pl.BlockSpec((1, d), lambda i, *_: (0, 0)),
                  pl.BlockSpec((1, d), lambda i, *_: (0, 0))],
        out_specs=pl.BlockSpec((tm, d), lambda i, *_: (i, 0)),
        scratch_shapes=[pltpu.VMEM((2, sp * PACK_SUB, LANES), yb.dtype), pltpu.SemaphoreType.DMA((2,))],
    )
    return pl.pallas_call(
        functools.partial(_combine_kernel, qmax=chunk_row.shape[0] // nsteps, chunk=chunk, nsteps=nsteps,
                          alpha=alpha),
        out_shape=jax.ShapeDtypeStruct((ttot, d), F32),
        grid_spec=grid_spec,
        compiler_params=_cp(("arbitrary",), VMEM_LIMIT),
        name="moe_combine",
    )(chunk_row, tile_nch, yb, x, tokm, g2, lng.reshape(1, d), lnb.reshape(1, d))


def _hier_moe_layer(x1, t, g2, lng, lnb, wg, bg, wf, bf, w1, w3, w2, *, layer, alpha):
    b, n, d = x1.shape
    ttot = b * n
    ng = wg.shape[1]
    ne = wf.shape[1]
    epg = ne // ng
    tm = min(MOE_TILE, n)
    chunk = MOE_CHUNK
    nt = ttot // tm
    tflat = t.reshape(ttot, d)
    wcat = jnp.zeros((d, LANES), F32).at[:, 0:ng].set(wg).at[:, SUBLANES:SUBLANES + ne].set(wf).astype(BF16)
    bcat = jnp.zeros((1, LANES), F32).at[0, 0:ng].set(bg).at[0, SUBLANES:SUBLANES + ne].set(bf)
    ar = jnp.arange(2 * tm, dtype=I32)
    upper = (ar[:, None] < ar[None, :]).astype(BF16)
    lp, tokm, cnt = _route(tflat, wcat, bcat, upper, ng=ng, epg=epg, tm=tm, chunk=chunk)

    tile_cnt = cnt.reshape(ne, nt, LANES)[:, :, 0].T.astype(I32)
    seg_len = ((tile_cnt + 1) // 2) * 2
    counts = jnp.sum(seg_len, axis=0)
    padded = ((counts + chunk + MOE_ROWS - 1) // MOE_ROWS) * MOE_ROWS
    pend = jnp.cumsum(padded)
    pstart = pend - padded
    base = jnp.cumsum(seg_len, axis=0) - seg_len
    seg_row = pstart[None, :] + base
    nch = (tile_cnt + chunk - 1) // chunk
    nch_end = jnp.cumsum(nch, axis=1)
    tile_nch = nch_end[:, -1].astype(I32)
    qmax = 2 * tm // chunk + ne
    qs = jnp.arange(qmax, dtype=I32)
    e_q = jnp.minimum(jnp.sum((nch_end[:, None, :] <= qs[None, :, None]).astype(I32), axis=2), ne - 1)
    pick = e_q[:, :, None] == jnp.arange(ne, dtype=I32)[None, None, :]
    first_q = jnp.sum(jnp.where(pick, (nch_end - nch)[:, None, :], 0), axis=2)
    first_row = jnp.sum(jnp.where(pick, seg_row[:, None, :], 0), axis=2)
    chunk_row = (first_row + (qs[None, :] - first_q) * chunk).astype(I32).reshape(nt * qmax)
    nb = -(-(2 * ttot + nt * ne + ne * (chunk + MOE_ROWS)) // MOE_ROWS)
    bstart = jnp.arange(nb, dtype=I32) * MOE_ROWS
    block_e = jnp.minimum(jnp.sum((pend[None, :] <= bstart[:, None]).astype(I32), axis=1), ne - 1)
    n_used = (pend[-1] // MOE_ROWS).astype(I32).reshape(1)

    xs = _dispatch(chunk_row, tile_nch, tflat, lp, n_rows=nb * MOE_ROWS, ne=ne, tm=tm, chunk=chunk)
    yb = _moe_experts(block_e.astype(I32), n_used, xs, w1, w3, w2, layer=layer)
    out = _combine(chunk_row, tile_nch, yb, x1.reshape(ttot, d), tokm, g2, lng, lnb,
                   ne=ne, tm=tm, chunk=chunk, n_per_batch=n, alpha=alpha)
    return out.reshape(b, n, d)


def kernel(x, c, ctx, c_ctx, ada_w, ada_b, ln_g, ln_b, ev_w_in, ev_conv_w, ev_conv_b, ev_gate_a_w,
           ev_gate_a_b, ev_gate_x_w, ev_gate_x_b, ev_lru_lambda, ev_da_lambda, ev_da_subln, ev_w_out,
           od_w_out, od_b_out, moe_wg, moe_bg, moe_wf, moe_bf, moe_w1, moe_w3, moe_w2):
    bsz, n_lat, d = x.shape
    depth = ada_w.shape[0]
    alpha = (2.0 * depth) ** 0.25
    lw = ev_conv_w.shape[-1]
    hd = ev_da_lambda.shape[-1]
    vd = ev_da_subln.shape[-1]
    aw = (ev_w_in.shape[-1] - 2 * lw) // 3
    heads = aw // vd
    fnet_groups = 4

    rows = ((bsz + 1 + SUBLANES - 1) // SUBLANES) * SUBLANES
    cond = jnp.zeros((rows, d), F32).at[0:bsz].set(c).at[bsz].set(c_ctx)
    ada = _ada_terms(cond, ada_w, ada_b).reshape(depth, rows, 6, d)

    def lat_term(l, k):
        return ada[l, 0:bsz, k, :].reshape(bsz, 1, d)

    def ctx_term(l, k):
        return jnp.broadcast_to(ada[l, bsz, k, :].reshape(1, 1, d), (bsz, 1, d))

    for l in range(depth):
        ctx_live = any(m % 2 == 0 for m in range(l + 1, depth))
        assert not ctx_live, "context stream update is only needed for depth > 2"
        sh1, sc1, g1, sh2, sc2, g2 = [lat_term(l, k) for k in range(6)]
        if l % 2 == 0:
            e = l // 2
            lam_init = 0.8 - 0.6 * math.exp(-0.3 * l)
            w_in_b = ev_w_in[e].astype(BF16)
            cos_t, sin_t = _rope_tables(n_lat, hd)
            qscale = hd ** -0.5 * math.log2(math.e)
            g_l, xr_l, q_l, k_l, v_l = _project_even(x, sh1, sc1, w_in_b, cos_t, sin_t, rope=True,
                                                     lw=lw, aw=aw, qscale=qscale, tm=512)
            n_ctx = ctx.shape[1]
            _, xr_c, _, k_c, v_c = _project_even(ctx, ctx_term(l, 0), ctx_term(l, 1), w_in_b,
                                                 cos_t[0:n_ctx], sin_t[0:n_ctx], rope=False,
                                                 lw=lw, aw=aw, qscale=qscale, tm=256)
            o_l = _diff_attention(q_l, k_c, k_l, v_c, v_l, ev_da_lambda[e], ev_da_subln[e],
                                  heads=heads, hd=hd, lam_init=lam_init)
            gwid = 256
            y = None
            for dirn, rev in ((0, False), (1, True)):
                wa_bd = _block_diag_groups(ev_gate_a_w[e, dirn], gwid)
                wx_bd = _block_diag_groups(ev_gate_x_w[e, dirn], gwid)
                common = (ev_conv_w[e], ev_conv_b[e], wa_bd, ev_gate_a_b[e, dirn], wx_bd,
                          ev_gate_x_b[e, dirn], ev_lru_lambda[e, dirn])
                h_zero = jnp.zeros((bsz, 1, lw), F32)
                _, h_fin = _rglru_dir(xr_c, None, *common, h_zero, reverse=rev, tn=256)
                y, _ = _rglru_dir(xr_l, y, *common, h_fin, reverse=rev, tn=512)
            x1, t = _outproj_even(y, g_l, o_l, x, ev_w_out[e].astype(BF16), g1, ln_g[l, 0], ln_b[l, 0],
                                  sh2, sc2, alpha=alpha, tm=1024)
        else:
            o = l // 2
            gd = d // fnet_groups
            zr, zi = _chan_dft(x, sh1, sc1, _chan_dft_table(gd), groups=fnet_groups, tm=1024)
            wv = _tok_fft(zr, zi)
            norm = 1.0 / math.sqrt(float(n_lat * gd))
            x1, t = _outproj_odd(wv, x, od_w_out[o].astype(BF16), od_b_out[o], g1, ln_g[l, 0], ln_b[l, 0],
                                 sh2, sc2, alpha=alpha, norm=norm, tm=1024)
        x = _hier_moe_layer(x1, t, g2, ln_g[l, 1], ln_b[l, 1], moe_wg[l], moe_bg[l], moe_wf[l], moe_bf[l],
                            moe_w1, moe_w3, moe_w2, layer=l, alpha=alpha)
    return x
```

```python
import functools
import math

import numpy as np
import jax
import jax.numpy as jnp
from jax import lax
from jax.experimental import pallas as pl
from jax.experimental.pallas import tpu as pltpu

F32 = jnp.float32
BF16 = jnp.bfloat16
I32 = jnp.int32

LN_EPS = 1e-6
LRU_C = 8.0
ROPE_BASE = 10000.0
GRID_W = 64
CONV_W = 4
LANES = 128
SUBLANES = 8
MOE_ROWS = 512
MOE_TILE = 256
MOE_CHUNK = 8
VMEM_LIMIT = 56 * 1024 * 1024


def _cp(sem, vmem=None):
    return pltpu.CompilerParams(dimension_semantics=sem, vmem_limit_bytes=vmem)


def _ln(x):
    mu = jnp.mean(x, axis=-1, keepdims=True)
    xc = x - mu
    var = jnp.mean(xc * xc, axis=-1, keepdims=True)
    return xc * lax.rsqrt(var + LN_EPS)


def _silu(x):
    return x * jax.nn.sigmoid(x)


PACK_SUB = 4


def _tile_row(r, mult):
    if isinstance(r, int):
        return r * PACK_SUB
    return pl.multiple_of(r * PACK_SUB, mult)


def _pack_rows(val):
    half = val.shape[1] // 2
    lo = lax.bitcast_convert_type(val[:, 0:half], jnp.uint32) >> 16
    hi = lax.bitcast_convert_type(val[:, half:], jnp.uint32) & jnp.uint32(0xFFFF0000)
    return lo | hi


def _unpack_rows(words):
    lo = lax.bitcast_convert_type(words << 16, F32)
    hi = lax.bitcast_convert_type(words & jnp.uint32(0xFFFF0000), F32)
    return jnp.concatenate([lo, hi], axis=1).astype(BF16)


def _wait_chunks(copy_of, n, nmax):
    b = 1
    while b <= nmax:
        @pl.when((n & b) != 0)
        def _(b=b):
            copy_of(b).wait()
        b *= 2


ISSUE_UNROLL = 4


def _for_each_chunk(n, start_one):
    groups = n // ISSUE_UNROLL

    def group(g, carry):
        for u in range(ISSUE_UNROLL):
            start_one(g * ISSUE_UNROLL + u)
        return carry
    lax.fori_loop(0, groups, group, 0)

    def single(q, carry):
        start_one(q)
        return carry
    lax.fori_loop(groups * ISSUE_UNROLL, n, single, 0)


def _rows_from_tiles(ref, nrows):
    return jnp.concatenate([ref[pl.ds(k, nrows, stride=PACK_SUB), :] for k in range(PACK_SUB)], axis=1)


def _rows_to_tiles(ref, val):
    nrows = val.shape[0]
    for k in range(PACK_SUB):
        ref[pl.ds(k, nrows, stride=PACK_SUB), :] = val[:, k * LANES:(k + 1) * LANES]


def _ada_kernel(c_ref, w_ref, b_ref, o_ref):
    s = _silu(c_ref[...]).astype(BF16)
    o_ref[...] = jnp.dot(s, w_ref[...].astype(BF16), preferred_element_type=F32) + b_ref[...]


def _ada_terms(cond, ada_w, ada_b):
    nl, d, d6 = ada_w.shape
    r = cond.shape[0]
    tn = 1024
    return pl.pallas_call(
        _ada_kernel,
        out_shape=jax.ShapeDtypeStruct((nl, r, d6), F32),
        grid=(nl, d6 // tn),
        in_specs=[pl.BlockSpec((r, d), lambda l, j: (0, 0)),
                  pl.BlockSpec((None, d, tn), lambda l, j: (l, 0, j)),
                  pl.BlockSpec((None, 1, tn), lambda l, j: (l, 0, j))],
        out_specs=pl.BlockSpec((None, r, tn), lambda l, j: (l, 0, j)),
        compiler_params=_cp(("parallel", "parallel")),
        name="ada_terms",
    )(cond, ada_w, ada_b.reshape(nl, 1, d6))


def _rope_apply(x, cos, sin_signed):
    tm = x.shape[0]
    lane = lax.broadcasted_iota(I32, (tm, LANES), 1)
    first_half = (lane % 32) < 16
    outs = []
    for j in range(x.shape[1] // LANES):
        xh = x[:, j * LANES:(j + 1) * LANES]
        partner = jnp.where(first_half, pltpu.roll(xh, LANES - 16, 1), pltpu.roll(xh, 16, 1))
        outs.append(xh * cos + partner * sin_signed)
    return jnp.concatenate(outs, axis=1)


def _proj_kernel(x_ref, sh_ref, sc_ref, w_ref, cos_ref, sin_ref,
                 g_ref, xr_ref, q_ref, k_ref, v_ref, *, rope, lw, aw, qscale):
    h = _ln(x_ref[...]) * (1.0 + sc_ref[...]) + sh_ref[...]
    hb = h.astype(BF16)

    def mm(c0, c1):
        return jnp.dot(hb, w_ref[:, c0:c1], preferred_element_type=F32)

    g_ref[...] = mm(0, lw).astype(BF16)
    xr_ref[...] = mm(lw, 2 * lw)
    q = mm(2 * lw, 2 * lw + aw)
    k = mm(2 * lw + aw, 2 * lw + 2 * aw)
    if rope:
        q = _rope_apply(q, cos_ref[...], sin_ref[...])
        k = _rope_apply(k, cos_ref[...], sin_ref[...])
    q_ref[...] = (q * qscale).astype(BF16)
    k_ref[...] = k.astype(BF16)
    v_ref[...] = mm(2 * lw + 2 * aw, 2 * lw + 3 * aw).astype(BF16)


def _project_even(x, shift, scale, w_in_b, cos_t, sin_t, *, rope, lw, aw, qscale, tm):
    b, n, d = x.shape
    tm = min(tm, n)
    nin = w_in_b.shape[1]
    tok = lambda bi, i: (bi, i, 0)
    per_b = lambda bi, i: (bi, 0, 0)
    outs = (jax.ShapeDtypeStruct((b, n, lw), BF16), jax.ShapeDtypeStruct((b, n, lw), F32),
            jax.ShapeDtypeStruct((b, n, aw), BF16), jax.ShapeDtypeStruct((b, n, aw), BF16),
            jax.ShapeDtypeStruct((b, n, aw), BF16))
    return pl.pallas_call(
        functools.partial(_proj_kernel, rope=rope, lw=lw, aw=aw, qscale=qscale),
        out_shape=outs,
        grid=(b, n // tm),
        in_specs=[pl.BlockSpec((None, tm, d), tok),
                  pl.BlockSpec((None, 1, d), per_b),
                  pl.BlockSpec((None, 1, d), per_b),
                  pl.BlockSpec((d, nin), lambda bi, i: (0, 0)),
                  pl.BlockSpec((tm, LANES), lambda bi, i: (i, 0)),
                  pl.BlockSpec((tm, LANES), lambda bi, i: (i, 0))],
        out_specs=(pl.BlockSpec((None, tm, lw), tok), pl.BlockSpec((None, tm, lw), tok),
                   pl.BlockSpec((None, tm, aw), tok), pl.BlockSpec((None, tm, aw), tok),
                   pl.BlockSpec((None, tm, aw), tok)),
        compiler_params=_cp(("parallel", "parallel"), VMEM_LIMIT),
        name="proj_even_rope" if rope else "proj_even_ctx",
    )(x, shift, scale, w_in_b, cos_t, sin_t)


def _rope_tables(n_tok, head_dim):
    t = jnp.arange(n_tok)
    row = (t // GRID_W).astype(F32)
    col = (t % GRID_W).astype(F32)
    nf = head_dim // 4
    freqs = ROPE_BASE ** (-jnp.arange(nf, dtype=F32) / nf)
    lane = np.arange(LANES)
    within = lane % head_dim
    axis = within // (2 * nf)
    half = (within % (2 * nf)) // nf
    f = within % nf
    pos = jnp.where(jnp.asarray(axis)[None, :] == 0, row[:, None], col[:, None])
    ang = pos * freqs[jnp.asarray(f)][None, :]
    sign = jnp.asarray(np.where(half == 0, -1.0, 1.0), F32)[None, :]
    return jnp.cos(ang).astype(F32), (jnp.sin(ang) * sign).astype(F32)


def _attn_kernel(q_ref, kc_ref, kl_ref, vc_ref, vl_ref, dl_ref, gain_ref, o_ref, kbuf, vbuf, sbuf, ebuf, abuf,
                 cbuf, *, nc, nl, hd, lam_init, rows):
    @pl.when((pl.program_id(0) == 0) & (pl.program_id(1) == 0))
    def _():
        sbuf[...] = jnp.zeros_like(sbuf)
        abuf[...] = jnp.zeros_like(abuf)
        cbuf[...] = jnp.zeros_like(cbuf)

    kbuf[0:nc, :] = kc_ref[...]
    kbuf[nc:nc + nl, :] = kl_ref[...]
    vbuf[0:nc, :] = vc_ref[...]
    vbuf[nc:nc + nl, :] = vl_ref[...]

    lf = dl_ref[...]
    lam = (jnp.exp(jnp.sum(lf[0:1] * lf[1:2], axis=1, keepdims=True))
           - jnp.exp(jnp.sum(lf[2:3] * lf[3:4], axis=1, keepdims=True)) + lam_init)
    gain = gain_ref[...] * (1.0 - lam_init)
    n_sub = nl // rows
    lane = lax.broadcasted_iota(I32, (rows, 2 * hd), 1)
    nt = (((1,), (1,)), ((), ()))

    def stage_a(j, slot):
        r0 = pl.multiple_of(jnp.minimum(j, n_sub - 1) * rows, rows)
        q = q_ref[pl.ds(r0, rows), :]
        zero = jnp.zeros_like(q)
        kk = kbuf[...]
        sbuf[slot, 0] = lax.dot_general(jnp.where(lane < hd, q, zero), kk, nt, preferred_element_type=F32)
        sbuf[slot, 1] = lax.dot_general(jnp.where(lane >= hd, q, zero), kk, nt, preferred_element_type=F32)

    def stage_b(slot):
        cols = [slice(c, c + LANES) for c in range(0, nc + nl, LANES)]
        ls = []
        for k in range(2):
            pm = sbuf[slot, k, :, cols[0]]
            for cs in cols[1:]:
                pm = jnp.maximum(pm, sbuf[slot, k, :, cs])
            m = jnp.max(pm, axis=1, keepdims=True)
            acc = jnp.zeros((rows, LANES), F32)
            for cs in cols:
                e = jnp.exp2(sbuf[slot, k, :, cs] - m)
                acc = acc + e
                ebuf[slot, k, :, cs] = e.astype(BF16)
            ls.append(jnp.sum(acc, axis=1, keepdims=True))
        ratio = (lam * ls[0] / ls[1]).astype(BF16)
        abuf[slot] = ebuf[slot, 0] - ratio * ebuf[slot, 1]
        cbuf[slot] = jnp.broadcast_to(1.0 / ls[0], cbuf.shape[1:])

    def stage_c(j, slot):
        r0 = pl.multiple_of(jnp.maximum(j - 2, 0) * rows, rows)
        o = jnp.dot(abuf[slot], vbuf[...], preferred_element_type=F32) * cbuf[slot]
        o = o * lax.rsqrt(jnp.mean(o * o, axis=1, keepdims=True) + LN_EPS) * gain
        o_ref[pl.ds(r0, rows), :] = o.astype(BF16)

    def body(t, carry):
        j = 2 * t
        stage_a(j, 0)
        stage_b(1)
        stage_c(j, 0)
        stage_a(j + 1, 1)
        stage_b(0)
        stage_c(j + 1, 1)
        return carry

    lax.fori_loop(0, (n_sub + 2) // 2, body, 0)


def _diff_attention(q, k_ctx, k_lat, v_ctx, v_lat, da_lambda, subln, *, heads, hd, lam_init):
    b, n, aw = q.shape
    nc = k_ctx.shape[1]
    vd = aw // heads
    rows = min(128, n)
    assert (n // rows) % 2 == 0
    blk_q = pl.BlockSpec((None, n, vd), lambda bi, h: (bi, 0, h))
    blk_c = pl.BlockSpec((None, nc, vd), lambda bi, h: (bi, 0, h))
    return pl.pallas_call(
        functools.partial(_attn_kernel, nc=nc, nl=n, hd=hd, lam_init=lam_init, rows=rows),
        out_shape=jax.ShapeDtypeStruct((b, n, aw), BF16),
        grid=(b, heads),
        in_specs=[blk_q, blk_c, blk_q, blk_c, blk_q,
                  pl.BlockSpec(da_lambda.shape, lambda bi, h: (0, 0)),
                  pl.BlockSpec((1, vd), lambda bi, h: (0, 0))],
        out_specs=blk_q,
        scratch_shapes=[pltpu.VMEM((nc + n, vd), BF16), pltpu.VMEM((nc + n, vd), BF16),
                        pltpu.VMEM((2, 2, rows, nc + n), F32), pltpu.VMEM((2, 2, rows, nc + n), BF16),
                        pltpu.VMEM((2, rows, nc + n), BF16), pltpu.VMEM((2, rows, vd), F32)],
        compiler_params=_cp(("arbitrary", "arbitrary"), VMEM_LIMIT),
        name="diff_attention",
    )(q, k_ctx, k_lat, v_ctx, v_lat, da_lambda, subln.reshape(1, vd))


def _lru_kernel(*refs, reverse, add_prev, nt, groups):
    if add_prev:
        (xp_ref, xc_ref, xn_ref, yprev_ref, cw_ref, cb_ref, wa_ref, ba_ref, wx_ref, bx_ref,
         lam_ref, h0_ref, y_ref, hf_ref, carry, ext) = refs
    else:
        (xp_ref, xc_ref, xn_ref, cw_ref, cb_ref, wa_ref, ba_ref, wx_ref, bx_ref,
         lam_ref, h0_ref, y_ref, hf_ref, carry, ext) = refs
        yprev_ref = None
    i = pl.program_id(1)
    ti = (nt - 1 - i) if reverse else i
    tn, w = xc_ref.shape

    @pl.when(i == 0)
    def _():
        carry[...] = h0_ref[...]

    ext[0:SUBLANES, :] = jnp.where(ti > 0, xp_ref[...], 0.0)
    ext[SUBLANES:SUBLANES + tn, :] = xc_ref[...]
    ext[SUBLANES + tn:2 * SUBLANES + tn, :] = jnp.where(ti < nt - 1, xn_ref[...], 0.0)
    left = CONV_W // 2
    xc = cb_ref[...]
    for k in range(CONV_W):
        off = SUBLANES - left + k
        xc = xc + ext[off:off + tn, :] * cw_ref[k:k + 1, :]

    xb = xc.astype(BF16)
    gw = w // groups

    def gate(w_ref, b_ref):
        parts = [jnp.dot(xb[:, g * gw:(g + 1) * gw], w_ref[g], preferred_element_type=F32)
                 for g in range(groups)]
        return jax.nn.sigmoid(jnp.concatenate(parts, axis=1) + b_ref[...])

    r = gate(wa_ref, ba_ref)
    ig = gate(wx_ref, bx_ref)
    log_a = (-LRU_C * jax.nn.softplus(-lam_ref[...])) * r
    a = jnp.exp(log_a)
    bcoef = jnp.sqrt(-jnp.tanh(log_a) * (a * a + 1.0)) * ig * xc

    ngroups = tn // SUBLANES
    a = a.reshape(ngroups, SUBLANES, w)
    bcoef = bcoef.reshape(ngroups, SUBLANES, w)
    row = lax.broadcasted_iota(I32, (ngroups, SUBLANES, w), 1)
    d = 1
    while d < SUBLANES:
        shift = (SUBLANES - d) if reverse else d
        a_sh = pltpu.roll(a, shift, 1)
        b_sh = pltpu.roll(bcoef, shift, 1)
        live = (row < SUBLANES - d) if reverse else (row >= d)
        bcoef = jnp.where(live, a * b_sh + bcoef, bcoef)
        a = jnp.where(live, a * a_sh, a)
        d *= 2
    hc = carry[...]
    npairs = ngroups // 2
    for p in (range(npairs - 1, -1, -1) if reverse else range(npairs)):
        hs = {}
        for g in ((2 * p + 1, 2 * p) if reverse else (2 * p, 2 * p + 1)):
            h = a[g] * hc + bcoef[g]
            hc = h[0:1, :] if reverse else h[SUBLANES - 1:SUBLANES, :]
            hs[g] = h
        rs = slice(2 * p * SUBLANES, (2 * p + 2) * SUBLANES)
        h2 = jnp.concatenate([hs[2 * p], hs[2 * p + 1]], axis=0)
        if add_prev:
            h2 = yprev_ref[rs, :].astype(F32) + h2
        y_ref[rs, :] = h2.astype(y_ref.dtype)
    carry[...] = hc
    hf_ref[...] = hc


def _rglru_dir(xr, y_prev, conv_w, conv_b, wa_bd, ba, wx_bd, bx, lam, h0, *, reverse, tn):
    b, n, w = xr.shape
    tn = min(tn, n)
    nt = n // tn
    groups = wa_bd.shape[0]
    nb8 = n // SUBLANES
    per8 = tn // SUBLANES

    def tmap(i):
        return (nt - 1 - i) if reverse else i

    cur = pl.BlockSpec((None, tn, w), lambda bi, i: (bi, tmap(i), 0))
    halo_p = pl.BlockSpec((None, SUBLANES, w), lambda bi, i: (bi, jnp.maximum(tmap(i) * per8 - 1, 0), 0))
    halo_n = pl.BlockSpec((None, SUBLANES, w), lambda bi, i: (bi, jnp.minimum((tmap(i) + 1) * per8, nb8 - 1), 0))
    row_w = pl.BlockSpec((1, w), lambda bi, i: (0, 0))
    per_b = pl.BlockSpec((None, 1, w), lambda bi, i: (bi, 0, 0))
    gate_w = pl.BlockSpec(wa_bd.shape, lambda bi, i: (0, 0, 0))
    add_prev = y_prev is not None
    in_specs = [halo_p, cur, halo_n] + ([cur] if add_prev else []) + [
        pl.BlockSpec((CONV_W, w), lambda bi, i: (0, 0)), row_w, gate_w, row_w, gate_w, row_w, row_w, per_b]
    args = [xr, xr, xr] + ([y_prev] if add_prev else []) + [
        conv_w, conv_b.reshape(1, w), wa_bd, ba.reshape(1, w), wx_bd, bx.reshape(1, w),
        lam.reshape(1, w), h0]
    return pl.pallas_call(
        functools.partial(_lru_kernel, reverse=reverse, add_prev=add_prev, nt=nt, groups=groups),
        out_shape=(jax.ShapeDtypeStruct((b, n, w), BF16), jax.ShapeDtypeStruct((b, 1, w), F32)),
        grid=(b, nt),
        in_specs=in_specs,
        out_specs=(cur, per_b),
        scratch_shapes=[pltpu.VMEM((1, w), F32), pltpu.VMEM((tn + 2 * SUBLANES, w), F32)],
        compiler_params=_cp(("parallel", "arbitrary"), VMEM_LIMIT),
        name="rglru_rev" if reverse else "rglru_fwd",
    )(*args)


def _block_diag_groups(wh, group_width):
    heads, blk, _ = wh.shape
    per = group_width // blk
    groups = heads // per
    whg = wh.reshape(groups, per, blk, blk)
    eye = jnp.eye(per, dtype=wh.dtype)
    bd = jnp.einsum('gpij,pq->gpiqj', whg, eye).reshape(groups, group_width, group_width)
    return bd.astype(BF16)


def _residual_ln_mod(x, y, g1, lng, lnb, sh2, sc2, alpha):
    x1 = _ln(alpha * x + g1 * y) * lng + lnb
    t = _ln(x1) * (1.0 + sc2) + sh2
    return x1, t


def _outproj_even_kernel(r_ref, g_ref, o_ref, x_ref, w_ref, g1_ref, lng_ref, lnb_ref, sh2_ref, sc2_ref,
                         x1_ref, t_ref, *, lw, alpha):
    z = (r_ref[...].astype(F32) * jax.nn.gelu(g_ref[...].astype(F32))).astype(BF16)
    y = (jnp.dot(z, w_ref[0:lw, :], preferred_element_type=F32)
         + jnp.dot(o_ref[...], w_ref[lw:, :], preferred_element_type=F32))
    x1, t = _residual_ln_mod(x_ref[...], y, g1_ref[...], lng_ref[...], lnb_ref[...],
                             sh2_ref[...], sc2_ref[...], alpha)
    x1_ref[...] = x1
    t_ref[...] = t.astype(BF16)


def _outproj_even(r, g, o, x, w_out_b, g1, lng, lnb, sh2, sc2, *, alpha, tm):
    b, n, d = x.shape
    lw = r.shape[2]
    aw = o.shape[2]
    tm = min(tm, n)
    tok = lambda bi, i: (bi, i, 0)
    per_b = pl.BlockSpec((None, 1, d), lambda bi, i: (bi, 0, 0))
    row = pl.BlockSpec((1, d), lambda bi, i: (0, 0))
    return pl.pallas_call(
        functools.partial(_outproj_even_kernel, lw=lw, alpha=alpha),
        out_shape=(jax.ShapeDtypeStruct((b, n, d), F32), jax.ShapeDtypeStruct((b, n, d), BF16)),
        grid=(b, n // tm),
        in_specs=[pl.BlockSpec((None, tm, lw), tok), pl.BlockSpec((None, tm, lw), tok),
                  pl.BlockSpec((None, tm, aw), tok), pl.BlockSpec((None, tm, d), tok),
                  pl.BlockSpec(w_out_b.shape, lambda bi, i: (0, 0)),
                  per_b, row, row, per_b, per_b],
        out_specs=(pl.BlockSpec((None, tm, d), tok), pl.BlockSpec((None, tm, d), tok)),
        compiler_params=_cp(("parallel", "parallel"), VMEM_LIMIT),
        name="outproj_even",
    )(r, g, o, x, w_out_b, g1, lng.reshape(1, d), lnb.reshape(1, d), sh2, sc2)


def _chan_dft_kernel(x_ref, sh_ref, sc_ref, tab_ref, zr_ref, zi_ref, *, groups, gd):
    h = (_ln(x_ref[...]) * (1.0 + sc_ref[...]) + sh_ref[...]).astype(BF16)
    for g in range(groups):
        z = jnp.dot(h[:, g * gd:(g + 1) * gd], tab_ref[...], preferred_element_type=F32)
        zr_ref[:, g * gd:(g + 1) * gd] = z[:, 0:gd].astype(BF16)
        zi_ref[:, g * gd:(g + 1) * gd] = z[:, gd:2 * gd].astype(BF16)


def _chan_dft(x, shift, scale, tab, *, groups, tm):
    b, n, d = x.shape
    gd = d // groups
    tm = min(tm, n)
    tok = lambda bi, i: (bi, i, 0)
    per_b = pl.BlockSpec((None, 1, d), lambda bi, i: (bi, 0, 0))
    return pl.pallas_call(
        functools.partial(_chan_dft_kernel, groups=groups, gd=gd),
        out_shape=(jax.ShapeDtypeStruct((b, n, d), BF16), jax.ShapeDtypeStruct((b, n, d), BF16)),
        grid=(b, n // tm),
        in_specs=[pl.BlockSpec((None, tm, d), tok), per_b, per_b,
                  pl.BlockSpec(tab.shape, lambda bi, i: (0, 0))],
        out_specs=(pl.BlockSpec((None, tm, d), tok), pl.BlockSpec((None, tm, d), tok)),
        compiler_params=_cp(("parallel", "parallel"), VMEM_LIMIT),
        name="chan_dft",
    )(x, shift, scale, tab)


FFT_C = 64


def _fft_pitch(group):
    p = -(-group // SUBLANES)
    return SUBLANES * (p if p % 2 else p + 1)


def _tok_fft_kernel(zr_ref, zi_ref, m1_ref, m3_ref, tc_ref, ts_ref, o_ref, zsr, zsi, asr, asi, ob, *, nr):
    c_len = FFT_C
    pz = zsr.shape[0] // nr
    pa = asr.shape[0] // c_len
    for r in range(nr):
        zsr[pz * r:pz * r + c_len, :] = zr_ref[c_len * r:c_len * (r + 1), :].astype(F32)
        zsi[pz * r:pz * r + c_len, :] = zi_ref[c_len * r:c_len * (r + 1), :].astype(F32)
    m1 = m1_ref[...]
    for c in range(c_len):
        x2 = jnp.concatenate([zsr[pl.ds(c, nr, stride=pz), :], zsi[pl.ds(c, nr, stride=pz), :]], axis=0)
        a2 = jnp.dot(m1, x2.astype(BF16), preferred_element_type=F32)
        ar, ai = a2[0:nr], a2[nr:2 * nr]
        tcv = tc_ref[c * nr:(c + 1) * nr, :]
        tsv = ts_ref[c * nr:(c + 1) * nr, :]
        asr[pa * c:pa * c + nr, :] = ar * tcv + ai * tsv
        asi[pa * c:pa * c + nr, :] = ai * tcv - ar * tsv
    m3 = m3_ref[...]
    for k1 in range(nr):
        y2 = jnp.concatenate([asr[pl.ds(k1, c_len, stride=pa), :], asi[pl.ds(k1, c_len, stride=pa), :]], axis=0)
        ob[pl.ds(k1, c_len, stride=pa), :] = jnp.dot(m3, y2.astype(BF16), preferred_element_type=F32)
    for k2 in range(c_len):
        o_ref[nr * k2:nr * (k2 + 1), :] = ob[pa * k2:pa * k2 + nr, :].astype(BF16)


def _tok_fft(zr, zi):
    b, n, d = zr.shape
    nr = n // FFT_C
    pz = _fft_pitch(FFT_C)
    pa = _fft_pitch(nr)
    kr = np.arange(nr, dtype=np.float64)
    ang_r = 2.0 * np.pi * (np.outer(kr, kr) % nr) / nr
    cr, sr = np.cos(ang_r), np.sin(ang_r)
    m1 = jnp.asarray(np.block([[cr, -sr], [-sr, -cr]]), F32).astype(BF16)
    kc = np.arange(FFT_C, dtype=np.float64)
    ang_c = 2.0 * np.pi * (np.outer(kc, kc) % FFT_C) / FFT_C
    m3 = jnp.asarray(np.concatenate([np.cos(ang_c), np.sin(ang_c)], axis=1), F32).astype(BF16)
    ang_t = 2.0 * np.pi * (np.outer(kc, kr) % n) / n
    tc = jnp.broadcast_to(jnp.asarray(np.cos(ang_t).reshape(FFT_C * nr, 1), F32), (FFT_C * nr, LANES))
    ts = jnp.broadcast_to(jnp.asarray(np.sin(ang_t).reshape(FFT_C * nr, 1), F32), (FFT_C * nr, LANES))
    slab = pl.BlockSpec((None, n, LANES), lambda bi, l: (bi, 0, l))
    const = lambda a: pl.BlockSpec(a.shape, lambda bi, l: (0, 0))
    return pl.pallas_call(
        functools.partial(_tok_fft_kernel, nr=nr),
        out_shape=jax.ShapeDtypeStruct((b, n, d), BF16),
        grid=(b, d // LANES),
        in_specs=[slab, slab, const(m1), const(m3), const(tc), const(ts)],
        out_specs=slab,
        scratch_shapes=[pltpu.VMEM((nr * pz, LANES), F32), pltpu.VMEM((nr * pz, LANES), F32),
                        pltpu.VMEM((FFT_C * pa, LANES), F32), pltpu.VMEM((FFT_C * pa, LANES), F32),
                        pltpu.VMEM((FFT_C * pa, LANES), F32)],
        compiler_params=_cp(("parallel", "parallel"), VMEM_LIMIT),
        name="tok_fft",
    )(zr, zi, m1, m3, tc, ts)


def _chan_dft_table(gd):
    c = np.arange(gd, dtype=np.float64)
    ang_c = 2.0 * np.pi * (np.outer(c, c) % gd) / gd
    return jnp.asarray(np.concatenate([np.cos(ang_c), np.sin(ang_c)], axis=1), F32).astype(BF16)


def _outproj_odd_kernel(wv_ref, x_ref, w_ref, b_ref, g1_ref, lng_ref, lnb_ref, sh2_ref, sc2_ref,
                        x1_ref, t_ref, *, alpha, norm):
    y = jnp.dot(wv_ref[...], w_ref[...], preferred_element_type=F32) * norm + b_ref[...]
    x1, t = _residual_ln_mod(x_ref[...], y, g1_ref[...], lng_ref[...], lnb_ref[...],
                             sh2_ref[...], sc2_ref[...], alpha)
    x1_ref[...] = x1
    t_ref[...] = t.astype(BF16)


def _outproj_odd(wv, x, w_b, bias, g1, lng, lnb, sh2, sc2, *, alpha, norm, tm):
    b, n, d = x.shape
    tm = min(tm, n)
    tok = lambda bi, i: (bi, i, 0)
    per_b = pl.BlockSpec((None, 1, d), lambda bi, i: (bi, 0, 0))
    row = pl.BlockSpec((1, d), lambda bi, i: (0, 0))
    return pl.pallas_call(
        functools.partial(_outproj_odd_kernel, alpha=alpha, norm=norm),
        out_shape=(jax.ShapeDtypeStruct((b, n, d), F32), jax.ShapeDtypeStruct((b, n, d), BF16)),
        grid=(b, n // tm),
        in_specs=[pl.BlockSpec((None, tm, d), tok), pl.BlockSpec((None, tm, d), tok),
                  pl.BlockSpec(w_b.shape, lambda bi, i: (0, 0)), row,
                  per_b, row, row, per_b, per_b],
        out_specs=(pl.BlockSpec((None, tm, d), tok), pl.BlockSpec((None, tm, d), tok)),
        compiler_params=_cp(("parallel", "parallel"), VMEM_LIMIT),
        name="outproj_odd",
    )(wv, x, w_b, bias.reshape(1, d), g1, lng.reshape(1, d), lnb.reshape(1, d), sh2, sc2)


def _route_kernel(t_ref, w_ref, b_ref, up_ref, lp_ref, tokm_ref, cnt_ref, *, ng, epg, chunk):
    tm = t_ref.shape[0]
    ne = ng * epg
    logits = jnp.dot(t_ref[...].astype(BF16), w_ref[...], preferred_element_type=F32) + b_ref[...]
    lt = logits.T
    best = lt[0:1, :]
    bi = jnp.zeros((1, tm), I32)
    for k in range(1, ng):
        gk = lt[k:k + 1, :]
        upd = gk > best
        bi = jnp.where(upd, k, bi)
        best = jnp.where(upd, gk, best)
    den = jnp.zeros((1, tm), F32)
    for k in range(ng):
        den = den + jnp.exp(lt[k:k + 1, :] - best)
    p_g = 1.0 / den
    fsel = lt[SUBLANES:SUBLANES + epg, :]
    for k in range(1, ng):
        fsel = jnp.where(bi == k, lt[SUBLANES + k * epg:SUBLANES + (k + 1) * epg, :], fsel)
    neg = jnp.full((1, tm), -jnp.inf, F32)
    m1, m2 = neg, neg
    i1 = jnp.zeros((1, tm), I32)
    i2 = jnp.zeros((1, tm), I32)
    for j in range(epg):
        v = fsel[j:j + 1, :]
        gt1 = v > m1
        gt2 = v > m2
        m2 = jnp.where(gt1, m1, jnp.where(gt2, v, m2))
        i2 = jnp.where(gt1, i1, jnp.where(gt2, j, i2))
        m1 = jnp.where(gt1, v, m1)
        i1 = jnp.where(gt1, j, i1)
    e21 = jnp.exp(m2 - m1)
    w1 = p_g / (1.0 + e21)
    w2 = p_g * e21 / (1.0 + e21)
    e1 = bi * epg + i1
    e2 = bi * epg + i2

    e = jnp.concatenate([e1, e2], axis=1)
    rows = lax.broadcasted_iota(I32, (ne, 2 * tm), 0)
    onehot = jnp.where(rows == e, 1.0, 0.0)
    before = jnp.dot(onehot.astype(BF16), up_ref[...], preferred_element_type=F32)
    tot = jnp.sum(onehot, axis=1, keepdims=True)
    slots = jnp.floor((tot + (chunk - 1.0)) * (1.0 / chunk)) * chunk
    slots_b = jnp.broadcast_to(slots, (ne, LANES))
    rowe = lax.broadcasted_iota(I32, (ne, LANES), 0)
    incl = slots_b
    d = 1
    while d < ne:
        incl = incl + jnp.where(rowe >= d, pltpu.roll(incl, d, 0), 0.0)
        d *= 2
    seg_off = jnp.tile(incl - slots_b, (1, 2 * tm // LANES))
    lpos = jnp.sum(onehot * (before + seg_off), axis=0, keepdims=True)
    lp0 = lpos[:, 0:tm]
    lp1 = lpos[:, tm:2 * tm]
    row8 = lax.broadcasted_iota(I32, (SUBLANES, tm), 0)
    lp_ref[...] = jnp.where(row8 == 0, lp0, jnp.where(row8 == 1, lp1, 0.0)).astype(I32)
    rowl = lax.broadcasted_iota(I32, (LANES, tm), 0)
    tokm = jnp.where(rowl == 0, w1, jnp.where(rowl == 1, w2, jnp.where(rowl == 2, lp0, jnp.where(rowl == 3, lp1, 0.0))))
    tokm_ref[...] = tokm.T
    cnt_ref[...] = jnp.broadcast_to(tot, (ne, LANES))


def _route(t, wcat_b, bcat, upper, *, ng, epg, tm, chunk):
    tt, d = t.shape
    ne = ng * epg
    nt = tt // tm
    return pl.pallas_call(
        functools.partial(_route_kernel, ng=ng, epg=epg, chunk=chunk),
        out_shape=(jax.ShapeDtypeStruct((SUBLANES, tt), I32), jax.ShapeDtypeStruct((tt, LANES), F32),
                   jax.ShapeDtypeStruct((ne, nt * LANES), F32)),
        grid=(nt,),
        in_specs=[pl.BlockSpec((tm, d), lambda i: (i, 0)),
                  pl.BlockSpec((d, LANES), lambda i: (0, 0)),
                  pl.BlockSpec((1, LANES), lambda i: (0, 0)),
                  pl.BlockSpec(upper.shape, lambda i: (0, 0))],
        out_specs=(pl.BlockSpec((SUBLANES, tm), lambda i: (0, i)),
                   pl.BlockSpec((tm, LANES), lambda i: (i, 0)),
                   pl.BlockSpec((ne, LANES), lambda i: (0, i))),
        compiler_params=_cp(("parallel",), VMEM_LIMIT),
        name="route_sort",
    )(t, wcat_b, bcat, upper)


def _dispatch_kernel(row_ref, tot_ref, zrow_ref, nz_ref, t_ref, lp_ref, xs_ref, stage, zbuf, sems,
                     *, qmax, zmax, chunk, nsteps):
    i = pl.program_id(0)
    slot = i % 2
    sp = stage.shape[1] // PACK_SUB
    tm = t_ref.shape[0]
    crow = chunk * PACK_SUB

    @pl.when(i == 0)
    def _():
        zbuf[...] = jnp.zeros_like(zbuf)
        _for_each_chunk(nz_ref[0], lambda q: pltpu.make_async_copy(
            zbuf, xs_ref.at[pl.ds(_tile_row(zrow_ref[q], SUBLANES), crow)], sems.at[1]).start())
        _wait_chunks(lambda k: pltpu.make_async_copy(xs_ref.at[pl.ds(0, k * crow)], xs_ref.at[pl.ds(0, k * crow)],
                                                     sems.at[1]), nz_ref[0], zmax)

    lp = lp_ref[...]
    prow = lax.broadcasted_iota(I32, (sp, tm), 0)
    perm = jnp.where(prow == lp[0:1, :], 1.0, jnp.where(prow == lp[1:2, :], 1.0, 0.0)).astype(BF16)
    srt = jnp.dot(perm, t_ref[...], preferred_element_type=F32)
    _rows_to_tiles(stage.at[slot], _pack_rows(srt))

    def chunk_copy(sl, src, dst):
        return pltpu.make_async_copy(stage.at[sl, pl.ds(_tile_row(src, crow), crow)],
                                     xs_ref.at[pl.ds(_tile_row(dst, SUBLANES), crow)], sems.at[sl])

    def drain(sl, n):
        def copy_of(k):
            return pltpu.make_async_copy(stage.at[sl, pl.ds(0, k * crow)], xs_ref.at[pl.ds(0, k * crow)], sems.at[sl])
        _wait_chunks(copy_of, n, qmax)

    @pl.when(i >= 1)
    def _():
        drain(1 - slot, tot_ref[jnp.maximum(i - 1, 0)])

    _for_each_chunk(tot_ref[i], lambda q: chunk_copy(slot, pl.multiple_of(q * chunk, chunk),
                                                     row_ref[i * qmax + q]).start())

    @pl.when(i == nsteps - 1)
    def _():
        drain(slot, tot_ref[i])


def _dispatch(chunk_row, tile_nch, zero_row, n_zero, t, lp, *, n_rows, ne, tm, chunk):
    ttot, d = t.shape
    nsteps = ttot // tm
    sp = 2 * tm + ne * chunk
    grid_spec = pltpu.PrefetchScalarGridSpec(
        num_scalar_prefetch=4,
        grid=(nsteps,),
        in_specs=[pl.BlockSpec((tm, d), lambda i, *_: (i, 0)),
                  pl.BlockSpec((SUBLANES, tm), lambda i, *_: (0, i))],
        out_specs=pl.BlockSpec(memory_space=pl.ANY),
        scratch_shapes=[pltpu.VMEM((2, sp * PACK_SUB, LANES), jnp.uint32),
                        pltpu.VMEM((chunk * PACK_SUB, LANES), jnp.uint32), pltpu.SemaphoreType.DMA((2,))],
    )
    assert d == 2 * PACK_SUB * LANES
    return pl.pallas_call(
        functools.partial(_dispatch_kernel, qmax=chunk_row.shape[0] // nsteps, zmax=zero_row.shape[0],
                          chunk=chunk, nsteps=nsteps),
        out_shape=jax.ShapeDtypeStruct((n_rows * PACK_SUB, LANES), jnp.uint32),
        grid_spec=grid_spec,
        compiler_params=_cp(("arbitrary",), VMEM_LIMIT),
        name="moe_dispatch",
    )(chunk_row, tile_nch, zero_row, n_zero, t, lp)


def _moe_kernel(be_ref, nu_ref, xs_ref, w1_ref, w3_ref, w2_ref, y_ref, w13b, w2b, *, ff):
    i = pl.program_id(0)

    @pl.when(i < nu_ref[0])
    def _():
        prev = be_ref[jnp.maximum(i - 1, 0)]

        @pl.when((i == 0) | (be_ref[i] != prev))
        def _():
            w13b[:, 0:ff] = w1_ref[...].astype(BF16)
            w13b[:, ff:2 * ff] = w3_ref[...].astype(BF16)
            w2b[...] = w2_ref[...].astype(BF16)

        x = _unpack_rows(_rows_from_tiles(xs_ref, MOE_ROWS))
        h = jnp.dot(x, w13b[...], preferred_element_type=F32)
        hid = (_silu(h[:, 0:ff]) * h[:, ff:2 * ff]).astype(BF16)
        y = jnp.dot(hid, w2b[...], preferred_element_type=F32)
        _rows_to_tiles(y_ref, _pack_rows(y.astype(BF16).astype(F32)))

    @pl.when(i >= nu_ref[0])
    def _():
        y_ref[...] = jnp.zeros_like(y_ref)


def _moe_experts(block_e, n_used, xs, w1, w3, w2, *, layer):
    d = w1.shape[-2]
    ff = w1.shape[-1]
    blk = MOE_ROWS * PACK_SUB
    nb = xs.shape[0] // blk
    wmap = lambda i, be, nu: (layer, be[i], 0, 0)
    grid_spec = pltpu.PrefetchScalarGridSpec(
        num_scalar_prefetch=2,
        grid=(nb,),
        in_specs=[pl.BlockSpec((blk, LANES), lambda i, be, nu: (i, 0)),
                  pl.BlockSpec((None, None, d, ff), wmap),
                  pl.BlockSpec((None, None, d, ff), wmap),
                  pl.BlockSpec((None, None, ff, d), wmap)],
        out_specs=pl.BlockSpec((blk, LANES), lambda i, be, nu: (i, 0)),
        scratch_shapes=[pltpu.VMEM((d, 2 * ff), BF16), pltpu.VMEM((ff, d), BF16)],
    )
    return pl.pallas_call(
        functools.partial(_moe_kernel, ff=ff),
        out_shape=jax.ShapeDtypeStruct(xs.shape, xs.dtype),
        grid_spec=grid_spec,
        compiler_params=_cp(("arbitrary",), VMEM_LIMIT),
        name="moe_experts",
    )(block_e, n_used, xs, w1, w3, w2)


def _combine_kernel(row_ref, tot_ref, yb_ref, x_ref, tokm_ref, g2_ref, lng_ref, lnb_ref,
                    o_ref, stage, sems, *, qmax, chunk, nsteps, alpha):
    i = pl.program_id(0)
    slot = i % 2
    sp = stage.shape[1] // PACK_SUB
    tm = x_ref.shape[0]
    crow = chunk * PACK_SUB

    def chunk_copy(sl, src, dst):
        return pltpu.make_async_copy(yb_ref.at[pl.ds(_tile_row(src, SUBLANES), crow)],
                                     stage.at[sl, pl.ds(_tile_row(dst, crow), crow)], sems.at[sl])

    def issue_tile(step, sl):
        _for_each_chunk(tot_ref[step], lambda q: chunk_copy(sl, row_ref[step * qmax + q],
                                                            pl.multiple_of(q * chunk, chunk)).start())

    @pl.when(i == 0)
    def _():
        stage[...] = jnp.zeros_like(stage)
        issue_tile(0, 0)

    @pl.when(i + 1 < nsteps)
    def _():
        issue_tile(jnp.minimum(i + 1, nsteps - 1), 1 - slot)

    def copy_of(k):
        return pltpu.make_async_copy(yb_ref.at[pl.ds(0, k * crow)], stage.at[slot, pl.ds(0, k * crow)], sems.at[slot])
    _wait_chunks(copy_of, tot_ref[i], qmax)

    tk = tokm_ref[...]
    pos = lax.broadcasted_iota(I32, (tm, sp), 1).astype(F32)
    st = _unpack_rows(_rows_from_tiles(stage.at[slot], sp))
    gsel = jnp.where(pos == tk[:, 2:3], tk[:, 0:1], jnp.where(pos == tk[:, 3:4], tk[:, 1:2], 0.0))
    m = jnp.dot(gsel.astype(BF16), st, preferred_element_type=F32)
    o_ref[...] = _ln(alpha * x_ref[...] + g2_ref[...] * m) * lng_ref[...] + lnb_ref[...]


def _combine(chunk_row, tile_nch, yb, x, tokm, g2, lng, lnb, *, ne, tm, chunk, n_per_batch, alpha):
    ttot, d = x.shape
    nsteps = ttot // tm
    per = n_per_batch // tm
    sp = 2 * tm + ne * chunk
    grid_spec = pltpu.PrefetchScalarGridSpec(
        num_scalar_prefetch=2,
        grid=(nsteps,),
        in_specs=[pl.BlockSpec(memory_space=pl.ANY),
                  pl.BlockSpec((tm, d), lambda i, *_: (i, 0)),
                  pl.BlockSpec((tm, LANES), lambda i, *_: (i, 0)),
                  pl.BlockSpec((None, 1, d), lambda i, *_: (i // per, 0, 0)),
                  pl.BlockSpec((1, d), lambda i, *_: (0, 0)),
                  pl.BlockSpec((1, d), lambda i, *_: (0, 0))],
        out_specs=pl.BlockSpec((tm, d), lambda i, *_: (i, 0)),
        scratch_shapes=[pltpu.VMEM((2, sp * PACK_SUB, LANES), yb.dtype), pltpu.SemaphoreType.DMA((2,))],
    )
    return pl.pallas_call(
        functools.partial(_combine_kernel, qmax=chunk_row.shape[0] // nsteps, chunk=chunk, nsteps=nsteps,
                          alpha=alpha),
        out_shape=jax.ShapeDtypeStruct((ttot, d), F32),
        grid_spec=grid_spec,
        compiler_params=_cp(("arbitrary",), VMEM_LIMIT),
        name="moe_combine",
    )(chunk_row, tile_nch, yb, x, tokm, g2, lng.reshape(1, d), lnb.reshape(1, d))


def _hier_moe_layer(x1, t, g2, lng, lnb, wg, bg, wf, bf, w1, w3, w2, *, layer, alpha):
    b, n, d = x1.shape
    ttot = b * n
    ng = wg.shape[1]
    ne = wf.shape[1]
    epg = ne // ng
    tm = min(MOE_TILE, n)
    chunk = MOE_CHUNK
    nt = ttot // tm
    tflat = t.reshape(ttot, d)
    wcat = jnp.zeros((d, LANES), F32).at[:, 0:ng].set(wg).at[:, SUBLANES:SUBLANES + ne].set(wf).astype(BF16)
    bcat = jnp.zeros((1, LANES), F32).at[0, 0:ng].set(bg).at[0, SUBLANES:SUBLANES + ne].set(bf)
    ar = jnp.arange(2 * tm, dtype=I32)
    upper = (ar[:, None] < ar[None, :]).astype(BF16)
    lp, tokm, cnt = _route(tflat, wcat, bcat, upper, ng=ng, epg=epg, tm=tm, chunk=chunk)

    tile_cnt = cnt.reshape(ne, nt, LANES)[:, :, 0].T.astype(I32)
    seg_len = ((tile_cnt + 1) // 2) * 2
    counts = jnp.sum(seg_len, axis=0)
    padded = ((counts + chunk + MOE_ROWS - 1) // MOE_ROWS) * MOE_ROWS
    pend = jnp.cumsum(padded)
    pstart = pend - padded
    base = jnp.cumsum(seg_len, axis=0) - seg_len
    seg_row = pstart[None, :] + base
    nch = (tile_cnt + chunk - 1) // chunk
    nch_end = jnp.cumsum(nch, axis=1)
    tile_nch = nch_end[:, -1].astype(I32)
    qmax = 2 * tm // chunk + ne
    qs = jnp.arange(qmax, dtype=I32)
    e_q = jnp.minimum(jnp.sum((nch_end[:, None, :] <= qs[None, :, None]).astype(I32), axis=2), ne - 1)
    pick = e_q[:, :, None] == jnp.arange(ne, dtype=I32)[None, None, :]
    first_q = jnp.sum(jnp.where(pick, (nch_end - nch)[:, None, :], 0), axis=2)
    first_row = jnp.sum(jnp.where(pick, seg_row[:, None, :], 0), axis=2)
    chunk_row = (first_row + (qs[None, :] - first_q) * chunk).astype(I32).reshape(nt * qmax)
    nb = -(-(2 * ttot + nt * ne + ne * (chunk + MOE_ROWS)) // MOE_ROWS)
    bstart = jnp.arange(nb, dtype=I32) * MOE_ROWS
    block_e = jnp.minimum(jnp.sum((pend[None, :] <= bstart[:, None]).astype(I32), axis=1), ne - 1)
    n_used = (pend[-1] // MOE_ROWS).astype(I32).reshape(1)
    zstart = jnp.concatenate([pstart + (counts // chunk) * chunk, pend[-1:]])
    zend = jnp.concatenate([pend, jnp.full((1,), nb * MOE_ROWS, pend.dtype)])
    zcnt = (zend - zstart) // chunk
    zcum = jnp.cumsum(zcnt)
    zmax = ne * ((MOE_ROWS + 2 * chunk) // chunk + 1) + (nb - (2 * ttot) // MOE_ROWS) * (MOE_ROWS // chunk)
    zq = jnp.arange(zmax, dtype=I32)
    zseg = jnp.minimum(jnp.sum((zcum[None, :] <= zq[:, None]).astype(I32), axis=1), ne)
    zpick = zseg[:, None] == jnp.arange(ne + 1, dtype=I32)[None, :]
    zfirst = jnp.sum(jnp.where(zpick, (zcum - zcnt)[None, :], 0), axis=1)
    zero_row = (jnp.sum(jnp.where(zpick, zstart[None, :], 0), axis=1) + (zq - zfirst) * chunk).astype(I32)
    n_zero = zcum[-1].astype(I32).reshape(1)

    xs = _dispatch(chunk_row, tile_nch, zero_row, n_zero, tflat, lp, n_rows=nb * MOE_ROWS, ne=ne, tm=tm,
                   chunk=chunk)
    yb = _moe_experts(block_e.astype(I32), n_used, xs, w1, w3, w2, layer=layer)
    out = _combine(chunk_row, tile_nch, yb, x1.reshape(ttot, d), tokm, g2, lng, lnb,
                   ne=ne, tm=tm, chunk=chunk, n_per_batch=n, alpha=alpha)
    return out.reshape(b, n, d)


def kernel(x, c, ctx, c_ctx, ada_w, ada_b, ln_g, ln_b, ev_w_in, ev_conv_w, ev_conv_b, ev_gate_a_w,
           ev_gate_a_b, ev_gate_x_w, ev_gate_x_b, ev_lru_lambda, ev_da_lambda, ev_da_subln, ev_w_out,
           od_w_out, od_b_out, moe_wg, moe_bg, moe_wf, moe_bf, moe_w1, moe_w3, moe_w2):
    bsz, n_lat, d = x.shape
    depth = ada_w.shape[0]
    alpha = (2.0 * depth) ** 0.25
    lw = ev_conv_w.shape[-1]
    hd = ev_da_lambda.shape[-1]
    vd = ev_da_subln.shape[-1]
    aw = (ev_w_in.shape[-1] - 2 * lw) // 3
    heads = aw // vd
    fnet_groups = 4

    rows = ((bsz + 1 + SUBLANES - 1) // SUBLANES) * SUBLANES
    cond = jnp.zeros((rows, d), F32).at[0:bsz].set(c).at[bsz].set(c_ctx)
    ada = _ada_terms(cond, ada_w, ada_b).reshape(depth, rows, 6, d)

    def lat_term(l, k):
        return ada[l, 0:bsz, k, :].reshape(bsz, 1, d)

    def ctx_term(l, k):
        return jnp.broadcast_to(ada[l, bsz, k, :].reshape(1, 1, d), (bsz, 1, d))

    for l in range(depth):
        ctx_live = any(m % 2 == 0 for m in range(l + 1, depth))
        assert not ctx_live, "context stream update is only needed for depth > 2"
        sh1, sc1, g1, sh2, sc2, g2 = [lat_term(l, k) for k in range(6)]
        if l % 2 == 0:
            e = l // 2
            lam_init = 0.8 - 0.6 * math.exp(-0.3 * l)
            w_in_b = ev_w_in[e].astype(BF16)
            cos_t, sin_t = _rope_tables(n_lat, hd)
            qscale = hd ** -0.5 * math.log2(math.e)
            g_l, xr_l, q_l, k_l, v_l = _project_even(x, sh1, sc1, w_in_b, cos_t, sin_t, rope=True,
                                                     lw=lw, aw=aw, qscale=qscale, tm=512)
            n_ctx = ctx.shape[1]
            _, xr_c, _, k_c, v_c = _project_even(ctx, ctx_term(l, 0), ctx_term(l, 1), w_in_b,
                                                 cos_t[0:n_ctx], sin_t[0:n_ctx], rope=False,
                                                 lw=lw, aw=aw, qscale=qscale, tm=256)
            o_l = _diff_attention(q_l, k_c, k_l, v_c, v_l, ev_da_lambda[e], ev_da_subln[e],
                                  heads=heads, hd=hd, lam_init=lam_init)
            gwid = 256
            y = None
            for dirn, rev in ((0, False), (1, True)):
                wa_bd = _block_diag_groups(ev_gate_a_w[e, dirn], gwid)
                wx_bd = _block_diag_groups(ev_gate_x_w[e, dirn], gwid)
                common = (ev_conv_w[e], ev_conv_b[e], wa_bd, ev_gate_a_b[e, dirn], wx_bd,
                          ev_gate_x_b[e, dirn], ev_lru_lambda[e, dirn])
                h_zero = jnp.zeros((bsz, 1, lw), F32)
                _, h_fin = _rglru_dir(xr_c, None, *common, h_zero, reverse=rev, tn=256)
                y, _ = _rglru_dir(xr_l, y, *common, h_fin, reverse=rev, tn=512)
            x1, t = _outproj_even(y, g_l, o_l, x, ev_w_out[e].astype(BF16), g1, ln_g[l, 0], ln_b[l, 0],
                                  sh2, sc2, alpha=alpha, tm=1024)
        else:
            o = l // 2
            gd = d // fnet_groups
            zr, zi = _chan_dft(x, sh1, sc1, _chan_dft_table(gd), groups=fnet_groups, tm=1024)
            wv = _tok_fft(zr, zi)
            norm = 1.0 / math.sqrt(float(n_lat * gd))
            x1, t = _outproj_odd(wv, x, od_w_out[o].astype(BF16), od_b_out[o], g1, ln_g[l, 0], ln_b[l, 0],
                                 sh2, sc2, alpha=alpha, norm=norm, tm=1024)
        x = _hier_moe_layer(x1, t, g2, ln_g[l, 1], ln_b[l, 1], moe_wg[l], moe_bg[l], moe_wf[l], moe_bf[l],
                            moe_w1, moe_w3, moe_w2, layer=l, alpha=alpha)
    return x
```

```python
import functools
import math

import numpy as np
import jax
import jax.numpy as jnp
from jax import lax
from jax.experimental import pallas as pl
from jax.experimental.pallas import tpu as pltpu

F32 = jnp.float32
BF16 = jnp.bfloat16
I32 = jnp.int32

LN_EPS = 1e-6
LRU_C = 8.0
ROPE_BASE = 10000.0
GRID_W = 64
CONV_W = 4
LANES = 128
SUBLANES = 8
MOE_ROWS = 512
MOE_TILE = 256
MOE_CHUNK = 16
VMEM_LIMIT = 56 * 1024 * 1024


def _cp(sem, vmem=None):
    return pltpu.CompilerParams(dimension_semantics=sem, vmem_limit_bytes=vmem)


def _ln(x):
    mu = jnp.mean(x, axis=-1, keepdims=True)
    xc = x - mu
    var = jnp.mean(xc * xc, axis=-1, keepdims=True)
    return xc * lax.rsqrt(var + LN_EPS)


def _silu(x):
    return x * jax.nn.sigmoid(x)


PACK_SUB = 4


def _tile_row(r, mult):
    if isinstance(r, int):
        return r * PACK_SUB
    return pl.multiple_of(r * PACK_SUB, mult)


def _pack_rows(val):
    half = val.shape[1] // 2
    lo = lax.bitcast_convert_type(val[:, 0:half], jnp.uint32) >> 16
    hi = lax.bitcast_convert_type(val[:, half:], jnp.uint32) & jnp.uint32(0xFFFF0000)
    return lo | hi


def _unpack_rows(words):
    lo = lax.bitcast_convert_type(words << 16, F32)
    hi = lax.bitcast_convert_type(words & jnp.uint32(0xFFFF0000), F32)
    return jnp.concatenate([lo, hi], axis=1).astype(BF16)


def _wait_chunks(copy_of, n, nmax):
    b = 1
    while b <= nmax:
        @pl.when((n & b) != 0)
        def _(b=b):
            copy_of(b).wait()
        b *= 2


ISSUE_UNROLL = 4


def _for_each_chunk(n, start_one):
    groups = n // ISSUE_UNROLL

    def group(g, carry):
        for u in range(ISSUE_UNROLL):
            start_one(g * ISSUE_UNROLL + u)
        return carry
    lax.fori_loop(0, groups, group, 0)

    def single(q, carry):
        start_one(q)
        return carry
    lax.fori_loop(groups * ISSUE_UNROLL, n, single, 0)


def _rows_from_tiles(ref, nrows):
    return jnp.concatenate([ref[pl.ds(k, nrows, stride=PACK_SUB), :] for k in range(PACK_SUB)], axis=1)


def _rows_to_tiles(ref, val):
    nrows = val.shape[0]
    for k in range(PACK_SUB):
        ref[pl.ds(k, nrows, stride=PACK_SUB), :] = val[:, k * LANES:(k + 1) * LANES]


def _ada_kernel(c_ref, w_ref, b_ref, o_ref):
    s = _silu(c_ref[...]).astype(BF16)
    o_ref[...] = jnp.dot(s, w_ref[...].astype(BF16), preferred_element_type=F32) + b_ref[...]


def _ada_terms(cond, ada_w, ada_b):
    nl, d, d6 = ada_w.shape
    r = cond.shape[0]
    tn = 1024
    return pl.pallas_call(
        _ada_kernel,
        out_shape=jax.ShapeDtypeStruct((nl, r, d6), F32),
        grid=(nl, d6 // tn),
        in_specs=[pl.BlockSpec((r, d), lambda l, j: (0, 0)),
                  pl.BlockSpec((None, d, tn), lambda l, j: (l, 0, j)),
                  pl.BlockSpec((None, 1, tn), lambda l, j: (l, 0, j))],
        out_specs=pl.BlockSpec((None, r, tn), lambda l, j: (l, 0, j)),
        compiler_params=_cp(("parallel", "parallel")),
        name="ada_terms",
    )(cond, ada_w, ada_b.reshape(nl, 1, d6))


def _rope_apply(x, cos, sin_signed):
    tm = x.shape[0]
    lane = lax.broadcasted_iota(I32, (tm, LANES), 1)
    first_half = (lane % 32) < 16
    outs = []
    for j in range(x.shape[1] // LANES):
        xh = x[:, j * LANES:(j + 1) * LANES]
        partner = jnp.where(first_half, pltpu.roll(xh, LANES - 16, 1), pltpu.roll(xh, 16, 1))
        outs.append(xh * cos + partner * sin_signed)
    return jnp.concatenate(outs, axis=1)


def _proj_kernel(x_ref, sh_ref, sc_ref, w_ref, cos_ref, sin_ref,
                 g_ref, xr_ref, q_ref, k_ref, v_ref, *, rope, lw, aw, qscale):
    h = _ln(x_ref[...]) * (1.0 + sc_ref[...]) + sh_ref[...]
    hb = h.astype(BF16)

    def mm(c0, c1):
        return jnp.dot(hb, w_ref[:, c0:c1], preferred_element_type=F32)

    g_ref[...] = mm(0, lw).astype(BF16)
    xr_ref[...] = mm(lw, 2 * lw)
    q = mm(2 * lw, 2 * lw + aw)
    k = mm(2 * lw + aw, 2 * lw + 2 * aw)
    if rope:
        q = _rope_apply(q, cos_ref[...], sin_ref[...])
        k = _rope_apply(k, cos_ref[...], sin_ref[...])
    q_ref[...] = (q * qscale).astype(BF16)
    k_ref[...] = k.astype(BF16)
    v_ref[...] = mm(2 * lw + 2 * aw, 2 * lw + 3 * aw).astype(BF16)


def _project_even(x, shift, scale, w_in_b, cos_t, sin_t, *, rope, lw, aw, qscale, tm):
    b, n, d = x.shape
    tm = min(tm, n)
    nin = w_in_b.shape[1]
    tok = lambda bi, i: (bi, i, 0)
    per_b = lambda bi, i: (bi, 0, 0)
    outs = (jax.ShapeDtypeStruct((b, n, lw), BF16), jax.ShapeDtypeStruct((b, n, lw), F32),
            jax.ShapeDtypeStruct((b, n, aw), BF16), jax.ShapeDtypeStruct((b, n, aw), BF16),
            jax.ShapeDtypeStruct((b, n, aw), BF16))
    return pl.pallas_call(
        functools.partial(_proj_kernel, rope=rope, lw=lw, aw=aw, qscale=qscale),
        out_shape=outs,
        grid=(b, n // tm),
        in_specs=[pl.BlockSpec((None, tm, d), tok),
                  pl.BlockSpec((None, 1, d), per_b),
                  pl.BlockSpec((None, 1, d), per_b),
                  pl.BlockSpec((d, nin), lambda bi, i: (0, 0)),
                  pl.BlockSpec((tm, LANES), lambda bi, i: (i, 0)),
                  pl.BlockSpec((tm, LANES), lambda bi, i: (i, 0))],
        out_specs=(pl.BlockSpec((None, tm, lw), tok), pl.BlockSpec((None, tm, lw), tok),
                   pl.BlockSpec((None, tm, aw), tok), pl.BlockSpec((None, tm, aw), tok),
                   pl.BlockSpec((None, tm, aw), tok)),
        compiler_params=_cp(("parallel", "parallel"), VMEM_LIMIT),
        name="proj_even_rope" if rope else "proj_even_ctx",
    )(x, shift, scale, w_in_b, cos_t, sin_t)


def _rope_tables(n_tok, head_dim):
    t = jnp.arange(n_tok)
    row = (t // GRID_W).astype(F32)
    col = (t % GRID_W).astype(F32)
    nf = head_dim // 4
    freqs = ROPE_BASE ** (-jnp.arange(nf, dtype=F32) / nf)
    lane = np.arange(LANES)
    within = lane % head_dim
    axis = within // (2 * nf)
    half = (within % (2 * nf)) // nf
    f = within % nf
    pos = jnp.where(jnp.asarray(axis)[None, :] == 0, row[:, None], col[:, None])
    ang = pos * freqs[jnp.asarray(f)][None, :]
    sign = jnp.asarray(np.where(half == 0, -1.0, 1.0), F32)[None, :]
    return jnp.cos(ang).astype(F32), (jnp.sin(ang) * sign).astype(F32)


def _attn_kernel(q_ref, kc_ref, kl_ref, vc_ref, vl_ref, dl_ref, gain_ref, o_ref, kbuf, vbuf, sbuf, ebuf, abuf,
                 cbuf, *, nc, nl, hd, lam_init, rows):
    @pl.when((pl.program_id(0) == 0) & (pl.program_id(1) == 0))
    def _():
        sbuf[...] = jnp.zeros_like(sbuf)
        abuf[...] = jnp.zeros_like(abuf)
        cbuf[...] = jnp.zeros_like(cbuf)

    kbuf[0:nc, :] = kc_ref[...]
    kbuf[nc:nc + nl, :] = kl_ref[...]
    vbuf[0:nc, :] = vc_ref[...]
    vbuf[nc:nc + nl, :] = vl_ref[...]

    lf = dl_ref[...]
    lam = (jnp.exp(jnp.sum(lf[0:1] * lf[1:2], axis=1, keepdims=True))
           - jnp.exp(jnp.sum(lf[2:3] * lf[3:4], axis=1, keepdims=True)) + lam_init)
    gain = gain_ref[...] * (1.0 - lam_init)
    n_sub = nl // rows
    lane = lax.broadcasted_iota(I32, (rows, 2 * hd), 1)
    nt = (((1,), (1,)), ((), ()))

    def stage_a(j, slot):
        r0 = pl.multiple_of(jnp.minimum(j, n_sub - 1) * rows, rows)
        q = q_ref[pl.ds(r0, rows), :]
        zero = jnp.zeros_like(q)
        kk = kbuf[...]
        sbuf[slot, 0] = lax.dot_general(jnp.where(lane < hd, q, zero), kk, nt, preferred_element_type=F32)
        sbuf[slot, 1] = lax.dot_general(jnp.where(lane >= hd, q, zero), kk, nt, preferred_element_type=F32)

    def stage_b(slot):
        cols = [slice(c, c + LANES) for c in range(0, nc + nl, LANES)]
        ls = []
        for k in range(2):
            pm = sbuf[slot, k, :, cols[0]]
            for cs in cols[1:]:
                pm = jnp.maximum(pm, sbuf[slot, k, :, cs])
            m = jnp.max(pm, axis=1, keepdims=True)
            acc = jnp.zeros((rows, LANES), F32)
            for cs in cols:
                e = jnp.exp2(sbuf[slot, k, :, cs] - m)
                acc = acc + e
                ebuf[slot, k, :, cs] = e.astype(BF16)
            ls.append(jnp.sum(acc, axis=1, keepdims=True))
        ratio = (lam * ls[0] / ls[1]).astype(BF16)
        abuf[slot] = ebuf[slot, 0] - ratio * ebuf[slot, 1]
        cbuf[slot] = jnp.broadcast_to(1.0 / ls[0], cbuf.shape[1:])

    def stage_c(j, slot):
        r0 = pl.multiple_of(jnp.maximum(j - 2, 0) * rows, rows)
        o = jnp.dot(abuf[slot], vbuf[...], preferred_element_type=F32) * cbuf[slot]
        o = o * lax.rsqrt(jnp.mean(o * o, axis=1, keepdims=True) + LN_EPS) * gain
        o_ref[pl.ds(r0, rows), :] = o.astype(BF16)

    def body(t, carry):
        j = 2 * t
        stage_a(j, 0)
        stage_b(1)
        stage_c(j, 0)
        stage_a(j + 1, 1)
        stage_b(0)
        stage_c(j + 1, 1)
        return carry

    lax.fori_loop(0, (n_sub + 2) // 2, body, 0)


def _diff_attention(q, k_ctx, k_lat, v_ctx, v_lat, da_lambda, subln, *, heads, hd, lam_init):
    b, n, aw = q.shape
    nc = k_ctx.shape[1]
    vd = aw // heads
    rows = min(128, n)
    assert (n // rows) % 2 == 0
    blk_q = pl.BlockSpec((None, n, vd), lambda bi, h: (bi, 0, h))
    blk_c = pl.BlockSpec((None, nc, vd), lambda bi, h: (bi, 0, h))
    return pl.pallas_call(
        functools.partial(_attn_kernel, nc=nc, nl=n, hd=hd, lam_init=lam_init, rows=rows),
        out_shape=jax.ShapeDtypeStruct((b, n, aw), BF16),
        grid=(b, heads),
        in_specs=[blk_q, blk_c, blk_q, blk_c, blk_q,
                  pl.BlockSpec(da_lambda.shape, lambda bi, h: (0, 0)),
                  pl.BlockSpec((1, vd), lambda bi, h: (0, 0))],
        out_specs=blk_q,
        scratch_shapes=[pltpu.VMEM((nc + n, vd), BF16), pltpu.VMEM((nc + n, vd), BF16),
                        pltpu.VMEM((2, 2, rows, nc + n), F32), pltpu.VMEM((2, 2, rows, nc + n), BF16),
                        pltpu.VMEM((2, rows, nc + n), BF16), pltpu.VMEM((2, rows, vd), F32)],
        compiler_params=_cp(("arbitrary", "arbitrary"), VMEM_LIMIT),
        name="diff_attention",
    )(q, k_ctx, k_lat, v_ctx, v_lat, da_lambda, subln.reshape(1, vd))


def _lru_kernel(*refs, reverse, add_prev, nt, groups):
    if add_prev:
        (xp_ref, xc_ref, xn_ref, yprev_ref, cw_ref, cb_ref, wa_ref, ba_ref, wx_ref, bx_ref,
         lam_ref, h0_ref, y_ref, hf_ref, carry, ext) = refs
    else:
        (xp_ref, xc_ref, xn_ref, cw_ref, cb_ref, wa_ref, ba_ref, wx_ref, bx_ref,
         lam_ref, h0_ref, y_ref, hf_ref, carry, ext) = refs
        yprev_ref = None
    i = pl.program_id(1)
    ti = (nt - 1 - i) if reverse else i
    tn, w = xc_ref.shape

    @pl.when(i == 0)
    def _():
        carry[...] = h0_ref[...]

    ext[0:SUBLANES, :] = jnp.where(ti > 0, xp_ref[...], 0.0)
    ext[SUBLANES:SUBLANES + tn, :] = xc_ref[...]
    ext[SUBLANES + tn:2 * SUBLANES + tn, :] = jnp.where(ti < nt - 1, xn_ref[...], 0.0)
    left = CONV_W // 2
    xc = cb_ref[...]
    for k in range(CONV_W):
        off = SUBLANES - left + k
        xc = xc + ext[off:off + tn, :] * cw_ref[k:k + 1, :]

    xb = xc.astype(BF16)
    gw = w // groups

    def gate(w_ref, b_ref):
        parts = [jnp.dot(xb[:, g * gw:(g + 1) * gw], w_ref[g], preferred_element_type=F32)
                 for g in range(groups)]
        return jax.nn.sigmoid(jnp.concatenate(parts, axis=1) + b_ref[...])

    r = gate(wa_ref, ba_ref)
    ig = gate(wx_ref, bx_ref)
    log_a = (-LRU_C * jax.nn.softplus(-lam_ref[...])) * r
    a = jnp.exp(log_a)
    bcoef = jnp.sqrt(-jnp.tanh(log_a) * (a * a + 1.0)) * ig * xc

    ngroups = tn // SUBLANES
    a = a.reshape(ngroups, SUBLANES, w)
    bcoef = bcoef.reshape(ngroups, SUBLANES, w)
    row = lax.broadcasted_iota(I32, (ngroups, SUBLANES, w), 1)
    d = 1
    while d < SUBLANES:
        shift = (SUBLANES - d) if reverse else d
        a_sh = pltpu.roll(a, shift, 1)
        b_sh = pltpu.roll(bcoef, shift, 1)
        live = (row < SUBLANES - d) if reverse else (row >= d)
        bcoef = jnp.where(live, a * b_sh + bcoef, bcoef)
        a = jnp.where(live, a * a_sh, a)
        d *= 2
    hc = carry[...]
    npairs = ngroups // 2
    for p in (range(npairs - 1, -1, -1) if reverse else range(npairs)):
        hs = {}
        for g in ((2 * p + 1, 2 * p) if reverse else (2 * p, 2 * p + 1)):
            h = a[g] * hc + bcoef[g]
            hc = h[0:1, :] if reverse else h[SUBLANES - 1:SUBLANES, :]
            hs[g] = h
        rs = slice(2 * p * SUBLANES, (2 * p + 2) * SUBLANES)
        h2 = jnp.concatenate([hs[2 * p], hs[2 * p + 1]], axis=0)
        if add_prev:
            h2 = yprev_ref[rs, :].astype(F32) + h2
        y_ref[rs, :] = h2.astype(y_ref.dtype)
    carry[...] = hc
    hf_ref[...] = hc


def _rglru_dir(xr, y_prev, conv_w, conv_b, wa_bd, ba, wx_bd, bx, lam, h0, *, reverse, tn):
    b, n, w = xr.shape
    tn = min(tn, n)
    nt = n // tn
    groups = wa_bd.shape[0]
    nb8 = n // SUBLANES
    per8 = tn // SUBLANES

    def tmap(i):
        return (nt - 1 - i) if reverse else i

    cur = pl.BlockSpec((None, tn, w), lambda bi, i: (bi, tmap(i), 0))
    halo_p = pl.BlockSpec((None, SUBLANES, w), lambda bi, i: (bi, jnp.maximum(tmap(i) * per8 - 1, 0), 0))
    halo_n = pl.BlockSpec((None, SUBLANES, w), lambda bi, i: (bi, jnp.minimum((tmap(i) + 1) * per8, nb8 - 1), 0))
    row_w = pl.BlockSpec((1, w), lambda bi, i: (0, 0))
    per_b = pl.BlockSpec((None, 1, w), lambda bi, i: (bi, 0, 0))
    gate_w = pl.BlockSpec(wa_bd.shape, lambda bi, i: (0, 0, 0))
    add_prev = y_prev is not None
    in_specs = [halo_p, cur, halo_n] + ([cur] if add_prev else []) + [
        pl.BlockSpec((CONV_W, w), lambda bi, i: (0, 0)), row_w, gate_w, row_w, gate_w, row_w, row_w, per_b]
    args = [xr, xr, xr] + ([y_prev] if add_prev else []) + [
        conv_w, conv_b.reshape(1, w), wa_bd, ba.reshape(1, w), wx_bd, bx.reshape(1, w),
        lam.reshape(1, w), h0]
    return pl.pallas_call(
        functools.partial(_lru_kernel, reverse=reverse, add_prev=add_prev, nt=nt, groups=groups),
        out_shape=(jax.ShapeDtypeStruct((b, n, w), BF16), jax.ShapeDtypeStruct((b, 1, w), F32)),
        grid=(b, nt),
        in_specs=in_specs,
        out_specs=(cur, per_b),
        scratch_shapes=[pltpu.VMEM((1, w), F32), pltpu.VMEM((tn + 2 * SUBLANES, w), F32)],
        compiler_params=_cp(("parallel", "arbitrary"), VMEM_LIMIT),
        name="rglru_rev" if reverse else "rglru_fwd",
    )(*args)


def _block_diag_groups(wh, group_width):
    heads, blk, _ = wh.shape
    per = group_width // blk
    groups = heads // per
    whg = wh.reshape(groups, per, blk, blk)
    eye = jnp.eye(per, dtype=wh.dtype)
    bd = jnp.einsum('gpij,pq->gpiqj', whg, eye).reshape(groups, group_width, group_width)
    return bd.astype(BF16)


def _residual_ln_mod(x, y, g1, lng, lnb, sh2, sc2, alpha):
    x1 = _ln(alpha * x + g1 * y) * lng + lnb
    t = _ln(x1) * (1.0 + sc2) + sh2
    return x1, t


def _outproj_even_kernel(r_ref, g_ref, o_ref, x_ref, w_ref, g1_ref, lng_ref, lnb_ref, sh2_ref, sc2_ref,
                         x1_ref, t_ref, *, lw, alpha):
    z = (r_ref[...].astype(F32) * jax.nn.gelu(g_ref[...].astype(F32))).astype(BF16)
    y = (jnp.dot(z, w_ref[0:lw, :], preferred_element_type=F32)
         + jnp.dot(o_ref[...], w_ref[lw:, :], preferred_element_type=F32))
    x1, t = _residual_ln_mod(x_ref[...], y, g1_ref[...], lng_ref[...], lnb_ref[...],
                             sh2_ref[...], sc2_ref[...], alpha)
    x1_ref[...] = x1
    t_ref[...] = t.astype(BF16)


def _outproj_even(r, g, o, x, w_out_b, g1, lng, lnb, sh2, sc2, *, alpha, tm):
    b, n, d = x.shape
    lw = r.shape[2]
    aw = o.shape[2]
    tm = min(tm, n)
    tok = lambda bi, i: (bi, i, 0)
    per_b = pl.BlockSpec((None, 1, d), lambda bi, i: (bi, 0, 0))
    row = pl.BlockSpec((1, d), lambda bi, i: (0, 0))
    return pl.pallas_call(
        functools.partial(_outproj_even_kernel, lw=lw, alpha=alpha),
        out_shape=(jax.ShapeDtypeStruct((b, n, d), F32), jax.ShapeDtypeStruct((b, n, d), BF16)),
        grid=(b, n // tm),
        in_specs=[pl.BlockSpec((None, tm, lw), tok), pl.BlockSpec((None, tm, lw), tok),
                  pl.BlockSpec((None, tm, aw), tok), pl.BlockSpec((None, tm, d), tok),
                  pl.BlockSpec(w_out_b.shape, lambda bi, i: (0, 0)),
                  per_b, row, row, per_b, per_b],
        out_specs=(pl.BlockSpec((None, tm, d), tok), pl.BlockSpec((None, tm, d), tok)),
        compiler_params=_cp(("parallel", "parallel"), VMEM_LIMIT),
        name="outproj_even",
    )(r, g, o, x, w_out_b, g1, lng.reshape(1, d), lnb.reshape(1, d), sh2, sc2)


def _chan_dft_kernel(x_ref, sh_ref, sc_ref, tab_ref, zr_ref, zi_ref, *, groups, gd):
    h = (_ln(x_ref[...]) * (1.0 + sc_ref[...]) + sh_ref[...]).astype(BF16)
    for g in range(groups):
        z = jnp.dot(h[:, g * gd:(g + 1) * gd], tab_ref[...], preferred_element_type=F32)
        zr_ref[:, g * gd:(g + 1) * gd] = z[:, 0:gd].astype(BF16)
        zi_ref[:, g * gd:(g + 1) * gd] = z[:, gd:2 * gd].astype(BF16)


def _chan_dft(x, shift, scale, tab, *, groups, tm):
    b, n, d = x.shape
    gd = d // groups
    tm = min(tm, n)
    tok = lambda bi, i: (bi, i, 0)
    per_b = pl.BlockSpec((None, 1, d), lambda bi, i: (bi, 0, 0))
    return pl.pallas_call(
        functools.partial(_chan_dft_kernel, groups=groups, gd=gd),
        out_shape=(jax.ShapeDtypeStruct((b, n, d), BF16), jax.ShapeDtypeStruct((b, n, d), BF16)),
        grid=(b, n // tm),
        in_specs=[pl.BlockSpec((None, tm, d), tok), per_b, per_b,
                  pl.BlockSpec(tab.shape, lambda bi, i: (0, 0))],
        out_specs=(pl.BlockSpec((None, tm, d), tok), pl.BlockSpec((None, tm, d), tok)),
        compiler_params=_cp(("parallel", "parallel"), VMEM_LIMIT),
        name="chan_dft",
    )(x, shift, scale, tab)


FFT_C = 64


def _fft_pitch(group):
    p = -(-group // SUBLANES)
    return SUBLANES * (p if p % 2 else p + 1)


def _tok_fft_kernel(zr_ref, zi_ref, m1_ref, m3_ref, tc_ref, ts_ref, o_ref, zsr, zsi, asr, asi, ob, *, nr):
    c_len = FFT_C
    pz = zsr.shape[0] // nr
    pa = asr.shape[0] // c_len
    for r in range(nr):
        zsr[pz * r:pz * r + c_len, :] = zr_ref[c_len * r:c_len * (r + 1), :].astype(F32)
        zsi[pz * r:pz * r + c_len, :] = zi_ref[c_len * r:c_len * (r + 1), :].astype(F32)
    m1 = m1_ref[...]
    for c in range(c_len):
        x2 = jnp.concatenate([zsr[pl.ds(c, nr, stride=pz), :], zsi[pl.ds(c, nr, stride=pz), :]], axis=0)
        a2 = jnp.dot(m1, x2.astype(BF16), preferred_element_type=F32)
        ar, ai = a2[0:nr], a2[nr:2 * nr]
        tcv = tc_ref[c * nr:(c + 1) * nr, :]
        tsv = ts_ref[c * nr:(c + 1) * nr, :]
        asr[pa * c:pa * c + nr, :] = ar * tcv + ai * tsv
        asi[pa * c:pa * c + nr, :] = ai * tcv - ar * tsv
    m3 = m3_ref[...]
    for k1 in range(nr):
        y2 = jnp.concatenate([asr[pl.ds(k1, c_len, stride=pa), :], asi[pl.ds(k1, c_len, stride=pa), :]], axis=0)
        ob[pl.ds(k1, c_len, stride=pa), :] = jnp.dot(m3, y2.astype(BF16), preferred_element_type=F32)
    for k2 in range(c_len):
        o_ref[nr * k2:nr * (k2 + 1), :] = ob[pa * k2:pa * k2 + nr, :].astype(BF16)


def _tok_fft(zr, zi):
    b, n, d = zr.shape
    nr = n // FFT_C
    pz = _fft_pitch(FFT_C)
    pa = _fft_pitch(nr)
    kr = np.arange(nr, dtype=np.float64)
    ang_r = 2.0 * np.pi * (np.outer(kr, kr) % nr) / nr
    cr, sr = np.cos(ang_r), np.sin(ang_r)
    m1 = jnp.asarray(np.block([[cr, -sr], [-sr, -cr]]), F32).astype(BF16)
    kc = np.arange(FFT_C, dtype=np.float64)
    ang_c = 2.0 * np.pi * (np.outer(kc, kc) % FFT_C) / FFT_C
    m3 = jnp.asarray(np.concatenate([np.cos(ang_c), np.sin(ang_c)], axis=1), F32).astype(BF16)
    ang_t = 2.0 * np.pi * (np.outer(kc, kr) % n) / n
    tc = jnp.broadcast_to(jnp.asarray(np.cos(ang_t).reshape(FFT_C * nr, 1), F32), (FFT_C * nr, LANES))
    ts = jnp.broadcast_to(jnp.asarray(np.sin(ang_t).reshape(FFT_C * nr, 1), F32), (FFT_C * nr, LANES))
    slab = pl.BlockSpec((None, n, LANES), lambda bi, l: (bi, 0, l))
    const = lambda a: pl.BlockSpec(a.shape, lambda bi, l: (0, 0))
    return pl.pallas_call(
        functools.partial(_tok_fft_kernel, nr=nr),
        out_shape=jax.ShapeDtypeStruct((b, n, d), BF16),
        grid=(b, d // LANES),
        in_specs=[slab, slab, const(m1), const(m3), const(tc), const(ts)],
        out_specs=slab,
        scratch_shapes=[pltpu.VMEM((nr * pz, LANES), F32), pltpu.VMEM((nr * pz, LANES), F32),
                        pltpu.VMEM((FFT_C * pa, LANES), F32), pltpu.VMEM((FFT_C * pa, LANES), F32),
                        pltpu.VMEM((FFT_C * pa, LANES), F32)],
        compiler_params=_cp(("parallel", "parallel"), VMEM_LIMIT),
        name="tok_fft",
    )(zr, zi, m1, m3, tc, ts)


def _chan_dft_table(gd):
    c = np.arange(gd, dtype=np.float64)
    ang_c = 2.0 * np.pi * (np.outer(c, c) % gd) / gd
    return jnp.asarray(np.concatenate([np.cos(ang_c), np.sin(ang_c)], axis=1), F32).astype(BF16)


def _outproj_odd_kernel(wv_ref, x_ref, w_ref, b_ref, g1_ref, lng_ref, lnb_ref, sh2_ref, sc2_ref,
                        x1_ref, t_ref, *, alpha, norm):
    y = jnp.dot(wv_ref[...], w_ref[...], preferred_element_type=F32) * norm + b_ref[...]
    x1, t = _residual_ln_mod(x_ref[...], y, g1_ref[...], lng_ref[...], lnb_ref[...],
                             sh2_ref[...], sc2_ref[...], alpha)
    x1_ref[...] = x1
    t_ref[...] = t.astype(BF16)


def _outproj_odd(wv, x, w_b, bias, g1, lng, lnb, sh2, sc2, *, alpha, norm, tm):
    b, n, d = x.shape
    tm = min(tm, n)
    tok = lambda bi, i: (bi, i, 0)
    per_b = pl.BlockSpec((None, 1, d), lambda bi, i: (bi, 0, 0))
    row = pl.BlockSpec((1, d), lambda bi, i: (0, 0))
    return pl.pallas_call(
        functools.partial(_outproj_odd_kernel, alpha=alpha, norm=norm),
        out_shape=(jax.ShapeDtypeStruct((b, n, d), F32), jax.ShapeDtypeStruct((b, n, d), BF16)),
        grid=(b, n // tm),
        in_specs=[pl.BlockSpec((None, tm, d), tok), pl.BlockSpec((None, tm, d), tok),
                  pl.BlockSpec(w_b.shape, lambda bi, i: (0, 0)), row,
                  per_b, row, row, per_b, per_b],
        out_specs=(pl.BlockSpec((None, tm, d), tok), pl.BlockSpec((None, tm, d), tok)),
        compiler_params=_cp(("parallel", "parallel"), VMEM_LIMIT),
        name="outproj_odd",
    )(wv, x, w_b, bias.reshape(1, d), g1, lng.reshape(1, d), lnb.reshape(1, d), sh2, sc2)


def _route_kernel(t_ref, w_ref, b_ref, up_ref, lp_ref, tokm_ref, cnt_ref, *, ng, epg, chunk):
    tm = t_ref.shape[0]
    ne = ng * epg
    logits = jnp.dot(t_ref[...].astype(BF16), w_ref[...], preferred_element_type=F32) + b_ref[...]
    lt = logits.T
    best = lt[0:1, :]
    bi = jnp.zeros((1, tm), I32)
    for k in range(1, ng):
        gk = lt[k:k + 1, :]
        upd = gk > best
        bi = jnp.where(upd, k, bi)
        best = jnp.where(upd, gk, best)
    den = jnp.zeros((1, tm), F32)
    for k in range(ng):
        den = den + jnp.exp(lt[k:k + 1, :] - best)
    p_g = 1.0 / den
    fsel = lt[SUBLANES:SUBLANES + epg, :]
    for k in range(1, ng):
        fsel = jnp.where(bi == k, lt[SUBLANES + k * epg:SUBLANES + (k + 1) * epg, :], fsel)
    neg = jnp.full((1, tm), -jnp.inf, F32)
    m1, m2 = neg, neg
    i1 = jnp.zeros((1, tm), I32)
    i2 = jnp.zeros((1, tm), I32)
    for j in range(epg):
        v = fsel[j:j + 1, :]
        gt1 = v > m1
        gt2 = v > m2
        m2 = jnp.where(gt1, m1, jnp.where(gt2, v, m2))
        i2 = jnp.where(gt1, i1, jnp.where(gt2, j, i2))
        m1 = jnp.where(gt1, v, m1)
        i1 = jnp.where(gt1, j, i1)
    e21 = jnp.exp(m2 - m1)
    w1 = p_g / (1.0 + e21)
    w2 = p_g * e21 / (1.0 + e21)
    e1 = bi * epg + i1
    e2 = bi * epg + i2

    e = jnp.concatenate([e1, e2], axis=1)
    rows = lax.broadcasted_iota(I32, (ne, 2 * tm), 0)
    onehot = jnp.where(rows == e, 1.0, 0.0)
    before = jnp.dot(onehot.astype(BF16), up_ref[...], preferred_element_type=F32)
    tot = jnp.sum(onehot, axis=1, keepdims=True)
    slots = jnp.floor((tot + (chunk - 1.0)) * (1.0 / chunk)) * chunk
    slots_b = jnp.broadcast_to(slots, (ne, LANES))
    rowe = lax.broadcasted_iota(I32, (ne, LANES), 0)
    incl = slots_b
    d = 1
    while d < ne:
        incl = incl + jnp.where(rowe >= d, pltpu.roll(incl, d, 0), 0.0)
        d *= 2
    seg_off = jnp.tile(incl - slots_b, (1, 2 * tm // LANES))
    lpos = jnp.sum(onehot * (before + seg_off), axis=0, keepdims=True)
    lp0 = lpos[:, 0:tm]
    lp1 = lpos[:, tm:2 * tm]
    row8 = lax.broadcasted_iota(I32, (SUBLANES, tm), 0)
    lp_ref[...] = jnp.where(row8 == 0, lp0, jnp.where(row8 == 1, lp1, 0.0)).astype(I32)
    rowl = lax.broadcasted_iota(I32, (LANES, tm), 0)
    tokm = jnp.where(rowl == 0, w1, jnp.where(rowl == 1, w2, jnp.where(rowl == 2, lp0, jnp.where(rowl == 3, lp1, 0.0))))
    tokm_ref[...] = tokm.T
    cnt_ref[...] = jnp.broadcast_to(tot, (ne, LANES))


def _route(t, wcat_b, bcat, upper, *, ng, epg, tm, chunk):
    tt, d = t.shape
    ne = ng * epg
    nt = tt // tm
    return pl.pallas_call(
        functools.partial(_route_kernel, ng=ng, epg=epg, chunk=chunk),
        out_shape=(jax.ShapeDtypeStruct((SUBLANES, tt), I32), jax.ShapeDtypeStruct((tt, LANES), F32),
                   jax.ShapeDtypeStruct((ne, nt * LANES), F32)),
        grid=(nt,),
        in_specs=[pl.BlockSpec((tm, d), lambda i: (i, 0)),
                  pl.BlockSpec((d, LANES), lambda i: (0, 0)),
                  pl.BlockSpec((1, LANES), lambda i: (0, 0)),
                  pl.BlockSpec(upper.shape, lambda i: (0, 0))],
        out_specs=(pl.BlockSpec((SUBLANES, tm), lambda i: (0, i)),
                   pl.BlockSpec((tm, LANES), lambda i: (i, 0)),
                   pl.BlockSpec((ne, LANES), lambda i: (0, i))),
        compiler_params=_cp(("parallel",), VMEM_LIMIT),
        name="route_sort",
    )(t, wcat_b, bcat, upper)


def _dispatch_kernel(row_ref, tot_ref, zrow_ref, nz_ref, t_ref, lp_ref, xs_ref, stage, zbuf, sems,
                     *, qmax, zmax, chunk, nsteps):
    i = pl.program_id(0)
    slot = i % 2
    sp = stage.shape[1] // PACK_SUB
    tm = t_ref.shape[0]
    crow = chunk * PACK_SUB

    @pl.when(i == 0)
    def _():
        zbuf[...] = jnp.zeros_like(zbuf)
        _for_each_chunk(nz_ref[0], lambda q: pltpu.make_async_copy(
            zbuf, xs_ref.at[pl.ds(_tile_row(zrow_ref[q], SUBLANES), crow)], sems.at[1]).start())
        _wait_chunks(lambda k: pltpu.make_async_copy(xs_ref.at[pl.ds(0, k * crow)], xs_ref.at[pl.ds(0, k * crow)],
                                                     sems.at[1]), nz_ref[0], zmax)

    lp = lp_ref[...]
    prow = lax.broadcasted_iota(I32, (sp, tm), 0)
    perm = jnp.where(prow == lp[0:1, :], 1.0, jnp.where(prow == lp[1:2, :], 1.0, 0.0)).astype(BF16)
    srt = jnp.dot(perm, t_ref[...], preferred_element_type=F32)
    _rows_to_tiles(stage.at[slot], _pack_rows(srt))

    def chunk_copy(sl, src, dst):
        return pltpu.make_async_copy(stage.at[sl, pl.ds(_tile_row(src, crow), crow)],
                                     xs_ref.at[pl.ds(_tile_row(dst, SUBLANES), crow)], sems.at[sl])

    def drain(sl, n):
        def copy_of(k):
            return pltpu.make_async_copy(stage.at[sl, pl.ds(0, k * crow)], xs_ref.at[pl.ds(0, k * crow)], sems.at[sl])
        _wait_chunks(copy_of, n, qmax)

    @pl.when(i >= 1)
    def _():
        drain(1 - slot, tot_ref[jnp.maximum(i - 1, 0)])

    _for_each_chunk(tot_ref[i], lambda q: chunk_copy(slot, pl.multiple_of(q * chunk, chunk),
                                                     row_ref[i * qmax + q]).start())

    @pl.when(i == nsteps - 1)
    def _():
        drain(slot, tot_ref[i])


def _dispatch(chunk_row, tile_nch, zero_row, n_zero, t, lp, *, n_rows, ne, tm, chunk):
    ttot, d = t.shape
    nsteps = ttot // tm
    sp = 2 * tm + ne * chunk
    grid_spec = pltpu.PrefetchScalarGridSpec(
        num_scalar_prefetch=4,
        grid=(nsteps,),
        in_specs=[pl.BlockSpec((tm, d), lambda i, *_: (i, 0)),
                  pl.BlockSpec((SUBLANES, tm), lambda i, *_: (0, i))],
        out_specs=pl.BlockSpec(memory_space=pl.ANY),
        scratch_shapes=[pltpu.VMEM((2, sp * PACK_SUB, LANES), jnp.uint32),
                        pltpu.VMEM((chunk * PACK_SUB, LANES), jnp.uint32), pltpu.SemaphoreType.DMA((2,))],
    )
    assert d == 2 * PACK_SUB * LANES
    return pl.pallas_call(
        functools.partial(_dispatch_kernel, qmax=chunk_row.shape[0] // nsteps, zmax=zero_row.shape[0],
                          chunk=chunk, nsteps=nsteps),
        out_shape=jax.ShapeDtypeStruct((n_rows * PACK_SUB, LANES), jnp.uint32),
        grid_spec=grid_spec,
        compiler_params=_cp(("arbitrary",), VMEM_LIMIT),
        name="moe_dispatch",
    )(chunk_row, tile_nch, zero_row, n_zero, t, lp)


def _moe_kernel(be_ref, nu_ref, xs_ref, w1_ref, w3_ref, w2_ref, y_ref, w13b, w2b, *, ff):
    i = pl.program_id(0)

    @pl.when(i < nu_ref[0])
    def _():
        prev = be_ref[jnp.maximum(i - 1, 0)]

        @pl.when((i == 0) | (be_ref[i] != prev))
        def _():
            w13b[:, 0:ff] = w1_ref[...].astype(BF16)
            w13b[:, ff:2 * ff] = w3_ref[...].astype(BF16)
            w2b[...] = w2_ref[...].astype(BF16)

        x = _unpack_rows(_rows_from_tiles(xs_ref, MOE_ROWS))
        h = jnp.dot(x, w13b[...], preferred_element_type=F32)
        hid = (_silu(h[:, 0:ff]) * h[:, ff:2 * ff]).astype(BF16)
        y = jnp.dot(hid, w2b[...], preferred_element_type=F32)
        _rows_to_tiles(y_ref, _pack_rows(y.astype(BF16).astype(F32)))

    @pl.when(i >= nu_ref[0])
    def _():
        y_ref[...] = jnp.zeros_like(y_ref)


def _moe_experts(block_e, n_used, xs, w1, w3, w2, *, layer):
    d = w1.shape[-2]
    ff = w1.shape[-1]
    blk = MOE_ROWS * PACK_SUB
    nb = xs.shape[0] // blk
    wmap = lambda i, be, nu: (layer, be[i], 0, 0)
    grid_spec = pltpu.PrefetchScalarGridSpec(
        num_scalar_prefetch=2,
        grid=(nb,),
        in_specs=[pl.BlockSpec((blk, LANES), lambda i, be, nu: (i, 0)),
                  pl.BlockSpec((None, None, d, ff), wmap),
                  pl.BlockSpec((None, None, d, ff), wmap),
                  pl.BlockSpec((None, None, ff, d), wmap)],
        out_specs=pl.BlockSpec((blk, LANES), lambda i, be, nu: (i, 0)),
        scratch_shapes=[pltpu.VMEM((d, 2 * ff), BF16), pltpu.VMEM((ff, d), BF16)],
    )
    return pl.pallas_call(
        functools.partial(_moe_kernel, ff=ff),
        out_shape=jax.ShapeDtypeStruct(xs.shape, xs.dtype),
        grid_spec=grid_spec,
        compiler_params=_cp(("arbitrary",), VMEM_LIMIT),
        name="moe_experts",
    )(block_e, n_used, xs, w1, w3, w2)


def _combine_kernel(row_ref, tot_ref, yb_ref, x_ref, tokm_ref, g2_ref, lng_ref, lnb_ref,
                    o_ref, stage, sems, *, qmax, chunk, nsteps, alpha):
    i = pl.program_id(0)
    slot = i % 2
    sp = stage.shape[1] // PACK_SUB
    tm = x_ref.shape[0]
    crow = chunk * PACK_SUB

    def chunk_copy(sl, src, dst):
        return pltpu.make_async_copy(yb_ref.at[pl.ds(_tile_row(src, SUBLANES), crow)],
                                     stage.at[sl, pl.ds(_tile_row(dst, crow), crow)], sems.at[sl])

    def issue_tile(step, sl):
        _for_each_chunk(tot_ref[step], lambda q: chunk_copy(sl, row_ref[step * qmax + q],
                                                            pl.multiple_of(q * chunk, chunk)).start())

    @pl.when(i == 0)
    def _():
        stage[...] = jnp.zeros_like(stage)
        issue_tile(0, 0)

    @pl.when(i + 1 < nsteps)
    def _():
        issue_tile(jnp.minimum(i + 1, nsteps - 1), 1 - slot)

    def copy_of(k):
        return pltpu.make_async_copy(yb_ref.at[pl.ds(0, k * crow)], stage.at[slot, pl.ds(0, k * crow)], sems.at[slot])
    _wait_chunks(copy_of, tot_ref[i], qmax)

    tk = tokm_ref[...]
    pos = lax.broadcasted_iota(I32, (tm, sp), 1).astype(F32)
    st = _unpack_rows(_rows_from_tiles(stage.at[slot], sp))
    gsel = jnp.where(pos == tk[:, 2:3], tk[:, 0:1], jnp.where(pos == tk[:, 3:4], tk[:, 1:2], 0.0))
    m = jnp.dot(gsel.astype(BF16), st, preferred_element_type=F32)
    o_ref[...] = _ln(alpha * x_ref[...] + g2_ref[...] * m) * lng_ref[...] + lnb_ref[...]


def _combine(chunk_row, tile_nch, yb, x, tokm, g2, lng, lnb, *, ne, tm, chunk, n_per_batch, alpha):
    ttot, d = x.shape
    nsteps = ttot // tm
    per = n_per_batch // tm
    sp = 2 * tm + ne * chunk
    grid_spec = pltpu.PrefetchScalarGridSpec(
        num_scalar_prefetch=2,
        grid=(nsteps,),
        in_specs=[pl.BlockSpec(memory_space=pl.ANY),
                  pl.BlockSpec((tm, d), lambda i, *_: (i, 0)),
                  pl.BlockSpec((tm, LANES), lambda i, *_: (i, 0)),
                  pl.BlockSpec((None, 1, d), lambda i, *_: (i // per, 0, 0)),
                  pl.BlockSpec((1, d), lambda i, *_: (0, 0)),
                  pl.BlockSpec((1, d), lambda i, *_: (0, 0))],
        out_specs=pl.BlockSpec((tm, d), lambda i, *_: (i, 0)),
        scratch_shapes=[pltpu.VMEM((2, sp * PACK_SUB, LANES), yb.dtype), pltpu.SemaphoreType.DMA((2,))],
    )
    return pl.pallas_call(
        functools.partial(_combine_kernel, qmax=chunk_row.shape[0] // nsteps, chunk=chunk, nsteps=nsteps,
                          alpha=alpha),
        out_shape=jax.ShapeDtypeStruct((ttot, d), F32),
        grid_spec=grid_spec,
        compiler_params=_cp(("arbitrary",), VMEM_LIMIT),
        name="moe_combine",
    )(chunk_row, tile_nch, yb, x, tokm, g2, lng.reshape(1, d), lnb.reshape(1, d))


def _hier_moe_layer(x1, t, g2, lng, lnb, wg, bg, wf, bf, w1, w3, w2, *, layer, alpha):
    b, n, d = x1.shape
    ttot = b * n
    ng = wg.shape[1]
    ne = wf.shape[1]
    epg = ne // ng
    tm = min(MOE_TILE, n)
    chunk = MOE_CHUNK
    nt = ttot // tm
    tflat = t.reshape(ttot, d)
    wcat = jnp.zeros((d, LANES), F32).at[:, 0:ng].set(wg).at[:, SUBLANES:SUBLANES + ne].set(wf).astype(BF16)
    bcat = jnp.zeros((1, LANES), F32).at[0, 0:ng].set(bg).at[0, SUBLANES:SUBLANES + ne].set(bf)
    ar = jnp.arange(2 * tm, dtype=I32)
    upper = (ar[:, None] < ar[None, :]).astype(BF16)
    lp, tokm, cnt = _route(tflat, wcat, bcat, upper, ng=ng, epg=epg, tm=tm, chunk=chunk)

    tile_cnt = cnt.reshape(ne, nt, LANES)[:, :, 0].T.astype(I32)
    seg_len = ((tile_cnt + 1) // 2) * 2
    counts = jnp.sum(seg_len, axis=0)
    padded = ((counts + chunk + MOE_ROWS - 1) // MOE_ROWS) * MOE_ROWS
    pend = jnp.cumsum(padded)
    pstart = pend - padded
    base = jnp.cumsum(seg_len, axis=0) - seg_len
    seg_row = pstart[None, :] + base
    nch = (tile_cnt + chunk - 1) // chunk
    nch_end = jnp.cumsum(nch, axis=1)
    tile_nch = nch_end[:, -1].astype(I32)
    qmax = 2 * tm // chunk + ne
    qs = jnp.arange(qmax, dtype=I32)
    e_q = jnp.minimum(jnp.sum((nch_end[:, None, :] <= qs[None, :, None]).astype(I32), axis=2), ne - 1)
    pick = e_q[:, :, None] == jnp.arange(ne, dtype=I32)[None, None, :]
    first_q = jnp.sum(jnp.where(pick, (nch_end - nch)[:, None, :], 0), axis=2)
    first_row = jnp.sum(jnp.where(pick, seg_row[:, None, :], 0), axis=2)
    chunk_row = (first_row + (qs[None, :] - first_q) * chunk).astype(I32).reshape(nt * qmax)
    nb = -(-(2 * ttot + nt * ne + ne * (chunk + MOE_ROWS)) // MOE_ROWS)
    bstart = jnp.arange(nb, dtype=I32) * MOE_ROWS
    block_e = jnp.minimum(jnp.sum((pend[None, :] <= bstart[:, None]).astype(I32), axis=1), ne - 1)
    n_used = (pend[-1] // MOE_ROWS).astype(I32).reshape(1)
    zstart = jnp.concatenate([pstart + (counts // chunk) * chunk, pend[-1:]])
    zend = jnp.concatenate([pend, jnp.full((1,), nb * MOE_ROWS, pend.dtype)])
    zcnt = (zend - zstart) // chunk
    zcum = jnp.cumsum(zcnt)
    zmax = ne * ((MOE_ROWS + 2 * chunk) // chunk + 1) + (nb - (2 * ttot) // MOE_ROWS) * (MOE_ROWS // chunk)
    zq = jnp.arange(zmax, dtype=I32)
    zseg = jnp.minimum(jnp.sum((zcum[None, :] <= zq[:, None]).astype(I32), axis=1), ne)
    zpick = zseg[:, None] == jnp.arange(ne + 1, dtype=I32)[None, :]
    zfirst = jnp.sum(jnp.where(zpick, (zcum - zcnt)[None, :], 0), axis=1)
    zero_row = (jnp.sum(jnp.where(zpick, zstart[None, :], 0), axis=1) + (zq - zfirst) * chunk).astype(I32)
    n_zero = zcum[-1].astype(I32).reshape(1)

    xs = _dispatch(chunk_row, tile_nch, zero_row, n_zero, tflat, lp, n_rows=nb * MOE_ROWS, ne=ne, tm=tm,
                   chunk=chunk)
    yb = _moe_experts(block_e.astype(I32), n_used, xs, w1, w3, w2, layer=layer)
    out = _combine(chunk_row, tile_nch, yb, x1.reshape(ttot, d), tokm, g2, lng, lnb,
                   ne=ne, tm=tm, chunk=chunk, n_per_batch=n, alpha=alpha)
    return out.reshape(b, n, d)


def kernel(x, c, ctx, c_ctx, ada_w, ada_b, ln_g, ln_b, ev_w_in, ev_conv_w, ev_conv_b, ev_gate_a_w,
           ev_gate_a_b, ev_gate_x_w, ev_gate_x_b, ev_lru_lambda, ev_da_lambda, ev_da_subln, ev_w_out,
           od_w_out, od_b_out, moe_wg, moe_bg, moe_wf, moe_bf, moe_w1, moe_w3, moe_w2):
    bsz, n_lat, d = x.shape
    depth = ada_w.shape[0]
    alpha = (2.0 * depth) ** 0.25
    lw = ev_conv_w.shape[-1]
    hd = ev_da_lambda.shape[-1]
    vd = ev_da_subln.shape[-1]
    aw = (ev_w_in.shape[-1] - 2 * lw) // 3
    heads = aw // vd
    fnet_groups = 4

    rows = ((bsz + 1 + SUBLANES - 1) // SUBLANES) * SUBLANES
    cond = jnp.zeros((rows, d), F32).at[0:bsz].set(c).at[bsz].set(c_ctx)
    ada = _ada_terms(cond, ada_w, ada_b).reshape(depth, rows, 6, d)

    def lat_term(l, k):
        return ada[l, 0:bsz, k, :].reshape(bsz, 1, d)

    def ctx_term(l, k):
        return jnp.broadcast_to(ada[l, bsz, k, :].reshape(1, 1, d), (bsz, 1, d))

    for l in range(depth):
        ctx_live = any(m % 2 == 0 for m in range(l + 1, depth))
        assert not ctx_live, "context stream update is only needed for depth > 2"
        sh1, sc1, g1, sh2, sc2, g2 = [lat_term(l, k) for k in range(6)]
        if l % 2 == 0:
            e = l // 2
            lam_init = 0.8 - 0.6 * math.exp(-0.3 * l)
            w_in_b = ev_w_in[e].astype(BF16)
            cos_t, sin_t = _rope_tables(n_lat, hd)
            qscale = hd ** -0.5 * math.log2(math.e)
            g_l, xr_l, q_l, k_l, v_l = _project_even(x, sh1, sc1, w_in_b, cos_t, sin_t, rope=True,
                                                     lw=lw, aw=aw, qscale=qscale, tm=1024)
            n_ctx = ctx.shape[1]
            _, xr_c, _, k_c, v_c = _project_even(ctx, ctx_term(l, 0), ctx_term(l, 1), w_in_b,
                                                 cos_t[0:n_ctx], sin_t[0:n_ctx], rope=False,
                                                 lw=lw, aw=aw, qscale=qscale, tm=256)
            o_l = _diff_attention(q_l, k_c, k_l, v_c, v_l, ev_da_lambda[e], ev_da_subln[e],
                                  heads=heads, hd=hd, lam_init=lam_init)
            gwid = 256
            y = None
            for dirn, rev in ((0, False), (1, True)):
                wa_bd = _block_diag_groups(ev_gate_a_w[e, dirn], gwid)
                wx_bd = _block_diag_groups(ev_gate_x_w[e, dirn], gwid)
                common = (ev_conv_w[e], ev_conv_b[e], wa_bd, ev_gate_a_b[e, dirn], wx_bd,
                          ev_gate_x_b[e, dirn], ev_lru_lambda[e, dirn])
                h_zero = jnp.zeros((bsz, 1, lw), F32)
                _, h_fin = _rglru_dir(xr_c, None, *common, h_zero, reverse=rev, tn=256)
                y, _ = _rglru_dir(xr_l, y, *common, h_fin, reverse=rev, tn=512)
            x1, t = _outproj_even(y, g_l, o_l, x, ev_w_out[e].astype(BF16), g1, ln_g[l, 0], ln_b[l, 0],
                                  sh2, sc2, alpha=alpha, tm=1024)
        else:
            o = l // 2
            gd = d // fnet_groups
            zr, zi = _chan_dft(x, sh1, sc1, _chan_dft_table(gd), groups=fnet_groups, tm=1024)
            wv = _tok_fft(zr, zi)
            norm = 1.0 / math.sqrt(float(n_lat * gd))
            x1, t = _outproj_odd(wv, x, od_w_out[o].astype(BF16), od_b_out[o], g1, ln_g[l, 0], ln_b[l, 0],
                                 sh2, sc2, alpha=alpha, norm=norm, tm=1024)
        x = _hier_moe_layer(x1, t, g2, ln_g[l, 1], ln_b[l, 1], moe_wg[l], moe_bg[l], moe_wf[l], moe_bf[l],
                            moe_w1, moe_w3, moe_w2, layer=l, alpha=alpha)
    return x
```

```python
import functools
import math

import numpy as np
import jax
import jax.numpy as jnp
from jax import lax
from jax.experimental import pallas as pl
from jax.experimental.pallas import tpu as pltpu

F32 = jnp.float32
BF16 = jnp.bfloat16
I32 = jnp.int32

LN_EPS = 1e-6
LRU_C = 8.0
ROPE_BASE = 10000.0
GRID_W = 64
CONV_W = 4
LANES = 128
SUBLANES = 8
MOE_ROWS = 512
MOE_TILE = 256
MOE_CHUNK = 8
VMEM_LIMIT = 56 * 1024 * 1024


def _cp(sem, vmem=None):
    return pltpu.CompilerParams(dimension_semantics=sem, vmem_limit_bytes=vmem)


def _ln(x):
    mu = jnp.mean(x, axis=-1, keepdims=True)
    xc = x - mu
    var = jnp.mean(xc * xc, axis=-1, keepdims=True)
    return xc * lax.rsqrt(var + LN_EPS)


def _silu(x):
    return x * jax.nn.sigmoid(x)


PACK_SUB = 4


def _tile_row(r, mult):
    if isinstance(r, int):
        return r * PACK_SUB
    return pl.multiple_of(r * PACK_SUB, mult)


def _pack_rows(val):
    half = val.shape[1] // 2
    lo = lax.bitcast_convert_type(val[:, 0:half], jnp.uint32) >> 16
    hi = lax.bitcast_convert_type(val[:, half:], jnp.uint32) & jnp.uint32(0xFFFF0000)
    return lo | hi


def _unpack_rows(words):
    lo = lax.bitcast_convert_type(words << 16, F32)
    hi = lax.bitcast_convert_type(words & jnp.uint32(0xFFFF0000), F32)
    return jnp.concatenate([lo, hi], axis=1).astype(BF16)


def _wait_chunks(copy_of, n, nmax):
    b = 1
    while b <= nmax:
        @pl.when((n & b) != 0)
        def _(b=b):
            copy_of(b).wait()
        b *= 2


ISSUE_UNROLL = 4


def _for_each_chunk(n, start_one):
    groups = n // ISSUE_UNROLL

    def group(g, carry):
        for u in range(ISSUE_UNROLL):
            start_one(g * ISSUE_UNROLL + u)
        return carry
    lax.fori_loop(0, groups, group, 0)

    def single(q, carry):
        start_one(q)
        return carry
    lax.fori_loop(groups * ISSUE_UNROLL, n, single, 0)


def _rows_from_tiles(ref, nrows):
    return jnp.concatenate([ref[pl.ds(k, nrows, stride=PACK_SUB), :] for k in range(PACK_SUB)], axis=1)


def _rows_to_tiles(ref, val):
    nrows = val.shape[0]
    for k in range(PACK_SUB):
        ref[pl.ds(k, nrows, stride=PACK_SUB), :] = val[:, k * LANES:(k + 1) * LANES]


def _ada_kernel(c_ref, w_ref, b_ref, o_ref):
    s = _silu(c_ref[...]).astype(BF16)
    o_ref[...] = jnp.dot(s, w_ref[...].astype(BF16), preferred_element_type=F32) + b_ref[...]


def _ada_terms(cond, ada_w, ada_b):
    nl, d, d6 = ada_w.shape
    r = cond.shape[0]
    tn = 1024
    return pl.pallas_call(
        _ada_kernel,
        out_shape=jax.ShapeDtypeStruct((nl, r, d6), F32),
        grid=(nl, d6 // tn),
        in_specs=[pl.BlockSpec((r, d), lambda l, j: (0, 0)),
                  pl.BlockSpec((None, d, tn), lambda l, j: (l, 0, j)),
                  pl.BlockSpec((None, 1, tn), lambda l, j: (l, 0, j))],
        out_specs=pl.BlockSpec((None, r, tn), lambda l, j: (l, 0, j)),
        compiler_params=_cp(("parallel", "parallel")),
        name="ada_terms",
    )(cond, ada_w, ada_b.reshape(nl, 1, d6))


def _rope_apply(x, cos, sin_signed):
    tm = x.shape[0]
    lane = lax.broadcasted_iota(I32, (tm, LANES), 1)
    first_half = (lane % 32) < 16
    outs = []
    for j in range(x.shape[1] // LANES):
        xh = x[:, j * LANES:(j + 1) * LANES]
        partner = jnp.where(first_half, pltpu.roll(xh, LANES - 16, 1), pltpu.roll(xh, 16, 1))
        outs.append(xh * cos + partner * sin_signed)
    return jnp.concatenate(outs, axis=1)


def _proj_kernel(x_ref, sh_ref, sc_ref, w_ref, cos_ref, sin_ref,
                 g_ref, xr_ref, q_ref, k_ref, v_ref, *, rope, lw, aw, qscale):
    h = _ln(x_ref[...]) * (1.0 + sc_ref[...]) + sh_ref[...]
    hb = h.astype(BF16)

    def mm(c0, c1):
        return jnp.dot(hb, w_ref[:, c0:c1], preferred_element_type=F32)

    g_ref[...] = mm(0, lw).astype(BF16)
    xr_ref[...] = mm(lw, 2 * lw)
    q = mm(2 * lw, 2 * lw + aw)
    k = mm(2 * lw + aw, 2 * lw + 2 * aw)
    if rope:
        q = _rope_apply(q, cos_ref[...], sin_ref[...])
        k = _rope_apply(k, cos_ref[...], sin_ref[...])
    q_ref[...] = (q * qscale).astype(BF16)
    k_ref[...] = k.astype(BF16)
    v_ref[...] = mm(2 * lw + 2 * aw, 2 * lw + 3 * aw).astype(BF16)


def _project_even(x, shift, scale, w_in_b, cos_t, sin_t, *, rope, lw, aw, qscale, tm):
    b, n, d = x.shape
    tm = min(tm, n)
    nin = w_in_b.shape[1]
    tok = lambda bi, i: (bi, i, 0)
    per_b = lambda bi, i: (bi, 0, 0)
    outs = (jax.ShapeDtypeStruct((b, n, lw), BF16), jax.ShapeDtypeStruct((b, n, lw), F32),
            jax.ShapeDtypeStruct((b, n, aw), BF16), jax.ShapeDtypeStruct((b, n, aw), BF16),
            jax.ShapeDtypeStruct((b, n, aw), BF16))
    return pl.pallas_call(
        functools.partial(_proj_kernel, rope=rope, lw=lw, aw=aw, qscale=qscale),
        out_shape=outs,
        grid=(b, n // tm),
        in_specs=[pl.BlockSpec((None, tm, d), tok),
                  pl.BlockSpec((None, 1, d), per_b),
                  pl.BlockSpec((None, 1, d), per_b),
                  pl.BlockSpec((d, nin), lambda bi, i: (0, 0)),
                  pl.BlockSpec((tm, LANES), lambda bi, i: (i, 0)),
                  pl.BlockSpec((tm, LANES), lambda bi, i: (i, 0))],
        out_specs=(pl.BlockSpec((None, tm, lw), tok), pl.BlockSpec((None, tm, lw), tok),
                   pl.BlockSpec((None, tm, aw), tok), pl.BlockSpec((None, tm, aw), tok),
                   pl.BlockSpec((None, tm, aw), tok)),
        compiler_params=_cp(("parallel", "parallel"), VMEM_LIMIT),
        name="proj_even_rope" if rope else "proj_even_ctx",
    )(x, shift, scale, w_in_b, cos_t, sin_t)


def _rope_tables(n_tok, head_dim):
    t = jnp.arange(n_tok)
    row = (t // GRID_W).astype(F32)
    col = (t % GRID_W).astype(F32)
    nf = head_dim // 4
    freqs = ROPE_BASE ** (-jnp.arange(nf, dtype=F32) / nf)
    lane = np.arange(LANES)
    within = lane % head_dim
    axis = within // (2 * nf)
    half = (within % (2 * nf)) // nf
    f = within % nf
    pos = jnp.where(jnp.asarray(axis)[None, :] == 0, row[:, None], col[:, None])
    ang = pos * freqs[jnp.asarray(f)][None, :]
    sign = jnp.asarray(np.where(half == 0, -1.0, 1.0), F32)[None, :]
    return jnp.cos(ang).astype(F32), (jnp.sin(ang) * sign).astype(F32)


def _attn_kernel(q_ref, kc_ref, kl_ref, vc_ref, vl_ref, dl_ref, gain_ref, o_ref, kbuf, vbuf, sbuf, ebuf, abuf,
                 cbuf, *, nc, nl, hd, lam_init, rows):
    @pl.when((pl.program_id(0) == 0) & (pl.program_id(1) == 0))
    def _():
        sbuf[...] = jnp.zeros_like(sbuf)
        abuf[...] = jnp.zeros_like(abuf)
        cbuf[...] = jnp.zeros_like(cbuf)

    kbuf[0:nc, :] = kc_ref[...]
    kbuf[nc:nc + nl, :] = kl_ref[...]
    vbuf[0:nc, :] = vc_ref[...]
    vbuf[nc:nc + nl, :] = vl_ref[...]

    lf = dl_ref[...]
    lam = (jnp.exp(jnp.sum(lf[0:1] * lf[1:2], axis=1, keepdims=True))
           - jnp.exp(jnp.sum(lf[2:3] * lf[3:4], axis=1, keepdims=True)) + lam_init)
    gain = gain_ref[...] * (1.0 - lam_init)
    n_sub = nl // rows
    lane = lax.broadcasted_iota(I32, (rows, 2 * hd), 1)
    nt = (((1,), (1,)), ((), ()))

    def stage_a(j, slot):
        r0 = pl.multiple_of(jnp.minimum(j, n_sub - 1) * rows, rows)
        q = q_ref[pl.ds(r0, rows), :]
        zero = jnp.zeros_like(q)
        kk = kbuf[...]
        sbuf[slot, 0] = lax.dot_general(jnp.where(lane < hd, q, zero), kk, nt, preferred_element_type=F32)
        sbuf[slot, 1] = lax.dot_general(jnp.where(lane >= hd, q, zero), kk, nt, preferred_element_type=F32)

    def stage_b(slot):
        cols = [slice(c, c + LANES) for c in range(0, nc + nl, LANES)]
        ls = []
        for k in range(2):
            pm = sbuf[slot, k, :, cols[0]]
            for cs in cols[1:]:
                pm = jnp.maximum(pm, sbuf[slot, k, :, cs])
            m = jnp.max(pm, axis=1, keepdims=True)
            acc = jnp.zeros((rows, LANES), F32)
            for cs in cols:
                e = jnp.exp2(sbuf[slot, k, :, cs] - m)
                acc = acc + e
                ebuf[slot, k, :, cs] = e.astype(BF16)
            ls.append(jnp.sum(acc, axis=1, keepdims=True))
        ratio = (lam * ls[0] / ls[1]).astype(BF16)
        abuf[slot] = ebuf[slot, 0] - ratio * ebuf[slot, 1]
        cbuf[slot] = jnp.broadcast_to(1.0 / ls[0], cbuf.shape[1:])

    def stage_c(j, slot):
        r0 = pl.multiple_of(jnp.maximum(j - 2, 0) * rows, rows)
        o = jnp.dot(abuf[slot], vbuf[...], preferred_element_type=F32) * cbuf[slot]
        o = o * lax.rsqrt(jnp.mean(o * o, axis=1, keepdims=True) + LN_EPS) * gain
        o_ref[pl.ds(r0, rows), :] = o.astype(BF16)

    def body(t, carry):
        j = 2 * t
        stage_a(j, 0)
        stage_b(1)
        stage_c(j, 0)
        stage_a(j + 1, 1)
        stage_b(0)
        stage_c(j + 1, 1)
        return carry

    lax.fori_loop(0, (n_sub + 2) // 2, body, 0)


def _diff_attention(q, k_ctx, k_lat, v_ctx, v_lat, da_lambda, subln, *, heads, hd, lam_init):
    b, n, aw = q.shape
    nc = k_ctx.shape[1]
    vd = aw // heads
    rows = min(128, n)
    assert (n // rows) % 2 == 0
    blk_q = pl.BlockSpec((None, n, vd), lambda bi, h: (bi, 0, h))
    blk_c = pl.BlockSpec((None, nc, vd), lambda bi, h: (bi, 0, h))
    return pl.pallas_call(
        functools.partial(_attn_kernel, nc=nc, nl=n, hd=hd, lam_init=lam_init, rows=rows),
        out_shape=jax.ShapeDtypeStruct((b, n, aw), BF16),
        grid=(b, heads),
        in_specs=[blk_q, blk_c, blk_q, blk_c, blk_q,
                  pl.BlockSpec(da_lambda.shape, lambda bi, h: (0, 0)),
                  pl.BlockSpec((1, vd), lambda bi, h: (0, 0))],
        out_specs=blk_q,
        scratch_shapes=[pltpu.VMEM((nc + n, vd), BF16), pltpu.VMEM((nc + n, vd), BF16),
                        pltpu.VMEM((2, 2, rows, nc + n), F32), pltpu.VMEM((2, 2, rows, nc + n), BF16),
                        pltpu.VMEM((2, rows, nc + n), BF16), pltpu.VMEM((2, rows, vd), F32)],
        compiler_params=_cp(("arbitrary", "arbitrary"), VMEM_LIMIT),
        name="diff_attention",
    )(q, k_ctx, k_lat, v_ctx, v_lat, da_lambda, subln.reshape(1, vd))


def _lru_kernel(*refs, reverse, add_prev, nt, groups):
    if add_prev:
        (xp_ref, xc_ref, xn_ref, yprev_ref, cw_ref, cb_ref, wa_ref, ba_ref, wx_ref, bx_ref,
         lam_ref, h0_ref, y_ref, hf_ref, carry, ext) = refs
    else:
        (xp_ref, xc_ref, xn_ref, cw_ref, cb_ref, wa_ref, ba_ref, wx_ref, bx_ref,
         lam_ref, h0_ref, y_ref, hf_ref, carry, ext) = refs
        yprev_ref = None
    i = pl.program_id(1)
    ti = (nt - 1 - i) if reverse else i
    tn, w = xc_ref.shape

    @pl.when(i == 0)
    def _():
        carry[...] = h0_ref[...]

    ext[0:SUBLANES, :] = jnp.where(ti > 0, xp_ref[...], 0.0)
    ext[SUBLANES:SUBLANES + tn, :] = xc_ref[...]
    ext[SUBLANES + tn:2 * SUBLANES + tn, :] = jnp.where(ti < nt - 1, xn_ref[...], 0.0)
    left = CONV_W // 2
    xc = cb_ref[...]
    for k in range(CONV_W):
        off = SUBLANES - left + k
        xc = xc + ext[off:off + tn, :] * cw_ref[k:k + 1, :]

    xb = xc.astype(BF16)
    gw = w // groups

    def gate(w_ref, b_ref):
        parts = [jnp.dot(xb[:, g * gw:(g + 1) * gw], w_ref[g], preferred_element_type=F32)
                 for g in range(groups)]
        return jax.nn.sigmoid(jnp.concatenate(parts, axis=1) + b_ref[...])

    r = gate(wa_ref, ba_ref)
    ig = gate(wx_ref, bx_ref)
    log_a = (-LRU_C * jax.nn.softplus(-lam_ref[...])) * r
    a = jnp.exp(log_a)
    bcoef = jnp.sqrt(-jnp.tanh(log_a) * (a * a + 1.0)) * ig * xc

    ngroups = tn // SUBLANES
    a = a.reshape(ngroups, SUBLANES, w)
    bcoef = bcoef.reshape(ngroups, SUBLANES, w)
    row = lax.broadcasted_iota(I32, (ngroups, SUBLANES, w), 1)
    d = 1
    while d < SUBLANES:
        shift = (SUBLANES - d) if reverse else d
        a_sh = pltpu.roll(a, shift, 1)
        b_sh = pltpu.roll(bcoef, shift, 1)
        live = (row < SUBLANES - d) if reverse else (row >= d)
        bcoef = jnp.where(live, a * b_sh + bcoef, bcoef)
        a = jnp.where(live, a * a_sh, a)
        d *= 2
    hc = carry[...]
    npairs = ngroups // 2
    for p in (range(npairs - 1, -1, -1) if reverse else range(npairs)):
        hs = {}
        for g in ((2 * p + 1, 2 * p) if reverse else (2 * p, 2 * p + 1)):
            h = a[g] * hc + bcoef[g]
            hc = h[0:1, :] if reverse else h[SUBLANES - 1:SUBLANES, :]
            hs[g] = h
        rs = slice(2 * p * SUBLANES, (2 * p + 2) * SUBLANES)
        h2 = jnp.concatenate([hs[2 * p], hs[2 * p + 1]], axis=0)
        if add_prev:
            h2 = yprev_ref[rs, :].astype(F32) + h2
        y_ref[rs, :] = h2.astype(y_ref.dtype)
    carry[...] = hc
    hf_ref[...] = hc


def _rglru_dir(xr, y_prev, conv_w, conv_b, wa_bd, ba, wx_bd, bx, lam, h0, *, reverse, tn):
    b, n, w = xr.shape
    tn = min(tn, n)
    nt = n // tn
    groups = wa_bd.shape[0]
    nb8 = n // SUBLANES
    per8 = tn // SUBLANES

    def tmap(i):
        return (nt - 1 - i) if reverse else i

    cur = pl.BlockSpec((None, tn, w), lambda bi, i: (bi, tmap(i), 0))
    halo_p = pl.BlockSpec((None, SUBLANES, w), lambda bi, i: (bi, jnp.maximum(tmap(i) * per8 - 1, 0), 0))
    halo_n = pl.BlockSpec((None, SUBLANES, w), lambda bi, i: (bi, jnp.minimum((tmap(i) + 1) * per8, nb8 - 1), 0))
    row_w = pl.BlockSpec((1, w), lambda bi, i: (0, 0))
    per_b = pl.BlockSpec((None, 1, w), lambda bi, i: (bi, 0, 0))
    gate_w = pl.BlockSpec(wa_bd.shape, lambda bi, i: (0, 0, 0))
    add_prev = y_prev is not None
    in_specs = [halo_p, cur, halo_n] + ([cur] if add_prev else []) + [
        pl.BlockSpec((CONV_W, w), lambda bi, i: (0, 0)), row_w, gate_w, row_w, gate_w, row_w, row_w, per_b]
    args = [xr, xr, xr] + ([y_prev] if add_prev else []) + [
        conv_w, conv_b.reshape(1, w), wa_bd, ba.reshape(1, w), wx_bd, bx.reshape(1, w),
        lam.reshape(1, w), h0]
    return pl.pallas_call(
        functools.partial(_lru_kernel, reverse=reverse, add_prev=add_prev, nt=nt, groups=groups),
        out_shape=(jax.ShapeDtypeStruct((b, n, w), BF16), jax.ShapeDtypeStruct((b, 1, w), F32)),
        grid=(b, nt),
        in_specs=in_specs,
        out_specs=(cur, per_b),
        scratch_shapes=[pltpu.VMEM((1, w), F32), pltpu.VMEM((tn + 2 * SUBLANES, w), F32)],
        compiler_params=_cp(("parallel", "arbitrary"), VMEM_LIMIT),
        name="rglru_rev" if reverse else "rglru_fwd",
    )(*args)


def _block_diag_groups(wh, group_width):
    heads, blk, _ = wh.shape
    per = group_width // blk
    groups = heads // per
    whg = wh.reshape(groups, per, blk, blk)
    eye = jnp.eye(per, dtype=wh.dtype)
    bd = jnp.einsum('gpij,pq->gpiqj', whg, eye).reshape(groups, group_width, group_width)
    return bd.astype(BF16)


def _residual_ln_mod(x, y, g1, lng, lnb, sh2, sc2, alpha):
    x1 = _ln(alpha * x + g1 * y) * lng + lnb
    t = _ln(x1) * (1.0 + sc2) + sh2
    return x1, t


def _outproj_even_kernel(r_ref, g_ref, o_ref, x_ref, w_ref, g1_ref, lng_ref, lnb_ref, sh2_ref, sc2_ref,
                         x1_ref, t_ref, *, lw, alpha):
    z = (r_ref[...].astype(F32) * jax.nn.gelu(g_ref[...].astype(F32))).astype(BF16)
    y = (jnp.dot(z, w_ref[0:lw, :], preferred_element_type=F32)
         + jnp.dot(o_ref[...], w_ref[lw:, :], preferred_element_type=F32))
    x1, t = _residual_ln_mod(x_ref[...], y, g1_ref[...], lng_ref[...], lnb_ref[...],
                             sh2_ref[...], sc2_ref[...], alpha)
    x1_ref[...] = x1
    t_ref[...] = t.astype(BF16)


def _outproj_even(r, g, o, x, w_out_b, g1, lng, lnb, sh2, sc2, *, alpha, tm):
    b, n, d = x.shape
    lw = r.shape[2]
    aw = o.shape[2]
    tm = min(tm, n)
    tok = lambda bi, i: (bi, i, 0)
    per_b = pl.BlockSpec((None, 1, d), lambda bi, i: (bi, 0, 0))
    row = pl.BlockSpec((1, d), lambda bi, i: (0, 0))
    return pl.pallas_call(
        functools.partial(_outproj_even_kernel, lw=lw, alpha=alpha),
        out_shape=(jax.ShapeDtypeStruct((b, n, d), F32), jax.ShapeDtypeStruct((b, n, d), BF16)),
        grid=(b, n // tm),
        in_specs=[pl.BlockSpec((None, tm, lw), tok), pl.BlockSpec((None, tm, lw), tok),
                  pl.BlockSpec((None, tm, aw), tok), pl.BlockSpec((None, tm, d), tok),
                  pl.BlockSpec(w_out_b.shape, lambda bi, i: (0, 0)),
                  per_b, row, row, per_b, per_b],
        out_specs=(pl.BlockSpec((None, tm, d), tok), pl.BlockSpec((None, tm, d), tok)),
        compiler_params=_cp(("parallel", "parallel"), VMEM_LIMIT),
        name="outproj_even",
    )(r, g, o, x, w_out_b, g1, lng.reshape(1, d), lnb.reshape(1, d), sh2, sc2)


def _chan_dft_kernel(x_ref, sh_ref, sc_ref, tab_ref, zr_ref, zi_ref, *, groups, gd):
    h = (_ln(x_ref[...]) * (1.0 + sc_ref[...]) + sh_ref[...]).astype(BF16)
    for g in range(groups):
        z = jnp.dot(h[:, g * gd:(g + 1) * gd], tab_ref[...], preferred_element_type=F32)
        zr_ref[:, g * gd:(g + 1) * gd] = z[:, 0:gd].astype(BF16)
        zi_ref[:, g * gd:(g + 1) * gd] = z[:, gd:2 * gd].astype(BF16)


def _chan_dft(x, shift, scale, tab, *, groups, tm):
    b, n, d = x.shape
    gd = d // groups
    tm = min(tm, n)
    tok = lambda bi, i: (bi, i, 0)
    per_b = pl.BlockSpec((None, 1, d), lambda bi, i: (bi, 0, 0))
    return pl.pallas_call(
        functools.partial(_chan_dft_kernel, groups=groups, gd=gd),
        out_shape=(jax.ShapeDtypeStruct((b, n, d), BF16), jax.ShapeDtypeStruct((b, n, d), BF16)),
        grid=(b, n // tm),
        in_specs=[pl.BlockSpec((None, tm, d), tok), per_b, per_b,
                  pl.BlockSpec(tab.shape, lambda bi, i: (0, 0))],
        out_specs=(pl.BlockSpec((None, tm, d), tok), pl.BlockSpec((None, tm, d), tok)),
        compiler_params=_cp(("parallel", "parallel"), VMEM_LIMIT),
        name="chan_dft",
    )(x, shift, scale, tab)


FFT_C = 64


def _fft_pitch(group):
    p = -(-group // SUBLANES)
    return SUBLANES * (p if p % 2 else p + 1)


def _tok_fft_kernel(zr_ref, zi_ref, m1_ref, m3_ref, tc_ref, ts_ref, o_ref, zsr, zsi, asr, asi, ob, *, nr):
    c_len = FFT_C
    pz = zsr.shape[0] // nr
    pa = asr.shape[0] // c_len
    for r in range(nr):
        zsr[pz * r:pz * r + c_len, :] = zr_ref[c_len * r:c_len * (r + 1), :].astype(F32)
        zsi[pz * r:pz * r + c_len, :] = zi_ref[c_len * r:c_len * (r + 1), :].astype(F32)
    m1 = m1_ref[...]
    for c in range(c_len):
        x2 = jnp.concatenate([zsr[pl.ds(c, nr, stride=pz), :], zsi[pl.ds(c, nr, stride=pz), :]], axis=0)
        a2 = jnp.dot(m1, x2.astype(BF16), preferred_element_type=F32)
        ar, ai = a2[0:nr], a2[nr:2 * nr]
        tcv = tc_ref[c * nr:(c + 1) * nr, :]
        tsv = ts_ref[c * nr:(c + 1) * nr, :]
        asr[pa * c:pa * c + nr, :] = ar * tcv + ai * tsv
        asi[pa * c:pa * c + nr, :] = ai * tcv - ar * tsv
    m3 = m3_ref[...]
    for k1 in range(nr):
        y2 = jnp.concatenate([asr[pl.ds(k1, c_len, stride=pa), :], asi[pl.ds(k1, c_len, stride=pa), :]], axis=0)
        ob[pl.ds(k1, c_len, stride=pa), :] = jnp.dot(m3, y2.astype(BF16), preferred_element_type=F32)
    for k2 in range(c_len):
        o_ref[nr * k2:nr * (k2 + 1), :] = ob[pa * k2:pa * k2 + nr, :].astype(BF16)


def _tok_fft(zr, zi):
    b, n, d = zr.shape
    nr = n // FFT_C
    pz = _fft_pitch(FFT_C)
    pa = _fft_pitch(nr)
    kr = np.arange(nr, dtype=np.float64)
    ang_r = 2.0 * np.pi * (np.outer(kr, kr) % nr) / nr
    cr, sr = np.cos(ang_r), np.sin(ang_r)
    m1 = jnp.asarray(np.block([[cr, -sr], [-sr, -cr]]), F32).astype(BF16)
    kc = np.arange(FFT_C, dtype=np.float64)
    ang_c = 2.0 * np.pi * (np.outer(kc, kc) % FFT_C) / FFT_C
    m3 = jnp.asarray(np.concatenate([np.cos(ang_c), np.sin(ang_c)], axis=1), F32).astype(BF16)
    ang_t = 2.0 * np.pi * (np.outer(kc, kr) % n) / n
    tc = jnp.broadcast_to(jnp.asarray(np.cos(ang_t).reshape(FFT_C * nr, 1), F32), (FFT_C * nr, LANES))
    ts = jnp.broadcast_to(jnp.asarray(np.sin(ang_t).reshape(FFT_C * nr, 1), F32), (FFT_C * nr, LANES))
    slab = pl.BlockSpec((None, n, LANES), lambda bi, l: (bi, 0, l))
    const = lambda a: pl.BlockSpec(a.shape, lambda bi, l: (0, 0))
    return pl.pallas_call(
        functools.partial(_tok_fft_kernel, nr=nr),
        out_shape=jax.ShapeDtypeStruct((b, n, d), BF16),
        grid=(b, d // LANES),
        in_specs=[slab, slab, const(m1), const(m3), const(tc), const(ts)],
        out_specs=slab,
        scratch_shapes=[pltpu.VMEM((nr * pz, LANES), F32), pltpu.VMEM((nr * pz, LANES), F32),
                        pltpu.VMEM((FFT_C * pa, LANES), F32), pltpu.VMEM((FFT_C * pa, LANES), F32),
                        pltpu.VMEM((FFT_C * pa, LANES), F32)],
        compiler_params=_cp(("parallel", "parallel"), VMEM_LIMIT),
        name="tok_fft",
    )(zr, zi, m1, m3, tc, ts)


def _chan_dft_table(gd):
    c = np.arange(gd, dtype=np.float64)
    ang_c = 2.0 * np.pi * (np.outer(c, c) % gd) / gd
    return jnp.asarray(np.concatenate([np.cos(ang_c), np.sin(ang_c)], axis=1), F32).astype(BF16)


def _outproj_odd_kernel(wv_ref, x_ref, w_ref, b_ref, g1_ref, lng_ref, lnb_ref, sh2_ref, sc2_ref,
                        x1_ref, t_ref, *, alpha, norm):
    y = jnp.dot(wv_ref[...], w_ref[...], preferred_element_type=F32) * norm + b_ref[...]
    x1, t = _residual_ln_mod(x_ref[...], y, g1_ref[...], lng_ref[...], lnb_ref[...],
                             sh2_ref[...], sc2_ref[...], alpha)
    x1_ref[...] = x1
    t_ref[...] = t.astype(BF16)


def _outproj_odd(wv, x, w_b, bias, g1, lng, lnb, sh2, sc2, *, alpha, norm, tm):
    b, n, d = x.shape
    tm = min(tm, n)
    tok = lambda bi, i: (bi, i, 0)
    per_b = pl.BlockSpec((None, 1, d), lambda bi, i: (bi, 0, 0))
    row = pl.BlockSpec((1, d), lambda bi, i: (0, 0))
    return pl.pallas_call(
        functools.partial(_outproj_odd_kernel, alpha=alpha, norm=norm),
        out_shape=(jax.ShapeDtypeStruct((b, n, d), F32), jax.ShapeDtypeStruct((b, n, d), BF16)),
        grid=(b, n // tm),
        in_specs=[pl.BlockSpec((None, tm, d), tok), pl.BlockSpec((None, tm, d), tok),
                  pl.BlockSpec(w_b.shape, lambda bi, i: (0, 0)), row,
                  per_b, row, row, per_b, per_b],
        out_specs=(pl.BlockSpec((None, tm, d), tok), pl.BlockSpec((None, tm, d), tok)),
        compiler_params=_cp(("parallel", "parallel"), VMEM_LIMIT),
        name="outproj_odd",
    )(wv, x, w_b, bias.reshape(1, d), g1, lng.reshape(1, d), lnb.reshape(1, d), sh2, sc2)


def _route_kernel(t_ref, w_ref, b_ref, up_ref, lp_ref, tokm_ref, cnt_ref, *, ng, epg, chunk):
    tm = t_ref.shape[0]
    ne = ng * epg
    logits = jnp.dot(t_ref[...].astype(BF16), w_ref[...], preferred_element_type=F32) + b_ref[...]
    lt = logits.T
    best = lt[0:1, :]
    bi = jnp.zeros((1, tm), I32)
    for k in range(1, ng):
        gk = lt[k:k + 1, :]
        upd = gk > best
        bi = jnp.where(upd, k, bi)
        best = jnp.where(upd, gk, best)
    den = jnp.zeros((1, tm), F32)
    for k in range(ng):
        den = den + jnp.exp(lt[k:k + 1, :] - best)
    p_g = 1.0 / den
    fsel = lt[SUBLANES:SUBLANES + epg, :]
    for k in range(1, ng):
        fsel = jnp.where(bi == k, lt[SUBLANES + k * epg:SUBLANES + (k + 1) * epg, :], fsel)
    neg = jnp.full((1, tm), -jnp.inf, F32)
    m1, m2 = neg, neg
    i1 = jnp.zeros((1, tm), I32)
    i2 = jnp.zeros((1, tm), I32)
    for j in range(epg):
        v = fsel[j:j + 1, :]
        gt1 = v > m1
        gt2 = v > m2
        m2 = jnp.where(gt1, m1, jnp.where(gt2, v, m2))
        i2 = jnp.where(gt1, i1, jnp.where(gt2, j, i2))
        m1 = jnp.where(gt1, v, m1)
        i1 = jnp.where(gt1, j, i1)
    e21 = jnp.exp(m2 - m1)
    w1 = p_g / (1.0 + e21)
    w2 = p_g * e21 / (1.0 + e21)
    e1 = bi * epg + i1
    e2 = bi * epg + i2

    e = jnp.concatenate([e1, e2], axis=1)
    rows = lax.broadcasted_iota(I32, (ne, 2 * tm), 0)
    onehot = jnp.where(rows == e, 1.0, 0.0)
    before = jnp.dot(onehot.astype(BF16), up_ref[...], preferred_element_type=F32)
    tot = jnp.sum(onehot, axis=1, keepdims=True)
    slots = jnp.floor((tot + (chunk - 1.0)) * (1.0 / chunk)) * chunk
    slots_b = jnp.broadcast_to(slots, (ne, LANES))
    rowe = lax.broadcasted_iota(I32, (ne, LANES), 0)
    incl = slots_b
    d = 1
    while d < ne:
        incl = incl + jnp.where(rowe >= d, pltpu.roll(incl, d, 0), 0.0)
        d *= 2
    seg_off = jnp.tile(incl - slots_b, (1, 2 * tm // LANES))
    lpos = jnp.sum(onehot * (before + seg_off), axis=0, keepdims=True)
    lp0 = lpos[:, 0:tm]
    lp1 = lpos[:, tm:2 * tm]
    row8 = lax.broadcasted_iota(I32, (SUBLANES, tm), 0)
    lp_ref[...] = jnp.where(row8 == 0, lp0, jnp.where(row8 == 1, lp1, 0.0)).astype(I32)
    rowl = lax.broadcasted_iota(I32, (LANES, tm), 0)
    tokm = jnp.where(rowl == 0, w1, jnp.where(rowl == 1, w2, jnp.where(rowl == 2, lp0, jnp.where(rowl == 3, lp1, 0.0))))
    tokm_ref[...] = tokm.T
    cnt_ref[...] = jnp.broadcast_to(tot, (ne, LANES))


def _route(t, wcat_b, bcat, upper, *, ng, epg, tm, chunk):
    tt, d = t.shape
    ne = ng * epg
    nt = tt // tm
    return pl.pallas_call(
        functools.partial(_route_kernel, ng=ng, epg=epg, chunk=chunk),
        out_shape=(jax.ShapeDtypeStruct((SUBLANES, tt), I32), jax.ShapeDtypeStruct((tt, LANES), F32),
                   jax.ShapeDtypeStruct((ne, nt * LANES), F32)),
        grid=(nt,),
        in_specs=[pl.BlockSpec((tm, d), lambda i: (i, 0)),
                  pl.BlockSpec((d, LANES), lambda i: (0, 0)),
                  pl.BlockSpec((1, LANES), lambda i: (0, 0)),
                  pl.BlockSpec(upper.shape, lambda i: (0, 0))],
        out_specs=(pl.BlockSpec((SUBLANES, tm), lambda i: (0, i)),
                   pl.BlockSpec((tm, LANES), lambda i: (i, 0)),
                   pl.BlockSpec((ne, LANES), lambda i: (0, i))),
        compiler_params=_cp(("parallel",), VMEM_LIMIT),
        name="route_sort",
    )(t, wcat_b, bcat, upper)


def _dispatch_kernel(row_ref, tot_ref, zrow_ref, nz_ref, t_ref, lp_ref, xs_ref, stage, zbuf, sems,
                     *, qmax, zmax, chunk, nsteps):
    i = pl.program_id(0)
    slot = i % 2
    sp = stage.shape[1] // PACK_SUB
    tm = t_ref.shape[0]
    crow = chunk * PACK_SUB

    @pl.when(i == 0)
    def _():
        zbuf[...] = jnp.zeros_like(zbuf)
        _for_each_chunk(nz_ref[0], lambda q: pltpu.make_async_copy(
            zbuf, xs_ref.at[pl.ds(_tile_row(zrow_ref[q], SUBLANES), crow)], sems.at[1]).start())
        _wait_chunks(lambda k: pltpu.make_async_copy(xs_ref.at[pl.ds(0, k * crow)], xs_ref.at[pl.ds(0, k * crow)],
                                                     sems.at[1]), nz_ref[0], zmax)

    lp = lp_ref[...]
    prow = lax.broadcasted_iota(I32, (sp, tm), 0)
    perm = jnp.where(prow == lp[0:1, :], 1.0, jnp.where(prow == lp[1:2, :], 1.0, 0.0)).astype(BF16)
    srt = jnp.dot(perm, t_ref[...], preferred_element_type=F32)
    _rows_to_tiles(stage.at[slot], _pack_rows(srt))

    def chunk_copy(sl, src, dst):
        return pltpu.make_async_copy(stage.at[sl, pl.ds(_tile_row(src, crow), crow)],
                                     xs_ref.at[pl.ds(_tile_row(dst, SUBLANES), crow)], sems.at[sl])

    def drain(sl, n):
        def copy_of(k):
            return pltpu.make_async_copy(stage.at[sl, pl.ds(0, k * crow)], xs_ref.at[pl.ds(0, k * crow)], sems.at[sl])
        _wait_chunks(copy_of, n, qmax)

    @pl.when(i >= 1)
    def _():
        drain(1 - slot, tot_ref[jnp.maximum(i - 1, 0)])

    _for_each_chunk(tot_ref[i], lambda q: chunk_copy(slot, pl.multiple_of(q * chunk, chunk),
                                                     row_ref[i * qmax + q]).start())

    @pl.when(i == nsteps - 1)
    def _():
        drain(slot, tot_ref[i])


def _dispatch(chunk_row, tile_nch, zero_row, n_zero, t, lp, *, n_rows, ne, tm, chunk):
    ttot, d = t.shape
    nsteps = ttot // tm
    sp = 2 * tm + ne * chunk
    grid_spec = pltpu.PrefetchScalarGridSpec(
        num_scalar_prefetch=4,
        grid=(nsteps,),
        in_specs=[pl.BlockSpec((tm, d), lambda i, *_: (i, 0)),
                  pl.BlockSpec((SUBLANES, tm), lambda i, *_: (0, i))],
        out_specs=pl.BlockSpec(memory_space=pl.ANY),
        scratch_shapes=[pltpu.VMEM((2, sp * PACK_SUB, LANES), jnp.uint32),
                        pltpu.VMEM((chunk * PACK_SUB, LANES), jnp.uint32), pltpu.SemaphoreType.DMA((2,))],
    )
    assert d == 2 * PACK_SUB * LANES
    return pl.pallas_call(
        functools.partial(_dispatch_kernel, qmax=chunk_row.shape[0] // nsteps, zmax=zero_row.shape[0],
                          chunk=chunk, nsteps=nsteps),
        out_shape=jax.ShapeDtypeStruct((n_rows * PACK_SUB, LANES), jnp.uint32),
        grid_spec=grid_spec,
        compiler_params=_cp(("arbitrary",), VMEM_LIMIT),
        name="moe_dispatch",
    )(chunk_row, tile_nch, zero_row, n_zero, t, lp)


def _moe_kernel(be_ref, nu_ref, xs_ref, w1_ref, w3_ref, w2_ref, y_ref, w13b, w2b, *, ff):
    i = pl.program_id(0)

    @pl.when(i < nu_ref[0])
    def _():
        prev = be_ref[jnp.maximum(i - 1, 0)]

        @pl.when((i == 0) | (be_ref[i] != prev))
        def _():
            w13b[:, 0:ff] = w1_ref[...].astype(BF16)
            w13b[:, ff:2 * ff] = w3_ref[...].astype(BF16)
            w2b[...] = w2_ref[...].astype(BF16)

        x = _unpack_rows(_rows_from_tiles(xs_ref, MOE_ROWS))
        h = jnp.dot(x, w13b[...], preferred_element_type=F32)
        hid = (_silu(h[:, 0:ff]) * h[:, ff:2 * ff]).astype(BF16)
        y = jnp.dot(hid, w2b[...], preferred_element_type=F32)
        _rows_to_tiles(y_ref, _pack_rows(y.astype(BF16).astype(F32)))

    @pl.when(i >= nu_ref[0])
    def _():
        y_ref[...] = jnp.zeros_like(y_ref)


def _moe_experts(block_e, n_used, xs, w1, w3, w2, *, layer):
    d = w1.shape[-2]
    ff = w1.shape[-1]
    blk = MOE_ROWS * PACK_SUB
    nb = xs.shape[0] // blk
    wmap = lambda i, be, nu: (layer, be[i], 0, 0)
    grid_spec = pltpu.PrefetchScalarGridSpec(
        num_scalar_prefetch=2,
        grid=(nb,),
        in_specs=[pl.BlockSpec((blk, LANES), lambda i, be, nu: (i, 0)),
                  pl.BlockSpec((None, None, d, ff), wmap),
                  pl.BlockSpec((None, None, d, ff), wmap),
                  pl.BlockSpec((None, None, ff, d), wmap)],
        out_specs=pl.BlockSpec((blk, LANES), lambda i, be, nu: (i, 0)),
        scratch_shapes=[pltpu.VMEM((d, 2 * ff), BF16), pltpu.VMEM((ff, d), BF16)],
    )
    return pl.pallas_call(
        functools.partial(_moe_kernel, ff=ff),
        out_shape=jax.ShapeDtypeStruct(xs.shape, xs.dtype),
        grid_spec=grid_spec,
        compiler_params=_cp(("arbitrary",), VMEM_LIMIT),
        name="moe_experts",
    )(block_e, n_used, xs, w1, w3, w2)


def _combine_kernel(row_ref, tot_ref, yb_ref, x_ref, tokm_ref, g2_ref, lng_ref, lnb_ref,
                    o_ref, stage, sems, *, qmax, chunk, nsteps, alpha):
    i = pl.program_id(0)
    slot = i % 2
    sp = stage.shape[1] // PACK_SUB
    tm = x_ref.shape[0]
    crow = chunk * PACK_SUB

    def chunk_copy(sl, src, dst):
        return pltpu.make_async_copy(yb_ref.at[pl.ds(_tile_row(src, SUBLANES), crow)],
                                     stage.at[sl, pl.ds(_tile_row(dst, crow), crow)], sems.at[sl])

    def issue_tile(step, sl):
        _for_each_chunk(tot_ref[step], lambda q: chunk_copy(sl, row_ref[step * qmax + q],
                                                            pl.multiple_of(q * chunk, chunk)).start())

    @pl.when(i == 0)
    def _():
        stage[...] = jnp.zeros_like(stage)
        issue_tile(0, 0)

    @pl.when(i + 1 < nsteps)
    def _():
        issue_tile(jnp.minimum(i + 1, nsteps - 1), 1 - slot)

    def copy_of(k):
        return pltpu.make_async_copy(yb_ref.at[pl.ds(0, k * crow)], stage.at[slot, pl.ds(0, k * crow)], sems.at[slot])
    _wait_chunks(copy_of, tot_ref[i], qmax)

    tk = tokm_ref[...]
    pos = lax.broadcasted_iota(I32, (tm, sp), 1).astype(F32)
    st = _unpack_rows(_rows_from_tiles(stage.at[slot], sp))
    gsel = jnp.where(pos == tk[:, 2:3], tk[:, 0:1], jnp.where(pos == tk[:, 3:4], tk[:, 1:2], 0.0))
    m = jnp.dot(gsel.astype(BF16), st, preferred_element_type=F32)
    o_ref[...] = _ln(alpha * x_ref[...] + g2_ref[...] * m) * lng_ref[...] + lnb_ref[...]


def _combine(chunk_row, tile_nch, yb, x, tokm, g2, lng, lnb, *, ne, tm, chunk, n_per_batch, alpha):
    ttot, d = x.shape
    nsteps = ttot // tm
    per = n_per_batch // tm
    sp = 2 * tm + ne * chunk
    grid_spec = pltpu.PrefetchScalarGridSpec(
        num_scalar_prefetch=2,
        grid=(nsteps,),
        in_specs=[pl.BlockSpec(memory_space=pl.ANY),
                  pl.BlockSpec((tm, d), lambda i, *_: (i, 0)),
                  pl.BlockSpec((tm, LANES), lambda i, *_: (i, 0)),
                  pl.BlockSpec((None, 1, d), lambda i, *_: (i // per, 0, 0)),
                  pl.BlockSpec((1, d), lambda i, *_: (0, 0)),
                  pl.BlockSpec((1, d), lambda i, *_: (0, 0))],
        out_specs=pl.BlockSpec((tm, d), lambda i, *_: (i, 0)),
        scratch_shapes=[pltpu.VMEM((2, sp * PACK_SUB, LANES), yb.dtype), pltpu.SemaphoreType.DMA((2,))],
    )
    return pl.pallas_call(
        functools.partial(_combine_kernel, qmax=chunk_row.shape[0] // nsteps, chunk=chunk, nsteps=nsteps,
                          alpha=alpha),
        out_shape=jax.ShapeDtypeStruct((ttot, d), F32),
        grid_spec=grid_spec,
        compiler_params=_cp(("arbitrary",), VMEM_LIMIT),
        name="moe_combine",
    )(chunk_row, tile_nch, yb, x, tokm, g2, lng.reshape(1, d), lnb.reshape(1, d))


def _hier_moe_layer(x1, t, g2, lng, lnb, wg, bg, wf, bf, w1, w3, w2, *, layer, alpha):
    b, n, d = x1.shape
    ttot = b * n
    ng = wg.shape[1]
    ne = wf.shape[1]
    epg = ne // ng
    tm = min(MOE_TILE, n)
    chunk = MOE_CHUNK
    nt = ttot // tm
    tflat = t.reshape(ttot, d)
    wcat = jnp.zeros((d, LANES), F32).at[:, 0:ng].set(wg).at[:, SUBLANES:SUBLANES + ne].set(wf).astype(BF16)
    bcat = jnp.zeros((1, LANES), F32).at[0, 0:ng].set(bg).at[0, SUBLANES:SUBLANES + ne].set(bf)
    ar = jnp.arange(2 * tm, dtype=I32)
    upper = (ar[:, None] < ar[None, :]).astype(BF16)
    lp, tokm, cnt = _route(tflat, wcat, bcat, upper, ng=ng, epg=epg, tm=tm, chunk=chunk)

    tile_cnt = cnt.reshape(ne, nt, LANES)[:, :, 0].T.astype(I32)
    seg_len = ((tile_cnt + 1) // 2) * 2
    counts = jnp.sum(seg_len, axis=0)
    padded = ((counts + chunk + MOE_ROWS - 1) // MOE_ROWS) * MOE_ROWS
    pend = jnp.cumsum(padded)
    pstart = pend - padded
    base = jnp.cumsum(seg_len, axis=0) - seg_len
    seg_row = pstart[None, :] + base
    nch = (tile_cnt + chunk - 1) // chunk
    nch_end = jnp.cumsum(nch, axis=1)
    tile_nch = nch_end[:, -1].astype(I32)
    qmax = 2 * tm // chunk + ne
    qs = jnp.arange(qmax, dtype=I32)
    e_q = jnp.minimum(jnp.sum((nch_end[:, None, :] <= qs[None, :, None]).astype(I32), axis=2), ne - 1)
    pick = e_q[:, :, None] == jnp.arange(ne, dtype=I32)[None, None, :]
    first_q = jnp.sum(jnp.where(pick, (nch_end - nch)[:, None, :], 0), axis=2)
    first_row = jnp.sum(jnp.where(pick, seg_row[:, None, :], 0), axis=2)
    chunk_row = (first_row + (qs[None, :] - first_q) * chunk).astype(I32).reshape(nt * qmax)
    nb = -(-(2 * ttot + nt * ne + ne * (chunk + MOE_ROWS)) // MOE_ROWS)
    bstart = jnp.arange(nb, dtype=I32) * MOE_ROWS
    block_e = jnp.minimum(jnp.sum((pend[None, :] <= bstart[:, None]).astype(I32), axis=1), ne - 1)
    n_used = (pend[-1] // MOE_ROWS).astype(I32).reshape(1)
    zstart = jnp.concatenate([pstart + (counts // chunk) * chunk, pend[-1:]])
    zend = jnp.concatenate([pend, jnp.full((1,), nb * MOE_ROWS, pend.dtype)])
    zcnt = (zend - zstart) // chunk
    zcum = jnp.cumsum(zcnt)
    zmax = ne * ((MOE_ROWS + 2 * chunk) // chunk + 1) + (nb - (2 * ttot) // MOE_ROWS) * (MOE_ROWS // chunk)
    zq = jnp.arange(zmax, dtype=I32)
    zseg = jnp.minimum(jnp.sum((zcum[None, :] <= zq[:, None]).astype(I32), axis=1), ne)
    zpick = zseg[:, None] == jnp.arange(ne + 1, dtype=I32)[None, :]
    zfirst = jnp.sum(jnp.where(zpick, (zcum - zcnt)[None, :], 0), axis=1)
    zero_row = (jnp.sum(jnp.where(zpick, zstart[None, :], 0), axis=1) + (zq - zfirst) * chunk).astype(I32)
    n_zero = zcum[-1].astype(I32).reshape(1)

    xs = _dispatch(chunk_row, tile_nch, zero_row, n_zero, tflat, lp, n_rows=nb * MOE_ROWS, ne=ne, tm=tm,
                   chunk=chunk)
    yb = _moe_experts(block_e.astype(I32), n_used, xs, w1, w3, w2, layer=layer)
    out = _combine(chunk_row, tile_nch, yb, x1.reshape(ttot, d), tokm, g2, lng, lnb,
                   ne=ne, tm=tm, chunk=chunk, n_per_batch=n, alpha=alpha)
    return out.reshape(b, n, d)


def kernel(x, c, ctx, c_ctx, ada_w, ada_b, ln_g, ln_b, ev_w_in, ev_conv_w, ev_conv_b, ev_gate_a_w,
           ev_gate_a_b, ev_gate_x_w, ev_gate_x_b, ev_lru_lambda, ev_da_lambda, ev_da_subln, ev_w_out,
           od_w_out, od_b_out, moe_wg, moe_bg, moe_wf, moe_bf, moe_w1, moe_w3, moe_w2):
    bsz, n_lat, d = x.shape
    depth = ada_w.shape[0]
    alpha = (2.0 * depth) ** 0.25
    lw = ev_conv_w.shape[-1]
    hd = ev_da_lambda.shape[-1]
    vd = ev_da_subln.shape[-1]
    aw = (ev_w_in.shape[-1] - 2 * lw) // 3
    heads = aw // vd
    fnet_groups = 4

    rows = ((bsz + 1 + SUBLANES - 1) // SUBLANES) * SUBLANES
    cond = jnp.zeros((rows, d), F32).at[0:bsz].set(c).at[bsz].set(c_ctx)
    ada = _ada_terms(cond, ada_w, ada_b).reshape(depth, rows, 6, d)

    def lat_term(l, k):
        return ada[l, 0:bsz, k, :].reshape(bsz, 1, d)

    def ctx_term(l, k):
        return jnp.broadcast_to(ada[l, bsz, k, :].reshape(1, 1, d), (bsz, 1, d))

    for l in range(depth):
        ctx_live = any(m % 2 == 0 for m in range(l + 1, depth))
        assert not ctx_live, "context stream update is only needed for depth > 2"
        sh1, sc1, g1, sh2, sc2, g2 = [lat_term(l, k) for k in range(6)]
        if l % 2 == 0:
            e = l // 2
            lam_init = 0.8 - 0.6 * math.exp(-0.3 * l)
            w_in_b = ev_w_in[e].astype(BF16)
            cos_t, sin_t = _rope_tables(n_lat, hd)
            qscale = hd ** -0.5 * math.log2(math.e)
            g_l, xr_l, q_l, k_l, v_l = _project_even(x, sh1, sc1, w_in_b, cos_t, sin_t, rope=True,
                                                     lw=lw, aw=aw, qscale=qscale, tm=1024)
            n_ctx = ctx.shape[1]
            _, xr_c, _, k_c, v_c = _project_even(ctx, ctx_term(l, 0), ctx_term(l, 1), w_in_b,
                                                 cos_t[0:n_ctx], sin_t[0:n_ctx], rope=False,
                                                 lw=lw, aw=aw, qscale=qscale, tm=256)
            o_l = _diff_attention(q_l, k_c, k_l, v_c, v_l, ev_da_lambda[e], ev_da_subln[e],
                                  heads=heads, hd=hd, lam_init=lam_init)
            gwid = 256
            y = None
            for dirn, rev in ((0, False), (1, True)):
                wa_bd = _block_diag_groups(ev_gate_a_w[e, dirn], gwid)
                wx_bd = _block_diag_groups(ev_gate_x_w[e, dirn], gwid)
                common = (ev_conv_w[e], ev_conv_b[e], wa_bd, ev_gate_a_b[e, dirn], wx_bd,
                          ev_gate_x_b[e, dirn], ev_lru_lambda[e, dirn])
                h_zero = jnp.zeros((bsz, 1, lw), F32)
                _, h_fin = _rglru_dir(xr_c, None, *common, h_zero, reverse=rev, tn=256)
                y, _ = _rglru_dir(xr_l, y, *common, h_fin, reverse=rev, tn=512)
            x1, t = _outproj_even(y, g_l, o_l, x, ev_w_out[e].astype(BF16), g1, ln_g[l, 0], ln_b[l, 0],
                                  sh2, sc2, alpha=alpha, tm=1024)
        else:
            o = l // 2
            gd = d // fnet_groups
            zr, zi = _chan_dft(x, sh1, sc1, _chan_dft_table(gd), groups=fnet_groups, tm=1024)
            wv = _tok_fft(zr, zi)
            norm = 1.0 / math.sqrt(float(n_lat * gd))
            x1, t = _outproj_odd(wv, x, od_w_out[o].astype(BF16), od_b_out[o], g1, ln_g[l, 0], ln_b[l, 0],
                                 sh2, sc2, alpha=alpha, norm=norm, tm=1024)
        x = _hier_moe_layer(x1, t, g2, ln_g[l, 1], ln_b[l, 1], moe_wg[l], moe_bg[l], moe_wf[l], moe_bf[l],
                            moe_w1, moe_w3, moe_w2, layer=l, alpha=alpha)
    return x
```

```python
import functools
import math

import numpy as np
import jax
import jax.numpy as jnp
from jax import lax
from jax.experimental import pallas as pl
from jax.experimental.pallas import tpu as pltpu

F32 = jnp.float32
BF16 = jnp.bfloat16
I32 = jnp.int32

LN_EPS = 1e-6
LRU_C = 8.0
ROPE_BASE = 10000.0
GRID_W = 64
CONV_W = 4
LANES = 128
SUBLANES = 8
MOE_ROWS = 512
MOE_TILE = 256
MOE_CHUNK = 8
VMEM_LIMIT = 56 * 1024 * 1024


def _cp(sem, vmem=None):
    return pltpu.CompilerParams(dimension_semantics=sem, vmem_limit_bytes=vmem)


def _ln(x):
    mu = jnp.mean(x, axis=-1, keepdims=True)
    xc = x - mu
    var = jnp.mean(xc * xc, axis=-1, keepdims=True)
    return xc * lax.rsqrt(var + LN_EPS)


def _silu(x):
    return x * jax.nn.sigmoid(x)


PACK_SUB = 4


def _tile_row(r, mult):
    if isinstance(r, int):
        return r * PACK_SUB
    return pl.multiple_of(r * PACK_SUB, mult)


def _pack_rows(val):
    half = val.shape[1] // 2
    lo = lax.bitcast_convert_type(val[:, 0:half], jnp.uint32) >> 16
    hi = lax.bitcast_convert_type(val[:, half:], jnp.uint32) & jnp.uint32(0xFFFF0000)
    return lo | hi


def _unpack_rows(words):
    lo = lax.bitcast_convert_type(words << 16, F32)
    hi = lax.bitcast_convert_type(words & jnp.uint32(0xFFFF0000), F32)
    return jnp.concatenate([lo, hi], axis=1).astype(BF16)


def _wait_chunks(copy_of, n, nmax):
    b = 1
    while b <= nmax:
        @pl.when((n & b) != 0)
        def _(b=b):
            copy_of(b).wait()
        b *= 2


ISSUE_UNROLL = 4


def _for_each_chunk(n, start_one):
    groups = n // ISSUE_UNROLL

    def group(g, carry):
        for u in range(ISSUE_UNROLL):
            start_one(g * ISSUE_UNROLL + u)
        return carry
    lax.fori_loop(0, groups, group, 0)

    def single(q, carry):
        start_one(q)
        return carry
    lax.fori_loop(groups * ISSUE_UNROLL, n, single, 0)


def _rows_from_tiles(ref, nrows):
    return jnp.concatenate([ref[pl.ds(k, nrows, stride=PACK_SUB), :] for k in range(PACK_SUB)], axis=1)


def _rows_to_tiles(ref, val):
    nrows = val.shape[0]
    for k in range(PACK_SUB):
        ref[pl.ds(k, nrows, stride=PACK_SUB), :] = val[:, k * LANES:(k + 1) * LANES]


def _ada_kernel(c_ref, w_ref, b_ref, o_ref):
    s = _silu(c_ref[...]).astype(BF16)
    o_ref[...] = jnp.dot(s, w_ref[...].astype(BF16), preferred_element_type=F32) + b_ref[...]


def _ada_terms(cond, ada_w, ada_b):
    nl, d, d6 = ada_w.shape
    r = cond.shape[0]
    tn = 1024
    return pl.pallas_call(
        _ada_kernel,
        out_shape=jax.ShapeDtypeStruct((nl, r, d6), F32),
        grid=(nl, d6 // tn),
        in_specs=[pl.BlockSpec((r, d), lambda l, j: (0, 0)),
                  pl.BlockSpec((None, d, tn), lambda l, j: (l, 0, j)),
                  pl.BlockSpec((None, 1, tn), lambda l, j: (l, 0, j))],
        out_specs=pl.BlockSpec((None, r, tn), lambda l, j: (l, 0, j)),
        compiler_params=_cp(("parallel", "parallel")),
        name="ada_terms",
    )(cond, ada_w, ada_b.reshape(nl, 1, d6))


def _rope_apply(x, cos, sin_signed):
    tm = x.shape[0]
    lane = lax.broadcasted_iota(I32, (tm, LANES), 1)
    first_half = (lane % 32) < 16
    outs = []
    for j in range(x.shape[1] // LANES):
        xh = x[:, j * LANES:(j + 1) * LANES]
        partner = jnp.where(first_half, pltpu.roll(xh, LANES - 16, 1), pltpu.roll(xh, 16, 1))
        outs.append(xh * cos + partner * sin_signed)
    return jnp.concatenate(outs, axis=1)


def _proj_kernel(x_ref, sh_ref, sc_ref, w_ref, cos_ref, sin_ref,
                 g_ref, xr_ref, q_ref, k_ref, v_ref, *, rope, lw, aw, qscale):
    h = _ln(x_ref[...]) * (1.0 + sc_ref[...]) + sh_ref[...]
    hb = h.astype(BF16)

    def mm(c0, c1):
        return jnp.dot(hb, w_ref[:, c0:c1], preferred_element_type=F32)

    g_ref[...] = mm(0, lw).astype(BF16)
    xr_ref[...] = mm(lw, 2 * lw)
    q = mm(2 * lw, 2 * lw + aw)
    k = mm(2 * lw + aw, 2 * lw + 2 * aw)
    if rope:
        q = _rope_apply(q, cos_ref[...], sin_ref[...])
        k = _rope_apply(k, cos_ref[...], sin_ref[...])
    q_ref[...] = (q * qscale).astype(BF16)
    k_ref[...] = k.astype(BF16)
    v_ref[...] = mm(2 * lw + 2 * aw, 2 * lw + 3 * aw).astype(BF16)


def _project_even(x, shift, scale, w_in_b, cos_t, sin_t, *, rope, lw, aw, qscale, tm):
    b, n, d = x.shape
    tm = min(tm, n)
    nin = w_in_b.shape[1]
    tok = lambda bi, i: (bi, i, 0)
    per_b = lambda bi, i: (bi, 0, 0)
    outs = (jax.ShapeDtypeStruct((b, n, lw), BF16), jax.ShapeDtypeStruct((b, n, lw), F32),
            jax.ShapeDtypeStruct((b, n, aw), BF16), jax.ShapeDtypeStruct((b, n, aw), BF16),
            jax.ShapeDtypeStruct((b, n, aw), BF16))
    return pl.pallas_call(
        functools.partial(_proj_kernel, rope=rope, lw=lw, aw=aw, qscale=qscale),
        out_shape=outs,
        grid=(b, n // tm),
        in_specs=[pl.BlockSpec((None, tm, d), tok),
                  pl.BlockSpec((None, 1, d), per_b),
                  pl.BlockSpec((None, 1, d), per_b),
                  pl.BlockSpec((d, nin), lambda bi, i: (0, 0)),
                  pl.BlockSpec((tm, LANES), lambda bi, i: (i, 0)),
                  pl.BlockSpec((tm, LANES), lambda bi, i: (i, 0))],
        out_specs=(pl.BlockSpec((None, tm, lw), tok), pl.BlockSpec((None, tm, lw), tok),
                   pl.BlockSpec((None, tm, aw), tok), pl.BlockSpec((None, tm, aw), tok),
                   pl.BlockSpec((None, tm, aw), tok)),
        compiler_params=_cp(("parallel", "parallel"), VMEM_LIMIT),
        name="proj_even_rope" if rope else "proj_even_ctx",
    )(x, shift, scale, w_in_b, cos_t, sin_t)


def _rope_tables(n_tok, head_dim):
    t = jnp.arange(n_tok)
    row = (t // GRID_W).astype(F32)
    col = (t % GRID_W).astype(F32)
    nf = head_dim // 4
    freqs = ROPE_BASE ** (-jnp.arange(nf, dtype=F32) / nf)
    lane = np.arange(LANES)
    within = lane % head_dim
    axis = within // (2 * nf)
    half = (within % (2 * nf)) // nf
    f = within % nf
    pos = jnp.where(jnp.asarray(axis)[None, :] == 0, row[:, None], col[:, None])
    ang = pos * freqs[jnp.asarray(f)][None, :]
    sign = jnp.asarray(np.where(half == 0, -1.0, 1.0), F32)[None, :]
    return jnp.cos(ang).astype(F32), (jnp.sin(ang) * sign).astype(F32)


def _attn_kernel(q_ref, kc_ref, kl_ref, vc_ref, vl_ref, dl_ref, gain_ref, o_ref, kbuf, vbuf, sbuf, ebuf, abuf,
                 cbuf, *, nc, nl, hd, lam_init, rows):
    @pl.when((pl.program_id(0) == 0) & (pl.program_id(1) == 0))
    def _():
        sbuf[...] = jnp.zeros_like(sbuf)
        abuf[...] = jnp.zeros_like(abuf)
        cbuf[...] = jnp.zeros_like(cbuf)

    kbuf[0:nc, :] = kc_ref[...]
    kbuf[nc:nc + nl, :] = kl_ref[...]
    vbuf[0:nc, :] = vc_ref[...]
    vbuf[nc:nc + nl, :] = vl_ref[...]

    lf = dl_ref[...]
    lam = (jnp.exp(jnp.sum(lf[0:1] * lf[1:2], axis=1, keepdims=True))
           - jnp.exp(jnp.sum(lf[2:3] * lf[3:4], axis=1, keepdims=True)) + lam_init)
    gain = gain_ref[...] * (1.0 - lam_init)
    n_sub = nl // rows
    lane = lax.broadcasted_iota(I32, (rows, 2 * hd), 1)
    nt = (((1,), (1,)), ((), ()))

    def stage_a(j, slot):
        r0 = pl.multiple_of(jnp.minimum(j, n_sub - 1) * rows, rows)
        q = q_ref[pl.ds(r0, rows), :]
        zero = jnp.zeros_like(q)
        kk = kbuf[...]
        sbuf[slot, 0] = lax.dot_general(jnp.where(lane < hd, q, zero), kk, nt, preferred_element_type=F32)
        sbuf[slot, 1] = lax.dot_general(jnp.where(lane >= hd, q, zero), kk, nt, preferred_element_type=F32)

    def stage_b(slot):
        cols = [slice(c, c + LANES) for c in range(0, nc + nl, LANES)]
        ls = []
        for k in range(2):
            pm = sbuf[slot, k, :, cols[0]]
            for cs in cols[1:]:
                pm = jnp.maximum(pm, sbuf[slot, k, :, cs])
            m = jnp.max(pm, axis=1, keepdims=True)
            acc = jnp.zeros((rows, LANES), F32)
            for cs in cols:
                e = jnp.exp2(sbuf[slot, k, :, cs] - m)
                acc = acc + e
                ebuf[slot, k, :, cs] = e.astype(BF16)
            ls.append(jnp.sum(acc, axis=1, keepdims=True))
        ratio = (lam * ls[0] / ls[1]).astype(BF16)
        abuf[slot] = ebuf[slot, 0] - ratio * ebuf[slot, 1]
        cbuf[slot] = jnp.broadcast_to(1.0 / ls[0], cbuf.shape[1:])

    def stage_c(j, slot):
        r0 = pl.multiple_of(jnp.maximum(j - 2, 0) * rows, rows)
        o = jnp.dot(abuf[slot], vbuf[...], preferred_element_type=F32) * cbuf[slot]
        o = o * lax.rsqrt(jnp.mean(o * o, axis=1, keepdims=True) + LN_EPS) * gain
        o_ref[pl.ds(r0, rows), :] = o.astype(BF16)

    def body(t, carry):
        j = 2 * t
        stage_a(j, 0)
        stage_b(1)
        stage_c(j, 0)
        stage_a(j + 1, 1)
        stage_b(0)
        stage_c(j + 1, 1)
        return carry

    lax.fori_loop(0, (n_sub + 2) // 2, body, 0)


def _diff_attention(q, k_ctx, k_lat, v_ctx, v_lat, da_lambda, subln, *, heads, hd, lam_init):
    b, n, aw = q.shape
    nc = k_ctx.shape[1]
    vd = aw // heads
    rows = min(128, n)
    assert (n // rows) % 2 == 0
    blk_q = pl.BlockSpec((None, n, vd), lambda bi, h: (bi, 0, h))
    blk_c = pl.BlockSpec((None, nc, vd), lambda bi, h: (bi, 0, h))
    return pl.pallas_call(
        functools.partial(_attn_kernel, nc=nc, nl=n, hd=hd, lam_init=lam_init, rows=rows),
        out_shape=jax.ShapeDtypeStruct((b, n, aw), BF16),
        grid=(b, heads),
        in_specs=[blk_q, blk_c, blk_q, blk_c, blk_q,
                  pl.BlockSpec(da_lambda.shape, lambda bi, h: (0, 0)),
                  pl.BlockSpec((1, vd), lambda bi, h: (0, 0))],
        out_specs=blk_q,
        scratch_shapes=[pltpu.VMEM((nc + n, vd), BF16), pltpu.VMEM((nc + n, vd), BF16),
                        pltpu.VMEM((2, 2, rows, nc + n), F32), pltpu.VMEM((2, 2, rows, nc + n), BF16),
                        pltpu.VMEM((2, rows, nc + n), BF16), pltpu.VMEM((2, rows, vd), F32)],
        compiler_params=_cp(("arbitrary", "arbitrary"), VMEM_LIMIT),
        name="diff_attention",
    )(q, k_ctx, k_lat, v_ctx, v_lat, da_lambda, subln.reshape(1, vd))


def _lru_kernel(*refs, reverse, add_prev, nt, groups):
    if add_prev:
        (xp_ref, xc_ref, xn_ref, yprev_ref, cw_ref, cb_ref, wa_ref, ba_ref, wx_ref, bx_ref,
         lam_ref, h0_ref, y_ref, hf_ref, carry, ext) = refs
    else:
        (xp_ref, xc_ref, xn_ref, cw_ref, cb_ref, wa_ref, ba_ref, wx_ref, bx_ref,
         lam_ref, h0_ref, y_ref, hf_ref, carry, ext) = refs
        yprev_ref = None
    i = pl.program_id(1)
    ti = (nt - 1 - i) if reverse else i
    tn, w = xc_ref.shape

    @pl.when(i == 0)
    def _():
        carry[...] = h0_ref[...]

    ext[0:SUBLANES, :] = jnp.where(ti > 0, xp_ref[...], 0.0)
    ext[SUBLANES:SUBLANES + tn, :] = xc_ref[...]
    ext[SUBLANES + tn:2 * SUBLANES + tn, :] = jnp.where(ti < nt - 1, xn_ref[...], 0.0)
    left = CONV_W // 2
    xc = cb_ref[...]
    for k in range(CONV_W):
        off = SUBLANES - left + k
        xc = xc + ext[off:off + tn, :] * cw_ref[k:k + 1, :]

    xb = xc.astype(BF16)
    gw = w // groups

    def gate(w_ref, b_ref):
        parts = [jnp.dot(xb[:, g * gw:(g + 1) * gw], w_ref[g], preferred_element_type=F32)
                 for g in range(groups)]
        return jax.nn.sigmoid(jnp.concatenate(parts, axis=1) + b_ref[...])

    r = gate(wa_ref, ba_ref)
    ig = gate(wx_ref, bx_ref)
    log_a = (-LRU_C * jax.nn.softplus(-lam_ref[...])) * r
    a = jnp.exp(log_a)
    bcoef = jnp.sqrt(-jnp.tanh(log_a) * (a * a + 1.0)) * ig * xc

    ngroups = tn // SUBLANES
    a = a.reshape(ngroups, SUBLANES, w)
    bcoef = bcoef.reshape(ngroups, SUBLANES, w)
    row = lax.broadcasted_iota(I32, (ngroups, SUBLANES, w), 1)
    d = 1
    while d < SUBLANES:
        shift = (SUBLANES - d) if reverse else d
        a_sh = pltpu.roll(a, shift, 1)
        b_sh = pltpu.roll(bcoef, shift, 1)
        live = (row < SUBLANES - d) if reverse else (row >= d)
        bcoef = jnp.where(live, a * b_sh + bcoef, bcoef)
        a = jnp.where(live, a * a_sh, a)
        d *= 2
    hc = carry[...]
    npairs = ngroups // 2
    for p in (range(npairs - 1, -1, -1) if reverse else range(npairs)):
        hs = {}
        for g in ((2 * p + 1, 2 * p) if reverse else (2 * p, 2 * p + 1)):
            h = a[g] * hc + bcoef[g]
            hc = h[0:1, :] if reverse else h[SUBLANES - 1:SUBLANES, :]
            hs[g] = h
        rs = slice(2 * p * SUBLANES, (2 * p + 2) * SUBLANES)
        h2 = jnp.concatenate([hs[2 * p], hs[2 * p + 1]], axis=0)
        if add_prev:
            h2 = yprev_ref[rs, :].astype(F32) + h2
        y_ref[rs, :] = h2.astype(y_ref.dtype)
    carry[...] = hc
    hf_ref[...] = hc


def _rglru_dir(xr, y_prev, conv_w, conv_b, wa_bd, ba, wx_bd, bx, lam, h0, *, reverse, tn):
    b, n, w = xr.shape
    tn = min(tn, n)
    nt = n // tn
    groups = wa_bd.shape[0]
    nb8 = n // SUBLANES
    per8 = tn // SUBLANES

    def tmap(i):
        return (nt - 1 - i) if reverse else i

    cur = pl.BlockSpec((None, tn, w), lambda bi, i: (bi, tmap(i), 0))
    halo_p = pl.BlockSpec((None, SUBLANES, w), lambda bi, i: (bi, jnp.maximum(tmap(i) * per8 - 1, 0), 0))
    halo_n = pl.BlockSpec((None, SUBLANES, w), lambda bi, i: (bi, jnp.minimum((tmap(i) + 1) * per8, nb8 - 1), 0))
    row_w = pl.BlockSpec((1, w), lambda bi, i: (0, 0))
    per_b = pl.BlockSpec((None, 1, w), lambda bi, i: (bi, 0, 0))
    gate_w = pl.BlockSpec(wa_bd.shape, lambda bi, i: (0, 0, 0))
    add_prev = y_prev is not None
    in_specs = [halo_p, cur, halo_n] + ([cur] if add_prev else []) + [
        pl.BlockSpec((CONV_W, w), lambda bi, i: (0, 0)), row_w, gate_w, row_w, gate_w, row_w, row_w, per_b]
    args = [xr, xr, xr] + ([y_prev] if add_prev else []) + [
        conv_w, conv_b.reshape(1, w), wa_bd, ba.reshape(1, w), wx_bd, bx.reshape(1, w),
        lam.reshape(1, w), h0]
    return pl.pallas_call(
        functools.partial(_lru_kernel, reverse=reverse, add_prev=add_prev, nt=nt, groups=groups),
        out_shape=(jax.ShapeDtypeStruct((b, n, w), BF16), jax.ShapeDtypeStruct((b, 1, w), F32)),
        grid=(b, nt),
        in_specs=in_specs,
        out_specs=(cur, per_b),
        scratch_shapes=[pltpu.VMEM((1, w), F32), pltpu.VMEM((tn + 2 * SUBLANES, w), F32)],
        compiler_params=_cp(("parallel", "arbitrary"), VMEM_LIMIT),
        name="rglru_rev" if reverse else "rglru_fwd",
    )(*args)


def _block_diag_groups(wh, group_width):
    heads, blk, _ = wh.shape
    per = group_width // blk
    groups = heads // per
    whg = wh.reshape(groups, per, blk, blk)
    eye = jnp.eye(per, dtype=wh.dtype)
    bd = jnp.einsum('gpij,pq->gpiqj', whg, eye).reshape(groups, group_width, group_width)
    return bd.astype(BF16)


def _residual_ln_mod(x, y, g1, lng, lnb, sh2, sc2, alpha):
    x1 = _ln(alpha * x + g1 * y) * lng + lnb
    t = _ln(x1) * (1.0 + sc2) + sh2
    return x1, t


def _outproj_even_kernel(r_ref, g_ref, o_ref, x_ref, w_ref, g1_ref, lng_ref, lnb_ref, sh2_ref, sc2_ref,
                         x1_ref, t_ref, *, lw, alpha):
    z = (r_ref[...].astype(F32) * jax.nn.gelu(g_ref[...].astype(F32))).astype(BF16)
    y = (jnp.dot(z, w_ref[0:lw, :], preferred_element_type=F32)
         + jnp.dot(o_ref[...], w_ref[lw:, :], preferred_element_type=F32))
    x1, t = _residual_ln_mod(x_ref[...], y, g1_ref[...], lng_ref[...], lnb_ref[...],
                             sh2_ref[...], sc2_ref[...], alpha)
    x1_ref[...] = x1
    t_ref[...] = t.astype(BF16)


def _outproj_even(r, g, o, x, w_out_b, g1, lng, lnb, sh2, sc2, *, alpha, tm):
    b, n, d = x.shape
    lw = r.shape[2]
    aw = o.shape[2]
    tm = min(tm, n)
    tok = lambda bi, i: (bi, i, 0)
    per_b = pl.BlockSpec((None, 1, d), lambda bi, i: (bi, 0, 0))
    row = pl.BlockSpec((1, d), lambda bi, i: (0, 0))
    return pl.pallas_call(
        functools.partial(_outproj_even_kernel, lw=lw, alpha=alpha),
        out_shape=(jax.ShapeDtypeStruct((b, n, d), F32), jax.ShapeDtypeStruct((b, n, d), BF16)),
        grid=(b, n // tm),
        in_specs=[pl.BlockSpec((None, tm, lw), tok), pl.BlockSpec((None, tm, lw), tok),
                  pl.BlockSpec((None, tm, aw), tok), pl.BlockSpec((None, tm, d), tok),
                  pl.BlockSpec(w_out_b.shape, lambda bi, i: (0, 0)),
                  per_b, row, row, per_b, per_b],
        out_specs=(pl.BlockSpec((None, tm, d), tok), pl.BlockSpec((None, tm, d), tok)),
        compiler_params=_cp(("parallel", "parallel"), VMEM_LIMIT),
        name="outproj_even",
    )(r, g, o, x, w_out_b, g1, lng.reshape(1, d), lnb.reshape(1, d), sh2, sc2)


def _chan_dft_kernel(x_ref, sh_ref, sc_ref, tab_ref, zr_ref, zi_ref, *, groups, gd):
    h = (_ln(x_ref[...]) * (1.0 + sc_ref[...]) + sh_ref[...]).astype(BF16)
    for g in range(groups):
        z = jnp.dot(h[:, g * gd:(g + 1) * gd], tab_ref[...], preferred_element_type=F32)
        zr_ref[:, g * gd:(g + 1) * gd] = z[:, 0:gd].astype(BF16)
        zi_ref[:, g * gd:(g + 1) * gd] = z[:, gd:2 * gd].astype(BF16)


def _chan_dft(x, shift, scale, tab, *, groups, tm):
    b, n, d = x.shape
    gd = d // groups
    tm = min(tm, n)
    tok = lambda bi, i: (bi, i, 0)
    per_b = pl.BlockSpec((None, 1, d), lambda bi, i: (bi, 0, 0))
    return pl.pallas_call(
        functools.partial(_chan_dft_kernel, groups=groups, gd=gd),
        out_shape=(jax.ShapeDtypeStruct((b, n, d), BF16), jax.ShapeDtypeStruct((b, n, d), BF16)),
        grid=(b, n // tm),
        in_specs=[pl.BlockSpec((None, tm, d), tok), per_b, per_b,
                  pl.BlockSpec(tab.shape, lambda bi, i: (0, 0))],
        out_specs=(pl.BlockSpec((None, tm, d), tok), pl.BlockSpec((None, tm, d), tok)),
        compiler_params=_cp(("parallel", "parallel"), VMEM_LIMIT),
        name="chan_dft",
    )(x, shift, scale, tab)


FFT_C = 64


def _fft_pitch(group):
    p = -(-group // SUBLANES)
    return SUBLANES * (p if p % 2 else p + 1)


def _tok_fft_kernel(zr_ref, zi_ref, m1_ref, m3_ref, tc_ref, ts_ref, o_ref, zsr, zsi, asr, asi, ob, *, nr):
    c_len = FFT_C
    pz = zsr.shape[0] // nr
    pa = asr.shape[0] // c_len
    for r in range(nr):
        zsr[pz * r:pz * r + c_len, :] = zr_ref[c_len * r:c_len * (r + 1), :].astype(F32)
        zsi[pz * r:pz * r + c_len, :] = zi_ref[c_len * r:c_len * (r + 1), :].astype(F32)
    m1 = m1_ref[...]
    for c in range(c_len):
        x2 = jnp.concatenate([zsr[pl.ds(c, nr, stride=pz), :], zsi[pl.ds(c, nr, stride=pz), :]], axis=0)
        a2 = jnp.dot(m1, x2.astype(BF16), preferred_element_type=F32)
        ar, ai = a2[0:nr], a2[nr:2 * nr]
        tcv = tc_ref[c * nr:(c + 1) * nr, :]
        tsv = ts_ref[c * nr:(c + 1) * nr, :]
        asr[pa * c:pa * c + nr, :] = ar * tcv + ai * tsv
        asi[pa * c:pa * c + nr, :] = ai * tcv - ar * tsv
    m3 = m3_ref[...]
    for k1 in range(nr):
        y2 = jnp.concatenate([asr[pl.ds(k1, c_len, stride=pa), :], asi[pl.ds(k1, c_len, stride=pa), :]], axis=0)
        ob[pl.ds(k1, c_len, stride=pa), :] = jnp.dot(m3, y2.astype(BF16), preferred_element_type=F32)
    for k2 in range(c_len):
        o_ref[nr * k2:nr * (k2 + 1), :] = ob[pa * k2:pa * k2 + nr, :].astype(BF16)


def _tok_fft(zr, zi):
    b, n, d = zr.shape
    nr = n // FFT_C
    pz = _fft_pitch(FFT_C)
    pa = _fft_pitch(nr)
    kr = np.arange(nr, dtype=np.float64)
    ang_r = 2.0 * np.pi * (np.outer(kr, kr) % nr) / nr
    cr, sr = np.cos(ang_r), np.sin(ang_r)
    m1 = jnp.asarray(np.block([[cr, -sr], [-sr, -cr]]), F32).astype(BF16)
    kc = np.arange(FFT_C, dtype=np.float64)
    ang_c = 2.0 * np.pi * (np.outer(kc, kc) % FFT_C) / FFT_C
    m3 = jnp.asarray(np.concatenate([np.cos(ang_c), np.sin(ang_c)], axis=1), F32).astype(BF16)
    ang_t = 2.0 * np.pi * (np.outer(kc, kr) % n) / n
    tc = jnp.broadcast_to(jnp.asarray(np.cos(ang_t).reshape(FFT_C * nr, 1), F32), (FFT_C * nr, LANES))
    ts = jnp.broadcast_to(jnp.asarray(np.sin(ang_t).reshape(FFT_C * nr, 1), F32), (FFT_C * nr, LANES))
    slab = pl.BlockSpec((None, n, LANES), lambda bi, l: (bi, 0, l))
    const = lambda a: pl.BlockSpec(a.shape, lambda bi, l: (0, 0))
    return pl.pallas_call(
        functools.partial(_tok_fft_kernel, nr=nr),
        out_shape=jax.ShapeDtypeStruct((b, n, d), BF16),
        grid=(b, d // LANES),
        in_specs=[slab, slab, const(m1), const(m3), const(tc), const(ts)],
        out_specs=slab,
        scratch_shapes=[pltpu.VMEM((nr * pz, LANES), F32), pltpu.VMEM((nr * pz, LANES), F32),
                        pltpu.VMEM((FFT_C * pa, LANES), F32), pltpu.VMEM((FFT_C * pa, LANES), F32),
                        pltpu.VMEM((FFT_C * pa, LANES), F32)],
        compiler_params=_cp(("parallel", "parallel"), VMEM_LIMIT),
        name="tok_fft",
    )(zr, zi, m1, m3, tc, ts)


def _chan_dft_table(gd):
    c = np.arange(gd, dtype=np.float64)
    ang_c = 2.0 * np.pi * (np.outer(c, c) % gd) / gd
    return jnp.asarray(np.concatenate([np.cos(ang_c), np.sin(ang_c)], axis=1), F32).astype(BF16)


def _outproj_odd_kernel(wv_ref, x_ref, w_ref, b_ref, g1_ref, lng_ref, lnb_ref, sh2_ref, sc2_ref,
                        x1_ref, t_ref, *, alpha, norm):
    y = jnp.dot(wv_ref[...], w_ref[...], preferred_element_type=F32) * norm + b_ref[...]
    x1, t = _residual_ln_mod(x_ref[...], y, g1_ref[...], lng_ref[...], lnb_ref[...],
                             sh2_ref[...], sc2_ref[...], alpha)
    x1_ref[...] = x1
    t_ref[...] = t.astype(BF16)


def _outproj_odd(wv, x, w_b, bias, g1, lng, lnb, sh2, sc2, *, alpha, norm, tm):
    b, n, d = x.shape
    tm = min(tm, n)
    tok = lambda bi, i: (bi, i, 0)
    per_b = pl.BlockSpec((None, 1, d), lambda bi, i: (bi, 0, 0))
    row = pl.BlockSpec((1, d), lambda bi, i: (0, 0))
    return pl.pallas_call(
        functools.partial(_outproj_odd_kernel, alpha=alpha, norm=norm),
        out_shape=(jax.ShapeDtypeStruct((b, n, d), F32), jax.ShapeDtypeStruct((b, n, d), BF16)),
        grid=(b, n // tm),
        in_specs=[pl.BlockSpec((None, tm, d), tok), pl.BlockSpec((None, tm, d), tok),
                  pl.BlockSpec(w_b.shape, lambda bi, i: (0, 0)), row,
                  per_b, row, row, per_b, per_b],
        out_specs=(pl.BlockSpec((None, tm, d), tok), pl.BlockSpec((None, tm, d), tok)),
        compiler_params=_cp(("parallel", "parallel"), VMEM_LIMIT),
        name="outproj_odd",
    )(wv, x, w_b, bias.reshape(1, d), g1, lng.reshape(1, d), lnb.reshape(1, d), sh2, sc2)


def _route_kernel(t_ref, w_ref, b_ref, up_ref, lp_ref, tokm_ref, cnt_ref, *, ng, epg, chunk):
    tm = t_ref.shape[0]
    ne = ng * epg
    logits = jnp.dot(t_ref[...].astype(BF16), w_ref[...], preferred_element_type=F32) + b_ref[...]
    lt = logits.T
    best = lt[0:1, :]
    bi = jnp.zeros((1, tm), I32)
    for k in range(1, ng):
        gk = lt[k:k + 1, :]
        upd = gk > best
        bi = jnp.where(upd, k, bi)
        best = jnp.where(upd, gk, best)
    den = jnp.zeros((1, tm), F32)
    for k in range(ng):
        den = den + jnp.exp(lt[k:k + 1, :] - best)
    p_g = 1.0 / den
    fsel = lt[SUBLANES:SUBLANES + epg, :]
    for k in range(1, ng):
        fsel = jnp.where(bi == k, lt[SUBLANES + k * epg:SUBLANES + (k + 1) * epg, :], fsel)
    neg = jnp.full((1, tm), -jnp.inf, F32)
    m1, m2 = neg, neg
    i1 = jnp.zeros((1, tm), I32)
    i2 = jnp.zeros((1, tm), I32)
    for j in range(epg):
        v = fsel[j:j + 1, :]
        gt1 = v > m1
        gt2 = v > m2
        m2 = jnp.where(gt1, m1, jnp.where(gt2, v, m2))
        i2 = jnp.where(gt1, i1, jnp.where(gt2, j, i2))
        m1 = jnp.where(gt1, v, m1)
        i1 = jnp.where(gt1, j, i1)
    e21 = jnp.exp(m2 - m1)
    w1 = p_g / (1.0 + e21)
    w2 = p_g * e21 / (1.0 + e21)
    e1 = bi * epg + i1
    e2 = bi * epg + i2

    e = jnp.concatenate([e1, e2], axis=1)
    rows = lax.broadcasted_iota(I32, (ne, 2 * tm), 0)
    onehot = jnp.where(rows == e, 1.0, 0.0)
    before = jnp.dot(onehot.astype(BF16), up_ref[...], preferred_element_type=F32)
    tot = jnp.sum(onehot, axis=1, keepdims=True)
    slots = jnp.floor((tot + (chunk - 1.0)) * (1.0 / chunk)) * chunk
    slots_b = jnp.broadcast_to(slots, (ne, LANES))
    rowe = lax.broadcasted_iota(I32, (ne, LANES), 0)
    incl = slots_b
    d = 1
    while d < ne:
        incl = incl + jnp.where(rowe >= d, pltpu.roll(incl, d, 0), 0.0)
        d *= 2
    seg_off = jnp.tile(incl - slots_b, (1, 2 * tm // LANES))
    lpos = jnp.sum(onehot * (before + seg_off), axis=0, keepdims=True)
    lp0 = lpos[:, 0:tm]
    lp1 = lpos[:, tm:2 * tm]
    row8 = lax.broadcasted_iota(I32, (SUBLANES, tm), 0)
    lp_ref[...] = jnp.where(row8 == 0, lp0, jnp.where(row8 == 1, lp1, 0.0)).astype(I32)
    rowl = lax.broadcasted_iota(I32, (LANES, tm), 0)
    tokm = jnp.where(rowl == 0, w1, jnp.where(rowl == 1, w2, jnp.where(rowl == 2, lp0, jnp.where(rowl == 3, lp1, 0.0))))
    tokm_ref[...] = tokm.T
    cnt_ref[...] = jnp.broadcast_to(tot, (ne, LANES))


def _route(t, wcat_b, bcat, upper, *, ng, epg, tm, chunk):
    tt, d = t.shape
    ne = ng * epg
    nt = tt // tm
    return pl.pallas_call(
        functools.partial(_route_kernel, ng=ng, epg=epg, chunk=chunk),
        out_shape=(jax.ShapeDtypeStruct((SUBLANES, tt), I32), jax.ShapeDtypeStruct((tt, LANES), F32),
                   jax.ShapeDtypeStruct((ne, nt * LANES), F32)),
        grid=(nt,),
        in_specs=[pl.BlockSpec((tm, d), lambda i: (i, 0)),
                  pl.BlockSpec((d, LANES), lambda i: (0, 0)),
                  pl.BlockSpec((1, LANES), lambda i: (0, 0)),
                  pl.BlockSpec(upper.shape, lambda i: (0, 0))],
        out_specs=(pl.BlockSpec((SUBLANES, tm), lambda i: (0, i)),
                   pl.BlockSpec((tm, LANES), lambda i: (i, 0)),
                   pl.BlockSpec((ne, LANES), lambda i: (0, i))),
        compiler_params=_cp(("parallel",), VMEM_LIMIT),
        name="route_sort",
    )(t, wcat_b, bcat, upper)


def _dispatch_kernel(row_ref, tot_ref, zrow_ref, nz_ref, t_ref, lp_ref, xs_ref, stage, zbuf, sems,
                     *, qmax, zmax, chunk, nsteps):
    i = pl.program_id(0)
    slot = i % 2
    sp = stage.shape[1] // PACK_SUB
    tm = t_ref.shape[0]
    crow = chunk * PACK_SUB

    @pl.when(i == 0)
    def _():
        zbuf[...] = jnp.zeros_like(zbuf)
        _for_each_chunk(nz_ref[0], lambda q: pltpu.make_async_copy(
            zbuf, xs_ref.at[pl.ds(_tile_row(zrow_ref[q], SUBLANES), crow)], sems.at[1]).start())
        _wait_chunks(lambda k: pltpu.make_async_copy(xs_ref.at[pl.ds(0, k * crow)], xs_ref.at[pl.ds(0, k * crow)],
                                                     sems.at[1]), nz_ref[0], zmax)

    lp = lp_ref[...]
    prow = lax.broadcasted_iota(I32, (sp, tm), 0)
    perm = jnp.where(prow == lp[0:1, :], 1.0, jnp.where(prow == lp[1:2, :], 1.0, 0.0)).astype(BF16)
    srt = jnp.dot(perm, t_ref[...], preferred_element_type=F32)
    _rows_to_tiles(stage.at[slot], _pack_rows(srt))

    def chunk_copy(sl, src, dst):
        return pltpu.make_async_copy(stage.at[sl, pl.ds(_tile_row(src, crow), crow)],
                                     xs_ref.at[pl.ds(_tile_row(dst, SUBLANES), crow)], sems.at[sl])

    def drain(sl, n):
        def copy_of(k):
            return pltpu.make_async_copy(stage.at[sl, pl.ds(0, k * crow)], xs_ref.at[pl.ds(0, k * crow)], sems.at[sl])
        _wait_chunks(copy_of, n, qmax)

    @pl.when(i >= 1)
    def _():
        drain(1 - slot, tot_ref[jnp.maximum(i - 1, 0)])

    _for_each_chunk(tot_ref[i], lambda q: chunk_copy(slot, pl.multiple_of(q * chunk, chunk),
                                                     row_ref[i * qmax + q]).start())

    @pl.when(i == nsteps - 1)
    def _():
        drain(slot, tot_ref[i])


def _dispatch(chunk_row, tile_nch, zero_row, n_zero, t, lp, *, n_rows, ne, tm, chunk):
    ttot, d = t.shape
    nsteps = ttot // tm
    sp = 2 * tm + ne * chunk
    grid_spec = pltpu.PrefetchScalarGridSpec(
        num_scalar_prefetch=4,
        grid=(nsteps,),
        in_specs=[pl.BlockSpec((tm, d), lambda i, *_: (i, 0)),
                  pl.BlockSpec((SUBLANES, tm), lambda i, *_: (0, i))],
        out_specs=pl.BlockSpec(memory_space=pl.ANY),
        scratch_shapes=[pltpu.VMEM((2, sp * PACK_SUB, LANES), jnp.uint32),
                        pltpu.VMEM((chunk * PACK_SUB, LANES), jnp.uint32), pltpu.SemaphoreType.DMA((2,))],
    )
    assert d == 2 * PACK_SUB * LANES
    return pl.pallas_call(
        functools.partial(_dispatch_kernel, qmax=chunk_row.shape[0] // nsteps, zmax=zero_row.shape[0],
                          chunk=chunk, nsteps=nsteps),
        out_shape=jax.ShapeDtypeStruct((n_rows * PACK_SUB, LANES), jnp.uint32),
        grid_spec=grid_spec,
        compiler_params=_cp(("arbitrary",), VMEM_LIMIT),
        name="moe_dispatch",
    )(chunk_row, tile_nch, zero_row, n_zero, t, lp)


def _moe_kernel(be_ref, bv_ref, nu_ref, xs_ref, w1_ref, w3_ref, w2_ref, y_ref, w13b, w2b, *, ff):
    i = pl.program_id(0)
    half = MOE_ROWS // 2

    def ffn(nrows):
        x = _unpack_rows(_rows_from_tiles(xs_ref, nrows))
        h = jnp.dot(x, w13b[...], preferred_element_type=F32)
        hid = (_silu(h[:, 0:ff]) * h[:, ff:2 * ff]).astype(BF16)
        y = jnp.dot(hid, w2b[...], preferred_element_type=F32)
        _rows_to_tiles(y_ref, _pack_rows(y.astype(BF16).astype(F32)))

    @pl.when(i < nu_ref[0])
    def _():
        prev = be_ref[jnp.maximum(i - 1, 0)]

        @pl.when((i == 0) | (be_ref[i] != prev))
        def _():
            w13b[:, 0:ff] = w1_ref[...].astype(BF16)
            w13b[:, ff:2 * ff] = w3_ref[...].astype(BF16)
            w2b[...] = w2_ref[...].astype(BF16)

        @pl.when(bv_ref[i] > half)
        def _():
            ffn(MOE_ROWS)

        @pl.when(bv_ref[i] <= half)
        def _():
            ffn(half)
            y_ref[half * PACK_SUB:, :] = jnp.zeros((half * PACK_SUB, LANES), y_ref.dtype)

    @pl.when(i >= nu_ref[0])
    def _():
        y_ref[...] = jnp.zeros_like(y_ref)


def _moe_experts(block_e, block_rows, n_used, xs, w1, w3, w2, *, layer):
    d = w1.shape[-2]
    ff = w1.shape[-1]
    blk = MOE_ROWS * PACK_SUB
    nb = xs.shape[0] // blk
    wmap = lambda i, be, bv, nu: (layer, be[i], 0, 0)
    grid_spec = pltpu.PrefetchScalarGridSpec(
        num_scalar_prefetch=3,
        grid=(nb,),
        in_specs=[pl.BlockSpec((blk, LANES), lambda i, be, bv, nu: (i, 0)),
                  pl.BlockSpec((None, None, d, ff), wmap),
                  pl.BlockSpec((None, None, d, ff), wmap),
                  pl.BlockSpec((None, None, ff, d), wmap)],
        out_specs=pl.BlockSpec((blk, LANES), lambda i, be, bv, nu: (i, 0)),
        scratch_shapes=[pltpu.VMEM((d, 2 * ff), BF16), pltpu.VMEM((ff, d), BF16)],
    )
    return pl.pallas_call(
        functools.partial(_moe_kernel, ff=ff),
        out_shape=jax.ShapeDtypeStruct(xs.shape, xs.dtype),
        grid_spec=grid_spec,
        compiler_params=_cp(("arbitrary",), VMEM_LIMIT),
        name="moe_experts",
    )(block_e, block_rows, n_used, xs, w1, w3, w2)


def _combine_kernel(row_ref, tot_ref, yb_ref, x_ref, tokm_ref, g2_ref, lng_ref, lnb_ref,
                    o_ref, stage, sems, *, qmax, chunk, nsteps, alpha):
    i = pl.program_id(0)
    slot = i % 2
    sp = stage.shape[1] // PACK_SUB
    tm = x_ref.shape[0]
    crow = chunk * PACK_SUB

    def chunk_copy(sl, src, dst):
        return pltpu.make_async_copy(yb_ref.at[pl.ds(_tile_row(src, SUBLANES), crow)],
                                     stage.at[sl, pl.ds(_tile_row(dst, crow), crow)], sems.at[sl])

    def issue_tile(step, sl):
        _for_each_chunk(tot_ref[step], lambda q: chunk_copy(sl, row_ref[step * qmax + q],
                                                            pl.multiple_of(q * chunk, chunk)).start())

    @pl.when(i == 0)
    def _():
        stage[...] = jnp.zeros_like(stage)
        issue_tile(0, 0)

    @pl.when(i + 1 < nsteps)
    def _():
        issue_tile(jnp.minimum(i + 1, nsteps - 1), 1 - slot)

    def copy_of(k):
        return pltpu.make_async_copy(yb_ref.at[pl.ds(0, k * crow)], stage.at[slot, pl.ds(0, k * crow)], sems.at[slot])
    _wait_chunks(copy_of, tot_ref[i], qmax)

    tk = tokm_ref[...]
    pos = lax.broadcasted_iota(I32, (tm, sp), 1).astype(F32)
    st = _unpack_rows(_rows_from_tiles(stage.at[slot], sp))
    gsel = jnp.where(pos == tk[:, 2:3], tk[:, 0:1], jnp.where(pos == tk[:, 3:4], tk[:, 1:2], 0.0))
    m = jnp.dot(gsel.astype(BF16), st, preferred_element_type=F32)
    o_ref[...] = _ln(alpha * x_ref[...] + g2_ref[...] * m) * lng_ref[...] + lnb_ref[...]


def _combine(chunk_row, tile_nch, yb, x, tokm, g2, lng, lnb, *, ne, tm, chunk, n_per_batch, alpha):
    ttot, d = x.shape
    nsteps = ttot // tm
    per = n_per_batch // tm
    sp = 2 * tm + ne * chunk
    grid_spec = pltpu.PrefetchScalarGridSpec(
        num_scalar_prefetch=2,
        grid=(nsteps,),
        in_specs=[pl.BlockSpec(memory_space=pl.ANY),
                  pl.BlockSpec((tm, d), lambda i, *_: (i, 0)),
                  pl.BlockSpec((tm, LANES), lambda i, *_: (i, 0)),
                  pl.BlockSpec((None, 1, d), lambda i, *_: (i // per, 0, 0)),
                  pl.BlockSpec((1, d), lambda i, *_: (0, 0)),
                  pl.BlockSpec((1, d), lambda i, *_: (0, 0))],
        out_specs=pl.BlockSpec((tm, d), lambda i, *_: (i, 0)),
        scratch_shapes=[pltpu.VMEM((2, sp * PACK_SUB, LANES), yb.dtype), pltpu.SemaphoreType.DMA((2,))],
    )
    return pl.pallas_call(
        functools.partial(_combine_kernel, qmax=chunk_row.shape[0] // nsteps, chunk=chunk, nsteps=nsteps,
                          alpha=alpha),
        out_shape=jax.ShapeDtypeStruct((ttot, d), F32),
        grid_spec=grid_spec,
        compiler_params=_cp(("arbitrary",), VMEM_LIMIT),
        name="moe_combine",
    )(chunk_row, tile_nch, yb, x, tokm, g2, lng.reshape(1, d), lnb.reshape(1, d))


def _hier_moe_layer(x1, t, g2, lng, lnb, wg, bg, wf, bf, w1, w3, w2, *, layer, alpha):
    b, n, d = x1.shape
    ttot = b * n
    ng = wg.shape[1]
    ne = wf.shape[1]
    epg = ne // ng
    tm = min(MOE_TILE, n)
    chunk = MOE_CHUNK
    nt = ttot // tm
    tflat = t.reshape(ttot, d)
    wcat = jnp.zeros((d, LANES), F32).at[:, 0:ng].set(wg).at[:, SUBLANES:SUBLANES + ne].set(wf).astype(BF16)
    bcat = jnp.zeros((1, LANES), F32).at[0, 0:ng].set(bg).at[0, SUBLANES:SUBLANES + ne].set(bf)
    ar = jnp.arange(2 * tm, dtype=I32)
    upper = (ar[:, None] < ar[None, :]).astype(BF16)
    lp, tokm, cnt = _route(tflat, wcat, bcat, upper, ng=ng, epg=epg, tm=tm, chunk=chunk)

    tile_cnt = cnt.reshape(ne, nt, LANES)[:, :, 0].T.astype(I32)
    seg_len = ((tile_cnt + 1) // 2) * 2
    counts = jnp.sum(seg_len, axis=0)
    padded = ((counts + chunk + MOE_ROWS - 1) // MOE_ROWS) * MOE_ROWS
    pend = jnp.cumsum(padded)
    pstart = pend - padded
    base = jnp.cumsum(seg_len, axis=0) - seg_len
    seg_row = pstart[None, :] + base
    nch = (tile_cnt + chunk - 1) // chunk
    nch_end = jnp.cumsum(nch, axis=1)
    tile_nch = nch_end[:, -1].astype(I32)
    qmax = 2 * tm // chunk + ne
    qs = jnp.arange(qmax, dtype=I32)
    e_q = jnp.minimum(jnp.sum((nch_end[:, None, :] <= qs[None, :, None]).astype(I32), axis=2), ne - 1)
    pick = e_q[:, :, None] == jnp.arange(ne, dtype=I32)[None, None, :]
    first_q = jnp.sum(jnp.where(pick, (nch_end - nch)[:, None, :], 0), axis=2)
    first_row = jnp.sum(jnp.where(pick, seg_row[:, None, :], 0), axis=2)
    chunk_row = (first_row + (qs[None, :] - first_q) * chunk).astype(I32).reshape(nt * qmax)
    nb = -(-(2 * ttot + nt * ne + ne * (chunk + MOE_ROWS)) // MOE_ROWS)
    bstart = jnp.arange(nb, dtype=I32) * MOE_ROWS
    block_e = jnp.minimum(jnp.sum((pend[None, :] <= bstart[:, None]).astype(I32), axis=1), ne - 1)
    n_used = (pend[-1] // MOE_ROWS).astype(I32).reshape(1)
    zstart = jnp.concatenate([pstart + (counts // chunk) * chunk, pend[-1:]])
    zend = jnp.concatenate([pend, jnp.full((1,), nb * MOE_ROWS, pend.dtype)])
    zcnt = (zend - zstart) // chunk
    zcum = jnp.cumsum(zcnt)
    zmax = ne * ((MOE_ROWS + 2 * chunk) // chunk + 1) + (nb - (2 * ttot) // MOE_ROWS) * (MOE_ROWS // chunk)
    zq = jnp.arange(zmax, dtype=I32)
    zseg = jnp.minimum(jnp.sum((zcum[None, :] <= zq[:, None]).astype(I32), axis=1), ne)
    zpick = zseg[:, None] == jnp.arange(ne + 1, dtype=I32)[None, :]
    zfirst = jnp.sum(jnp.where(zpick, (zcum - zcnt)[None, :], 0), axis=1)
    zero_row = (jnp.sum(jnp.where(zpick, zstart[None, :], 0), axis=1) + (zq - zfirst) * chunk).astype(I32)
    n_zero = zcum[-1].astype(I32).reshape(1)

    xs = _dispatch(chunk_row, tile_nch, zero_row, n_zero, tflat, lp, n_rows=nb * MOE_ROWS, ne=ne, tm=tm,
                   chunk=chunk)
    bpick = block_e[:, None] == jnp.arange(ne, dtype=I32)[None, :]
    block_rows = jnp.clip(jnp.sum(jnp.where(bpick, (pstart + counts)[None, :], 0), axis=1) - bstart,
                          0, MOE_ROWS).astype(I32)
    yb = _moe_experts(block_e.astype(I32), block_rows, n_used, xs, w1, w3, w2, layer=layer)
    out = _combine(chunk_row, tile_nch, yb, x1.reshape(ttot, d), tokm, g2, lng, lnb,
                   ne=ne, tm=tm, chunk=chunk, n_per_batch=n, alpha=alpha)
    return out.reshape(b, n, d)


def kernel(x, c, ctx, c_ctx, ada_w, ada_b, ln_g, ln_b, ev_w_in, ev_conv_w, ev_conv_b, ev_gate_a_w,
           ev_gate_a_b, ev_gate_x_w, ev_gate_x_b, ev_lru_lambda, ev_da_lambda, ev_da_subln, ev_w_out,
           od_w_out, od_b_out, moe_wg, moe_bg, moe_wf, moe_bf, moe_w1, moe_w3, moe_w2):
    bsz, n_lat, d = x.shape
    depth = ada_w.shape[0]
    alpha = (2.0 * depth) ** 0.25
    lw = ev_conv_w.shape[-1]
    hd = ev_da_lambda.shape[-1]
    vd = ev_da_subln.shape[-1]
    aw = (ev_w_in.shape[-1] - 2 * lw) // 3
    heads = aw // vd
    fnet_groups = 4

    rows = ((bsz + 1 + SUBLANES - 1) // SUBLANES) * SUBLANES
    cond = jnp.zeros((rows, d), F32).at[0:bsz].set(c).at[bsz].set(c_ctx)
    ada = _ada_terms(cond, ada_w, ada_b).reshape(depth, rows, 6, d)

    def lat_term(l, k):
        return ada[l, 0:bsz, k, :].reshape(bsz, 1, d)

    def ctx_term(l, k):
        return jnp.broadcast_to(ada[l, bsz, k, :].reshape(1, 1, d), (bsz, 1, d))

    for l in range(depth):
        ctx_live = any(m % 2 == 0 for m in range(l + 1, depth))
        assert not ctx_live, "context stream update is only needed for depth > 2"
        sh1, sc1, g1, sh2, sc2, g2 = [lat_term(l, k) for k in range(6)]
        if l % 2 == 0:
            e = l // 2
            lam_init = 0.8 - 0.6 * math.exp(-0.3 * l)
            w_in_b = ev_w_in[e].astype(BF16)
            cos_t, sin_t = _rope_tables(n_lat, hd)
            qscale = hd ** -0.5 * math.log2(math.e)
            g_l, xr_l, q_l, k_l, v_l = _project_even(x, sh1, sc1, w_in_b, cos_t, sin_t, rope=True,
                                                     lw=lw, aw=aw, qscale=qscale, tm=512)
            n_ctx = ctx.shape[1]
            _, xr_c, _, k_c, v_c = _project_even(ctx, ctx_term(l, 0), ctx_term(l, 1), w_in_b,
                                                 cos_t[0:n_ctx], sin_t[0:n_ctx], rope=False,
                                                 lw=lw, aw=aw, qscale=qscale, tm=256)
            o_l = _diff_attention(q_l, k_c, k_l, v_c, v_l, ev_da_lambda[e], ev_da_subln[e],
                                  heads=heads, hd=hd, lam_init=lam_init)
            gwid = 256
            y = None
            for dirn, rev in ((0, False), (1, True)):
                wa_bd = _block_diag_groups(ev_gate_a_w[e, dirn], gwid)
                wx_bd = _block_diag_groups(ev_gate_x_w[e, dirn], gwid)
                common = (ev_conv_w[e], ev_conv_b[e], wa_bd, ev_gate_a_b[e, dirn], wx_bd,
                          ev_gate_x_b[e, dirn], ev_lru_lambda[e, dirn])
                h_zero = jnp.zeros((bsz, 1, lw), F32)
                _, h_fin = _rglru_dir(xr_c, None, *common, h_zero, reverse=rev, tn=256)
                y, _ = _rglru_dir(xr_l, y, *common, h_fin, reverse=rev, tn=512)
            x1, t = _outproj_even(y, g_l, o_l, x, ev_w_out[e].astype(BF16), g1, ln_g[l, 0], ln_b[l, 0],
                                  sh2, sc2, alpha=alpha, tm=1024)
        else:
            o = l // 2
            gd = d // fnet_groups
            zr, zi = _chan_dft(x, sh1, sc1, _chan_dft_table(gd), groups=fnet_groups, tm=1024)
            wv = _tok_fft(zr, zi)
            norm = 1.0 / math.sqrt(float(n_lat * gd))
            x1, t = _outproj_odd(wv, x, od_w_out[o].astype(BF16), od_b_out[o], g1, ln_g[l, 0], ln_b[l, 0],
                                 sh2, sc2, alpha=alpha, norm=norm, tm=1024)
        x = _hier_moe_layer(x1, t, g2, ln_g[l, 1], ln_b[l, 1], moe_wg[l], moe_bg[l], moe_wf[l], moe_bf[l],
                            moe_w1, moe_w3, moe_w2, layer=l, alpha=alpha)
    return x
```

```python
import functools
import math

import numpy as np
import jax
import jax.numpy as jnp
from jax import lax
from jax.experimental import pallas as pl
from jax.experimental.pallas import tpu as pltpu

F32 = jnp.float32
BF16 = jnp.bfloat16
I32 = jnp.int32

LN_EPS = 1e-6
LRU_C = 8.0
ROPE_BASE = 10000.0
GRID_W = 64
CONV_W = 4
LANES = 128
SUBLANES = 8
MOE_ROWS = 512
MOE_TILE = 256
MOE_CHUNK = 8
VMEM_LIMIT = 56 * 1024 * 1024


def _cp(sem, vmem=None):
    return pltpu.CompilerParams(dimension_semantics=sem, vmem_limit_bytes=vmem)


def _ln(x):
    mu = jnp.mean(x, axis=-1, keepdims=True)
    xc = x - mu
    var = jnp.mean(xc * xc, axis=-1, keepdims=True)
    return xc * lax.rsqrt(var + LN_EPS)


def _silu(x):
    return x * jax.nn.sigmoid(x)


PACK_SUB = 4


def _tile_row(r, mult):
    if isinstance(r, int):
        return r * PACK_SUB
    return pl.multiple_of(r * PACK_SUB, mult)


def _pack_rows(val):
    half = val.shape[1] // 2
    lo = lax.bitcast_convert_type(val[:, 0:half], jnp.uint32) >> 16
    hi = lax.bitcast_convert_type(val[:, half:], jnp.uint32) & jnp.uint32(0xFFFF0000)
    return lo | hi


def _unpack_rows(words):
    lo = lax.bitcast_convert_type(words << 16, F32)
    hi = lax.bitcast_convert_type(words & jnp.uint32(0xFFFF0000), F32)
    return jnp.concatenate([lo, hi], axis=1).astype(BF16)


def _wait_chunks(copy_of, n, nmax):
    b = 1
    while b <= nmax:
        @pl.when((n & b) != 0)
        def _(b=b):
            copy_of(b).wait()
        b *= 2


ISSUE_UNROLL = 4


def _for_each_chunk(n, start_one):
    groups = n // ISSUE_UNROLL

    def group(g, carry):
        for u in range(ISSUE_UNROLL):
            start_one(g * ISSUE_UNROLL + u)
        return carry
    lax.fori_loop(0, groups, group, 0)

    def single(q, carry):
        start_one(q)
        return carry
    lax.fori_loop(groups * ISSUE_UNROLL, n, single, 0)


def _rows_from_tiles(ref, nrows):
    return jnp.concatenate([ref[pl.ds(k, nrows, stride=PACK_SUB), :] for k in range(PACK_SUB)], axis=1)


def _rows_to_tiles(ref, val):
    nrows = val.shape[0]
    for k in range(PACK_SUB):
        ref[pl.ds(k, nrows, stride=PACK_SUB), :] = val[:, k * LANES:(k + 1) * LANES]


def _ada_kernel(c_ref, w_ref, b_ref, o_ref):
    s = _silu(c_ref[...]).astype(BF16)
    o_ref[...] = jnp.dot(s, w_ref[...].astype(BF16), preferred_element_type=F32) + b_ref[...]


def _ada_terms(cond, ada_w, ada_b):
    nl, d, d6 = ada_w.shape
    r = cond.shape[0]
    tn = 1024
    return pl.pallas_call(
        _ada_kernel,
        out_shape=jax.ShapeDtypeStruct((nl, r, d6), F32),
        grid=(nl, d6 // tn),
        in_specs=[pl.BlockSpec((r, d), lambda l, j: (0, 0)),
                  pl.BlockSpec((None, d, tn), lambda l, j: (l, 0, j)),
                  pl.BlockSpec((None, 1, tn), lambda l, j: (l, 0, j))],
        out_specs=pl.BlockSpec((None, r, tn), lambda l, j: (l, 0, j)),
        compiler_params=_cp(("parallel", "parallel")),
        name="ada_terms",
    )(cond, ada_w, ada_b.reshape(nl, 1, d6))


def _rope_apply(x, cos, sin_signed):
    tm = x.shape[0]
    lane = lax.broadcasted_iota(I32, (tm, LANES), 1)
    first_half = (lane % 32) < 16
    outs = []
    for j in range(x.shape[1] // LANES):
        xh = x[:, j * LANES:(j + 1) * LANES]
        partner = jnp.where(first_half, pltpu.roll(xh, LANES - 16, 1), pltpu.roll(xh, 16, 1))
        outs.append(xh * cos + partner * sin_signed)
    return jnp.concatenate(outs, axis=1)


def _proj_kernel(x_ref, sh_ref, sc_ref, w_ref, cos_ref, sin_ref,
                 g_ref, xr_ref, q_ref, k_ref, v_ref, *, rope, lw, aw, qscale):
    h = _ln(x_ref[...]) * (1.0 + sc_ref[...]) + sh_ref[...]
    hb = h.astype(BF16)

    def mm(c0, c1):
        return jnp.dot(hb, w_ref[:, c0:c1], preferred_element_type=F32)

    g_ref[...] = mm(0, lw).astype(BF16)
    xr_ref[...] = mm(lw, 2 * lw)
    q = mm(2 * lw, 2 * lw + aw)
    k = mm(2 * lw + aw, 2 * lw + 2 * aw)
    if rope:
        q = _rope_apply(q, cos_ref[...], sin_ref[...])
        k = _rope_apply(k, cos_ref[...], sin_ref[...])
    q_ref[...] = (q * qscale).astype(BF16)
    k_ref[...] = k.astype(BF16)
    v_ref[...] = mm(2 * lw + 2 * aw, 2 * lw + 3 * aw).astype(BF16)


def _project_even(x, shift, scale, w_in_b, cos_t, sin_t, *, rope, lw, aw, qscale, tm):
    b, n, d = x.shape
    tm = min(tm, n)
    nin = w_in_b.shape[1]
    tok = lambda bi, i: (bi, i, 0)
    per_b = lambda bi, i: (bi, 0, 0)
    outs = (jax.ShapeDtypeStruct((b, n, lw), BF16), jax.ShapeDtypeStruct((b, n, lw), F32),
            jax.ShapeDtypeStruct((b, n, aw), BF16), jax.ShapeDtypeStruct((b, n, aw), BF16),
            jax.ShapeDtypeStruct((b, n, aw), BF16))
    return pl.pallas_call(
        functools.partial(_proj_kernel, rope=rope, lw=lw, aw=aw, qscale=qscale),
        out_shape=outs,
        grid=(b, n // tm),
        in_specs=[pl.BlockSpec((None, tm, d), tok),
                  pl.BlockSpec((None, 1, d), per_b),
                  pl.BlockSpec((None, 1, d), per_b),
                  pl.BlockSpec((d, nin), lambda bi, i: (0, 0)),
                  pl.BlockSpec((tm, LANES), lambda bi, i: (i, 0)),
                  pl.BlockSpec((tm, LANES), lambda bi, i: (i, 0))],
        out_specs=(pl.BlockSpec((None, tm, lw), tok), pl.BlockSpec((None, tm, lw), tok),
                   pl.BlockSpec((None, tm, aw), tok), pl.BlockSpec((None, tm, aw), tok),
                   pl.BlockSpec((None, tm, aw), tok)),
        compiler_params=_cp(("parallel", "parallel"), VMEM_LIMIT),
        name="proj_even_rope" if rope else "proj_even_ctx",
    )(x, shift, scale, w_in_b, cos_t, sin_t)


def _rope_tables(n_tok, head_dim):
    t = jnp.arange(n_tok)
    row = (t // GRID_W).astype(F32)
    col = (t % GRID_W).astype(F32)
    nf = head_dim // 4
    freqs = ROPE_BASE ** (-jnp.arange(nf, dtype=F32) / nf)
    lane = np.arange(LANES)
    within = lane % head_dim
    axis = within // (2 * nf)
    half = (within % (2 * nf)) // nf
    f = within % nf
    pos = jnp.where(jnp.asarray(axis)[None, :] == 0, row[:, None], col[:, None])
    ang = pos * freqs[jnp.asarray(f)][None, :]
    sign = jnp.asarray(np.where(half == 0, -1.0, 1.0), F32)[None, :]
    return jnp.cos(ang).astype(F32), (jnp.sin(ang) * sign).astype(F32)


def _attn_kernel(q_ref, kc_ref, kl_ref, vc_ref, vl_ref, dl_ref, gain_ref, o_ref, kbuf, vbuf, sbuf, ebuf, abuf,
                 cbuf, *, nc, nl, hd, lam_init, rows):
    @pl.when((pl.program_id(0) == 0) & (pl.program_id(1) == 0))
    def _():
        sbuf[...] = jnp.zeros_like(sbuf)
        abuf[...] = jnp.zeros_like(abuf)
        cbuf[...] = jnp.zeros_like(cbuf)

    kbuf[0:nc, :] = kc_ref[...]
    kbuf[nc:nc + nl, :] = kl_ref[...]
    vbuf[0:nc, :] = vc_ref[...]
    vbuf[nc:nc + nl, :] = vl_ref[...]

    lf = dl_ref[...]
    lam = (jnp.exp(jnp.sum(lf[0:1] * lf[1:2], axis=1, keepdims=True))
           - jnp.exp(jnp.sum(lf[2:3] * lf[3:4], axis=1, keepdims=True)) + lam_init)
    gain = gain_ref[...] * (1.0 - lam_init)
    n_sub = nl // rows
    lane = lax.broadcasted_iota(I32, (rows, 2 * hd), 1)
    nt = (((1,), (1,)), ((), ()))

    def stage_a(j, slot):
        r0 = pl.multiple_of(jnp.minimum(j, n_sub - 1) * rows, rows)
        q = q_ref[pl.ds(r0, rows), :]
        zero = jnp.zeros_like(q)
        kk = kbuf[...]
        sbuf[slot, 0] = lax.dot_general(jnp.where(lane < hd, q, zero), kk, nt, preferred_element_type=F32)
        sbuf[slot, 1] = lax.dot_general(jnp.where(lane >= hd, q, zero), kk, nt, preferred_element_type=F32)

    def stage_b(slot):
        cols = [slice(c, c + LANES) for c in range(0, nc + nl, LANES)]
        ls = []
        for k in range(2):
            pm = sbuf[slot, k, :, cols[0]]
            for cs in cols[1:]:
                pm = jnp.maximum(pm, sbuf[slot, k, :, cs])
            m = jnp.max(pm, axis=1, keepdims=True)
            acc = jnp.zeros((rows, LANES), F32)
            for cs in cols:
                e = jnp.exp2(sbuf[slot, k, :, cs] - m)
                acc = acc + e
                ebuf[slot, k, :, cs] = e.astype(BF16)
            ls.append(jnp.sum(acc, axis=1, keepdims=True))
        ratio = (lam * ls[0] / ls[1]).astype(BF16)
        abuf[slot] = ebuf[slot, 0] - ratio * ebuf[slot, 1]
        cbuf[slot] = jnp.broadcast_to(1.0 / ls[0], cbuf.shape[1:])

    def stage_c(j, slot):
        r0 = pl.multiple_of(jnp.maximum(j - 2, 0) * rows, rows)
        o = jnp.dot(abuf[slot], vbuf[...], preferred_element_type=F32) * cbuf[slot]
        o = o * lax.rsqrt(jnp.mean(o * o, axis=1, keepdims=True) + LN_EPS) * gain
        o_ref[pl.ds(r0, rows), :] = o.astype(BF16)

    def body(t, carry):
        j = 2 * t
        stage_a(j, 0)
        stage_b(1)
        stage_c(j, 0)
        stage_a(j + 1, 1)
        stage_b(0)
        stage_c(j + 1, 1)
        return carry

    lax.fori_loop(0, (n_sub + 2) // 2, body, 0)


def _diff_attention(q, k_ctx, k_lat, v_ctx, v_lat, da_lambda, subln, *, heads, hd, lam_init):
    b, n, aw = q.shape
    nc = k_ctx.shape[1]
    vd = aw // heads
    rows = min(128, n)
    assert (n // rows) % 2 == 0
    blk_q = pl.BlockSpec((None, n, vd), lambda bi, h: (bi, 0, h))
    blk_c = pl.BlockSpec((None, nc, vd), lambda bi, h: (bi, 0, h))
    return pl.pallas_call(
        functools.partial(_attn_kernel, nc=nc, nl=n, hd=hd, lam_init=lam_init, rows=rows),
        out_shape=jax.ShapeDtypeStruct((b, n, aw), BF16),
        grid=(b, heads),
        in_specs=[blk_q, blk_c, blk_q, blk_c, blk_q,
                  pl.BlockSpec(da_lambda.shape, lambda bi, h: (0, 0)),
                  pl.BlockSpec((1, vd), lambda bi, h: (0, 0))],
        out_specs=blk_q,
        scratch_shapes=[pltpu.VMEM((nc + n, vd), BF16), pltpu.VMEM((nc + n, vd), BF16),
                        pltpu.VMEM((2, 2, rows, nc + n), F32), pltpu.VMEM((2, 2, rows, nc + n), BF16),
                        pltpu.VMEM((2, rows, nc + n), BF16), pltpu.VMEM((2, rows, vd), F32)],
        compiler_params=_cp(("arbitrary", "arbitrary"), VMEM_LIMIT),
        name="diff_attention",
    )(q, k_ctx, k_lat, v_ctx, v_lat, da_lambda, subln.reshape(1, vd))


def _lru_kernel(*refs, reverse, add_prev, nt, groups):
    if add_prev:
        (xp_ref, xc_ref, xn_ref, yprev_ref, cw_ref, cb_ref, wa_ref, ba_ref, wx_ref, bx_ref,
         lam_ref, h0_ref, y_ref, hf_ref, carry, ext) = refs
    else:
        (xp_ref, xc_ref, xn_ref, cw_ref, cb_ref, wa_ref, ba_ref, wx_ref, bx_ref,
         lam_ref, h0_ref, y_ref, hf_ref, carry, ext) = refs
        yprev_ref = None
    i = pl.program_id(1)
    ti = (nt - 1 - i) if reverse else i
    tn, w = xc_ref.shape

    @pl.when(i == 0)
    def _():
        carry[...] = h0_ref[...]

    ext[0:SUBLANES, :] = jnp.where(ti > 0, xp_ref[...], 0.0)
    ext[SUBLANES:SUBLANES + tn, :] = xc_ref[...]
    ext[SUBLANES + tn:2 * SUBLANES + tn, :] = jnp.where(ti < nt - 1, xn_ref[...], 0.0)
    left = CONV_W // 2
    xc = cb_ref[...]
    for k in range(CONV_W):
        off = SUBLANES - left + k
        xc = xc + ext[off:off + tn, :] * cw_ref[k:k + 1, :]

    xb = xc.astype(BF16)
    gw = w // groups

    def gate(w_ref, b_ref):
        parts = [jnp.dot(xb[:, g * gw:(g + 1) * gw], w_ref[g], preferred_element_type=F32)
                 for g in range(groups)]
        return jax.nn.sigmoid(jnp.concatenate(parts, axis=1) + b_ref[...])

    r = gate(wa_ref, ba_ref)
    ig = gate(wx_ref, bx_ref)
    log_a = (-LRU_C * jax.nn.softplus(-lam_ref[...])) * r
    a = jnp.exp(log_a)
    bcoef = jnp.sqrt(-jnp.tanh(log_a) * (a * a + 1.0)) * ig * xc

    ngroups = tn // SUBLANES
    a = a.reshape(ngroups, SUBLANES, w)
    bcoef = bcoef.reshape(ngroups, SUBLANES, w)
    row = lax.broadcasted_iota(I32, (ngroups, SUBLANES, w), 1)
    d = 1
    while d < SUBLANES:
        shift = (SUBLANES - d) if reverse else d
        a_sh = pltpu.roll(a, shift, 1)
        b_sh = pltpu.roll(bcoef, shift, 1)
        live = (row < SUBLANES - d) if reverse else (row >= d)
        bcoef = jnp.where(live, a * b_sh + bcoef, bcoef)
        a = jnp.where(live, a * a_sh, a)
        d *= 2
    hc = carry[...]
    npairs = ngroups // 2
    for p in (range(npairs - 1, -1, -1) if reverse else range(npairs)):
        hs = {}
        for g in ((2 * p + 1, 2 * p) if reverse else (2 * p, 2 * p + 1)):
            h = a[g] * hc + bcoef[g]
            hc = h[0:1, :] if reverse else h[SUBLANES - 1:SUBLANES, :]
            hs[g] = h
        rs = slice(2 * p * SUBLANES, (2 * p + 2) * SUBLANES)
        h2 = jnp.concatenate([hs[2 * p], hs[2 * p + 1]], axis=0)
        if add_prev:
            h2 = yprev_ref[rs, :].astype(F32) + h2
        y_ref[rs, :] = h2.astype(y_ref.dtype)
    carry[...] = hc
    hf_ref[...] = hc


def _rglru_dir(xr, y_prev, conv_w, conv_b, wa_bd, ba, wx_bd, bx, lam, h0, *, reverse, tn):
    b, n, w = xr.shape
    tn = min(tn, n)
    nt = n // tn
    groups = wa_bd.shape[0]
    nb8 = n // SUBLANES
    per8 = tn // SUBLANES

    def tmap(i):
        return (nt - 1 - i) if reverse else i

    cur = pl.BlockSpec((None, tn, w), lambda bi, i: (bi, tmap(i), 0))
    halo_p = pl.BlockSpec((None, SUBLANES, w), lambda bi, i: (bi, jnp.maximum(tmap(i) * per8 - 1, 0), 0))
    halo_n = pl.BlockSpec((None, SUBLANES, w), lambda bi, i: (bi, jnp.minimum((tmap(i) + 1) * per8, nb8 - 1), 0))
    row_w = pl.BlockSpec((1, w), lambda bi, i: (0, 0))
    per_b = pl.BlockSpec((None, 1, w), lambda bi, i: (bi, 0, 0))
    gate_w = pl.BlockSpec(wa_bd.shape, lambda bi, i: (0, 0, 0))
    add_prev = y_prev is not None
    in_specs = [halo_p, cur, halo_n] + ([cur] if add_prev else []) + [
        pl.BlockSpec((CONV_W, w), lambda bi, i: (0, 0)), row_w, gate_w, row_w, gate_w, row_w, row_w, per_b]
    args = [xr, xr, xr] + ([y_prev] if add_prev else []) + [
        conv_w, conv_b.reshape(1, w), wa_bd, ba.reshape(1, w), wx_bd, bx.reshape(1, w),
        lam.reshape(1, w), h0]
    return pl.pallas_call(
        functools.partial(_lru_kernel, reverse=reverse, add_prev=add_prev, nt=nt, groups=groups),
        out_shape=(jax.ShapeDtypeStruct((b, n, w), BF16), jax.ShapeDtypeStruct((b, 1, w), F32)),
        grid=(b, nt),
        in_specs=in_specs,
        out_specs=(cur, per_b),
        scratch_shapes=[pltpu.VMEM((1, w), F32), pltpu.VMEM((tn + 2 * SUBLANES, w), F32)],
        compiler_params=_cp(("parallel", "arbitrary"), VMEM_LIMIT),
        name="rglru_rev" if reverse else "rglru_fwd",
    )(*args)


def _block_diag_groups(wh, group_width):
    heads, blk, _ = wh.shape
    per = group_width // blk
    groups = heads // per
    whg = wh.reshape(groups, per, blk, blk)
    eye = jnp.eye(per, dtype=wh.dtype)
    bd = jnp.einsum('gpij,pq->gpiqj', whg, eye).reshape(groups, group_width, group_width)
    return bd.astype(BF16)


def _residual_ln_mod(x, y, g1, lng, lnb, sh2, sc2, alpha):
    x1 = _ln(alpha * x + g1 * y) * lng + lnb
    t = _ln(x1) * (1.0 + sc2) + sh2
    return x1, t


def _outproj_even_kernel(r_ref, g_ref, o_ref, x_ref, w_ref, g1_ref, lng_ref, lnb_ref, sh2_ref, sc2_ref,
                         x1_ref, t_ref, *, lw, alpha):
    z = (r_ref[...].astype(F32) * jax.nn.gelu(g_ref[...].astype(F32))).astype(BF16)
    y = (jnp.dot(z, w_ref[0:lw, :], preferred_element_type=F32)
         + jnp.dot(o_ref[...], w_ref[lw:, :], preferred_element_type=F32))
    x1, t = _residual_ln_mod(x_ref[...], y, g1_ref[...], lng_ref[...], lnb_ref[...],
                             sh2_ref[...], sc2_ref[...], alpha)
    x1_ref[...] = x1
    t_ref[...] = t.astype(BF16)


def _outproj_even(r, g, o, x, w_out_b, g1, lng, lnb, sh2, sc2, *, alpha, tm):
    b, n, d = x.shape
    lw = r.shape[2]
    aw = o.shape[2]
    tm = min(tm, n)
    tok = lambda bi, i: (bi, i, 0)
    per_b = pl.BlockSpec((None, 1, d), lambda bi, i: (bi, 0, 0))
    row = pl.BlockSpec((1, d), lambda bi, i: (0, 0))
    return pl.pallas_call(
        functools.partial(_outproj_even_kernel, lw=lw, alpha=alpha),
        out_shape=(jax.ShapeDtypeStruct((b, n, d), F32), jax.ShapeDtypeStruct((b, n, d), BF16)),
        grid=(b, n // tm),
        in_specs=[pl.BlockSpec((None, tm, lw), tok), pl.BlockSpec((None, tm, lw), tok),
                  pl.BlockSpec((None, tm, aw), tok), pl.BlockSpec((None, tm, d), tok),
                  pl.BlockSpec(w_out_b.shape, lambda bi, i: (0, 0)),
                  per_b, row, row, per_b, per_b],
        out_specs=(pl.BlockSpec((None, tm, d), tok), pl.BlockSpec((None, tm, d), tok)),
        compiler_params=_cp(("parallel", "parallel"), VMEM_LIMIT),
        name="outproj_even",
    )(r, g, o, x, w_out_b, g1, lng.reshape(1, d), lnb.reshape(1, d), sh2, sc2)


def _chan_dft_kernel(x_ref, sh_ref, sc_ref, tab_ref, zr_ref, zi_ref, *, groups, gd):
    h = (_ln(x_ref[...]) * (1.0 + sc_ref[...]) + sh_ref[...]).astype(BF16)
    for g in range(groups):
        z = jnp.dot(h[:, g * gd:(g + 1) * gd], tab_ref[...], preferred_element_type=F32)
        zr_ref[:, g * gd:(g + 1) * gd] = z[:, 0:gd].astype(BF16)
        zi_ref[:, g * gd:(g + 1) * gd] = z[:, gd:2 * gd].astype(BF16)


def _chan_dft(x, shift, scale, tab, *, groups, tm):
    b, n, d = x.shape
    gd = d // groups
    tm = min(tm, n)
    tok = lambda bi, i: (bi, i, 0)
    per_b = pl.BlockSpec((None, 1, d), lambda bi, i: (bi, 0, 0))
    return pl.pallas_call(
        functools.partial(_chan_dft_kernel, groups=groups, gd=gd),
        out_shape=(jax.ShapeDtypeStruct((b, n, d), BF16), jax.ShapeDtypeStruct((b, n, d), BF16)),
        grid=(b, n // tm),
        in_specs=[pl.BlockSpec((None, tm, d), tok), per_b, per_b,
                  pl.BlockSpec(tab.shape, lambda bi, i: (0, 0))],
        out_specs=(pl.BlockSpec((None, tm, d), tok), pl.BlockSpec((None, tm, d), tok)),
        compiler_params=_cp(("parallel", "parallel"), VMEM_LIMIT),
        name="chan_dft",
    )(x, shift, scale, tab)


FFT_C = 64


def _fft_pitch(group):
    p = -(-group // SUBLANES)
    return SUBLANES * (p if p % 2 else p + 1)


def _tok_fft_kernel(zr_ref, zi_ref, m1_ref, m3_ref, tc_ref, ts_ref, o_ref, zsr, zsi, asr, asi, ob, *, nr):
    c_len = FFT_C
    pz = zsr.shape[0] // nr
    pa = asr.shape[0] // c_len
    for r in range(nr):
        zsr[pz * r:pz * r + c_len, :] = zr_ref[c_len * r:c_len * (r + 1), :].astype(F32)
        zsi[pz * r:pz * r + c_len, :] = zi_ref[c_len * r:c_len * (r + 1), :].astype(F32)
    m1 = m1_ref[...]
    for c in range(c_len):
        x2 = jnp.concatenate([zsr[pl.ds(c, nr, stride=pz), :], zsi[pl.ds(c, nr, stride=pz), :]], axis=0)
        a2 = jnp.dot(m1, x2.astype(BF16), preferred_element_type=F32)
        ar, ai = a2[0:nr], a2[nr:2 * nr]
        tcv = tc_ref[c * nr:(c + 1) * nr, :]
        tsv = ts_ref[c * nr:(c + 1) * nr, :]
        asr[pa * c:pa * c + nr, :] = ar * tcv + ai * tsv
        asi[pa * c:pa * c + nr, :] = ai * tcv - ar * tsv
    m3 = m3_ref[...]
    for k1 in range(nr):
        y2 = jnp.concatenate([asr[pl.ds(k1, c_len, stride=pa), :], asi[pl.ds(k1, c_len, stride=pa), :]], axis=0)
        ob[pl.ds(k1, c_len, stride=pa), :] = jnp.dot(m3, y2.astype(BF16), preferred_element_type=F32)
    for k2 in range(c_len):
        o_ref[nr * k2:nr * (k2 + 1), :] = ob[pa * k2:pa * k2 + nr, :].astype(BF16)


def _tok_fft(zr, zi):
    b, n, d = zr.shape
    nr = n // FFT_C
    pz = _fft_pitch(FFT_C)
    pa = _fft_pitch(nr)
    kr = np.arange(nr, dtype=np.float64)
    ang_r = 2.0 * np.pi * (np.outer(kr, kr) % nr) / nr
    cr, sr = np.cos(ang_r), np.sin(ang_r)
    m1 = jnp.asarray(np.block([[cr, -sr], [-sr, -cr]]), F32).astype(BF16)
    kc = np.arange(FFT_C, dtype=np.float64)
    ang_c = 2.0 * np.pi * (np.outer(kc, kc) % FFT_C) / FFT_C
    m3 = jnp.asarray(np.concatenate([np.cos(ang_c), np.sin(ang_c)], axis=1), F32).astype(BF16)
    ang_t = 2.0 * np.pi * (np.outer(kc, kr) % n) / n
    tc = jnp.broadcast_to(jnp.asarray(np.cos(ang_t).reshape(FFT_C * nr, 1), F32), (FFT_C * nr, LANES))
    ts = jnp.broadcast_to(jnp.asarray(np.sin(ang_t).reshape(FFT_C * nr, 1), F32), (FFT_C * nr, LANES))
    slab = pl.BlockSpec((None, n, LANES), lambda bi, l: (bi, 0, l))
    const = lambda a: pl.BlockSpec(a.shape, lambda bi, l: (0, 0))
    return pl.pallas_call(
        functools.partial(_tok_fft_kernel, nr=nr),
        out_shape=jax.ShapeDtypeStruct((b, n, d), BF16),
        grid=(b, d // LANES),
        in_specs=[slab, slab, const(m1), const(m3), const(tc), const(ts)],
        out_specs=slab,
        scratch_shapes=[pltpu.VMEM((nr * pz, LANES), F32), pltpu.VMEM((nr * pz, LANES), F32),
                        pltpu.VMEM((FFT_C * pa, LANES), F32), pltpu.VMEM((FFT_C * pa, LANES), F32),
                        pltpu.VMEM((FFT_C * pa, LANES), F32)],
        compiler_params=_cp(("parallel", "parallel"), VMEM_LIMIT),
        name="tok_fft",
    )(zr, zi, m1, m3, tc, ts)


def _chan_dft_table(gd):
    c = np.arange(gd, dtype=np.float64)
    ang_c = 2.0 * np.pi * (np.outer(c, c) % gd) / gd
    return jnp.asarray(np.concatenate([np.cos(ang_c), np.sin(ang_c)], axis=1), F32).astype(BF16)


def _outproj_odd_kernel(wv_ref, x_ref, w_ref, b_ref, g1_ref, lng_ref, lnb_ref, sh2_ref, sc2_ref,
                        x1_ref, t_ref, *, alpha, norm):
    y = jnp.dot(wv_ref[...], w_ref[...], preferred_element_type=F32) * norm + b_ref[...]
    x1, t = _residual_ln_mod(x_ref[...], y, g1_ref[...], lng_ref[...], lnb_ref[...],
                             sh2_ref[...], sc2_ref[...], alpha)
    x1_ref[...] = x1
    t_ref[...] = t.astype(BF16)


def _outproj_odd(wv, x, w_b, bias, g1, lng, lnb, sh2, sc2, *, alpha, norm, tm):
    b, n, d = x.shape
    tm = min(tm, n)
    tok = lambda bi, i: (bi, i, 0)
    per_b = pl.BlockSpec((None, 1, d), lambda bi, i: (bi, 0, 0))
    row = pl.BlockSpec((1, d), lambda bi, i: (0, 0))
    return pl.pallas_call(
        functools.partial(_outproj_odd_kernel, alpha=alpha, norm=norm),
        out_shape=(jax.ShapeDtypeStruct((b, n, d), F32), jax.ShapeDtypeStruct((b, n, d), BF16)),
        grid=(b, n // tm),
        in_specs=[pl.BlockSpec((None, tm, d), tok), pl.BlockSpec((None, tm, d), tok),
                  pl.BlockSpec(w_b.shape, lambda bi, i: (0, 0)), row,
                  per_b, row, row, per_b, per_b],
        out_specs=(pl.BlockSpec((None, tm, d), tok), pl.BlockSpec((None, tm, d), tok)),
        compiler_params=_cp(("parallel", "parallel"), VMEM_LIMIT),
        name="outproj_odd",
    )(wv, x, w_b, bias.reshape(1, d), g1, lng.reshape(1, d), lnb.reshape(1, d), sh2, sc2)


def _route_kernel(t_ref, w_ref, b_ref, up_ref, lp_ref, tokm_ref, cnt_ref, *, ng, epg, chunk):
    tm = t_ref.shape[0]
    ne = ng * epg
    logits = jnp.dot(t_ref[...].astype(BF16), w_ref[...], preferred_element_type=F32) + b_ref[...]
    lt = logits.T
    best = lt[0:1, :]
    bi = jnp.zeros((1, tm), I32)
    for k in range(1, ng):
        gk = lt[k:k + 1, :]
        upd = gk > best
        bi = jnp.where(upd, k, bi)
        best = jnp.where(upd, gk, best)
    den = jnp.zeros((1, tm), F32)
    for k in range(ng):
        den = den + jnp.exp(lt[k:k + 1, :] - best)
    p_g = 1.0 / den
    fsel = lt[SUBLANES:SUBLANES + epg, :]
    for k in range(1, ng):
        fsel = jnp.where(bi == k, lt[SUBLANES + k * epg:SUBLANES + (k + 1) * epg, :], fsel)
    neg = jnp.full((1, tm), -jnp.inf, F32)
    m1, m2 = neg, neg
    i1 = jnp.zeros((1, tm), I32)
    i2 = jnp.zeros((1, tm), I32)
    for j in range(epg):
        v = fsel[j:j + 1, :]
        gt1 = v > m1
        gt2 = v > m2
        m2 = jnp.where(gt1, m1, jnp.where(gt2, v, m2))
        i2 = jnp.where(gt1, i1, jnp.where(gt2, j, i2))
        m1 = jnp.where(gt1, v, m1)
        i1 = jnp.where(gt1, j, i1)
    e21 = jnp.exp(m2 - m1)
    w1 = p_g / (1.0 + e21)
    w2 = p_g * e21 / (1.0 + e21)
    e1 = bi * epg + i1
    e2 = bi * epg + i2

    e = jnp.concatenate([e1, e2], axis=1)
    rows = lax.broadcasted_iota(I32, (ne, 2 * tm), 0)
    onehot = jnp.where(rows == e, 1.0, 0.0)
    before = jnp.dot(onehot.astype(BF16), up_ref[...], preferred_element_type=F32)
    tot = jnp.sum(onehot, axis=1, keepdims=True)
    slots = jnp.floor((tot + (chunk - 1.0)) * (1.0 / chunk)) * chunk
    slots_b = jnp.broadcast_to(slots, (ne, LANES))
    rowe = lax.broadcasted_iota(I32, (ne, LANES), 0)
    incl = slots_b
    d = 1
    while d < ne:
        incl = incl + jnp.where(rowe >= d, pltpu.roll(incl, d, 0), 0.0)
        d *= 2
    seg_off = jnp.tile(incl - slots_b, (1, 2 * tm // LANES))
    lpos = jnp.sum(onehot * (before + seg_off), axis=0, keepdims=True)
    lp0 = lpos[:, 0:tm]
    lp1 = lpos[:, tm:2 * tm]
    row8 = lax.broadcasted_iota(I32, (SUBLANES, tm), 0)
    lp_ref[...] = jnp.where(row8 == 0, lp0, jnp.where(row8 == 1, lp1, 0.0)).astype(I32)
    rowl = lax.broadcasted_iota(I32, (LANES, tm), 0)
    tokm = jnp.where(rowl == 0, w1, jnp.where(rowl == 1, w2, jnp.where(rowl == 2, lp0, jnp.where(rowl == 3, lp1, 0.0))))
    tokm_ref[...] = tokm.T
    cnt_ref[...] = jnp.broadcast_to(tot, (ne, LANES))


def _route(t, wcat_b, bcat, upper, *, ng, epg, tm, chunk):
    tt, d = t.shape
    ne = ng * epg
    nt = tt // tm
    return pl.pallas_call(
        functools.partial(_route_kernel, ng=ng, epg=epg, chunk=chunk),
        out_shape=(jax.ShapeDtypeStruct((SUBLANES, tt), I32), jax.ShapeDtypeStruct((tt, LANES), F32),
                   jax.ShapeDtypeStruct((ne, nt * LANES), F32)),
        grid=(nt,),
        in_specs=[pl.BlockSpec((tm, d), lambda i: (i, 0)),
                  pl.BlockSpec((d, LANES), lambda i: (0, 0)),
                  pl.BlockSpec((1, LANES), lambda i: (0, 0)),
                  pl.BlockSpec(upper.shape, lambda i: (0, 0))],
        out_specs=(pl.BlockSpec((SUBLANES, tm), lambda i: (0, i)),
                   pl.BlockSpec((tm, LANES), lambda i: (i, 0)),
                   pl.BlockSpec((ne, LANES), lambda i: (0, i))),
        compiler_params=_cp(("parallel",), VMEM_LIMIT),
        name="route_sort",
    )(t, wcat_b, bcat, upper)


def _dispatch_kernel(row_ref, tot_ref, zrow_ref, nz_ref, t_ref, lp_ref, xs_ref, stage, zbuf, sems,
                     *, qmax, zmax, chunk, nsteps):
    i = pl.program_id(0)
    slot = i % 2
    sp = stage.shape[1] // PACK_SUB
    tm = t_ref.shape[0]
    crow = chunk * PACK_SUB

    @pl.when(i == 0)
    def _():
        zbuf[...] = jnp.zeros_like(zbuf)
        _for_each_chunk(nz_ref[0], lambda q: pltpu.make_async_copy(
            zbuf, xs_ref.at[pl.ds(_tile_row(zrow_ref[q], SUBLANES), crow)], sems.at[1]).start())
        _wait_chunks(lambda k: pltpu.make_async_copy(xs_ref.at[pl.ds(0, k * crow)], xs_ref.at[pl.ds(0, k * crow)],
                                                     sems.at[1]), nz_ref[0], zmax)

    lp = lp_ref[...]
    prow = lax.broadcasted_iota(I32, (sp, tm), 0)
    perm = jnp.where(prow == lp[0:1, :], 1.0, jnp.where(prow == lp[1:2, :], 1.0, 0.0)).astype(BF16)
    srt = jnp.dot(perm, t_ref[...], preferred_element_type=F32)
    _rows_to_tiles(stage.at[slot], _pack_rows(srt))

    def chunk_copy(sl, src, dst):
        return pltpu.make_async_copy(stage.at[sl, pl.ds(_tile_row(src, crow), crow)],
                                     xs_ref.at[pl.ds(_tile_row(dst, SUBLANES), crow)], sems.at[sl])

    def drain(sl, n):
        def copy_of(k):
            return pltpu.make_async_copy(stage.at[sl, pl.ds(0, k * crow)], xs_ref.at[pl.ds(0, k * crow)], sems.at[sl])
        _wait_chunks(copy_of, n, qmax)

    @pl.when(i >= 1)
    def _():
        drain(1 - slot, tot_ref[jnp.maximum(i - 1, 0)])

    _for_each_chunk(tot_ref[i], lambda q: chunk_copy(slot, pl.multiple_of(q * chunk, chunk),
                                                     row_ref[i * qmax + q]).start())

    @pl.when(i == nsteps - 1)
    def _():
        drain(slot, tot_ref[i])


def _dispatch(chunk_row, tile_nch, zero_row, n_zero, t, lp, *, n_rows, ne, tm, chunk):
    ttot, d = t.shape
    nsteps = ttot // tm
    sp = 2 * tm + ne * chunk
    grid_spec = pltpu.PrefetchScalarGridSpec(
        num_scalar_prefetch=4,
        grid=(nsteps,),
        in_specs=[pl.BlockSpec((tm, d), lambda i, *_: (i, 0)),
                  pl.BlockSpec((SUBLANES, tm), lambda i, *_: (0, i))],
        out_specs=pl.BlockSpec(memory_space=pl.ANY),
        scratch_shapes=[pltpu.VMEM((2, sp * PACK_SUB, LANES), jnp.uint32),
                        pltpu.VMEM((chunk * PACK_SUB, LANES), jnp.uint32), pltpu.SemaphoreType.DMA((2,))],
    )
    assert d == 2 * PACK_SUB * LANES
    return pl.pallas_call(
        functools.partial(_dispatch_kernel, qmax=chunk_row.shape[0] // nsteps, zmax=zero_row.shape[0],
                          chunk=chunk, nsteps=nsteps),
        out_shape=jax.ShapeDtypeStruct((n_rows * PACK_SUB, LANES), jnp.uint32),
        grid_spec=grid_spec,
        compiler_params=_cp(("arbitrary",), VMEM_LIMIT),
        name="moe_dispatch",
    )(chunk_row, tile_nch, zero_row, n_zero, t, lp)


def _moe_kernel(be_ref, bv_ref, nu_ref, xs_ref, w1_ref, w3_ref, w2_ref, y_ref, w13b, w2b, *, ff):
    i = pl.program_id(0)
    half = MOE_ROWS // 2

    def ffn(nrows):
        x = _unpack_rows(_rows_from_tiles(xs_ref, nrows))
        h = jnp.dot(x, w13b[...], preferred_element_type=F32)
        hid = (_silu(h[:, 0:ff]) * h[:, ff:2 * ff]).astype(BF16)
        y = jnp.dot(hid, w2b[...], preferred_element_type=F32)
        _rows_to_tiles(y_ref, _pack_rows(y.astype(BF16).astype(F32)))

    @pl.when(i < nu_ref[0])
    def _():
        prev = be_ref[jnp.maximum(i - 1, 0)]

        @pl.when((i == 0) | (be_ref[i] != prev))
        def _():
            w13b[:, 0:ff] = w1_ref[...].astype(BF16)
            w13b[:, ff:2 * ff] = w3_ref[...].astype(BF16)
            w2b[...] = w2_ref[...].astype(BF16)

        @pl.when(bv_ref[i] > half)
        def _():
            ffn(MOE_ROWS)

        @pl.when(bv_ref[i] <= half)
        def _():
            ffn(half)
            y_ref[half * PACK_SUB:, :] = jnp.zeros((half * PACK_SUB, LANES), y_ref.dtype)

    @pl.when(i >= nu_ref[0])
    def _():
        y_ref[...] = jnp.zeros_like(y_ref)


def _moe_experts(block_e, block_rows, n_used, xs, w1, w3, w2, *, layer):
    d = w1.shape[-2]
    ff = w1.shape[-1]
    blk = MOE_ROWS * PACK_SUB
    nb = xs.shape[0] // blk
    wmap = lambda i, be, bv, nu: (layer, be[i], 0, 0)
    grid_spec = pltpu.PrefetchScalarGridSpec(
        num_scalar_prefetch=3,
        grid=(nb,),
        in_specs=[pl.BlockSpec((blk, LANES), lambda i, be, bv, nu: (i, 0)),
                  pl.BlockSpec((None, None, d, ff), wmap),
                  pl.BlockSpec((None, None, d, ff), wmap),
                  pl.BlockSpec((None, None, ff, d), wmap)],
        out_specs=pl.BlockSpec((blk, LANES), lambda i, be, bv, nu: (i, 0)),
        scratch_shapes=[pltpu.VMEM((d, 2 * ff), BF16), pltpu.VMEM((ff, d), BF16)],
    )
    return pl.pallas_call(
        functools.partial(_moe_kernel, ff=ff),
        out_shape=jax.ShapeDtypeStruct(xs.shape, xs.dtype),
        grid_spec=grid_spec,
        compiler_params=_cp(("arbitrary",), VMEM_LIMIT),
        name="moe_experts",
    )(block_e, block_rows, n_used, xs, w1, w3, w2)


def _combine_kernel(row_ref, tot_ref, yb_ref, x_ref, tokm_ref, g2_ref, lng_ref, lnb_ref, *rest,
                    qmax, chunk, nsteps, alpha, dft_groups):
    if dft_groups:
        sh_ref, sc_ref, tab_ref, o_ref, zr_ref, zi_ref, stage, sems = rest
    else:
        o_ref, stage, sems = rest
    i = pl.program_id(0)
    slot = i % 2
    sp = stage.shape[1] // PACK_SUB
    tm = x_ref.shape[0]
    crow = chunk * PACK_SUB

    def chunk_copy(sl, src, dst):
        return pltpu.make_async_copy(yb_ref.at[pl.ds(_tile_row(src, SUBLANES), crow)],
                                     stage.at[sl, pl.ds(_tile_row(dst, crow), crow)], sems.at[sl])

    def issue_tile(step, sl):
        _for_each_chunk(tot_ref[step], lambda q: chunk_copy(sl, row_ref[step * qmax + q],
                                                            pl.multiple_of(q * chunk, chunk)).start())

    @pl.when(i == 0)
    def _():
        stage[...] = jnp.zeros_like(stage)
        issue_tile(0, 0)

    @pl.when(i + 1 < nsteps)
    def _():
        issue_tile(jnp.minimum(i + 1, nsteps - 1), 1 - slot)

    def copy_of(k):
        return pltpu.make_async_copy(yb_ref.at[pl.ds(0, k * crow)], stage.at[slot, pl.ds(0, k * crow)], sems.at[slot])
    _wait_chunks(copy_of, tot_ref[i], qmax)

    tk = tokm_ref[...]
    pos = lax.broadcasted_iota(I32, (tm, sp), 1).astype(F32)
    st = _unpack_rows(_rows_from_tiles(stage.at[slot], sp))
    gsel = jnp.where(pos == tk[:, 2:3], tk[:, 0:1], jnp.where(pos == tk[:, 3:4], tk[:, 1:2], 0.0))
    m = jnp.dot(gsel.astype(BF16), st, preferred_element_type=F32)
    xo = _ln(alpha * x_ref[...] + g2_ref[...] * m) * lng_ref[...] + lnb_ref[...]
    o_ref[...] = xo
    if dft_groups:
        gd = xo.shape[1] // dft_groups
        h = (_ln(xo) * (1.0 + sc_ref[...]) + sh_ref[...]).astype(BF16)
        for g in range(dft_groups):
            z = jnp.dot(h[:, g * gd:(g + 1) * gd], tab_ref[...], preferred_element_type=F32)
            zr_ref[:, g * gd:(g + 1) * gd] = z[:, 0:gd].astype(BF16)
            zi_ref[:, g * gd:(g + 1) * gd] = z[:, gd:2 * gd].astype(BF16)


def _combine(chunk_row, tile_nch, yb, x, tokm, g2, lng, lnb, dft, *, ne, tm, chunk, n_per_batch, alpha):
    ttot, d = x.shape
    nsteps = ttot // tm
    per = n_per_batch // tm
    sp = 2 * tm + ne * chunk
    tok = pl.BlockSpec((tm, d), lambda i, *_: (i, 0))
    per_b = pl.BlockSpec((None, 1, d), lambda i, *_: (i // per, 0, 0))
    row = pl.BlockSpec((1, d), lambda i, *_: (0, 0))
    in_specs = [pl.BlockSpec(memory_space=pl.ANY), tok, pl.BlockSpec((tm, LANES), lambda i, *_: (i, 0)),
                per_b, row, row]
    args = [chunk_row, tile_nch, yb, x, tokm, g2, lng.reshape(1, d), lnb.reshape(1, d)]
    out_specs, out_shape, groups = tok, jax.ShapeDtypeStruct((ttot, d), F32), 0
    if dft is not None:
        shift, scale, tab, groups = dft
        in_specs += [per_b, per_b, pl.BlockSpec(tab.shape, lambda i, *_: (0, 0))]
        args += [shift, scale, tab]
        out_specs = (tok, tok, tok)
        out_shape = (out_shape, jax.ShapeDtypeStruct((ttot, d), BF16), jax.ShapeDtypeStruct((ttot, d), BF16))
    grid_spec = pltpu.PrefetchScalarGridSpec(
        num_scalar_prefetch=2,
        grid=(nsteps,),
        in_specs=in_specs,
        out_specs=out_specs,
        scratch_shapes=[pltpu.VMEM((2, sp * PACK_SUB, LANES), yb.dtype), pltpu.SemaphoreType.DMA((2,))],
    )
    return pl.pallas_call(
        functools.partial(_combine_kernel, qmax=chunk_row.shape[0] // nsteps, chunk=chunk, nsteps=nsteps,
                          alpha=alpha, dft_groups=groups),
        out_shape=out_shape,
        grid_spec=grid_spec,
        compiler_params=_cp(("arbitrary",), VMEM_LIMIT),
        name="moe_combine_dft" if groups else "moe_combine",
    )(*args)


def _hier_moe_layer(x1, t, g2, lng, lnb, wg, bg, wf, bf, w1, w3, w2, dft, *, layer, alpha):
    b, n, d = x1.shape
    ttot = b * n
    ng = wg.shape[1]
    ne = wf.shape[1]
    epg = ne // ng
    tm = min(MOE_TILE, n)
    chunk = MOE_CHUNK
    nt = ttot // tm
    tflat = t.reshape(ttot, d)
    wcat = jnp.zeros((d, LANES), F32).at[:, 0:ng].set(wg).at[:, SUBLANES:SUBLANES + ne].set(wf).astype(BF16)
    bcat = jnp.zeros((1, LANES), F32).at[0, 0:ng].set(bg).at[0, SUBLANES:SUBLANES + ne].set(bf)
    ar = jnp.arange(2 * tm, dtype=I32)
    upper = (ar[:, None] < ar[None, :]).astype(BF16)
    lp, tokm, cnt = _route(tflat, wcat, bcat, upper, ng=ng, epg=epg, tm=tm, chunk=chunk)

    tile_cnt = cnt.reshape(ne, nt, LANES)[:, :, 0].T.astype(I32)
    seg_len = ((tile_cnt + 1) // 2) * 2
    counts = jnp.sum(seg_len, axis=0)
    padded = ((counts + chunk + MOE_ROWS - 1) // MOE_ROWS) * MOE_ROWS
    pend = jnp.cumsum(padded)
    pstart = pend - padded
    base = jnp.cumsum(seg_len, axis=0) - seg_len
    seg_row = pstart[None, :] + base
    nch = (tile_cnt + chunk - 1) // chunk
    nch_end = jnp.cumsum(nch, axis=1)
    tile_nch = nch_end[:, -1].astype(I32)
    qmax = 2 * tm // chunk + ne
    qs = jnp.arange(qmax, dtype=I32)
    e_q = jnp.minimum(jnp.sum((nch_end[:, None, :] <= qs[None, :, None]).astype(I32), axis=2), ne - 1)
    pick = e_q[:, :, None] == jnp.arange(ne, dtype=I32)[None, None, :]
    first_q = jnp.sum(jnp.where(pick, (nch_end - nch)[:, None, :], 0), axis=2)
    first_row = jnp.sum(jnp.where(pick, seg_row[:, None, :], 0), axis=2)
    chunk_row = (first_row + (qs[None, :] - first_q) * chunk).astype(I32).reshape(nt * qmax)
    nb = -(-(2 * ttot + nt * ne + ne * (chunk + MOE_ROWS)) // MOE_ROWS)
    bstart = jnp.arange(nb, dtype=I32) * MOE_ROWS
    block_e = jnp.minimum(jnp.sum((pend[None, :] <= bstart[:, None]).astype(I32), axis=1), ne - 1)
    n_used = (pend[-1] // MOE_ROWS).astype(I32).reshape(1)
    zstart = jnp.concatenate([pstart + (counts // chunk) * chunk, pend[-1:]])
    zend = jnp.concatenate([pend, jnp.full((1,), nb * MOE_ROWS, pend.dtype)])
    zcnt = (zend - zstart) // chunk
    zcum = jnp.cumsum(zcnt)
    zmax = ne * ((MOE_ROWS + 2 * chunk) // chunk + 1) + (nb - (2 * ttot) // MOE_ROWS) * (MOE_ROWS // chunk)
    zq = jnp.arange(zmax, dtype=I32)
    zseg = jnp.minimum(jnp.sum((zcum[None, :] <= zq[:, None]).astype(I32), axis=1), ne)
    zpick = zseg[:, None] == jnp.arange(ne + 1, dtype=I32)[None, :]
    zfirst = jnp.sum(jnp.where(zpick, (zcum - zcnt)[None, :], 0), axis=1)
    zero_row = (jnp.sum(jnp.where(zpick, zstart[None, :], 0), axis=1) + (zq - zfirst) * chunk).astype(I32)
    n_zero = zcum[-1].astype(I32).reshape(1)

    xs = _dispatch(chunk_row, tile_nch, zero_row, n_zero, tflat, lp, n_rows=nb * MOE_ROWS, ne=ne, tm=tm,
                   chunk=chunk)
    bpick = block_e[:, None] == jnp.arange(ne, dtype=I32)[None, :]
    block_rows = jnp.clip(jnp.sum(jnp.where(bpick, (pstart + counts)[None, :], 0), axis=1) - bstart,
                          0, MOE_ROWS).astype(I32)
    yb = _moe_experts(block_e.astype(I32), block_rows, n_used, xs, w1, w3, w2, layer=layer)
    out = _combine(chunk_row, tile_nch, yb, x1.reshape(ttot, d), tokm, g2, lng, lnb, dft,
                   ne=ne, tm=tm, chunk=chunk, n_per_batch=n, alpha=alpha)
    if dft is None:
        return out.reshape(b, n, d), None
    return out[0].reshape(b, n, d), (out[1].reshape(b, n, d), out[2].reshape(b, n, d))


def kernel(x, c, ctx, c_ctx, ada_w, ada_b, ln_g, ln_b, ev_w_in, ev_conv_w, ev_conv_b, ev_gate_a_w,
           ev_gate_a_b, ev_gate_x_w, ev_gate_x_b, ev_lru_lambda, ev_da_lambda, ev_da_subln, ev_w_out,
           od_w_out, od_b_out, moe_wg, moe_bg, moe_wf, moe_bf, moe_w1, moe_w3, moe_w2):
    bsz, n_lat, d = x.shape
    depth = ada_w.shape[0]
    alpha = (2.0 * depth) ** 0.25
    lw = ev_conv_w.shape[-1]
    hd = ev_da_lambda.shape[-1]
    vd = ev_da_subln.shape[-1]
    aw = (ev_w_in.shape[-1] - 2 * lw) // 3
    heads = aw // vd
    fnet_groups = 4

    rows = ((bsz + 1 + SUBLANES - 1) // SUBLANES) * SUBLANES
    cond = jnp.zeros((rows, d), F32).at[0:bsz].set(c).at[bsz].set(c_ctx)
    ada = _ada_terms(cond, ada_w, ada_b).reshape(depth, rows, 6, d)

    def lat_term(l, k):
        return ada[l, 0:bsz, k, :].reshape(bsz, 1, d)

    def ctx_term(l, k):
        return jnp.broadcast_to(ada[l, bsz, k, :].reshape(1, 1, d), (bsz, 1, d))

    pre_dft = None
    for l in range(depth):
        ctx_live = any(m % 2 == 0 for m in range(l + 1, depth))
        assert not ctx_live, "context stream update is only needed for depth > 2"
        sh1, sc1, g1, sh2, sc2, g2 = [lat_term(l, k) for k in range(6)]
        if l % 2 == 0:
            e = l // 2
            lam_init = 0.8 - 0.6 * math.exp(-0.3 * l)
            w_in_b = ev_w_in[e].astype(BF16)
            cos_t, sin_t = _rope_tables(n_lat, hd)
            qscale = hd ** -0.5 * math.log2(math.e)
            g_l, xr_l, q_l, k_l, v_l = _project_even(x, sh1, sc1, w_in_b, cos_t, sin_t, rope=True,
                                                     lw=lw, aw=aw, qscale=qscale, tm=512)
            n_ctx = ctx.shape[1]
            _, xr_c, _, k_c, v_c = _project_even(ctx, ctx_term(l, 0), ctx_term(l, 1), w_in_b,
                                                 cos_t[0:n_ctx], sin_t[0:n_ctx], rope=False,
                                                 lw=lw, aw=aw, qscale=qscale, tm=256)
            o_l = _diff_attention(q_l, k_c, k_l, v_c, v_l, ev_da_lambda[e], ev_da_subln[e],
                                  heads=heads, hd=hd, lam_init=lam_init)
            gwid = 256
            y = None
            for dirn, rev in ((0, False), (1, True)):
                wa_bd = _block_diag_groups(ev_gate_a_w[e, dirn], gwid)
                wx_bd = _block_diag_groups(ev_gate_x_w[e, dirn], gwid)
                common = (ev_conv_w[e], ev_conv_b[e], wa_bd, ev_gate_a_b[e, dirn], wx_bd,
                          ev_gate_x_b[e, dirn], ev_lru_lambda[e, dirn])
                h_zero = jnp.zeros((bsz, 1, lw), F32)
                _, h_fin = _rglru_dir(xr_c, None, *common, h_zero, reverse=rev, tn=256)
                y, _ = _rglru_dir(xr_l, y, *common, h_fin, reverse=rev, tn=512)
            x1, t = _outproj_even(y, g_l, o_l, x, ev_w_out[e].astype(BF16), g1, ln_g[l, 0], ln_b[l, 0],
                                  sh2, sc2, alpha=alpha, tm=1024)
        else:
            o = l // 2
            gd = d // fnet_groups
            if pre_dft is None:
                pre_dft = _chan_dft(x, sh1, sc1, _chan_dft_table(gd), groups=fnet_groups, tm=1024)
            wv = _tok_fft(*pre_dft)
            norm = 1.0 / math.sqrt(float(n_lat * gd))
            x1, t = _outproj_odd(wv, x, od_w_out[o].astype(BF16), od_b_out[o], g1, ln_g[l, 0], ln_b[l, 0],
                                 sh2, sc2, alpha=alpha, norm=norm, tm=1024)
        dft = None
        if l + 1 < depth and (l + 1) % 2 == 1:
            dft = (lat_term(l + 1, 0), lat_term(l + 1, 1), _chan_dft_table(d // fnet_groups), fnet_groups)
        x, pre_dft = _hier_moe_layer(x1, t, g2, ln_g[l, 1], ln_b[l, 1], moe_wg[l], moe_bg[l], moe_wf[l],
                                     moe_bf[l], moe_w1, moe_w3, moe_w2, dft, layer=l, alpha=alpha)
    return x
```

```python
import functools
import math

import numpy as np
import jax
import jax.numpy as jnp
from jax import lax
from jax.experimental import pallas as pl
from jax.experimental.pallas import tpu as pltpu

F32 = jnp.float32
BF16 = jnp.bfloat16
I32 = jnp.int32

LN_EPS = 1e-6
LRU_C = 8.0
ROPE_BASE = 10000.0
GRID_W = 64
CONV_W = 4
LANES = 128
SUBLANES = 8
MOE_ROWS = 512
MOE_TILE = 256
MOE_CHUNK = 8
VMEM_LIMIT = 56 * 1024 * 1024


def _cp(sem, vmem=None):
    return pltpu.CompilerParams(dimension_semantics=sem, vmem_limit_bytes=vmem)


def _ln(x):
    mu = jnp.mean(x, axis=-1, keepdims=True)
    xc = x - mu
    var = jnp.mean(xc * xc, axis=-1, keepdims=True)
    return xc * lax.rsqrt(var + LN_EPS)


def _silu(x):
    return x * jax.nn.sigmoid(x)


PACK_SUB = 4


def _tile_row(r, mult):
    if isinstance(r, int):
        return r * PACK_SUB
    return pl.multiple_of(r * PACK_SUB, mult)


def _pack_rows(val):
    half = val.shape[1] // 2
    lo = lax.bitcast_convert_type(val[:, 0:half], jnp.uint32) >> 16
    hi = lax.bitcast_convert_type(val[:, half:], jnp.uint32) & jnp.uint32(0xFFFF0000)
    return lo | hi


def _unpack_rows(words):
    lo = lax.bitcast_convert_type(words << 16, F32)
    hi = lax.bitcast_convert_type(words & jnp.uint32(0xFFFF0000), F32)
    return jnp.concatenate([lo, hi], axis=1).astype(BF16)


def _wait_chunks(copy_of, n, nmax):
    b = 1
    while b <= nmax:
        @pl.when((n & b) != 0)
        def _(b=b):
            copy_of(b).wait()
        b *= 2


ISSUE_UNROLL = 4


def _for_each_chunk(n, start_one):
    groups = n // ISSUE_UNROLL

    def group(g, carry):
        for u in range(ISSUE_UNROLL):
            start_one(g * ISSUE_UNROLL + u)
        return carry
    lax.fori_loop(0, groups, group, 0)

    def single(q, carry):
        start_one(q)
        return carry
    lax.fori_loop(groups * ISSUE_UNROLL, n, single, 0)


def _rows_from_tiles(ref, nrows):
    return jnp.concatenate([ref[pl.ds(k, nrows, stride=PACK_SUB), :] for k in range(PACK_SUB)], axis=1)


def _rows_to_tiles(ref, val):
    nrows = val.shape[0]
    for k in range(PACK_SUB):
        ref[pl.ds(k, nrows, stride=PACK_SUB), :] = val[:, k * LANES:(k + 1) * LANES]


def _ada_kernel(c_ref, w_ref, b_ref, o_ref):
    s = _silu(c_ref[...]).astype(BF16)
    o_ref[...] = jnp.dot(s, w_ref[...].astype(BF16), preferred_element_type=F32) + b_ref[...]


def _ada_terms(cond, ada_w, ada_b):
    nl, d, d6 = ada_w.shape
    r = cond.shape[0]
    tn = 1024
    return pl.pallas_call(
        _ada_kernel,
        out_shape=jax.ShapeDtypeStruct((nl, r, d6), F32),
        grid=(nl, d6 // tn),
        in_specs=[pl.BlockSpec((r, d), lambda l, j: (0, 0)),
                  pl.BlockSpec((None, d, tn), lambda l, j: (l, 0, j)),
                  pl.BlockSpec((None, 1, tn), lambda l, j: (l, 0, j))],
        out_specs=pl.BlockSpec((None, r, tn), lambda l, j: (l, 0, j)),
        compiler_params=_cp(("parallel", "parallel")),
        name="ada_terms",
    )(cond, ada_w, ada_b.reshape(nl, 1, d6))


def _rope_apply(x, cos, sin_signed):
    tm = x.shape[0]
    lane = lax.broadcasted_iota(I32, (tm, LANES), 1)
    first_half = (lane % 32) < 16
    outs = []
    for j in range(x.shape[1] // LANES):
        xh = x[:, j * LANES:(j + 1) * LANES]
        partner = jnp.where(first_half, pltpu.roll(xh, LANES - 16, 1), pltpu.roll(xh, 16, 1))
        outs.append(xh * cos + partner * sin_signed)
    return jnp.concatenate(outs, axis=1)


def _proj_kernel(x_ref, sh_ref, sc_ref, w_ref, cos_ref, sin_ref,
                 g_ref, xr_ref, q_ref, k_ref, v_ref, *, rope, lw, aw, qscale):
    h = _ln(x_ref[...]) * (1.0 + sc_ref[...]) + sh_ref[...]
    hb = h.astype(BF16)

    def mm(c0, c1):
        return jnp.dot(hb, w_ref[:, c0:c1], preferred_element_type=F32)

    g_ref[...] = mm(0, lw).astype(BF16)
    xr_ref[...] = mm(lw, 2 * lw)
    q = mm(2 * lw, 2 * lw + aw)
    k = mm(2 * lw + aw, 2 * lw + 2 * aw)
    if rope:
        q = _rope_apply(q, cos_ref[...], sin_ref[...])
        k = _rope_apply(k, cos_ref[...], sin_ref[...])
    q_ref[...] = (q * qscale).astype(BF16)
    k_ref[...] = k.astype(BF16)
    v_ref[...] = mm(2 * lw + 2 * aw, 2 * lw + 3 * aw).astype(BF16)


def _project_even(x, shift, scale, w_in_b, cos_t, sin_t, *, rope, lw, aw, qscale, tm):
    b, n, d = x.shape
    tm = min(tm, n)
    nin = w_in_b.shape[1]
    tok = lambda bi, i: (bi, i, 0)
    per_b = lambda bi, i: (bi, 0, 0)
    outs = (jax.ShapeDtypeStruct((b, n, lw), BF16), jax.ShapeDtypeStruct((b, n, lw), F32),
            jax.ShapeDtypeStruct((b, n, aw), BF16), jax.ShapeDtypeStruct((b, n, aw), BF16),
            jax.ShapeDtypeStruct((b, n, aw), BF16))
    return pl.pallas_call(
        functools.partial(_proj_kernel, rope=rope, lw=lw, aw=aw, qscale=qscale),
        out_shape=outs,
        grid=(b, n // tm),
        in_specs=[pl.BlockSpec((None, tm, d), tok),
                  pl.BlockSpec((None, 1, d), per_b),
                  pl.BlockSpec((None, 1, d), per_b),
                  pl.BlockSpec((d, nin), lambda bi, i: (0, 0)),
                  pl.BlockSpec((tm, LANES), lambda bi, i: (i, 0)),
                  pl.BlockSpec((tm, LANES), lambda bi, i: (i, 0))],
        out_specs=(pl.BlockSpec((None, tm, lw), tok), pl.BlockSpec((None, tm, lw), tok),
                   pl.BlockSpec((None, tm, aw), tok), pl.BlockSpec((None, tm, aw), tok),
                   pl.BlockSpec((None, tm, aw), tok)),
        compiler_params=_cp(("parallel", "parallel"), VMEM_LIMIT),
        name="proj_even_rope" if rope else "proj_even_ctx",
    )(x, shift, scale, w_in_b, cos_t, sin_t)


def _rope_tables(n_tok, head_dim):
    t = jnp.arange(n_tok)
    row = (t // GRID_W).astype(F32)
    col = (t % GRID_W).astype(F32)
    nf = head_dim // 4
    freqs = ROPE_BASE ** (-jnp.arange(nf, dtype=F32) / nf)
    lane = np.arange(LANES)
    within = lane % head_dim
    axis = within // (2 * nf)
    half = (within % (2 * nf)) // nf
    f = within % nf
    pos = jnp.where(jnp.asarray(axis)[None, :] == 0, row[:, None], col[:, None])
    ang = pos * freqs[jnp.asarray(f)][None, :]
    sign = jnp.asarray(np.where(half == 0, -1.0, 1.0), F32)[None, :]
    return jnp.cos(ang).astype(F32), (jnp.sin(ang) * sign).astype(F32)


def _attn_kernel(q_ref, kc_ref, kl_ref, vc_ref, vl_ref, dl_ref, gain_ref, o_ref, kbuf, vbuf, sbuf, ebuf, abuf,
                 cbuf, *, nc, nl, hd, lam_init, rows):
    @pl.when((pl.program_id(0) == 0) & (pl.program_id(1) == 0))
    def _():
        sbuf[...] = jnp.zeros_like(sbuf)
        abuf[...] = jnp.zeros_like(abuf)
        cbuf[...] = jnp.zeros_like(cbuf)

    kbuf[0:nc, :] = kc_ref[...]
    kbuf[nc:nc + nl, :] = kl_ref[...]
    vbuf[0:nc, :] = vc_ref[...]
    vbuf[nc:nc + nl, :] = vl_ref[...]

    lf = dl_ref[...]
    lam = (jnp.exp(jnp.sum(lf[0:1] * lf[1:2], axis=1, keepdims=True))
           - jnp.exp(jnp.sum(lf[2:3] * lf[3:4], axis=1, keepdims=True)) + lam_init)
    gain = gain_ref[...] * (1.0 - lam_init)
    n_sub = nl // rows
    lane = lax.broadcasted_iota(I32, (rows, 2 * hd), 1)
    nt = (((1,), (1,)), ((), ()))

    def stage_a(j, slot):
        r0 = pl.multiple_of(jnp.minimum(j, n_sub - 1) * rows, rows)
        q = q_ref[pl.ds(r0, rows), :]
        zero = jnp.zeros_like(q)
        kk = kbuf[...]
        sbuf[slot, 0] = lax.dot_general(jnp.where(lane < hd, q, zero), kk, nt, preferred_element_type=F32)
        sbuf[slot, 1] = lax.dot_general(jnp.where(lane >= hd, q, zero), kk, nt, preferred_element_type=F32)

    def stage_b(slot):
        cols = [slice(c, c + LANES) for c in range(0, nc + nl, LANES)]
        ls = []
        for k in range(2):
            pm = sbuf[slot, k, :, cols[0]]
            for cs in cols[1:]:
                pm = jnp.maximum(pm, sbuf[slot, k, :, cs])
            m = jnp.max(pm, axis=1, keepdims=True)
            acc = jnp.zeros((rows, LANES), F32)
            for cs in cols:
                e = jnp.exp2(sbuf[slot, k, :, cs] - m)
                acc = acc + e
                ebuf[slot, k, :, cs] = e.astype(BF16)
            ls.append(jnp.sum(acc, axis=1, keepdims=True))
        ratio = (lam * ls[0] / ls[1]).astype(BF16)
        abuf[slot] = ebuf[slot, 0] - ratio * ebuf[slot, 1]
        cbuf[slot] = jnp.broadcast_to(1.0 / ls[0], cbuf.shape[1:])

    def stage_c(j, slot):
        r0 = pl.multiple_of(jnp.maximum(j - 2, 0) * rows, rows)
        o = jnp.dot(abuf[slot], vbuf[...], preferred_element_type=F32) * cbuf[slot]
        o = o * lax.rsqrt(jnp.mean(o * o, axis=1, keepdims=True) + LN_EPS) * gain
        o_ref[pl.ds(r0, rows), :] = o.astype(BF16)

    def body(t, carry):
        j = 2 * t
        stage_a(j, 0)
        stage_b(1)
        stage_c(j, 0)
        stage_a(j + 1, 1)
        stage_b(0)
        stage_c(j + 1, 1)
        return carry

    lax.fori_loop(0, (n_sub + 2) // 2, body, 0)


def _diff_attention(q, k_ctx, k_lat, v_ctx, v_lat, da_lambda, subln, *, heads, hd, lam_init):
    b, n, aw = q.shape
    nc = k_ctx.shape[1]
    vd = aw // heads
    rows = min(128, n)
    assert (n // rows) % 2 == 0
    blk_q = pl.BlockSpec((None, n, vd), lambda bi, h: (bi, 0, h))
    blk_c = pl.BlockSpec((None, nc, vd), lambda bi, h: (bi, 0, h))
    return pl.pallas_call(
        functools.partial(_attn_kernel, nc=nc, nl=n, hd=hd, lam_init=lam_init, rows=rows),
        out_shape=jax.ShapeDtypeStruct((b, n, aw), BF16),
        grid=(b, heads),
        in_specs=[blk_q, blk_c, blk_q, blk_c, blk_q,
                  pl.BlockSpec(da_lambda.shape, lambda bi, h: (0, 0)),
                  pl.BlockSpec((1, vd), lambda bi, h: (0, 0))],
        out_specs=blk_q,
        scratch_shapes=[pltpu.VMEM((nc + n, vd), BF16), pltpu.VMEM((nc + n, vd), BF16),
                        pltpu.VMEM((2, 2, rows, nc + n), F32), pltpu.VMEM((2, 2, rows, nc + n), BF16),
                        pltpu.VMEM((2, rows, nc + n), BF16), pltpu.VMEM((2, rows, vd), F32)],
        compiler_params=_cp(("arbitrary", "arbitrary"), VMEM_LIMIT),
        name="diff_attention",
    )(q, k_ctx, k_lat, v_ctx, v_lat, da_lambda, subln.reshape(1, vd))


def _lru_kernel(*refs, reverse, add_prev, nt, groups):
    if add_prev:
        (xp_ref, xc_ref, xn_ref, yprev_ref, cw_ref, cb_ref, wa_ref, ba_ref, wx_ref, bx_ref,
         lam_ref, h0_ref, y_ref, hf_ref, carry, ext) = refs
    else:
        (xp_ref, xc_ref, xn_ref, cw_ref, cb_ref, wa_ref, ba_ref, wx_ref, bx_ref,
         lam_ref, h0_ref, y_ref, hf_ref, carry, ext) = refs
        yprev_ref = None
    i = pl.program_id(1)
    ti = (nt - 1 - i) if reverse else i
    tn, w = xc_ref.shape

    @pl.when(i == 0)
    def _():
        carry[...] = h0_ref[...]

    ext[0:SUBLANES, :] = jnp.where(ti > 0, xp_ref[...], 0.0)
    ext[SUBLANES:SUBLANES + tn, :] = xc_ref[...]
    ext[SUBLANES + tn:2 * SUBLANES + tn, :] = jnp.where(ti < nt - 1, xn_ref[...], 0.0)
    left = CONV_W // 2
    xc = cb_ref[...]
    for k in range(CONV_W):
        off = SUBLANES - left + k
        xc = xc + ext[off:off + tn, :] * cw_ref[k:k + 1, :]

    xb = xc.astype(BF16)
    gw = w // groups

    def gate(w_ref, b_ref):
        parts = [jnp.dot(xb[:, g * gw:(g + 1) * gw], w_ref[g], preferred_element_type=F32)
                 for g in range(groups)]
        return jax.nn.sigmoid(jnp.concatenate(parts, axis=1) + b_ref[...])

    r = gate(wa_ref, ba_ref)
    ig = gate(wx_ref, bx_ref)
    log_a = (-LRU_C * jax.nn.softplus(-lam_ref[...])) * r
    a = jnp.exp(log_a)
    bcoef = jnp.sqrt(-jnp.tanh(log_a) * (a * a + 1.0)) * ig * xc

    ngroups = tn // SUBLANES
    a = a.reshape(ngroups, SUBLANES, w)
    bcoef = bcoef.reshape(ngroups, SUBLANES, w)
    row = lax.broadcasted_iota(I32, (ngroups, SUBLANES, w), 1)
    d = 1
    while d < SUBLANES:
        shift = (SUBLANES - d) if reverse else d
        a_sh = pltpu.roll(a, shift, 1)
        b_sh = pltpu.roll(bcoef, shift, 1)
        live = (row < SUBLANES - d) if reverse else (row >= d)
        bcoef = jnp.where(live, a * b_sh + bcoef, bcoef)
        a = jnp.where(live, a * a_sh, a)
        d *= 2
    hc = carry[...]
    npairs = ngroups // 2
    for p in (range(npairs - 1, -1, -1) if reverse else range(npairs)):
        hs = {}
        for g in ((2 * p + 1, 2 * p) if reverse else (2 * p, 2 * p + 1)):
            h = a[g] * hc + bcoef[g]
            hc = h[0:1, :] if reverse else h[SUBLANES - 1:SUBLANES, :]
            hs[g] = h
        rs = slice(2 * p * SUBLANES, (2 * p + 2) * SUBLANES)
        h2 = jnp.concatenate([hs[2 * p], hs[2 * p + 1]], axis=0)
        if add_prev:
            h2 = yprev_ref[rs, :].astype(F32) + h2
        y_ref[rs, :] = h2.astype(y_ref.dtype)
    carry[...] = hc
    hf_ref[...] = hc


def _rglru_dir(xr, y_prev, conv_w, conv_b, wa_bd, ba, wx_bd, bx, lam, h0, *, reverse, tn):
    b, n, w = xr.shape
    tn = min(tn, n)
    nt = n // tn
    groups = wa_bd.shape[0]
    nb8 = n // SUBLANES
    per8 = tn // SUBLANES

    def tmap(i):
        return (nt - 1 - i) if reverse else i

    cur = pl.BlockSpec((None, tn, w), lambda bi, i: (bi, tmap(i), 0))
    halo_p = pl.BlockSpec((None, SUBLANES, w), lambda bi, i: (bi, jnp.maximum(tmap(i) * per8 - 1, 0), 0))
    halo_n = pl.BlockSpec((None, SUBLANES, w), lambda bi, i: (bi, jnp.minimum((tmap(i) + 1) * per8, nb8 - 1), 0))
    row_w = pl.BlockSpec((1, w), lambda bi, i: (0, 0))
    per_b = pl.BlockSpec((None, 1, w), lambda bi, i: (bi, 0, 0))
    gate_w = pl.BlockSpec(wa_bd.shape, lambda bi, i: (0, 0, 0))
    add_prev = y_prev is not None
    in_specs = [halo_p, cur, halo_n] + ([cur] if add_prev else []) + [
        pl.BlockSpec((CONV_W, w), lambda bi, i: (0, 0)), row_w, gate_w, row_w, gate_w, row_w, row_w, per_b]
    args = [xr, xr, xr] + ([y_prev] if add_prev else []) + [
        conv_w, conv_b.reshape(1, w), wa_bd, ba.reshape(1, w), wx_bd, bx.reshape(1, w),
        lam.reshape(1, w), h0]
    return pl.pallas_call(
        functools.partial(_lru_kernel, reverse=reverse, add_prev=add_prev, nt=nt, groups=groups),
        out_shape=(jax.ShapeDtypeStruct((b, n, w), BF16), jax.ShapeDtypeStruct((b, 1, w), F32)),
        grid=(b, nt),
        in_specs=in_specs,
        out_specs=(cur, per_b),
        scratch_shapes=[pltpu.VMEM((1, w), F32), pltpu.VMEM((tn + 2 * SUBLANES, w), F32)],
        compiler_params=_cp(("parallel", "arbitrary"), VMEM_LIMIT),
        name="rglru_rev" if reverse else "rglru_fwd",
    )(*args)


def _block_diag_groups(wh, group_width):
    heads, blk, _ = wh.shape
    per = group_width // blk
    groups = heads // per
    whg = wh.reshape(groups, per, blk, blk)
    eye = jnp.eye(per, dtype=wh.dtype)
    bd = jnp.einsum('gpij,pq->gpiqj', whg, eye).reshape(groups, group_width, group_width)
    return bd.astype(BF16)


def _residual_ln_mod(x, y, g1, lng, lnb, sh2, sc2, alpha):
    x1 = _ln(alpha * x + g1 * y) * lng + lnb
    t = _ln(x1) * (1.0 + sc2) + sh2
    return x1, t


def _outproj_even_kernel(r_ref, g_ref, o_ref, x_ref, w_ref, g1_ref, lng_ref, lnb_ref, sh2_ref, sc2_ref,
                         x1_ref, t_ref, *, lw, alpha):
    z = (r_ref[...].astype(F32) * jax.nn.gelu(g_ref[...].astype(F32))).astype(BF16)
    y = (jnp.dot(z, w_ref[0:lw, :], preferred_element_type=F32)
         + jnp.dot(o_ref[...], w_ref[lw:, :], preferred_element_type=F32))
    x1, t = _residual_ln_mod(x_ref[...], y, g1_ref[...], lng_ref[...], lnb_ref[...],
                             sh2_ref[...], sc2_ref[...], alpha)
    x1_ref[...] = x1
    t_ref[...] = t.astype(BF16)


def _outproj_even(r, g, o, x, w_out_b, g1, lng, lnb, sh2, sc2, *, alpha, tm):
    b, n, d = x.shape
    lw = r.shape[2]
    aw = o.shape[2]
    tm = min(tm, n)
    tok = lambda bi, i: (bi, i, 0)
    per_b = pl.BlockSpec((None, 1, d), lambda bi, i: (bi, 0, 0))
    row = pl.BlockSpec((1, d), lambda bi, i: (0, 0))
    return pl.pallas_call(
        functools.partial(_outproj_even_kernel, lw=lw, alpha=alpha),
        out_shape=(jax.ShapeDtypeStruct((b, n, d), F32), jax.ShapeDtypeStruct((b, n, d), BF16)),
        grid=(b, n // tm),
        in_specs=[pl.BlockSpec((None, tm, lw), tok), pl.BlockSpec((None, tm, lw), tok),
                  pl.BlockSpec((None, tm, aw), tok), pl.BlockSpec((None, tm, d), tok),
                  pl.BlockSpec(w_out_b.shape, lambda bi, i: (0, 0)),
                  per_b, row, row, per_b, per_b],
        out_specs=(pl.BlockSpec((None, tm, d), tok), pl.BlockSpec((None, tm, d), tok)),
        compiler_params=_cp(("parallel", "parallel"), VMEM_LIMIT),
        name="outproj_even",
    )(r, g, o, x, w_out_b, g1, lng.reshape(1, d), lnb.reshape(1, d), sh2, sc2)


def _chan_dft_kernel(x_ref, sh_ref, sc_ref, tab_ref, zr_ref, zi_ref, *, groups, gd):
    h = (_ln(x_ref[...]) * (1.0 + sc_ref[...]) + sh_ref[...]).astype(BF16)
    for g in range(groups):
        z = jnp.dot(h[:, g * gd:(g + 1) * gd], tab_ref[...], preferred_element_type=F32)
        zr_ref[:, g * gd:(g + 1) * gd] = z[:, 0:gd].astype(BF16)
        zi_ref[:, g * gd:(g + 1) * gd] = z[:, gd:2 * gd].astype(BF16)


def _chan_dft(x, shift, scale, tab, *, groups, tm):
    b, n, d = x.shape
    gd = d // groups
    tm = min(tm, n)
    tok = lambda bi, i: (bi, i, 0)
    per_b = pl.BlockSpec((None, 1, d), lambda bi, i: (bi, 0, 0))
    return pl.pallas_call(
        functools.partial(_chan_dft_kernel, groups=groups, gd=gd),
        out_shape=(jax.ShapeDtypeStruct((b, n, d), BF16), jax.ShapeDtypeStruct((b, n, d), BF16)),
        grid=(b, n // tm),
        in_specs=[pl.BlockSpec((None, tm, d), tok), per_b, per_b,
                  pl.BlockSpec(tab.shape, lambda bi, i: (0, 0))],
        out_specs=(pl.BlockSpec((None, tm, d), tok), pl.BlockSpec((None, tm, d), tok)),
        compiler_params=_cp(("parallel", "parallel"), VMEM_LIMIT),
        name="chan_dft",
    )(x, shift, scale, tab)


FFT_C = 64


def _fft_pitch(group):
    p = -(-group // SUBLANES)
    return SUBLANES * (p if p % 2 else p + 1)


def _tok_fft_kernel(zr_ref, zi_ref, m1_ref, m3_ref, tc_ref, ts_ref, o_ref, zsr, zsi, asr, asi, ob, *, nr):
    c_len = FFT_C
    pz = zsr.shape[0] // nr
    pa = asr.shape[0] // c_len
    for r in range(nr):
        zsr[pz * r:pz * r + c_len, :] = zr_ref[c_len * r:c_len * (r + 1), :].astype(F32)
        zsi[pz * r:pz * r + c_len, :] = zi_ref[c_len * r:c_len * (r + 1), :].astype(F32)
    m1 = m1_ref[...]
    for c in range(c_len):
        x2 = jnp.concatenate([zsr[pl.ds(c, nr, stride=pz), :], zsi[pl.ds(c, nr, stride=pz), :]], axis=0)
        a2 = jnp.dot(m1, x2.astype(BF16), preferred_element_type=F32)
        ar, ai = a2[0:nr], a2[nr:2 * nr]
        tcv = tc_ref[c * nr:(c + 1) * nr, :]
        tsv = ts_ref[c * nr:(c + 1) * nr, :]
        asr[pa * c:pa * c + nr, :] = ar * tcv + ai * tsv
        asi[pa * c:pa * c + nr, :] = ai * tcv - ar * tsv
    m3 = m3_ref[...]
    for k1 in range(nr):
        y2 = jnp.concatenate([asr[pl.ds(k1, c_len, stride=pa), :], asi[pl.ds(k1, c_len, stride=pa), :]], axis=0)
        ob[pl.ds(k1, c_len, stride=pa), :] = jnp.dot(m3, y2.astype(BF16), preferred_element_type=F32)
    for k2 in range(c_len):
        o_ref[nr * k2:nr * (k2 + 1), :] = ob[pa * k2:pa * k2 + nr, :].astype(BF16)


def _tok_fft(zr, zi):
    b, n, d = zr.shape
    nr = n // FFT_C
    pz = _fft_pitch(FFT_C)
    pa = _fft_pitch(nr)
    kr = np.arange(nr, dtype=np.float64)
    ang_r = 2.0 * np.pi * (np.outer(kr, kr) % nr) / nr
    cr, sr = np.cos(ang_r), np.sin(ang_r)
    m1 = jnp.asarray(np.block([[cr, -sr], [-sr, -cr]]), F32).astype(BF16)
    kc = np.arange(FFT_C, dtype=np.float64)
    ang_c = 2.0 * np.pi * (np.outer(kc, kc) % FFT_C) / FFT_C
    m3 = jnp.asarray(np.concatenate([np.cos(ang_c), np.sin(ang_c)], axis=1), F32).astype(BF16)
    ang_t = 2.0 * np.pi * (np.outer(kc, kr) % n) / n
    tc = jnp.broadcast_to(jnp.asarray(np.cos(ang_t).reshape(FFT_C * nr, 1), F32), (FFT_C * nr, LANES))
    ts = jnp.broadcast_to(jnp.asarray(np.sin(ang_t).reshape(FFT_C * nr, 1), F32), (FFT_C * nr, LANES))
    slab = pl.BlockSpec((None, n, LANES), lambda bi, l: (bi, 0, l))
    const = lambda a: pl.BlockSpec(a.shape, lambda bi, l: (0, 0))
    return pl.pallas_call(
        functools.partial(_tok_fft_kernel, nr=nr),
        out_shape=jax.ShapeDtypeStruct((b, n, d), BF16),
        grid=(b, d // LANES),
        in_specs=[slab, slab, const(m1), const(m3), const(tc), const(ts)],
        out_specs=slab,
        scratch_shapes=[pltpu.VMEM((nr * pz, LANES), F32), pltpu.VMEM((nr * pz, LANES), F32),
                        pltpu.VMEM((FFT_C * pa, LANES), F32), pltpu.VMEM((FFT_C * pa, LANES), F32),
                        pltpu.VMEM((FFT_C * pa, LANES), F32)],
        compiler_params=_cp(("parallel", "parallel"), VMEM_LIMIT),
        name="tok_fft",
    )(zr, zi, m1, m3, tc, ts)


def _chan_dft_table(gd):
    c = np.arange(gd, dtype=np.float64)
    ang_c = 2.0 * np.pi * (np.outer(c, c) % gd) / gd
    return jnp.asarray(np.concatenate([np.cos(ang_c), np.sin(ang_c)], axis=1), F32).astype(BF16)


def _outproj_odd_kernel(wv_ref, x_ref, w_ref, b_ref, g1_ref, lng_ref, lnb_ref, sh2_ref, sc2_ref,
                        x1_ref, t_ref, *, alpha, norm):
    y = jnp.dot(wv_ref[...], w_ref[...], preferred_element_type=F32) * norm + b_ref[...]
    x1, t = _residual_ln_mod(x_ref[...], y, g1_ref[...], lng_ref[...], lnb_ref[...],
                             sh2_ref[...], sc2_ref[...], alpha)
    x1_ref[...] = x1
    t_ref[...] = t.astype(BF16)


def _outproj_odd(wv, x, w_b, bias, g1, lng, lnb, sh2, sc2, *, alpha, norm, tm):
    b, n, d = x.shape
    tm = min(tm, n)
    tok = lambda bi, i: (bi, i, 0)
    per_b = pl.BlockSpec((None, 1, d), lambda bi, i: (bi, 0, 0))
    row = pl.BlockSpec((1, d), lambda bi, i: (0, 0))
    return pl.pallas_call(
        functools.partial(_outproj_odd_kernel, alpha=alpha, norm=norm),
        out_shape=(jax.ShapeDtypeStruct((b, n, d), F32), jax.ShapeDtypeStruct((b, n, d), BF16)),
        grid=(b, n // tm),
        in_specs=[pl.BlockSpec((None, tm, d), tok), pl.BlockSpec((None, tm, d), tok),
                  pl.BlockSpec(w_b.shape, lambda bi, i: (0, 0)), row,
                  per_b, row, row, per_b, per_b],
        out_specs=(pl.BlockSpec((None, tm, d), tok), pl.BlockSpec((None, tm, d), tok)),
        compiler_params=_cp(("parallel", "parallel"), VMEM_LIMIT),
        name="outproj_odd",
    )(wv, x, w_b, bias.reshape(1, d), g1, lng.reshape(1, d), lnb.reshape(1, d), sh2, sc2)


def _route_kernel(t_ref, w_ref, b_ref, up_ref, lp_ref, tokm_ref, cnt_ref, *, ng, epg, chunk):
    tm = t_ref.shape[0]
    ne = ng * epg
    logits = jnp.dot(t_ref[...].astype(BF16), w_ref[...], preferred_element_type=F32) + b_ref[...]
    lt = logits.T
    best = lt[0:1, :]
    bi = jnp.zeros((1, tm), I32)
    for k in range(1, ng):
        gk = lt[k:k + 1, :]
        upd = gk > best
        bi = jnp.where(upd, k, bi)
        best = jnp.where(upd, gk, best)
    den = jnp.zeros((1, tm), F32)
    for k in range(ng):
        den = den + jnp.exp(lt[k:k + 1, :] - best)
    p_g = 1.0 / den
    fsel = lt[SUBLANES:SUBLANES + epg, :]
    for k in range(1, ng):
        fsel = jnp.where(bi == k, lt[SUBLANES + k * epg:SUBLANES + (k + 1) * epg, :], fsel)
    neg = jnp.full((1, tm), -jnp.inf, F32)
    m1, m2 = neg, neg
    i1 = jnp.zeros((1, tm), I32)
    i2 = jnp.zeros((1, tm), I32)
    for j in range(epg):
        v = fsel[j:j + 1, :]
        gt1 = v > m1
        gt2 = v > m2
        m2 = jnp.where(gt1, m1, jnp.where(gt2, v, m2))
        i2 = jnp.where(gt1, i1, jnp.where(gt2, j, i2))
        m1 = jnp.where(gt1, v, m1)
        i1 = jnp.where(gt1, j, i1)
    e21 = jnp.exp(m2 - m1)
    w1 = p_g / (1.0 + e21)
    w2 = p_g * e21 / (1.0 + e21)
    e1 = bi * epg + i1
    e2 = bi * epg + i2

    e = jnp.concatenate([e1, e2], axis=1)
    rows = lax.broadcasted_iota(I32, (ne, 2 * tm), 0)
    onehot = jnp.where(rows == e, 1.0, 0.0)
    before = jnp.dot(onehot.astype(BF16), up_ref[...], preferred_element_type=F32)
    tot = jnp.sum(onehot, axis=1, keepdims=True)
    slots = jnp.floor((tot + (chunk - 1.0)) * (1.0 / chunk)) * chunk
    slots_b = jnp.broadcast_to(slots, (ne, LANES))
    rowe = lax.broadcasted_iota(I32, (ne, LANES), 0)
    incl = slots_b
    d = 1
    while d < ne:
        incl = incl + jnp.where(rowe >= d, pltpu.roll(incl, d, 0), 0.0)
        d *= 2
    seg_off = jnp.tile(incl - slots_b, (1, 2 * tm // LANES))
    lpos = jnp.sum(onehot * (before + seg_off), axis=0, keepdims=True)
    lp0 = lpos[:, 0:tm]
    lp1 = lpos[:, tm:2 * tm]
    row8 = lax.broadcasted_iota(I32, (SUBLANES, tm), 0)
    lp_ref[...] = jnp.where(row8 == 0, lp0, jnp.where(row8 == 1, lp1, 0.0)).astype(I32)
    rowl = lax.broadcasted_iota(I32, (LANES, tm), 0)
    tokm = jnp.where(rowl == 0, w1, jnp.where(rowl == 1, w2, jnp.where(rowl == 2, lp0, jnp.where(rowl == 3, lp1, 0.0))))
    tokm_ref[...] = tokm.T
    cnt_ref[...] = jnp.broadcast_to(tot, (ne, LANES))


def _route(t, wcat_b, bcat, upper, *, ng, epg, tm, chunk):
    tt, d = t.shape
    ne = ng * epg
    nt = tt // tm
    return pl.pallas_call(
        functools.partial(_route_kernel, ng=ng, epg=epg, chunk=chunk),
        out_shape=(jax.ShapeDtypeStruct((SUBLANES, tt), I32), jax.ShapeDtypeStruct((tt, LANES), F32),
                   jax.ShapeDtypeStruct((ne, nt * LANES), F32)),
        grid=(nt,),
        in_specs=[pl.BlockSpec((tm, d), lambda i: (i, 0)),
                  pl.BlockSpec((d, LANES), lambda i: (0, 0)),
                  pl.BlockSpec((1, LANES), lambda i: (0, 0)),
                  pl.BlockSpec(upper.shape, lambda i: (0, 0))],
        out_specs=(pl.BlockSpec((SUBLANES, tm), lambda i: (0, i)),
                   pl.BlockSpec((tm, LANES), lambda i: (i, 0)),
                   pl.BlockSpec((ne, LANES), lambda i: (0, i))),
        compiler_params=_cp(("parallel",), VMEM_LIMIT),
        name="route_sort",
    )(t, wcat_b, bcat, upper)


def _dispatch_kernel(row_ref, tot_ref, zrow_ref, nz_ref, t_ref, lp_ref, xs_ref, stage, zbuf, sems,
                     *, qmax, zmax, chunk, nsteps):
    i = pl.program_id(0)
    slot = i % 2
    sp = stage.shape[1] // PACK_SUB
    tm = t_ref.shape[0]
    crow = chunk * PACK_SUB

    @pl.when(i == 0)
    def _():
        zbuf[...] = jnp.zeros_like(zbuf)
        _for_each_chunk(nz_ref[0], lambda q: pltpu.make_async_copy(
            zbuf, xs_ref.at[pl.ds(_tile_row(zrow_ref[q], SUBLANES), crow)], sems.at[1]).start())
        _wait_chunks(lambda k: pltpu.make_async_copy(xs_ref.at[pl.ds(0, k * crow)], xs_ref.at[pl.ds(0, k * crow)],
                                                     sems.at[1]), nz_ref[0], zmax)

    lp = lp_ref[...]
    prow = lax.broadcasted_iota(I32, (sp, tm), 0)
    perm = jnp.where(prow == lp[0:1, :], 1.0, jnp.where(prow == lp[1:2, :], 1.0, 0.0)).astype(BF16)
    srt = jnp.dot(perm, t_ref[...], preferred_element_type=F32)
    _rows_to_tiles(stage.at[slot], _pack_rows(srt))

    def chunk_copy(sl, src, dst):
        return pltpu.make_async_copy(stage.at[sl, pl.ds(_tile_row(src, crow), crow)],
                                     xs_ref.at[pl.ds(_tile_row(dst, SUBLANES), crow)], sems.at[sl])

    def drain(sl, n):
        def copy_of(k):
            return pltpu.make_async_copy(stage.at[sl, pl.ds(0, k * crow)], xs_ref.at[pl.ds(0, k * crow)], sems.at[sl])
        _wait_chunks(copy_of, n, qmax)

    @pl.when(i >= 1)
    def _():
        drain(1 - slot, tot_ref[jnp.maximum(i - 1, 0)])

    _for_each_chunk(tot_ref[i], lambda q: chunk_copy(slot, pl.multiple_of(q * chunk, chunk),
                                                     row_ref[i * qmax + q]).start())

    @pl.when(i == nsteps - 1)
    def _():
        drain(slot, tot_ref[i])


def _dispatch(chunk_row, tile_nch, zero_row, n_zero, t, lp, *, n_rows, ne, tm, chunk):
    ttot, d = t.shape
    nsteps = ttot // tm
    sp = 2 * tm + ne * chunk
    grid_spec = pltpu.PrefetchScalarGridSpec(
        num_scalar_prefetch=4,
        grid=(nsteps,),
        in_specs=[pl.BlockSpec((tm, d), lambda i, *_: (i, 0)),
                  pl.BlockSpec((SUBLANES, tm), lambda i, *_: (0, i))],
        out_specs=pl.BlockSpec(memory_space=pl.ANY),
        scratch_shapes=[pltpu.VMEM((2, sp * PACK_SUB, LANES), jnp.uint32),
                        pltpu.VMEM((chunk * PACK_SUB, LANES), jnp.uint32), pltpu.SemaphoreType.DMA((2,))],
    )
    assert d == 2 * PACK_SUB * LANES
    return pl.pallas_call(
        functools.partial(_dispatch_kernel, qmax=chunk_row.shape[0] // nsteps, zmax=zero_row.shape[0],
                          chunk=chunk, nsteps=nsteps),
        out_shape=jax.ShapeDtypeStruct((n_rows * PACK_SUB, LANES), jnp.uint32),
        grid_spec=grid_spec,
        compiler_params=_cp(("arbitrary",), VMEM_LIMIT),
        name="moe_dispatch",
    )(chunk_row, tile_nch, zero_row, n_zero, t, lp)


def _moe_kernel(be_ref, bv_ref, nu_ref, xs_ref, w1_ref, w3_ref, w2_ref, y_ref, w13b, w2b, *, ff):
    i = pl.program_id(0)
    half = MOE_ROWS // 2

    def ffn(nrows):
        x = _unpack_rows(_rows_from_tiles(xs_ref, nrows))
        h = jnp.dot(x, w13b[...], preferred_element_type=F32)
        hid = (_silu(h[:, 0:ff]) * h[:, ff:2 * ff]).astype(BF16)
        y = jnp.dot(hid, w2b[...], preferred_element_type=F32)
        _rows_to_tiles(y_ref, _pack_rows(y.astype(BF16).astype(F32)))

    @pl.when(i < nu_ref[0])
    def _():
        prev = be_ref[jnp.maximum(i - 1, 0)]

        @pl.when((i == 0) | (be_ref[i] != prev))
        def _():
            w13b[:, 0:ff] = w1_ref[...].astype(BF16)
            w13b[:, ff:2 * ff] = w3_ref[...].astype(BF16)
            w2b[...] = w2_ref[...].astype(BF16)

        @pl.when(bv_ref[i] > half)
        def _():
            ffn(MOE_ROWS)

        @pl.when(bv_ref[i] <= half)
        def _():
            ffn(half)
            y_ref[half * PACK_SUB:, :] = jnp.zeros((half * PACK_SUB, LANES), y_ref.dtype)

    @pl.when(i >= nu_ref[0])
    def _():
        y_ref[...] = jnp.zeros_like(y_ref)


def _moe_experts(block_e, block_rows, n_used, xs, w1, w3, w2, *, layer):
    d = w1.shape[-2]
    ff = w1.shape[-1]
    blk = MOE_ROWS * PACK_SUB
    nb = xs.shape[0] // blk
    wmap = lambda i, be, bv, nu: (layer, be[i], 0, 0)
    grid_spec = pltpu.PrefetchScalarGridSpec(
        num_scalar_prefetch=3,
        grid=(nb,),
        in_specs=[pl.BlockSpec((blk, LANES), lambda i, be, bv, nu: (i, 0)),
                  pl.BlockSpec((None, None, d, ff), wmap),
                  pl.BlockSpec((None, None, d, ff), wmap),
                  pl.BlockSpec((None, None, ff, d), wmap)],
        out_specs=pl.BlockSpec((blk, LANES), lambda i, be, bv, nu: (i, 0)),
        scratch_shapes=[pltpu.VMEM((d, 2 * ff), BF16), pltpu.VMEM((ff, d), BF16)],
    )
    return pl.pallas_call(
        functools.partial(_moe_kernel, ff=ff),
        out_shape=jax.ShapeDtypeStruct(xs.shape, xs.dtype),
        grid_spec=grid_spec,
        compiler_params=_cp(("arbitrary",), VMEM_LIMIT),
        name="moe_experts",
    )(block_e, block_rows, n_used, xs, w1, w3, w2)


def _combine_kernel(row_ref, tot_ref, yb_ref, x_ref, tokm_ref, g2_ref, lng_ref, lnb_ref, *rest,
                    qmax, chunk, nsteps, alpha, dft_groups):
    if dft_groups:
        sh_ref, sc_ref, tab_ref, o_ref, zr_ref, zi_ref, stage, sems = rest
    else:
        o_ref, stage, sems = rest
    i = pl.program_id(0)
    slot = i % 2
    sp = stage.shape[1] // PACK_SUB
    tm = x_ref.shape[0]
    crow = chunk * PACK_SUB

    def chunk_copy(sl, src, dst):
        return pltpu.make_async_copy(yb_ref.at[pl.ds(_tile_row(src, SUBLANES), crow)],
                                     stage.at[sl, pl.ds(_tile_row(dst, crow), crow)], sems.at[sl])

    def issue_tile(step, sl):
        _for_each_chunk(tot_ref[step], lambda q: chunk_copy(sl, row_ref[step * qmax + q],
                                                            pl.multiple_of(q * chunk, chunk)).start())

    @pl.when(i == 0)
    def _():
        stage[...] = jnp.zeros_like(stage)
        issue_tile(0, 0)

    @pl.when(i + 1 < nsteps)
    def _():
        issue_tile(jnp.minimum(i + 1, nsteps - 1), 1 - slot)

    def copy_of(k):
        return pltpu.make_async_copy(yb_ref.at[pl.ds(0, k * crow)], stage.at[slot, pl.ds(0, k * crow)], sems.at[slot])
    _wait_chunks(copy_of, tot_ref[i], qmax)

    tk = tokm_ref[...]
    pos = lax.broadcasted_iota(I32, (tm, sp), 1).astype(F32)
    st = _unpack_rows(_rows_from_tiles(stage.at[slot], sp))
    gsel = jnp.where(pos == tk[:, 2:3], tk[:, 0:1], jnp.where(pos == tk[:, 3:4], tk[:, 1:2], 0.0))
    m = jnp.dot(gsel.astype(BF16), st, preferred_element_type=F32)
    xo = _ln(alpha * x_ref[...] + g2_ref[...] * m) * lng_ref[...] + lnb_ref[...]
    o_ref[...] = xo
    if dft_groups:
        gd = xo.shape[1] // dft_groups
        h = (_ln(xo) * (1.0 + sc_ref[...]) + sh_ref[...]).astype(BF16)
        for g in range(dft_groups):
            z = jnp.dot(h[:, g * gd:(g + 1) * gd], tab_ref[...], preferred_element_type=F32)
            zr_ref[:, g * gd:(g + 1) * gd] = z[:, 0:gd].astype(BF16)
            zi_ref[:, g * gd:(g + 1) * gd] = z[:, gd:2 * gd].astype(BF16)


def _combine(chunk_row, tile_nch, yb, x, tokm, g2, lng, lnb, dft, *, ne, tm, chunk, n_per_batch, alpha):
    ttot, d = x.shape
    nsteps = ttot // tm
    per = n_per_batch // tm
    sp = 2 * tm + ne * chunk
    tok = pl.BlockSpec((tm, d), lambda i, *_: (i, 0))
    per_b = pl.BlockSpec((None, 1, d), lambda i, *_: (i // per, 0, 0))
    row = pl.BlockSpec((1, d), lambda i, *_: (0, 0))
    in_specs = [pl.BlockSpec(memory_space=pl.ANY), tok, pl.BlockSpec((tm, LANES), lambda i, *_: (i, 0)),
                per_b, row, row]
    args = [chunk_row, tile_nch, yb, x, tokm, g2, lng.reshape(1, d), lnb.reshape(1, d)]
    out_specs, out_shape, groups = tok, jax.ShapeDtypeStruct((ttot, d), F32), 0
    if dft is not None:
        shift, scale, tab, groups = dft
        in_specs += [per_b, per_b, pl.BlockSpec(tab.shape, lambda i, *_: (0, 0))]
        args += [shift, scale, tab]
        out_specs = (tok, tok, tok)
        out_shape = (out_shape, jax.ShapeDtypeStruct((ttot, d), BF16), jax.ShapeDtypeStruct((ttot, d), BF16))
    grid_spec = pltpu.PrefetchScalarGridSpec(
        num_scalar_prefetch=2,
        grid=(nsteps,),
        in_specs=in_specs,
        out_specs=out_specs,
        scratch_shapes=[pltpu.VMEM((2, sp * PACK_SUB, LANES), yb.dtype), pltpu.SemaphoreType.DMA((2,))],
    )
    return pl.pallas_call(
        functools.partial(_combine_kernel, qmax=chunk_row.shape[0] // nsteps, chunk=chunk, nsteps=nsteps,
                          alpha=alpha, dft_groups=groups),
        out_shape=out_shape,
        grid_spec=grid_spec,
        compiler_params=_cp(("arbitrary",), VMEM_LIMIT),
        name="moe_combine_dft" if groups else "moe_combine",
    )(*args)


def _hier_moe_layer(x1, t, g2, lng, lnb, wg, bg, wf, bf, w1, w3, w2, dft, *, layer, alpha):
    b, n, d = x1.shape
    ttot = b * n
    ng = wg.shape[1]
    ne = wf.shape[1]
    epg = ne // ng
    tm = min(MOE_TILE, n)
    chunk = MOE_CHUNK
    nt = ttot // tm
    tflat = t.reshape(ttot, d)
    wcat = jnp.zeros((d, LANES), F32).at[:, 0:ng].set(wg).at[:, SUBLANES:SUBLANES + ne].set(wf).astype(BF16)
    bcat = jnp.zeros((1, LANES), F32).at[0, 0:ng].set(bg).at[0, SUBLANES:SUBLANES + ne].set(bf)
    ar = jnp.arange(2 * tm, dtype=I32)
    upper = (ar[:, None] < ar[None, :]).astype(BF16)
    lp, tokm, cnt = _route(tflat, wcat, bcat, upper, ng=ng, epg=epg, tm=tm, chunk=chunk)

    tile_cnt = cnt.reshape(ne, nt, LANES)[:, :, 0].T.astype(I32)
    seg_len = ((tile_cnt + 1) // 2) * 2
    counts = jnp.sum(seg_len, axis=0)
    padded = ((counts + chunk + MOE_ROWS - 1) // MOE_ROWS) * MOE_ROWS
    pend = jnp.cumsum(padded)
    pstart = pend - padded
    base = jnp.cumsum(seg_len, axis=0) - seg_len
    seg_row = pstart[None, :] + base
    nch = (tile_cnt + chunk - 1) // chunk
    nch_end = jnp.cumsum(nch, axis=1)
    tile_nch = nch_end[:, -1].astype(I32)
    qmax = 2 * tm // chunk + ne
    qs = jnp.arange(qmax, dtype=I32)
    e_q = jnp.minimum(jnp.sum((nch_end[:, None, :] <= qs[None, :, None]).astype(I32), axis=2), ne - 1)
    pick = e_q[:, :, None] == jnp.arange(ne, dtype=I32)[None, None, :]
    first_q = jnp.sum(jnp.where(pick, (nch_end - nch)[:, None, :], 0), axis=2)
    first_row = jnp.sum(jnp.where(pick, seg_row[:, None, :], 0), axis=2)
    chunk_row = (first_row + (qs[None, :] - first_q) * chunk).astype(I32).reshape(nt * qmax)
    nb = -(-(2 * ttot + nt * ne + ne * (chunk + MOE_ROWS)) // MOE_ROWS)
    bstart = jnp.arange(nb, dtype=I32) * MOE_ROWS
    block_e = jnp.minimum(jnp.sum((pend[None, :] <= bstart[:, None]).astype(I32), axis=1), ne - 1)
    n_used = (pend[-1] // MOE_ROWS).astype(I32).reshape(1)
    zstart = jnp.concatenate([pstart + (counts // chunk) * chunk, pend[-1:]])
    zend = jnp.concatenate([pend, jnp.full((1,), nb * MOE_ROWS, pend.dtype)])
    zcnt = (zend - zstart) // chunk
    zcum = jnp.cumsum(zcnt)
    zmax = ne * ((MOE_ROWS + 2 * chunk) // chunk + 1) + (nb - (2 * ttot) // MOE_ROWS) * (MOE_ROWS // chunk)
    zq = jnp.arange(zmax, dtype=I32)
    zseg = jnp.minimum(jnp.sum((zcum[None, :] <= zq[:, None]).astype(I32), axis=1), ne)
    zpick = zseg[:, None] == jnp.arange(ne + 1, dtype=I32)[None, :]
    zfirst = jnp.sum(jnp.where(zpick, (zcum - zcnt)[None, :], 0), axis=1)
    zero_row = (jnp.sum(jnp.where(zpick, zstart[None, :], 0), axis=1) + (zq - zfirst) * chunk).astype(I32)
    n_zero = zcum[-1].astype(I32).reshape(1)

    xs = _dispatch(chunk_row, tile_nch, zero_row, n_zero, tflat, lp, n_rows=nb * MOE_ROWS, ne=ne, tm=tm,
                   chunk=chunk)
    bpick = block_e[:, None] == jnp.arange(ne, dtype=I32)[None, :]
    block_rows = jnp.clip(jnp.sum(jnp.where(bpick, (pstart + counts)[None, :], 0), axis=1) - bstart,
                          0, MOE_ROWS).astype(I32)
    yb = _moe_experts(block_e.astype(I32), block_rows, n_used, xs, w1, w3, w2, layer=layer)
    out = _combine(chunk_row, tile_nch, yb, x1.reshape(ttot, d), tokm, g2, lng, lnb, dft,
                   ne=ne, tm=tm, chunk=chunk, n_per_batch=n, alpha=alpha)
    if dft is None:
        return out.reshape(b, n, d), None
    return out[0].reshape(b, n, d), (out[1].reshape(b, n, d), out[2].reshape(b, n, d))


def kernel(x, c, ctx, c_ctx, ada_w, ada_b, ln_g, ln_b, ev_w_in, ev_conv_w, ev_conv_b, ev_gate_a_w,
           ev_gate_a_b, ev_gate_x_w, ev_gate_x_b, ev_lru_lambda, ev_da_lambda, ev_da_subln, ev_w_out,
           od_w_out, od_b_out, moe_wg, moe_bg, moe_wf, moe_bf, moe_w1, moe_w3, moe_w2):
    bsz, n_lat, d = x.shape
    depth = ada_w.shape[0]
    alpha = (2.0 * depth) ** 0.25
    lw = ev_conv_w.shape[-1]
    hd = ev_da_lambda.shape[-1]
    vd = ev_da_subln.shape[-1]
    aw = (ev_w_in.shape[-1] - 2 * lw) // 3
    heads = aw // vd
    fnet_groups = 4

    rows = ((bsz + 1 + SUBLANES - 1) // SUBLANES) * SUBLANES
    cond = jnp.zeros((rows, d), F32).at[0:bsz].set(c).at[bsz].set(c_ctx)
    ada = _ada_terms(cond, ada_w, ada_b).reshape(depth, rows, 6, d)

    def lat_term(l, k):
        return ada[l, 0:bsz, k, :].reshape(bsz, 1, d)

    def ctx_term(l, k):
        return jnp.broadcast_to(ada[l, bsz, k, :].reshape(1, 1, d), (bsz, 1, d))

    pre_dft = None
    for l in range(depth):
        ctx_live = any(m % 2 == 0 for m in range(l + 1, depth))
        assert not ctx_live, "context stream update is only needed for depth > 2"
        sh1, sc1, g1, sh2, sc2, g2 = [lat_term(l, k) for k in range(6)]
        if l % 2 == 0:
            e = l // 2
            lam_init = 0.8 - 0.6 * math.exp(-0.3 * l)
            w_in_b = ev_w_in[e].astype(BF16)
            cos_t, sin_t = _rope_tables(n_lat, hd)
            qscale = hd ** -0.5 * math.log2(math.e)
            g_l, xr_l, q_l, k_l, v_l = _project_even(x, sh1, sc1, w_in_b, cos_t, sin_t, rope=True,
                                                     lw=lw, aw=aw, qscale=qscale, tm=1024)
            n_ctx = ctx.shape[1]
            _, xr_c, _, k_c, v_c = _project_even(ctx, ctx_term(l, 0), ctx_term(l, 1), w_in_b,
                                                 cos_t[0:n_ctx], sin_t[0:n_ctx], rope=False,
                                                 lw=lw, aw=aw, qscale=qscale, tm=256)
            o_l = _diff_attention(q_l, k_c, k_l, v_c, v_l, ev_da_lambda[e], ev_da_subln[e],
                                  heads=heads, hd=hd, lam_init=lam_init)
            gwid = 256
            y = None
            for dirn, rev in ((0, False), (1, True)):
                wa_bd = _block_diag_groups(ev_gate_a_w[e, dirn], gwid)
                wx_bd = _block_diag_groups(ev_gate_x_w[e, dirn], gwid)
                common = (ev_conv_w[e], ev_conv_b[e], wa_bd, ev_gate_a_b[e, dirn], wx_bd,
                          ev_gate_x_b[e, dirn], ev_lru_lambda[e, dirn])
                h_zero = jnp.zeros((bsz, 1, lw), F32)
                _, h_fin = _rglru_dir(xr_c, None, *common, h_zero, reverse=rev, tn=256)
                y, _ = _rglru_dir(xr_l, y, *common, h_fin, reverse=rev, tn=512)
            x1, t = _outproj_even(y, g_l, o_l, x, ev_w_out[e].astype(BF16), g1, ln_g[l, 0], ln_b[l, 0],
                                  sh2, sc2, alpha=alpha, tm=1024)
        else:
            o = l // 2
            gd = d // fnet_groups
            if pre_dft is None:
                pre_dft = _chan_dft(x, sh1, sc1, _chan_dft_table(gd), groups=fnet_groups, tm=1024)
            wv = _tok_fft(*pre_dft)
            norm = 1.0 / math.sqrt(float(n_lat * gd))
            x1, t = _outproj_odd(wv, x, od_w_out[o].astype(BF16), od_b_out[o], g1, ln_g[l, 0], ln_b[l, 0],
                                 sh2, sc2, alpha=alpha, norm=norm, tm=1024)
        dft = None
        if l + 1 < depth and (l + 1) % 2 == 1:
            dft = (lat_term(l + 1, 0), lat_term(l + 1, 1), _chan_dft_table(d // fnet_groups), fnet_groups)
        x, pre_dft = _hier_moe_layer(x1, t, g2, ln_g[l, 1], ln_b[l, 1], moe_wg[l], moe_bg[l], moe_wf[l],
                                     moe_bf[l], moe_w1, moe_w3, moe_w2, dft, layer=l, alpha=alpha)
    return x
```

```python
import functools
import math

import numpy as np
import jax
import jax.numpy as jnp
from jax import lax
from jax.experimental import pallas as pl
from jax.experimental.pallas import tpu as pltpu

F32 = jnp.float32
BF16 = jnp.bfloat16
I32 = jnp.int32

LN_EPS = 1e-6
LRU_C = 8.0
ROPE_BASE = 10000.0
GRID_W = 64
CONV_W = 4
LANES = 128
SUBLANES = 8
MOE_ROWS = 512
MOE_TILE = 256
MOE_CHUNK = 8
VMEM_LIMIT = 56 * 1024 * 1024


def _cp(sem, vmem=None):
    return pltpu.CompilerParams(dimension_semantics=sem, vmem_limit_bytes=vmem)


def _ln(x):
    mu = jnp.mean(x, axis=-1, keepdims=True)
    xc = x - mu
    var = jnp.mean(xc * xc, axis=-1, keepdims=True)
    return xc * lax.rsqrt(var + LN_EPS)


def _silu(x):
    return x * jax.nn.sigmoid(x)


PACK_SUB = 4


def _tile_row(r, mult):
    if isinstance(r, int):
        return r * PACK_SUB
    return pl.multiple_of(r * PACK_SUB, mult)


def _pack_rows(val):
    half = val.shape[1] // 2
    lo = lax.bitcast_convert_type(val[:, 0:half], jnp.uint32) >> 16
    hi = lax.bitcast_convert_type(val[:, half:], jnp.uint32) & jnp.uint32(0xFFFF0000)
    return lo | hi


def _unpack_rows(words):
    lo = lax.bitcast_convert_type(words << 16, F32)
    hi = lax.bitcast_convert_type(words & jnp.uint32(0xFFFF0000), F32)
    return jnp.concatenate([lo, hi], axis=1).astype(BF16)


def _wait_chunks(copy_of, n, nmax):
    b = 1
    while b <= nmax:
        @pl.when((n & b) != 0)
        def _(b=b):
            copy_of(b).wait()
        b *= 2


ISSUE_UNROLL = 4


def _for_each_chunk(n, start_one):
    groups = n // ISSUE_UNROLL

    def group(g, carry):
        for u in range(ISSUE_UNROLL):
            start_one(g * ISSUE_UNROLL + u)
        return carry
    lax.fori_loop(0, groups, group, 0)

    def single(q, carry):
        start_one(q)
        return carry
    lax.fori_loop(groups * ISSUE_UNROLL, n, single, 0)


def _rows_from_tiles(ref, nrows):
    return jnp.concatenate([ref[pl.ds(k, nrows, stride=PACK_SUB), :] for k in range(PACK_SUB)], axis=1)


def _rows_to_tiles(ref, val):
    nrows = val.shape[0]
    for k in range(PACK_SUB):
        ref[pl.ds(k, nrows, stride=PACK_SUB), :] = val[:, k * LANES:(k + 1) * LANES]


def _ada_kernel(c_ref, w_ref, b_ref, o_ref):
    s = _silu(c_ref[...]).astype(BF16)
    o_ref[...] = jnp.dot(s, w_ref[...].astype(BF16), preferred_element_type=F32) + b_ref[...]


def _ada_terms(cond, ada_w, ada_b):
    nl, d, d6 = ada_w.shape
    r = cond.shape[0]
    tn = 1024
    return pl.pallas_call(
        _ada_kernel,
        out_shape=jax.ShapeDtypeStruct((nl, r, d6), F32),
        grid=(nl, d6 // tn),
        in_specs=[pl.BlockSpec((r, d), lambda l, j: (0, 0)),
                  pl.BlockSpec((None, d, tn), lambda l, j: (l, 0, j)),
                  pl.BlockSpec((None, 1, tn), lambda l, j: (l, 0, j))],
        out_specs=pl.BlockSpec((None, r, tn), lambda l, j: (l, 0, j)),
        compiler_params=_cp(("parallel", "parallel")),
        name="ada_terms",
    )(cond, ada_w, ada_b.reshape(nl, 1, d6))


def _rope_apply(x, cos, sin_signed):
    tm = x.shape[0]
    lane = lax.broadcasted_iota(I32, (tm, LANES), 1)
    first_half = (lane % 32) < 16
    outs = []
    for j in range(x.shape[1] // LANES):
        xh = x[:, j * LANES:(j + 1) * LANES]
        partner = jnp.where(first_half, pltpu.roll(xh, LANES - 16, 1), pltpu.roll(xh, 16, 1))
        outs.append(xh * cos + partner * sin_signed)
    return jnp.concatenate(outs, axis=1)


def _proj_kernel(x_ref, sh_ref, sc_ref, w_ref, cos_ref, sin_ref,
                 g_ref, xr_ref, q_ref, k_ref, v_ref, *, rope, lw, aw, qscale):
    h = _ln(x_ref[...]) * (1.0 + sc_ref[...]) + sh_ref[...]
    hb = h.astype(BF16)

    def mm(c0, c1):
        return jnp.dot(hb, w_ref[:, c0:c1], preferred_element_type=F32)

    g_ref[...] = mm(0, lw).astype(BF16)
    xr_ref[...] = mm(lw, 2 * lw)
    q = mm(2 * lw, 2 * lw + aw)
    k = mm(2 * lw + aw, 2 * lw + 2 * aw)
    if rope:
        q = _rope_apply(q, cos_ref[...], sin_ref[...])
        k = _rope_apply(k, cos_ref[...], sin_ref[...])
    q_ref[...] = (q * qscale).astype(BF16)
    k_ref[...] = k.astype(BF16)
    v_ref[...] = mm(2 * lw + 2 * aw, 2 * lw + 3 * aw).astype(BF16)


def _project_even(x, shift, scale, w_in_b, cos_t, sin_t, *, rope, lw, aw, qscale, tm):
    b, n, d = x.shape
    tm = min(tm, n)
    nin = w_in_b.shape[1]
    tok = lambda bi, i: (bi, i, 0)
    per_b = lambda bi, i: (bi, 0, 0)
    outs = (jax.ShapeDtypeStruct((b, n, lw), BF16), jax.ShapeDtypeStruct((b, n, lw), F32),
            jax.ShapeDtypeStruct((b, n, aw), BF16), jax.ShapeDtypeStruct((b, n, aw), BF16),
            jax.ShapeDtypeStruct((b, n, aw), BF16))
    return pl.pallas_call(
        functools.partial(_proj_kernel, rope=rope, lw=lw, aw=aw, qscale=qscale),
        out_shape=outs,
        grid=(b, n // tm),
        in_specs=[pl.BlockSpec((None, tm, d), tok),
                  pl.BlockSpec((None, 1, d), per_b),
                  pl.BlockSpec((None, 1, d), per_b),
                  pl.BlockSpec((d, nin), lambda bi, i: (0, 0)),
                  pl.BlockSpec((tm, LANES), lambda bi, i: (i, 0)),
                  pl.BlockSpec((tm, LANES), lambda bi, i: (i, 0))],
        out_specs=(pl.BlockSpec((None, tm, lw), tok), pl.BlockSpec((None, tm, lw), tok),
                   pl.BlockSpec((None, tm, aw), tok), pl.BlockSpec((None, tm, aw), tok),
                   pl.BlockSpec((None, tm, aw), tok)),
        compiler_params=_cp(("parallel", "parallel"), VMEM_LIMIT),
        name="proj_even_rope" if rope else "proj_even_ctx",
    )(x, shift, scale, w_in_b, cos_t, sin_t)


def _rope_tables(n_tok, head_dim):
    t = jnp.arange(n_tok)
    row = (t // GRID_W).astype(F32)
    col = (t % GRID_W).astype(F32)
    nf = head_dim // 4
    freqs = ROPE_BASE ** (-jnp.arange(nf, dtype=F32) / nf)
    lane = np.arange(LANES)
    within = lane % head_dim
    axis = within // (2 * nf)
    half = (within % (2 * nf)) // nf
    f = within % nf
    pos = jnp.where(jnp.asarray(axis)[None, :] == 0, row[:, None], col[:, None])
    ang = pos * freqs[jnp.asarray(f)][None, :]
    sign = jnp.asarray(np.where(half == 0, -1.0, 1.0), F32)[None, :]
    return jnp.cos(ang).astype(F32), (jnp.sin(ang) * sign).astype(F32)


def _attn_kernel(q_ref, kc_ref, kl_ref, vc_ref, vl_ref, dl_ref, gain_ref, o_ref, kbuf, vbuf, sbuf, ebuf, abuf,
                 cbuf, *, nc, nl, hd, lam_init, rows):
    @pl.when((pl.program_id(0) == 0) & (pl.program_id(1) == 0))
    def _():
        sbuf[...] = jnp.zeros_like(sbuf)
        abuf[...] = jnp.zeros_like(abuf)
        cbuf[...] = jnp.zeros_like(cbuf)

    kbuf[0:nc, :] = kc_ref[...]
    kbuf[nc:nc + nl, :] = kl_ref[...]
    vbuf[0:nc, :] = vc_ref[...]
    vbuf[nc:nc + nl, :] = vl_ref[...]

    lf = dl_ref[...]
    lam = (jnp.exp(jnp.sum(lf[0:1] * lf[1:2], axis=1, keepdims=True))
           - jnp.exp(jnp.sum(lf[2:3] * lf[3:4], axis=1, keepdims=True)) + lam_init)
    gain = gain_ref[...] * (1.0 - lam_init)
    n_sub = nl // rows
    lane = lax.broadcasted_iota(I32, (rows, 2 * hd), 1)
    nt = (((1,), (1,)), ((), ()))

    def stage_a(j, slot):
        r0 = pl.multiple_of(jnp.minimum(j, n_sub - 1) * rows, rows)
        q = q_ref[pl.ds(r0, rows), :]
        zero = jnp.zeros_like(q)
        kk = kbuf[...]
        sbuf[slot, 0] = lax.dot_general(jnp.where(lane < hd, q, zero), kk, nt, preferred_element_type=F32)
        sbuf[slot, 1] = lax.dot_general(jnp.where(lane >= hd, q, zero), kk, nt, preferred_element_type=F32)

    def stage_b(slot):
        cols = [slice(c, c + LANES) for c in range(0, nc + nl, LANES)]
        ls = []
        for k in range(2):
            pm = sbuf[slot, k, :, cols[0]]
            for cs in cols[1:]:
                pm = jnp.maximum(pm, sbuf[slot, k, :, cs])
            m = jnp.max(pm, axis=1, keepdims=True)
            acc = jnp.zeros((rows, LANES), F32)
            for cs in cols:
                e = jnp.exp2(sbuf[slot, k, :, cs] - m)
                acc = acc + e
                ebuf[slot, k, :, cs] = e.astype(BF16)
            ls.append(jnp.sum(acc, axis=1, keepdims=True))
        ratio = (lam * ls[0] / ls[1]).astype(BF16)
        abuf[slot] = ebuf[slot, 0] - ratio * ebuf[slot, 1]
        cbuf[slot] = jnp.broadcast_to(1.0 / ls[0], cbuf.shape[1:])

    def stage_c(j, slot):
        r0 = pl.multiple_of(jnp.maximum(j - 2, 0) * rows, rows)
        o = jnp.dot(abuf[slot], vbuf[...], preferred_element_type=F32) * cbuf[slot]
        o = o * lax.rsqrt(jnp.mean(o * o, axis=1, keepdims=True) + LN_EPS) * gain
        o_ref[pl.ds(r0, rows), :] = o.astype(BF16)

    def body(t, carry):
        j = 2 * t
        stage_a(j, 0)
        stage_b(1)
        stage_c(j, 0)
        stage_a(j + 1, 1)
        stage_b(0)
        stage_c(j + 1, 1)
        return carry

    lax.fori_loop(0, (n_sub + 2) // 2, body, 0)


def _diff_attention(q, k_ctx, k_lat, v_ctx, v_lat, da_lambda, subln, *, heads, hd, lam_init):
    b, n, aw = q.shape
    nc = k_ctx.shape[1]
    vd = aw // heads
    rows = min(128, n)
    assert (n // rows) % 2 == 0
    blk_q = pl.BlockSpec((None, n, vd), lambda bi, h: (bi, 0, h))
    blk_c = pl.BlockSpec((None, nc, vd), lambda bi, h: (bi, 0, h))
    return pl.pallas_call(
        functools.partial(_attn_kernel, nc=nc, nl=n, hd=hd, lam_init=lam_init, rows=rows),
        out_shape=jax.ShapeDtypeStruct((b, n, aw), BF16),
        grid=(b, heads),
        in_specs=[blk_q, blk_c, blk_q, blk_c, blk_q,
                  pl.BlockSpec(da_lambda.shape, lambda bi, h: (0, 0)),
                  pl.BlockSpec((1, vd), lambda bi, h: (0, 0))],
        out_specs=blk_q,
        scratch_shapes=[pltpu.VMEM((nc + n, vd), BF16), pltpu.VMEM((nc + n, vd), BF16),
                        pltpu.VMEM((2, 2, rows, nc + n), F32), pltpu.VMEM((2, 2, rows, nc + n), BF16),
                        pltpu.VMEM((2, rows, nc + n), BF16), pltpu.VMEM((2, rows, vd), F32)],
        compiler_params=_cp(("arbitrary", "arbitrary"), VMEM_LIMIT),
        name="diff_attention",
    )(q, k_ctx, k_lat, v_ctx, v_lat, da_lambda, subln.reshape(1, vd))


def _lru_kernel(*refs, reverse, add_prev, nt, groups):
    if add_prev:
        (xp_ref, xc_ref, xn_ref, yprev_ref, cw_ref, cb_ref, wa_ref, ba_ref, wx_ref, bx_ref,
         lam_ref, h0_ref, y_ref, hf_ref, carry, ext) = refs
    else:
        (xp_ref, xc_ref, xn_ref, cw_ref, cb_ref, wa_ref, ba_ref, wx_ref, bx_ref,
         lam_ref, h0_ref, y_ref, hf_ref, carry, ext) = refs
        yprev_ref = None
    i = pl.program_id(1)
    ti = (nt - 1 - i) if reverse else i
    tn, w = xc_ref.shape

    @pl.when(i == 0)
    def _():
        carry[...] = h0_ref[...]

    ext[0:SUBLANES, :] = jnp.where(ti > 0, xp_ref[...], 0.0)
    ext[SUBLANES:SUBLANES + tn, :] = xc_ref[...]
    ext[SUBLANES + tn:2 * SUBLANES + tn, :] = jnp.where(ti < nt - 1, xn_ref[...], 0.0)
    left = CONV_W // 2
    xc = cb_ref[...]
    for k in range(CONV_W):
        off = SUBLANES - left + k
        xc = xc + ext[off:off + tn, :] * cw_ref[k:k + 1, :]

    xb = xc.astype(BF16)
    gw = w // groups

    def gate(w_ref, b_ref):
        parts = [jnp.dot(xb[:, g * gw:(g + 1) * gw], w_ref[g], preferred_element_type=F32)
                 for g in range(groups)]
        return jax.nn.sigmoid(jnp.concatenate(parts, axis=1) + b_ref[...])

    r = gate(wa_ref, ba_ref)
    ig = gate(wx_ref, bx_ref)
    log_a = (-LRU_C * jax.nn.softplus(-lam_ref[...])) * r
    a = jnp.exp(log_a)
    bcoef = jnp.sqrt(-jnp.tanh(log_a) * (a * a + 1.0)) * ig * xc

    ngroups = tn // SUBLANES
    a = a.reshape(ngroups, SUBLANES, w)
    bcoef = bcoef.reshape(ngroups, SUBLANES, w)
    row = lax.broadcasted_iota(I32, (ngroups, SUBLANES, w), 1)
    d = 1
    while d < SUBLANES:
        shift = (SUBLANES - d) if reverse else d
        a_sh = pltpu.roll(a, shift, 1)
        b_sh = pltpu.roll(bcoef, shift, 1)
        live = (row < SUBLANES - d) if reverse else (row >= d)
        bcoef = jnp.where(live, a * b_sh + bcoef, bcoef)
        a = jnp.where(live, a * a_sh, a)
        d *= 2
    hc = carry[...]
    npairs = ngroups // 2
    for p in (range(npairs - 1, -1, -1) if reverse else range(npairs)):
        hs = {}
        for g in ((2 * p + 1, 2 * p) if reverse else (2 * p, 2 * p + 1)):
            h = a[g] * hc + bcoef[g]
            hc = h[0:1, :] if reverse else h[SUBLANES - 1:SUBLANES, :]
            hs[g] = h
        rs = slice(2 * p * SUBLANES, (2 * p + 2) * SUBLANES)
        h2 = jnp.concatenate([hs[2 * p], hs[2 * p + 1]], axis=0)
        if add_prev:
            h2 = yprev_ref[rs, :].astype(F32) + h2
        y_ref[rs, :] = h2.astype(y_ref.dtype)
    carry[...] = hc
    hf_ref[...] = hc


def _rglru_dir(xr, y_prev, conv_w, conv_b, wa_bd, ba, wx_bd, bx, lam, h0, *, reverse, tn):
    b, n, w = xr.shape
    tn = min(tn, n)
    nt = n // tn
    groups = wa_bd.shape[0]
    nb8 = n // SUBLANES
    per8 = tn // SUBLANES

    def tmap(i):
        return (nt - 1 - i) if reverse else i

    cur = pl.BlockSpec((None, tn, w), lambda bi, i: (bi, tmap(i), 0))
    halo_p = pl.BlockSpec((None, SUBLANES, w), lambda bi, i: (bi, jnp.maximum(tmap(i) * per8 - 1, 0), 0))
    halo_n = pl.BlockSpec((None, SUBLANES, w), lambda bi, i: (bi, jnp.minimum((tmap(i) + 1) * per8, nb8 - 1), 0))
    row_w = pl.BlockSpec((1, w), lambda bi, i: (0, 0))
    per_b = pl.BlockSpec((None, 1, w), lambda bi, i: (bi, 0, 0))
    gate_w = pl.BlockSpec(wa_bd.shape, lambda bi, i: (0, 0, 0))
    add_prev = y_prev is not None
    in_specs = [halo_p, cur, halo_n] + ([cur] if add_prev else []) + [
        pl.BlockSpec((CONV_W, w), lambda bi, i: (0, 0)), row_w, gate_w, row_w, gate_w, row_w, row_w, per_b]
    args = [xr, xr, xr] + ([y_prev] if add_prev else []) + [
        conv_w, conv_b.reshape(1, w), wa_bd, ba.reshape(1, w), wx_bd, bx.reshape(1, w),
        lam.reshape(1, w), h0]
    return pl.pallas_call(
        functools.partial(_lru_kernel, reverse=reverse, add_prev=add_prev, nt=nt, groups=groups),
        out_shape=(jax.ShapeDtypeStruct((b, n, w), BF16), jax.ShapeDtypeStruct((b, 1, w), F32)),
        grid=(b, nt),
        in_specs=in_specs,
        out_specs=(cur, per_b),
        scratch_shapes=[pltpu.VMEM((1, w), F32), pltpu.VMEM((tn + 2 * SUBLANES, w), F32)],
        compiler_params=_cp(("parallel", "arbitrary"), VMEM_LIMIT),
        name="rglru_rev" if reverse else "rglru_fwd",
    )(*args)


def _block_diag_groups(wh, group_width):
    heads, blk, _ = wh.shape
    per = group_width // blk
    groups = heads // per
    whg = wh.reshape(groups, per, blk, blk)
    eye = jnp.eye(per, dtype=wh.dtype)
    bd = jnp.einsum('gpij,pq->gpiqj', whg, eye).reshape(groups, group_width, group_width)
    return bd.astype(BF16)


def _residual_ln_mod(x, y, g1, lng, lnb, sh2, sc2, alpha):
    x1 = _ln(alpha * x + g1 * y) * lng + lnb
    t = _ln(x1) * (1.0 + sc2) + sh2
    return x1, t


def _outproj_even_kernel(r_ref, g_ref, o_ref, x_ref, w_ref, g1_ref, lng_ref, lnb_ref, sh2_ref, sc2_ref,
                         x1_ref, t_ref, *, lw, alpha):
    z = (r_ref[...].astype(F32) * jax.nn.gelu(g_ref[...].astype(F32))).astype(BF16)
    y = (jnp.dot(z, w_ref[0:lw, :], preferred_element_type=F32)
         + jnp.dot(o_ref[...], w_ref[lw:, :], preferred_element_type=F32))
    x1, t = _residual_ln_mod(x_ref[...], y, g1_ref[...], lng_ref[...], lnb_ref[...],
                             sh2_ref[...], sc2_ref[...], alpha)
    x1_ref[...] = x1
    t_ref[...] = t.astype(BF16)


def _outproj_even(r, g, o, x, w_out_b, g1, lng, lnb, sh2, sc2, *, alpha, tm):
    b, n, d = x.shape
    lw = r.shape[2]
    aw = o.shape[2]
    tm = min(tm, n)
    tok = lambda bi, i: (bi, i, 0)
    per_b = pl.BlockSpec((None, 1, d), lambda bi, i: (bi, 0, 0))
    row = pl.BlockSpec((1, d), lambda bi, i: (0, 0))
    return pl.pallas_call(
        functools.partial(_outproj_even_kernel, lw=lw, alpha=alpha),
        out_shape=(jax.ShapeDtypeStruct((b, n, d), F32), jax.ShapeDtypeStruct((b, n, d), BF16)),
        grid=(b, n // tm),
        in_specs=[pl.BlockSpec((None, tm, lw), tok), pl.BlockSpec((None, tm, lw), tok),
                  pl.BlockSpec((None, tm, aw), tok), pl.BlockSpec((None, tm, d), tok),
                  pl.BlockSpec(w_out_b.shape, lambda bi, i: (0, 0)),
                  per_b, row, row, per_b, per_b],
        out_specs=(pl.BlockSpec((None, tm, d), tok), pl.BlockSpec((None, tm, d), tok)),
        compiler_params=_cp(("parallel", "parallel"), VMEM_LIMIT),
        name="outproj_even",
    )(r, g, o, x, w_out_b, g1, lng.reshape(1, d), lnb.reshape(1, d), sh2, sc2)


def _chan_dft_kernel(x_ref, sh_ref, sc_ref, tab_ref, zr_ref, zi_ref, *, groups, gd):
    h = (_ln(x_ref[...]) * (1.0 + sc_ref[...]) + sh_ref[...]).astype(BF16)
    for g in range(groups):
        z = jnp.dot(h[:, g * gd:(g + 1) * gd], tab_ref[...], preferred_element_type=F32)
        zr_ref[:, g * gd:(g + 1) * gd] = z[:, 0:gd].astype(BF16)
        zi_ref[:, g * gd:(g + 1) * gd] = z[:, gd:2 * gd].astype(BF16)


def _chan_dft(x, shift, scale, tab, *, groups, tm):
    b, n, d = x.shape
    gd = d // groups
    tm = min(tm, n)
    tok = lambda bi, i: (bi, i, 0)
    per_b = pl.BlockSpec((None, 1, d), lambda bi, i: (bi, 0, 0))
    return pl.pallas_call(
        functools.partial(_chan_dft_kernel, groups=groups, gd=gd),
        out_shape=(jax.ShapeDtypeStruct((b, n, d), BF16), jax.ShapeDtypeStruct((b, n, d), BF16)),
        grid=(b, n // tm),
        in_specs=[pl.BlockSpec((None, tm, d), tok), per_b, per_b,
                  pl.BlockSpec(tab.shape, lambda bi, i: (0, 0))],
        out_specs=(pl.BlockSpec((None, tm, d), tok), pl.BlockSpec((None, tm, d), tok)),
        compiler_params=_cp(("parallel", "parallel"), VMEM_LIMIT),
        name="chan_dft",
    )(x, shift, scale, tab)


FFT_C = 64


def _fft_pitch(group):
    p = -(-group // SUBLANES)
    return SUBLANES * (p if p % 2 else p + 1)


def _tok_fft_kernel(zr_ref, zi_ref, m1_ref, m3_ref, tc_ref, ts_ref, o_ref, zsr, zsi, asr, asi, ob, *, nr):
    c_len = FFT_C
    pz = zsr.shape[0] // nr
    pa = asr.shape[0] // c_len
    for r in range(nr):
        zsr[pz * r:pz * r + c_len, :] = zr_ref[c_len * r:c_len * (r + 1), :].astype(F32)
        zsi[pz * r:pz * r + c_len, :] = zi_ref[c_len * r:c_len * (r + 1), :].astype(F32)
    m1 = m1_ref[...]
    for c in range(c_len):
        x2 = jnp.concatenate([zsr[pl.ds(c, nr, stride=pz), :], zsi[pl.ds(c, nr, stride=pz), :]], axis=0)
        a2 = jnp.dot(m1, x2.astype(BF16), preferred_element_type=F32)
        ar, ai = a2[0:nr], a2[nr:2 * nr]
        tcv = tc_ref[c * nr:(c + 1) * nr, :]
        tsv = ts_ref[c * nr:(c + 1) * nr, :]
        asr[pa * c:pa * c + nr, :] = ar * tcv + ai * tsv
        asi[pa * c:pa * c + nr, :] = ai * tcv - ar * tsv
    m3 = m3_ref[...]
    for k1 in range(nr):
        y2 = jnp.concatenate([asr[pl.ds(k1, c_len, stride=pa), :], asi[pl.ds(k1, c_len, stride=pa), :]], axis=0)
        ob[pl.ds(k1, c_len, stride=pa), :] = jnp.dot(m3, y2.astype(BF16), preferred_element_type=F32)
    for k2 in range(c_len):
        o_ref[nr * k2:nr * (k2 + 1), :] = ob[pa * k2:pa * k2 + nr, :].astype(BF16)


def _tok_fft(zr, zi):
    b, n, d = zr.shape
    nr = n // FFT_C
    pz = _fft_pitch(FFT_C)
    pa = _fft_pitch(nr)
    kr = np.arange(nr, dtype=np.float64)
    ang_r = 2.0 * np.pi * (np.outer(kr, kr) % nr) / nr
    cr, sr = np.cos(ang_r), np.sin(ang_r)
    m1 = jnp.asarray(np.block([[cr, -sr], [-sr, -cr]]), F32).astype(BF16)
    kc = np.arange(FFT_C, dtype=np.float64)
    ang_c = 2.0 * np.pi * (np.outer(kc, kc) % FFT_C) / FFT_C
    m3 = jnp.asarray(np.concatenate([np.cos(ang_c), np.sin(ang_c)], axis=1), F32).astype(BF16)
    ang_t = 2.0 * np.pi * (np.outer(kc, kr) % n) / n
    tc = jnp.broadcast_to(jnp.asarray(np.cos(ang_t).reshape(FFT_C * nr, 1), F32), (FFT_C * nr, LANES))
    ts = jnp.broadcast_to(jnp.asarray(np.sin(ang_t).reshape(FFT_C * nr, 1), F32), (FFT_C * nr, LANES))
    slab = pl.BlockSpec((None, n, LANES), lambda bi, l: (bi, 0, l))
    const = lambda a: pl.BlockSpec(a.shape, lambda bi, l: (0, 0))
    return pl.pallas_call(
        functools.partial(_tok_fft_kernel, nr=nr),
        out_shape=jax.ShapeDtypeStruct((b, n, d), BF16),
        grid=(b, d // LANES),
        in_specs=[slab, slab, const(m1), const(m3), const(tc), const(ts)],
        out_specs=slab,
        scratch_shapes=[pltpu.VMEM((nr * pz, LANES), F32), pltpu.VMEM((nr * pz, LANES), F32),
                        pltpu.VMEM((FFT_C * pa, LANES), F32), pltpu.VMEM((FFT_C * pa, LANES), F32),
                        pltpu.VMEM((FFT_C * pa, LANES), F32)],
        compiler_params=_cp(("parallel", "parallel"), VMEM_LIMIT),
        name="tok_fft",
    )(zr, zi, m1, m3, tc, ts)


def _chan_dft_table(gd):
    c = np.arange(gd, dtype=np.float64)
    ang_c = 2.0 * np.pi * (np.outer(c, c) % gd) / gd
    return jnp.asarray(np.concatenate([np.cos(ang_c), np.sin(ang_c)], axis=1), F32).astype(BF16)


def _outproj_odd_kernel(wv_ref, x_ref, w_ref, b_ref, g1_ref, lng_ref, lnb_ref, sh2_ref, sc2_ref,
                        x1_ref, t_ref, *, alpha, norm):
    y = jnp.dot(wv_ref[...], w_ref[...], preferred_element_type=F32) * norm + b_ref[...]
    x1, t = _residual_ln_mod(x_ref[...], y, g1_ref[...], lng_ref[...], lnb_ref[...],
                             sh2_ref[...], sc2_ref[...], alpha)
    x1_ref[...] = x1
    t_ref[...] = t.astype(BF16)


def _outproj_odd(wv, x, w_b, bias, g1, lng, lnb, sh2, sc2, *, alpha, norm, tm):
    b, n, d = x.shape
    tm = min(tm, n)
    tok = lambda bi, i: (bi, i, 0)
    per_b = pl.BlockSpec((None, 1, d), lambda bi, i: (bi, 0, 0))
    row = pl.BlockSpec((1, d), lambda bi, i: (0, 0))
    return pl.pallas_call(
        functools.partial(_outproj_odd_kernel, alpha=alpha, norm=norm),
        out_shape=(jax.ShapeDtypeStruct((b, n, d), F32), jax.ShapeDtypeStruct((b, n, d), BF16)),
        grid=(b, n // tm),
        in_specs=[pl.BlockSpec((None, tm, d), tok), pl.BlockSpec((None, tm, d), tok),
                  pl.BlockSpec(w_b.shape, lambda bi, i: (0, 0)), row,
                  per_b, row, row, per_b, per_b],
        out_specs=(pl.BlockSpec((None, tm, d), tok), pl.BlockSpec((None, tm, d), tok)),
        compiler_params=_cp(("parallel", "parallel"), VMEM_LIMIT),
        name="outproj_odd",
    )(wv, x, w_b, bias.reshape(1, d), g1, lng.reshape(1, d), lnb.reshape(1, d), sh2, sc2)


def _route_kernel(t_ref, w_ref, b_ref, up_ref, lp_ref, tokm_ref, cnt_ref, *, ng, epg, chunk):
    tm = t_ref.shape[0]
    ne = ng * epg
    logits = jnp.dot(t_ref[...].astype(BF16), w_ref[...], preferred_element_type=F32) + b_ref[...]
    lt = logits.T
    best = lt[0:1, :]
    bi = jnp.zeros((1, tm), I32)
    for k in range(1, ng):
        gk = lt[k:k + 1, :]
        upd = gk > best
        bi = jnp.where(upd, k, bi)
        best = jnp.where(upd, gk, best)
    den = jnp.zeros((1, tm), F32)
    for k in range(ng):
        den = den + jnp.exp(lt[k:k + 1, :] - best)
    p_g = 1.0 / den
    fsel = lt[SUBLANES:SUBLANES + epg, :]
    for k in range(1, ng):
        fsel = jnp.where(bi == k, lt[SUBLANES + k * epg:SUBLANES + (k + 1) * epg, :], fsel)
    neg = jnp.full((1, tm), -jnp.inf, F32)
    m1, m2 = neg, neg
    i1 = jnp.zeros((1, tm), I32)
    i2 = jnp.zeros((1, tm), I32)
    for j in range(epg):
        v = fsel[j:j + 1, :]
        gt1 = v > m1
        gt2 = v > m2
        m2 = jnp.where(gt1, m1, jnp.where(gt2, v, m2))
        i2 = jnp.where(gt1, i1, jnp.where(gt2, j, i2))
        m1 = jnp.where(gt1, v, m1)
        i1 = jnp.where(gt1, j, i1)
    e21 = jnp.exp(m2 - m1)
    w1 = p_g / (1.0 + e21)
    w2 = p_g * e21 / (1.0 + e21)
    e1 = bi * epg + i1
    e2 = bi * epg + i2

    e = jnp.concatenate([e1, e2], axis=1)
    rows = lax.broadcasted_iota(I32, (ne, 2 * tm), 0)
    onehot = jnp.where(rows == e, 1.0, 0.0)
    before = jnp.dot(onehot.astype(BF16), up_ref[...], preferred_element_type=F32)
    tot = jnp.sum(onehot, axis=1, keepdims=True)
    slots = jnp.floor((tot + (chunk - 1.0)) * (1.0 / chunk)) * chunk
    slots_b = jnp.broadcast_to(slots, (ne, LANES))
    rowe = lax.broadcasted_iota(I32, (ne, LANES), 0)
    incl = slots_b
    d = 1
    while d < ne:
        incl = incl + jnp.where(rowe >= d, pltpu.roll(incl, d, 0), 0.0)
        d *= 2
    seg_off = jnp.tile(incl - slots_b, (1, 2 * tm // LANES))
    lpos = jnp.sum(onehot * (before + seg_off), axis=0, keepdims=True)
    lp0 = lpos[:, 0:tm]
    lp1 = lpos[:, tm:2 * tm]
    row8 = lax.broadcasted_iota(I32, (SUBLANES, tm), 0)
    lp_ref[...] = jnp.where(row8 == 0, lp0, jnp.where(row8 == 1, lp1, 0.0)).astype(I32)
    rowl = lax.broadcasted_iota(I32, (LANES, tm), 0)
    tokm = jnp.where(rowl == 0, w1, jnp.where(rowl == 1, w2, jnp.where(rowl == 2, lp0, jnp.where(rowl == 3, lp1, 0.0))))
    tokm_ref[...] = tokm.T
    cnt_ref[...] = jnp.broadcast_to(tot, (ne, LANES))


def _route(t, wcat_b, bcat, upper, *, ng, epg, tm, chunk):
    tt, d = t.shape
    ne = ng * epg
    nt = tt // tm
    return pl.pallas_call(
        functools.partial(_route_kernel, ng=ng, epg=epg, chunk=chunk),
        out_shape=(jax.ShapeDtypeStruct((SUBLANES, tt), I32), jax.ShapeDtypeStruct((tt, LANES), F32),
                   jax.ShapeDtypeStruct((ne, nt * LANES), F32)),
        grid=(nt,),
        in_specs=[pl.BlockSpec((tm, d), lambda i: (i, 0)),
                  pl.BlockSpec((d, LANES), lambda i: (0, 0)),
                  pl.BlockSpec((1, LANES), lambda i: (0, 0)),
                  pl.BlockSpec(upper.shape, lambda i: (0, 0))],
        out_specs=(pl.BlockSpec((SUBLANES, tm), lambda i: (0, i)),
                   pl.BlockSpec((tm, LANES), lambda i: (i, 0)),
                   pl.BlockSpec((ne, LANES), lambda i: (0, i))),
        compiler_params=_cp(("parallel",), VMEM_LIMIT),
        name="route_sort",
    )(t, wcat_b, bcat, upper)


def _dispatch_kernel(row_ref, tot_ref, zrow_ref, nz_ref, t_ref, lp_ref, xs_ref, stage, zbuf, sems,
                     *, qmax, zmax, chunk, nsteps):
    i = pl.program_id(0)
    slot = i % 2
    sp = stage.shape[1] // PACK_SUB
    tm = t_ref.shape[0]
    crow = chunk * PACK_SUB

    @pl.when(i == 0)
    def _():
        zbuf[...] = jnp.zeros_like(zbuf)
        _for_each_chunk(nz_ref[0], lambda q: pltpu.make_async_copy(
            zbuf, xs_ref.at[pl.ds(_tile_row(zrow_ref[q], SUBLANES), crow)], sems.at[1]).start())
        _wait_chunks(lambda k: pltpu.make_async_copy(xs_ref.at[pl.ds(0, k * crow)], xs_ref.at[pl.ds(0, k * crow)],
                                                     sems.at[1]), nz_ref[0], zmax)

    lp = lp_ref[...]
    prow = lax.broadcasted_iota(I32, (sp, tm), 0)
    perm = jnp.where(prow == lp[0:1, :], 1.0, jnp.where(prow == lp[1:2, :], 1.0, 0.0)).astype(BF16)
    srt = jnp.dot(perm, t_ref[...], preferred_element_type=F32)
    _rows_to_tiles(stage.at[slot], _pack_rows(srt))

    def chunk_copy(sl, src, dst):
        return pltpu.make_async_copy(stage.at[sl, pl.ds(_tile_row(src, crow), crow)],
                                     xs_ref.at[pl.ds(_tile_row(dst, SUBLANES), crow)], sems.at[sl])

    def drain(sl, n):
        def copy_of(k):
            return pltpu.make_async_copy(stage.at[sl, pl.ds(0, k * crow)], xs_ref.at[pl.ds(0, k * crow)], sems.at[sl])
        _wait_chunks(copy_of, n, qmax)

    @pl.when(i >= 1)
    def _():
        drain(1 - slot, tot_ref[jnp.maximum(i - 1, 0)])

    _for_each_chunk(tot_ref[i], lambda q: chunk_copy(slot, pl.multiple_of(q * chunk, chunk),
                                                     row_ref[i * qmax + q]).start())

    @pl.when(i == nsteps - 1)
    def _():
        drain(slot, tot_ref[i])


def _dispatch(chunk_row, tile_nch, zero_row, n_zero, t, lp, *, n_rows, ne, tm, chunk):
    ttot, d = t.shape
    nsteps = ttot // tm
    sp = 2 * tm + ne * chunk
    grid_spec = pltpu.PrefetchScalarGridSpec(
        num_scalar_prefetch=4,
        grid=(nsteps,),
        in_specs=[pl.BlockSpec((tm, d), lambda i, *_: (i, 0)),
                  pl.BlockSpec((SUBLANES, tm), lambda i, *_: (0, i))],
        out_specs=pl.BlockSpec(memory_space=pl.ANY),
        scratch_shapes=[pltpu.VMEM((2, sp * PACK_SUB, LANES), jnp.uint32),
                        pltpu.VMEM((chunk * PACK_SUB, LANES), jnp.uint32), pltpu.SemaphoreType.DMA((2,))],
    )
    assert d == 2 * PACK_SUB * LANES
    return pl.pallas_call(
        functools.partial(_dispatch_kernel, qmax=chunk_row.shape[0] // nsteps, zmax=zero_row.shape[0],
                          chunk=chunk, nsteps=nsteps),
        out_shape=jax.ShapeDtypeStruct((n_rows * PACK_SUB, LANES), jnp.uint32),
        grid_spec=grid_spec,
        compiler_params=_cp(("arbitrary",), VMEM_LIMIT),
        name="moe_dispatch",
    )(chunk_row, tile_nch, zero_row, n_zero, t, lp)


def _moe_kernel(be_ref, bv_ref, nu_ref, xs_ref, w1_ref, w3_ref, w2_ref, y_ref, w13b, w2b, *, ff):
    i = pl.program_id(0)
    half = MOE_ROWS // 2

    def ffn(nrows):
        x = _unpack_rows(_rows_from_tiles(xs_ref, nrows))
        h = jnp.dot(x, w13b[...], preferred_element_type=F32)
        hid = (_silu(h[:, 0:ff]) * h[:, ff:2 * ff]).astype(BF16)
        y = jnp.dot(hid, w2b[...], preferred_element_type=F32)
        _rows_to_tiles(y_ref, _pack_rows(y.astype(BF16).astype(F32)))

    @pl.when(i < nu_ref[0])
    def _():
        prev = be_ref[jnp.maximum(i - 1, 0)]

        @pl.when((i == 0) | (be_ref[i] != prev))
        def _():
            w13b[:, 0:ff] = w1_ref[...].astype(BF16)
            w13b[:, ff:2 * ff] = w3_ref[...].astype(BF16)
            w2b[...] = w2_ref[...].astype(BF16)

        @pl.when(bv_ref[i] > half)
        def _():
            ffn(MOE_ROWS)

        @pl.when(bv_ref[i] <= half)
        def _():
            ffn(half)
            y_ref[half * PACK_SUB:, :] = jnp.zeros((half * PACK_SUB, LANES), y_ref.dtype)

    @pl.when(i >= nu_ref[0])
    def _():
        y_ref[...] = jnp.zeros_like(y_ref)


def _moe_experts(block_e, block_rows, n_used, xs, w1, w3, w2, *, layer):
    d = w1.shape[-2]
    ff = w1.shape[-1]
    blk = MOE_ROWS * PACK_SUB
    nb = xs.shape[0] // blk
    wmap = lambda i, be, bv, nu: (layer, be[i], 0, 0)
    grid_spec = pltpu.PrefetchScalarGridSpec(
        num_scalar_prefetch=3,
        grid=(nb,),
        in_specs=[pl.BlockSpec((blk, LANES), lambda i, be, bv, nu: (i, 0)),
                  pl.BlockSpec((None, None, d, ff), wmap),
                  pl.BlockSpec((None, None, d, ff), wmap),
                  pl.BlockSpec((None, None, ff, d), wmap)],
        out_specs=pl.BlockSpec((blk, LANES), lambda i, be, bv, nu: (i, 0)),
        scratch_shapes=[pltpu.VMEM((d, 2 * ff), BF16), pltpu.VMEM((ff, d), BF16)],
    )
    return pl.pallas_call(
        functools.partial(_moe_kernel, ff=ff),
        out_shape=jax.ShapeDtypeStruct(xs.shape, xs.dtype),
        grid_spec=grid_spec,
        compiler_params=_cp(("arbitrary",), VMEM_LIMIT),
        name="moe_experts",
    )(block_e, block_rows, n_used, xs, w1, w3, w2)


def _combine_kernel(row_ref, tot_ref, yb_ref, x_ref, tokm_ref, g2_ref, lng_ref, lnb_ref, *rest,
                    qmax, chunk, nsteps, alpha, dft_groups):
    if dft_groups:
        sh_ref, sc_ref, tab_ref, o_ref, zr_ref, zi_ref, stage, sems = rest
    else:
        o_ref, stage, sems = rest
    i = pl.program_id(0)
    slot = i % 2
    sp = stage.shape[1] // PACK_SUB
    tm = x_ref.shape[0]
    crow = chunk * PACK_SUB

    def chunk_copy(sl, src, dst):
        return pltpu.make_async_copy(yb_ref.at[pl.ds(_tile_row(src, SUBLANES), crow)],
                                     stage.at[sl, pl.ds(_tile_row(dst, crow), crow)], sems.at[sl])

    def issue_tile(step, sl):
        _for_each_chunk(tot_ref[step], lambda q: chunk_copy(sl, row_ref[step * qmax + q],
                                                            pl.multiple_of(q * chunk, chunk)).start())

    @pl.when(i == 0)
    def _():
        stage[...] = jnp.zeros_like(stage)
        issue_tile(0, 0)

    @pl.when(i + 1 < nsteps)
    def _():
        issue_tile(jnp.minimum(i + 1, nsteps - 1), 1 - slot)

    def copy_of(k):
        return pltpu.make_async_copy(yb_ref.at[pl.ds(0, k * crow)], stage.at[slot, pl.ds(0, k * crow)], sems.at[slot])
    _wait_chunks(copy_of, tot_ref[i], qmax)

    tk = tokm_ref[...]
    pos = lax.broadcasted_iota(I32, (tm, sp), 1).astype(F32)
    st = _unpack_rows(_rows_from_tiles(stage.at[slot], sp))
    gsel = jnp.where(pos == tk[:, 2:3], tk[:, 0:1], jnp.where(pos == tk[:, 3:4], tk[:, 1:2], 0.0))
    m = jnp.dot(gsel.astype(BF16), st, preferred_element_type=F32)
    xo = _ln(alpha * x_ref[...] + g2_ref[...] * m) * lng_ref[...] + lnb_ref[...]
    o_ref[...] = xo
    if dft_groups:
        gd = xo.shape[1] // dft_groups
        h = (_ln(xo) * (1.0 + sc_ref[...]) + sh_ref[...]).astype(BF16)
        for g in range(dft_groups):
            z = jnp.dot(h[:, g * gd:(g + 1) * gd], tab_ref[...], preferred_element_type=F32)
            zr_ref[:, g * gd:(g + 1) * gd] = z[:, 0:gd].astype(BF16)
            zi_ref[:, g * gd:(g + 1) * gd] = z[:, gd:2 * gd].astype(BF16)


def _combine(chunk_row, tile_nch, yb, x, tokm, g2, lng, lnb, dft, *, ne, tm, chunk, n_per_batch, alpha):
    ttot, d = x.shape
    nsteps = ttot // tm
    per = n_per_batch // tm
    sp = 2 * tm + ne * chunk
    tok = pl.BlockSpec((tm, d), lambda i, *_: (i, 0))
    per_b = pl.BlockSpec((None, 1, d), lambda i, *_: (i // per, 0, 0))
    row = pl.BlockSpec((1, d), lambda i, *_: (0, 0))
    in_specs = [pl.BlockSpec(memory_space=pl.ANY), tok, pl.BlockSpec((tm, LANES), lambda i, *_: (i, 0)),
                per_b, row, row]
    args = [chunk_row, tile_nch, yb, x, tokm, g2, lng.reshape(1, d), lnb.reshape(1, d)]
    out_specs, out_shape, groups = tok, jax.ShapeDtypeStruct((ttot, d), F32), 0
    if dft is not None:
        shift, scale, tab, groups = dft
        in_specs += [per_b, per_b, pl.BlockSpec(tab.shape, lambda i, *_: (0, 0))]
        args += [shift, scale, tab]
        out_specs = (tok, tok, tok)
        out_shape = (out_shape, jax.ShapeDtypeStruct((ttot, d), BF16), jax.ShapeDtypeStruct((ttot, d), BF16))
    grid_spec = pltpu.PrefetchScalarGridSpec(
        num_scalar_prefetch=2,
        grid=(nsteps,),
        in_specs=in_specs,
        out_specs=out_specs,
        scratch_shapes=[pltpu.VMEM((2, sp * PACK_SUB, LANES), yb.dtype), pltpu.SemaphoreType.DMA((2,))],
    )
    return pl.pallas_call(
        functools.partial(_combine_kernel, qmax=chunk_row.shape[0] // nsteps, chunk=chunk, nsteps=nsteps,
                          alpha=alpha, dft_groups=groups),
        out_shape=out_shape,
        grid_spec=grid_spec,
        compiler_params=_cp(("arbitrary",), VMEM_LIMIT),
        name="moe_combine_dft" if groups else "moe_combine",
    )(*args)


def _hier_moe_layer(x1, t, g2, lng, lnb, wg, bg, wf, bf, w1, w3, w2, dft, *, layer, alpha):
    b, n, d = x1.shape
    ttot = b * n
    ng = wg.shape[1]
    ne = wf.shape[1]
    epg = ne // ng
    tm = min(MOE_TILE, n)
    chunk = MOE_CHUNK
    nt = ttot // tm
    tflat = t.reshape(ttot, d)
    wcat = jnp.zeros((d, LANES), F32).at[:, 0:ng].set(wg).at[:, SUBLANES:SUBLANES + ne].set(wf).astype(BF16)
    bcat = jnp.zeros((1, LANES), F32).at[0, 0:ng].set(bg).at[0, SUBLANES:SUBLANES + ne].set(bf)
    ar = jnp.arange(2 * tm, dtype=I32)
    upper = (ar[:, None] < ar[None, :]).astype(BF16)
    lp, tokm, cnt = _route(tflat, wcat, bcat, upper, ng=ng, epg=epg, tm=tm, chunk=chunk)

    tile_cnt = cnt.reshape(ne, nt, LANES)[:, :, 0].T.astype(I32)
    seg_len = ((tile_cnt + 1) // 2) * 2
    counts = jnp.sum(seg_len, axis=0)
    padded = ((counts + chunk + MOE_ROWS - 1) // MOE_ROWS) * MOE_ROWS
    pend = jnp.cumsum(padded)
    pstart = pend - padded
    base = jnp.cumsum(seg_len, axis=0) - seg_len
    seg_row = pstart[None, :] + base
    nch = (tile_cnt + chunk - 1) // chunk
    nch_end = jnp.cumsum(nch, axis=1)
    tile_nch = nch_end[:, -1].astype(I32)
    qmax = 2 * tm // chunk + ne
    qs = jnp.arange(qmax, dtype=I32)
    e_q = jnp.minimum(jnp.sum((nch_end[:, None, :] <= qs[None, :, None]).astype(I32), axis=2), ne - 1)
    pick = e_q[:, :, None] == jnp.arange(ne, dtype=I32)[None, None, :]
    first_q = jnp.sum(jnp.where(pick, (nch_end - nch)[:, None, :], 0), axis=2)
    first_row = jnp.sum(jnp.where(pick, seg_row[:, None, :], 0), axis=2)
    chunk_row = (first_row + (qs[None, :] - first_q) * chunk).astype(I32).reshape(nt * qmax)
    nb = -(-(2 * ttot + nt * ne + ne * (chunk + MOE_ROWS)) // MOE_ROWS)
    bstart = jnp.arange(nb, dtype=I32) * MOE_ROWS
    block_e = jnp.minimum(jnp.sum((pend[None, :] <= bstart[:, None]).astype(I32), axis=1), ne - 1)
    n_used = (pend[-1] // MOE_ROWS).astype(I32).reshape(1)
    zstart = jnp.concatenate([pstart + (counts // chunk) * chunk, pend[-1:]])
    zend = jnp.concatenate([pend, jnp.full((1,), nb * MOE_ROWS, pend.dtype)])
    zcnt = (zend - zstart) // chunk
    zcum = jnp.cumsum(zcnt)
    zmax = ne * ((MOE_ROWS + 2 * chunk) // chunk + 1) + (nb - (2 * ttot) // MOE_ROWS) * (MOE_ROWS // chunk)
    zq = jnp.arange(zmax, dtype=I32)
    zseg = jnp.minimum(jnp.sum((zcum[None, :] <= zq[:, None]).astype(I32), axis=1), ne)
    zpick = zseg[:, None] == jnp.arange(ne + 1, dtype=I32)[None, :]
    zfirst = jnp.sum(jnp.where(zpick, (zcum - zcnt)[None, :], 0), axis=1)
    zero_row = (jnp.sum(jnp.where(zpick, zstart[None, :], 0), axis=1) + (zq - zfirst) * chunk).astype(I32)
    n_zero = zcum[-1].astype(I32).reshape(1)

    xs = _dispatch(chunk_row, tile_nch, zero_row, n_zero, tflat, lp, n_rows=nb * MOE_ROWS, ne=ne, tm=tm,
                   chunk=chunk)
    bpick = block_e[:, None] == jnp.arange(ne, dtype=I32)[None, :]
    block_rows = jnp.clip(jnp.sum(jnp.where(bpick, (pstart + counts)[None, :], 0), axis=1) - bstart,
                          0, MOE_ROWS).astype(I32)
    yb = _moe_experts(block_e.astype(I32), block_rows, n_used, xs, w1, w3, w2, layer=layer)
    out = _combine(chunk_row, tile_nch, yb, x1.reshape(ttot, d), tokm, g2, lng, lnb, dft,
                   ne=ne, tm=tm, chunk=chunk, n_per_batch=n, alpha=alpha)
    if dft is None:
        return out.reshape(b, n, d), None
    return out[0].reshape(b, n, d), (out[1].reshape(b, n, d), out[2].reshape(b, n, d))


def kernel(x, c, ctx, c_ctx, ada_w, ada_b, ln_g, ln_b, ev_w_in, ev_conv_w, ev_conv_b, ev_gate_a_w,
           ev_gate_a_b, ev_gate_x_w, ev_gate_x_b, ev_lru_lambda, ev_da_lambda, ev_da_subln, ev_w_out,
           od_w_out, od_b_out, moe_wg, moe_bg, moe_wf, moe_bf, moe_w1, moe_w3, moe_w2):
    bsz, n_lat, d = x.shape
    depth = ada_w.shape[0]
    alpha = (2.0 * depth) ** 0.25
    lw = ev_conv_w.shape[-1]
    hd = ev_da_lambda.shape[-1]
    vd = ev_da_subln.shape[-1]
    aw = (ev_w_in.shape[-1] - 2 * lw) // 3
    heads = aw // vd
    fnet_groups = 4

    rows = ((bsz + 1 + SUBLANES - 1) // SUBLANES) * SUBLANES
    cond = jnp.zeros((rows, d), F32).at[0:bsz].set(c).at[bsz].set(c_ctx)
    ada = _ada_terms(cond, ada_w, ada_b).reshape(depth, rows, 6, d)

    def lat_term(l, k):
        return ada[l, 0:bsz, k, :].reshape(bsz, 1, d)

    def ctx_term(l, k):
        return jnp.broadcast_to(ada[l, bsz, k, :].reshape(1, 1, d), (bsz, 1, d))

    pre_dft = None
    for l in range(depth):
        ctx_live = any(m % 2 == 0 for m in range(l + 1, depth))
        assert not ctx_live, "context stream update is only needed for depth > 2"
        sh1, sc1, g1, sh2, sc2, g2 = [lat_term(l, k) for k in range(6)]
        if l % 2 == 0:
            e = l // 2
            lam_init = 0.8 - 0.6 * math.exp(-0.3 * l)
            w_in_b = ev_w_in[e].astype(BF16)
            cos_t, sin_t = _rope_tables(n_lat, hd)
            qscale = hd ** -0.5 * math.log2(math.e)
            g_l, xr_l, q_l, k_l, v_l = _project_even(x, sh1, sc1, w_in_b, cos_t, sin_t, rope=True,
                                                     lw=lw, aw=aw, qscale=qscale, tm=1024)
            n_ctx = ctx.shape[1]
            _, xr_c, _, k_c, v_c = _project_even(ctx, ctx_term(l, 0), ctx_term(l, 1), w_in_b,
                                                 cos_t[0:n_ctx], sin_t[0:n_ctx], rope=False,
                                                 lw=lw, aw=aw, qscale=qscale, tm=256)
            o_l = _diff_attention(q_l, k_c, k_l, v_c, v_l, ev_da_lambda[e], ev_da_subln[e],
                                  heads=heads, hd=hd, lam_init=lam_init)
            gwid = 256
            y = None
            for dirn, rev in ((0, False), (1, True)):
                wa_bd = _block_diag_groups(ev_gate_a_w[e, dirn], gwid)
                wx_bd = _block_diag_groups(ev_gate_x_w[e, dirn], gwid)
                common = (ev_conv_w[e], ev_conv_b[e], wa_bd, ev_gate_a_b[e, dirn], wx_bd,
                          ev_gate_x_b[e, dirn], ev_lru_lambda[e, dirn])
                h_zero = jnp.zeros((bsz, 1, lw), F32)
                _, h_fin = _rglru_dir(xr_c, None, *common, h_zero, reverse=rev, tn=256)
                y, _ = _rglru_dir(xr_l, y, *common, h_fin, reverse=rev, tn=1024)
            x1, t = _outproj_even(y, g_l, o_l, x, ev_w_out[e].astype(BF16), g1, ln_g[l, 0], ln_b[l, 0],
                                  sh2, sc2, alpha=alpha, tm=1024)
        else:
            o = l // 2
            gd = d // fnet_groups
            if pre_dft is None:
                pre_dft = _chan_dft(x, sh1, sc1, _chan_dft_table(gd), groups=fnet_groups, tm=1024)
            wv = _tok_fft(*pre_dft)
            norm = 1.0 / math.sqrt(float(n_lat * gd))
            x1, t = _outproj_odd(wv, x, od_w_out[o].astype(BF16), od_b_out[o], g1, ln_g[l, 0], ln_b[l, 0],
                                 sh2, sc2, alpha=alpha, norm=norm, tm=1024)
        dft = None
        if l + 1 < depth and (l + 1) % 2 == 1:
            dft = (lat_term(l + 1, 0), lat_term(l + 1, 1), _chan_dft_table(d // fnet_groups), fnet_groups)
        x, pre_dft = _hier_moe_layer(x1, t, g2, ln_g[l, 1], ln_b[l, 1], moe_wg[l], moe_bg[l], moe_wf[l],
                                     moe_bf[l], moe_w1, moe_w3, moe_w2, dft, layer=l, alpha=alpha)
    return x
```
